```python
import math
import jax
import jax.numpy as jnp
from jax import lax
import numpy as np

D_MODEL = 1024
BATCH = 4
SEQ = 4096
DEPTH = 2

N_EVEN = (DEPTH + 1) // 2
N_ODD = DEPTH // 2
NORM_EPS = 1e-6

HG_DK = 128
HG_HEADS = D_MODEL // HG_DK
HG_WIDTH = HG_HEADS * HG_DK
HG_CHUNK = 64

SSM_HEADDIM = 64
SSM_HEADS = D_MODEL // SSM_HEADDIM
SSM_WIDTH = SSM_HEADS * SSM_HEADDIM
SSM_GROUPS = 2
SSM_STATE = 128
SSM_CONV = 4
SSM_CHUNK = 64
SSM_XBC = SSM_WIDTH + 2 * SSM_GROUPS * SSM_STATE

IN_SIZES = (HG_WIDTH, HG_WIDTH, HG_WIDTH, HG_WIDTH, SSM_WIDTH, SSM_XBC, SSM_HEADS)
IN_COLS = 4 * HG_WIDTH + SSM_WIDTH + SSM_XBC + SSM_HEADS
MIX_WIDTH = HG_WIDTH + SSM_WIDTH

RW_HEADSIZE = 64
RW_HEADS = D_MODEL // RW_HEADSIZE
RW_DECAY_LORA = 64
RW_AAA_LORA = 64
RW_GATE_LORA = 128
RW_GN_EPS = 64e-5

N_EXPERTS = 16
N_EXPERT_GROUPS = 4
EXPERTS_PER_GROUP = N_EXPERTS // N_EXPERT_GROUPS
TOP_K = 2
D_EXPERT = D_MODEL // 2

kernel_name = "hybrid_hgrn2_ssd_rwkv7_grouped_moe"


def group_rms_norm(x, w, n_groups):
    shp = x.shape
    xg = x.astype(jnp.float32).reshape(shp[:-1] + (n_groups, shp[-1] // n_groups))
    y = xg * lax.rsqrt(jnp.mean(xg * xg, axis=-1, keepdims=True) + NORM_EPS)
    return (y.reshape(shp) * w.astype(jnp.float32)).astype(x.dtype)


def rms_norm(x, w):
    return group_rms_norm(x, w, 1)


def modulate(h, shift, scale):
    return h * (1.0 + scale[:, None, :]) + shift[:, None, :]


def causal_dwconv(x, w, b):
    k = w.shape[0]
    y = lax.conv_general_dilated(x, w[:, None, :].astype(x.dtype), window_strides=(1,),
                                 padding=[(k - 1, 0)], dimension_numbers=('NWC', 'WIO', 'NWC'),
                                 feature_group_count=x.shape[-1])
    return y + b


def segsum(x):
    t = x.shape[-1]
    idx = jnp.arange(t)
    xe = jnp.broadcast_to(x[..., :, None], x.shape + (t,))
    cs = jnp.cumsum(jnp.where(idx[:, None] > idx[None, :], xe, 0.0), axis=-2)
    return jnp.where(idx[:, None] >= idx[None, :], cs, -jnp.inf)


def hgrn2_chunk_scan(q, k, v, log_f):
    bsz, s_len, h, dk = q.shape
    dv = v.shape[-1]
    nc = s_len // HG_CHUNK

    def to_chunks(t):
        return t.reshape(bsz, nc, HG_CHUNK, h, t.shape[-1]).transpose(1, 0, 3, 2, 4)

    causal = jnp.tril(jnp.ones((HG_CHUNK, HG_CHUNK), dtype=bool))

    def step(state, inp):
        qi, ki, vi, gi = inp
        b = jnp.cumsum(gi, axis=2)
        b_last = b[:, :, -1:, :]
        o_inter = jnp.einsum('bhtk,bhkv->bhtv', qi * jnp.exp(b), state)
        diff = b[:, :, :, None, :] - b[:, :, None, :, :]
        decay = jnp.exp(jnp.where(causal[:, :, None], diff, -jnp.inf))
        scores = jnp.einsum('bhtk,bhtsk,bhsk->bhts', qi, decay, ki)
        o_intra = jnp.einsum('bhts,bhsv->bhtv', scores, vi)
        new_state = (jnp.exp(b_last[:, :, 0, :])[..., None] * state
                     + jnp.einsum('bhsk,bhsv->bhkv', ki * jnp.exp(b_last - b), vi))
        return new_state, o_inter + o_intra

    s0 = jnp.zeros((bsz, h, dk, dv), jnp.float32)
    _, o = lax.scan(step, s0, (to_chunks(q), to_chunks(k), to_chunks(v), to_chunks(log_f)))
    return o.transpose(1, 0, 3, 2, 4).reshape(bsz, s_len, h, dv)


def ssd_chunk_scan(x, dt, a, bm, cm):
    bsz, s_len, h, p = x.shape
    g, n = bm.shape[-2:]
    j = h // g
    nc = s_len // SSM_CHUNK
    l = SSM_CHUNK
    xc = (x * dt[..., None]).reshape(bsz, nc, l, g, j, p)
    da = (dt * a).reshape(bsz, nc, l, g, j).transpose(0, 3, 4, 1, 2)
    bc = bm.reshape(bsz, nc, l, g, n)
    cc = cm.reshape(bsz, nc, l, g, n)
    a_cs = jnp.cumsum(da, axis=-1)
    lmat = jnp.exp(segsum(da))
    cb = jnp.einsum('bclgn,bcsgn->bgcls', cc, bc)
    y_diag = jnp.einsum('bgjcls,bcsgjp->bclgjp', cb[:, :, None] * lmat, xc)
    decay_states = jnp.exp(a_cs[..., -1:] - a_cs).transpose(0, 3, 4, 1, 2)
    states = jnp.einsum('bclgn,bclgjp->bcgjpn', bc, xc * decay_states[..., None])
    chunk_decay = jnp.exp(segsum(jnp.pad(a_cs[..., -1], ((0, 0), (0, 0), (0, 0), (1, 0)))))
    states = jnp.concatenate([jnp.zeros_like(states[:, :1]), states], axis=1)
    states = jnp.einsum('bgjzc,bcgjpn->bzgjpn', chunk_decay, states)[:, :-1]
    state_decay_out = jnp.exp(a_cs).transpose(0, 3, 4, 1, 2)
    y_off = jnp.einsum('bclgn,bcgjpn->bclgjp', cc, states) * state_decay_out[..., None]
    return (y_diag + y_off).reshape(bsz, s_len, h, p)


def hgrn2_ssd_mixer(h, lb, w_in, hg_norm, conv_w, conv_b, dt_bias, a_log, d_skip, ssm_norm, w_out):
    bsz, s_len, _ = h.shape
    f32 = jnp.float32
    proj = jnp.einsum('bsd,de->bse', h, w_in)
    bounds = [sum(IN_SIZES[:m]) for m in range(1, len(IN_SIZES))]
    q, f_raw, i_val, g_out, z, xbc, dt_raw = jnp.split(proj, bounds, axis=-1)
    f = lb + (1.0 - lb) * jax.nn.sigmoid(f_raw.astype(f32))
    hg = lambda t: t.astype(f32).reshape(bsz, s_len, HG_HEADS, HG_DK)
    o_a = hgrn2_chunk_scan(hg(q), hg(1.0 - f), hg(i_val), hg(jnp.log(f)))
    o_a = group_rms_norm(o_a.reshape(bsz, s_len, HG_WIDTH), hg_norm, HG_HEADS) * jax.nn.silu(g_out.astype(f32))
    xbc = jax.nn.silu(causal_dwconv(xbc, conv_w, conv_b))
    xs, bm, cm = jnp.split(xbc, [SSM_WIDTH, SSM_WIDTH + SSM_GROUPS * SSM_STATE], axis=-1)
    dt = jax.nn.softplus(dt_raw.astype(f32) + dt_bias.astype(f32))
    a = -jnp.exp(a_log.astype(f32))
    xs_h = xs.astype(f32).reshape(bsz, s_len, SSM_HEADS, SSM_HEADDIM)
    y = ssd_chunk_scan(xs_h, dt, a,
                       bm.astype(f32).reshape(bsz, s_len, SSM_GROUPS, SSM_STATE),
                       cm.astype(f32).reshape(bsz, s_len, SSM_GROUPS, SSM_STATE))
    y = y + d_skip.astype(f32)[:, None] * xs_h
    o_b = group_rms_norm(y.reshape(bsz, s_len, SSM_WIDTH) * jax.nn.silu(z.astype(f32)), ssm_norm, SSM_GROUPS)
    o = jnp.concatenate([o_a, o_b], axis=-1).astype(h.dtype)
    return jnp.einsum('bse,ed->bsd', o, w_out).astype(h.dtype)


def rwkv7_scan(r, w, k, v, kk, a):
    def step(state, inp):
        r_t, w_t, k_t, v_t, kk_t, a_t = inp
        sa = jnp.einsum('bhvk,bhk->bhv', state, -kk_t)
        state = (state * w_t[:, :, None, :] + sa[..., None] * (kk_t * a_t)[:, :, None, :]
                 + v_t[..., None] * k_t[:, :, None, :])
        return state, jnp.einsum('bhvk,bhk->bhv', state, r_t)

    bsz, _, h, n = r.shape
    seq_first = tuple(t.transpose(1, 0, 2, 3) for t in (r, w, k, v, kk, a))
    s0 = jnp.zeros((bsz, h, n, n), jnp.float32)
    _, y = lax.scan(step, s0, seq_first)
    return y.transpose(1, 0, 2, 3)


def rwkv7_mixer(h, mu, w_rkv, w_dec0, w_dec1, w_dec2, a0, a1, a2, g1, g2, k_k, k_a, r_k, ln_w, ln_b, w_o):
    bsz, s_len, d = h.shape
    f32 = jnp.float32
    xx = jnp.pad(h[:, :-1], ((0, 0), (1, 0), (0, 0))) - h
    mixed = h[None] + xx[None] * mu[:, None, None, :]
    r, k, v = jnp.einsum('ibsd,ide->ibse', mixed[:3], w_rkv)
    xw, xa, xg = mixed[3], mixed[4], mixed[5]
    w_log = -jax.nn.softplus(-(w_dec0 + jnp.tanh(xw @ w_dec1) @ w_dec2).astype(f32)) - 0.5
    decay = jnp.exp(-jnp.exp(w_log))
    a = jax.nn.sigmoid((a0 + (xa @ a1) @ a2).astype(f32))
    g = jax.nn.sigmoid(xg @ g1) @ g2
    heads = lambda t: t.astype(f32).reshape(bsz, s_len, RW_HEADS, RW_HEADSIZE)
    r_h, v_h, w_h, a_h = heads(r), heads(v), heads(decay), heads(a)
    kk = heads(k * k_k)
    kk = kk / jnp.maximum(jnp.sqrt(jnp.sum(kk * kk, axis=-1, keepdims=True)), 1e-12)
    k_h = heads(k) * (1.0 + (a_h - 1.0) * k_a.astype(f32).reshape(RW_HEADS, RW_HEADSIZE))
    o = rwkv7_scan(r_h, w_h, k_h, v_h, kk, a_h)
    mean = jnp.mean(o, axis=-1, keepdims=True)
    var = jnp.mean(jnp.square(o - mean), axis=-1, keepdims=True)
    o = ((o - mean) * lax.rsqrt(var + RW_GN_EPS) * ln_w.astype(f32).reshape(RW_HEADS, RW_HEADSIZE)
         + ln_b.astype(f32).reshape(RW_HEADS, RW_HEADSIZE))
    o = o + jnp.sum(r_h * k_h * r_k.astype(f32), axis=-1, keepdims=True) * v_h
    o = (o.reshape(bsz, s_len, d) * g).astype(h.dtype)
    return (o @ w_o).astype(h.dtype)


def grouped_moe(h, router_w, router_b, w1, w3, w2):
    bsz, s_len, _ = h.shape
    logits = jnp.einsum('bsd,de->bse', h, router_w).astype(jnp.float32) + router_b.astype(jnp.float32)
    probs = jax.nn.softmax(logits, axis=-1)
    grouped = probs.reshape(bsz, s_len, N_EXPERT_GROUPS, EXPERTS_PER_GROUP)
    group_score = jnp.sum(lax.top_k(grouped, TOP_K)[0], axis=-1)
    group_mask = jax.nn.one_hot(jnp.argmax(group_score, axis=-1), N_EXPERT_GROUPS, dtype=bool)
    masked = jnp.where(group_mask[..., None], grouped, -1.0).reshape(bsz, s_len, N_EXPERTS)
    top_w, top_i = lax.top_k(masked, TOP_K)
    top_w = top_w / jnp.sum(top_w, axis=-1, keepdims=True)
    gates = jnp.sum(jax.nn.one_hot(top_i, N_EXPERTS, dtype=jnp.float32) * top_w[..., None], axis=-2)
    hidden = jax.nn.silu(jnp.einsum('bsd,edf->bsef', h, w1)) * jnp.einsum('bsd,edf->bsef', h, w3)
    hidden = hidden * gates.astype(hidden.dtype)[..., None]
    return jnp.einsum('bsef,efd->bsd', hidden, w2).astype(h.dtype)


def setup_inputs(seed: int = 0) -> dict:
    key = jax.random.key(seed)
    ks = iter(jax.random.split(key, 64))
    f32 = jnp.float32
    nrm = lambda shape, scale: jax.random.normal(next(ks), shape, f32) * scale
    uni = lambda shape, lo, hi: jax.random.uniform(next(ks), shape, f32, lo, hi)
    d = D_MODEL
    x = nrm((BATCH, SEQ, d), 1.0)
    c = nrm((BATCH, d), 1.0)
    mod_w = nrm((DEPTH, d, 6 * d), 0.5 * d ** -0.5)
    mod_b = nrm((DEPTH, 6 * d), 0.02)
    norm_w = 1.0 + nrm((DEPTH, 2, d), 0.02)
    hg_lb_logits = nrm((DEPTH + 1, HG_WIDTH), 1.0)
    ev_w_in = nrm((N_EVEN, d, IN_COLS), d ** -0.5)
    ev_hg_norm = 1.0 + nrm((N_EVEN, HG_WIDTH), 0.02)
    ev_conv_w = nrm((N_EVEN, SSM_CONV, SSM_XBC), SSM_CONV ** -0.5)
    ev_conv_b = nrm((N_EVEN, SSM_XBC), 0.02)
    dt0 = jnp.exp(uni((N_EVEN, SSM_HEADS), math.log(1e-3), math.log(1e-1)))
    ev_dt_bias = dt0 + jnp.log(-jnp.expm1(-dt0))
    ev_a_log = jnp.log(uni((N_EVEN, SSM_HEADS), 1.0, 16.0))
    ev_d_skip = 1.0 + nrm((N_EVEN, SSM_HEADS), 0.1)
    ev_ssm_norm = 1.0 + nrm((N_EVEN, SSM_WIDTH), 0.02)
    ev_w_out = nrm((N_EVEN, MIX_WIDTH, d), MIX_WIDTH ** -0.5)
    od_mu = uni((N_ODD, 6, d), 0.0, 1.0)
    od_w_rkv = nrm((N_ODD, 3, d, d), d ** -0.5)
    od_w_dec0 = uni((N_ODD, d), -6.0, 0.0)
    od_w_dec1 = nrm((N_ODD, d, RW_DECAY_LORA), d ** -0.5)
    od_w_dec2 = nrm((N_ODD, RW_DECAY_LORA, d), 0.5 * RW_DECAY_LORA ** -0.5)
    od_a0 = nrm((N_ODD, d), 0.5)
    od_a1 = nrm((N_ODD, d, RW_AAA_LORA), d ** -0.5)
    od_a2 = nrm((N_ODD, RW_AAA_LORA, d), 0.5 * RW_AAA_LORA ** -0.5)
    od_g1 = nrm((N_ODD, d, RW_GATE_LORA), d ** -0.5)
    od_g2 = nrm((N_ODD, RW_GATE_LORA, d), RW_GATE_LORA ** -0.5)
    od_k_k = 0.85 + nrm((N_ODD, d), 0.05)
    od_k_a = 1.0 + nrm((N_ODD, d), 0.05)
    od_r_k = nrm((N_ODD, RW_HEADS, RW_HEADSIZE), 0.1)
    od_ln_w = 1.0 + nrm((N_ODD, d), 0.02)
    od_ln_b = nrm((N_ODD, d), 0.02)
    od_w_o = nrm((N_ODD, d, d), d ** -0.5)
    router_w = nrm((d, N_EXPERTS), d ** -0.5)
    router_b = nrm((N_EXPERTS,), 0.01)
    moe_w1 = nrm((DEPTH, N_EXPERTS, d, D_EXPERT), d ** -0.5)
    moe_w3 = nrm((DEPTH, N_EXPERTS, d, D_EXPERT), d ** -0.5)
    moe_w2 = nrm((DEPTH, N_EXPERTS, D_EXPERT, d), D_EXPERT ** -0.5)
    final_norm_w = 1.0 + nrm((d,), 0.02)
    return {"x": x, "c": c, "mod_w": mod_w, "mod_b": mod_b, "norm_w": norm_w,
            "hg_lb_logits": hg_lb_logits, "ev_w_in": ev_w_in, "ev_hg_norm": ev_hg_norm,
            "ev_conv_w": ev_conv_w, "ev_conv_b": ev_conv_b, "ev_dt_bias": ev_dt_bias,
            "ev_a_log": ev_a_log, "ev_d_skip": ev_d_skip, "ev_ssm_norm": ev_ssm_norm,
            "ev_w_out": ev_w_out, "od_mu": od_mu, "od_w_rkv": od_w_rkv, "od_w_dec0": od_w_dec0,
            "od_w_dec1": od_w_dec1, "od_w_dec2": od_w_dec2, "od_a0": od_a0, "od_a1": od_a1,
            "od_a2": od_a2, "od_g1": od_g1, "od_g2": od_g2, "od_k_k": od_k_k, "od_k_a": od_k_a,
            "od_r_k": od_r_k, "od_ln_w": od_ln_w, "od_ln_b": od_ln_b, "od_w_o": od_w_o,
            "router_w": router_w, "router_b": router_b, "moe_w1": moe_w1, "moe_w3": moe_w3,
            "moe_w2": moe_w2, "final_norm_w": final_norm_w}


def reference(x, c, mod_w, mod_b, norm_w, hg_lb_logits, ev_w_in, ev_hg_norm, ev_conv_w, ev_conv_b,
              ev_dt_bias, ev_a_log, ev_d_skip, ev_ssm_norm, ev_w_out, od_mu, od_w_rkv, od_w_dec0,
              od_w_dec1, od_w_dec2, od_a0, od_a1, od_a2, od_g1, od_g2, od_k_k, od_k_a, od_r_k,
              od_ln_w, od_ln_b, od_w_o, router_w, router_b, moe_w1, moe_w3, moe_w2, final_norm_w):
    mod = jnp.einsum('bd,lde->lbe', jax.nn.silu(c), mod_w) + mod_b[:, None, :]
    gamma = jax.nn.softmax(hg_lb_logits.astype(jnp.float32), axis=0)
    lower_bounds = jnp.cumsum(gamma, axis=0) - gamma[0]
    for l in range(DEPTH):
        sh_m, sc_m, gt_m, sh_f, sc_f, gt_f = jnp.split(mod[l], 6, axis=-1)
        h = modulate(rms_norm(x, norm_w[l, 0]), sh_m, sc_m)
        j = l // 2
        if l % 2 == 0:
            y = hgrn2_ssd_mixer(h, lower_bounds[l + 1], ev_w_in[j], ev_hg_norm[j], ev_conv_w[j],
                                ev_conv_b[j], ev_dt_bias[j], ev_a_log[j], ev_d_skip[j],
                                ev_ssm_norm[j], ev_w_out[j])
        else:
            y = rwkv7_mixer(h, od_mu[j], od_w_rkv[j], od_w_dec0[j], od_w_dec1[j], od_w_dec2[j],
                            od_a0[j], od_a1[j], od_a2[j], od_g1[j], od_g2[j], od_k_k[j], od_k_a[j],
                            od_r_k[j], od_ln_w[j], od_ln_b[j], od_w_o[j])
        x = x + gt_m[:, None, :] * y
        h = modulate(rms_norm(x, norm_w[l, 1]), sh_f, sc_f)
        x = x + gt_f[:, None, :] * grouped_moe(h, router_w, router_b, moe_w1[l], moe_w3[l], moe_w2[l])
    return rms_norm(x, final_norm_w)
```

```python
import functools

import jax
import jax.numpy as jnp
from jax import lax
from jax.experimental import pallas as pl
from jax.experimental.pallas import tpu as pltpu

F32 = jnp.float32
BF16 = jnp.bfloat16
I32 = jnp.int32

NORM_EPS = 1e-6
RW_GN_EPS = 64e-5
LANES = 128
SUBLANES = 8
VMEM_LIMIT = 56 * 1024 * 1024

HG_DK = 128
SSM_P = 64
SSM_N = 128
SSM_GROUPS = 2
SSM_CONV = 4
RW_N = 64
N_GROUPS_MOE = 4
TOP_K = 2

CHUNK = 128
RW_CHUNK = 64
RW_PACK = 4
MOE_TM = 256
ROW_TM = 256


def _cparams(sem):
    return pltpu.CompilerParams(dimension_semantics=sem, vmem_limit_bytes=VMEM_LIMIT)


def _dot(a, b):
    return lax.dot_general(a, b, (((1,), (0,)), ((), ())), preferred_element_type=F32)


def _dot_nt(a, b):
    return lax.dot_general(a, b, (((1,), (1,)), ((), ())), preferred_element_type=F32)


def _dot_tn(a, b):
    return lax.dot_general(a, b, (((0,), (0,)), ((), ())), preferred_element_type=F32)


def _split(x):
    hi = x.astype(BF16)
    return hi, (x - hi.astype(F32)).astype(BF16)


def _dot_sel(sel, x):
    hi, lo = _split(x)
    return _dot(sel, hi) + _dot(sel, lo)


def _dot_rsel(x, sel):
    hi, lo = _split(x)
    return _dot(hi, sel) + _dot(lo, sel)


def _sigmoid(x):
    return 1.0 / (1.0 + jnp.exp(-x))


def _silu(x):
    return x * _sigmoid(x)


def _softplus(x):
    return jnp.maximum(x, 0.0) + jnp.log(1.0 + jnp.exp(-jnp.abs(x)))


def _iota(shape, dim):
    return lax.broadcasted_iota(I32, shape, dim)


def _mod_kernel(c_ref, w_ref, b_ref, o_ref):
    c = c_ref[...]
    o_ref[0] = _dot(_silu(c).astype(BF16), w_ref[0].astype(BF16)) + b_ref[0]


def _adaln_mod(c, mod_w, mod_b):
    depth, d, width = mod_w.shape
    bsz = c.shape[0]
    c_pad = jnp.zeros((SUBLANES, d), F32).at[:bsz].set(c)
    tn = 1536
    out = pl.pallas_call(
        _mod_kernel,
        grid=(depth, width // tn),
        in_specs=[pl.BlockSpec((SUBLANES, d), lambda l, j: (0, 0)),
                  pl.BlockSpec((1, d, tn), lambda l, j: (l, 0, j)),
                  pl.BlockSpec((1, 1, tn), lambda l, j: (l, 0, j))],
        out_specs=pl.BlockSpec((1, SUBLANES, tn), lambda l, j: (l, 0, j)),
        out_shape=jax.ShapeDtypeStruct((depth, SUBLANES, width), F32),
        compiler_params=_cparams(("arbitrary", "arbitrary")),
        name="adaln_mod",
    )(c_pad, mod_w, mod_b.reshape(depth, 1, width))
    return out[:, :bsz]


def _rms_mod(x, nw, shift, scale):
    y = x * lax.rsqrt(jnp.mean(x * x, axis=-1, keepdims=True) + NORM_EPS) * nw
    return y * (1.0 + scale) + shift


def _normmod_kernel(x_ref, nw_ref, sh_ref, sc_ref, h_ref):
    h_ref[...] = _rms_mod(x_ref[...], nw_ref[...], sh_ref[0], sc_ref[0]).astype(h_ref.dtype)


def _norm_mod(x2, nw, shift, scale, bsz, seq):
    t, d = x2.shape
    tm = ROW_TM
    ns = seq // tm
    return pl.pallas_call(
        _normmod_kernel,
        grid=(bsz, ns),
        in_specs=[pl.BlockSpec((tm, d), lambda b, s: (b * ns + s, 0)),
                  pl.BlockSpec((1, d), lambda b, s: (0, 0)),
                  pl.BlockSpec((1, 1, d), lambda b, s: (b, 0, 0)),
                  pl.BlockSpec((1, 1, d), lambda b, s: (b, 0, 0))],
        out_specs=pl.BlockSpec((tm, d), lambda b, s: (b * ns + s, 0)),
        out_shape=jax.ShapeDtypeStruct((t, d), BF16),
        compiler_params=_cparams(("arbitrary", "arbitrary")),
        name="norm_mod",
    )(x2, nw.reshape(1, d), shift.reshape(bsz, 1, d), scale.reshape(bsz, 1, d))


def _mm_kernel(x_ref, w_ref, o_ref):
    o_ref[...] = _dot(x_ref[...], w_ref[...]).astype(o_ref.dtype)


def _matmul(x, w, out_dtype, tm, tn):
    t, k = x.shape
    n = w.shape[1]
    return pl.pallas_call(
        _mm_kernel,
        grid=(n // tn, t // tm),
        in_specs=[pl.BlockSpec((tm, k), lambda j, i: (i, 0)),
                  pl.BlockSpec((k, tn), lambda j, i: (0, j))],
        out_specs=pl.BlockSpec((tm, tn), lambda j, i: (i, j)),
        out_shape=jax.ShapeDtypeStruct((t, n), out_dtype),
        compiler_params=_cparams(("arbitrary", "arbitrary")),
        name="matmul",
    )(x, w)


def _hgrn2_kernel(q_ref, f_ref, i_ref, g_ref, lb_ref, nw_ref, o_ref, st_ref):
    L = CHUNK
    dk = HG_DK
    n_heads = st_ref.shape[0]

    @pl.when(pl.program_id(1) == 0)
    def _():
        st_ref[...] = jnp.zeros_like(st_ref)

    lb = lb_ref[...]
    f = lb + (1.0 - lb) * _sigmoid(f_ref[...])
    logf = jnp.log(f)
    t_i = _iota((L, L), 0)
    s_i = _iota((L, L), 1)
    tril = (t_i >= s_i).astype(BF16)
    b_all = _dot_sel(tril, logf)

    levels = []
    n2 = L
    while n2 >= 2 * SUBLANES:
        n = n2 // 2
        sh = n2.bit_length() - 1
        m = ((t_i >> sh) == (s_i >> sh)) & ((t_i & (n2 - 1)) >= n) & ((s_i & (n2 - 1)) < n)
        levels.append((n2, m))
        n2 = n
    nb = L // SUBLANES
    blk3 = _iota((nb, SUBLANES, L), 0)
    sub3 = _iota((nb, SUBLANES, L), 1)
    lane3 = _iota((nb, SUBLANES, L), 2)

    for h in range(n_heads):
        sl = slice(h * dk, (h + 1) * dk)
        q = q_ref[:, sl]
        k = 1.0 - f[:, sl]
        v = i_ref[:, sl]
        b = b_all[:, sl]
        st = st_ref[h]
        b_last = b[L - 1:L, :]
        o = _dot_nt((q * jnp.exp(b)).astype(BF16), st.astype(BF16))
        a = jnp.zeros((L, L), F32)
        for n2, m in levels:
            n = n2 // 2
            r = jnp.broadcast_to(b.reshape(L // n2, n2, dk)[:, n - 1:n, :], (L // n2, n2, dk)).reshape(L, dk)
            qe = q * jnp.exp(jnp.minimum(b - r, 0.0))
            ke = k * jnp.exp(jnp.minimum(r - b, 0.0))
            a = a + jnp.where(m, _dot_nt(qe.astype(BF16), ke.astype(BF16)), 0.0)
        b3 = b.reshape(nb, SUBLANES, dk)
        q3 = q.reshape(nb, SUBLANES, dk)
        k3 = k.reshape(nb, SUBLANES, dk)
        a8 = jnp.zeros((nb, SUBLANES, L), F32)
        for s in range(SUBLANES):
            x = q3 * k3[:, s:s + 1, :] * jnp.exp(jnp.minimum(b3 - b3[:, s:s + 1, :], 0.0))
            rs = jnp.sum(x, axis=-1, keepdims=True)
            a8 = jnp.where((lane3 == blk3 * SUBLANES + s) & (sub3 >= s), rs, a8)
        a = a + a8.reshape(L, L)
        o = o + _dot(a.astype(BF16), v.astype(BF16))
        ke = k * jnp.exp(b_last - b)
        st_ref[h] = st * jnp.exp(b_last) + _dot_tn(v.astype(BF16), ke.astype(BF16))
        g = g_ref[:, sl]
        ms = jnp.mean(o * o, axis=-1, keepdims=True)
        o_ref[:, sl] = (o * lax.rsqrt(ms + NORM_EPS) * nw_ref[:, sl] * _silu(g)).astype(o_ref.dtype)


def _hgrn2(proj, lb, hg_norm, bsz, seq):
    t = proj.shape[0]
    width = lb.shape[0]
    n_heads = width // HG_DK
    ns = seq // CHUNK
    blk = lambda j: pl.BlockSpec((CHUNK, width), lambda b, s, j=j: (b * ns + s, j))
    return pl.pallas_call(
        _hgrn2_kernel,
        grid=(bsz, ns),
        in_specs=[blk(0), blk(1), blk(2), blk(3),
                  pl.BlockSpec((1, width), lambda b, s: (0, 0)),
                  pl.BlockSpec((1, width), lambda b, s: (0, 0))],
        out_specs=pl.BlockSpec((CHUNK, width), lambda b, s: (b * ns + s, 0)),
        out_shape=jax.ShapeDtypeStruct((t, width), BF16),
        scratch_shapes=[pltpu.VMEM((n_heads, HG_DK, HG_DK), F32)],
        compiler_params=_cparams(("arbitrary", "arbitrary")),
        name="hgrn2_scan",
    )(proj, proj, proj, proj, lb.reshape(1, width), hg_norm.reshape(1, width))


def _ssd_kernel(z_ref, xbc_ref, dt_ref, cw_ref, cb_ref, dtb_ref, a_ref, dsk_ref, nw_ref, o_ref,
                carry_ref, xpad_ref, st_ref):
    L = CHUNK
    width = z_ref.shape[1]
    n_heads = width // SSM_P
    gw = width // SSM_GROUPS
    heads_per_group = n_heads // SSM_GROUPS

    @pl.when(pl.program_id(1) == 0)
    def _():
        carry_ref[...] = jnp.zeros_like(carry_ref)
        st_ref[...] = jnp.zeros_like(st_ref)

    xraw = xbc_ref[...]
    xpad_ref[0:SUBLANES, :] = carry_ref[...]
    xpad_ref[SUBLANES:SUBLANES + L, :] = xraw
    carry_ref[...] = xraw[L - SUBLANES:L, :]
    acc = cb_ref[...] + jnp.zeros_like(xraw)
    for j in range(SSM_CONV):
        acc = acc + cw_ref[j:j + 1, :] * xpad_ref[pl.ds(SUBLANES - (SSM_CONV - 1) + j, L), :]
    xc = _silu(acc)
    xs = xc[:, :width]
    bm = xc[:, width:width + SSM_GROUPS * SSM_N]
    cm = xc[:, width + SSM_GROUPS * SSM_N:]

    dt = _softplus(dt_ref[...] + dtb_ref[...])
    da = dt * a_ref[...]
    da_t = da.T
    dt_t = dt.T
    t_i = _iota((L, L), 0)
    s_i = _iota((L, L), 1)
    triu = (t_i <= s_i).astype(BF16)
    acs_t = _dot_rsel(da_t, triu)
    causal = t_i >= s_i
    diag = t_i == s_i
    lane_lo = _iota((L, 2 * SSM_P), 1) < SSM_P
    bd_mask = (_iota((2 * L, 2 * SSM_P), 0) < L) == (_iota((2 * L, 2 * SSM_P), 1) < SSM_P)

    y_pairs = []
    for g in range(SSM_GROUPS):
        bg = bm[:, g * SSM_N:(g + 1) * SSM_N]
        cg = cm[:, g * SSM_N:(g + 1) * SSM_N]
        cb = _dot_nt(cg.astype(BF16), bg.astype(BF16))
        hg = st_ref[:, g * gw:(g + 1) * gw]
        yoff_g = _dot(cg.astype(BF16), hg.astype(BF16))
        xsc_parts, decay_parts = [], []
        for pr in range(heads_per_group // 2):
            j0 = g * heads_per_group + 2 * pr
            gs, ds, ecol, elast = [], [], [], []
            for j in (j0, j0 + 1):
                row_b = jnp.broadcast_to(acs_t[j:j + 1, :], (L, L))
                col_b = row_b.T
                dt_row = jnp.broadcast_to(dt_t[j:j + 1, :], (L, L))
                lmat = jnp.exp(jnp.minimum(col_b - row_b, 0.0))
                gs.append(jnp.where(causal, cb * lmat * dt_row, 0.0))
                a_last = acs_t[j:j + 1, L - 1:L]
                ds.append(jnp.where(diag, jnp.exp(a_last - row_b) * dt_row, 0.0))
                ecol.append(jnp.exp(col_b))
                elast.append(jnp.exp(a_last))
            lhs = jnp.concatenate([jnp.concatenate(gs, axis=1), jnp.concatenate(ds, axis=1)], axis=0)
            xs_pair = xs[:, j0 * SSM_P:(j0 + 2) * SSM_P]
            bd = jnp.where(bd_mask, jnp.concatenate([xs_pair, xs_pair], axis=0), 0.0)
            res = _dot(lhs.astype(BF16), bd.astype(BF16))
            yoff = yoff_g[:, pr * 2 * SSM_P:(pr + 1) * 2 * SSM_P] * jnp.where(lane_lo, ecol[0], ecol[1])
            y_pairs.append(res[:L] + yoff)
            xsc_parts.append(res[L:])
            decay_parts.append(jnp.where(lane_lo[0:1], elast[0], elast[1]))
        xsc_g = jnp.concatenate(xsc_parts, axis=1)
        decay_g = jnp.concatenate(decay_parts, axis=1)
        st_ref[:, g * gw:(g + 1) * gw] = hg * decay_g + _dot_tn(bg.astype(BF16), xsc_g.astype(BF16))
    y = jnp.concatenate(y_pairs, axis=1) + dsk_ref[...] * xs
    yz = y * _silu(z_ref[...])
    for g in range(SSM_GROUPS):
        seg = yz[:, g * gw:(g + 1) * gw]
        ms = jnp.mean(seg * seg, axis=-1, keepdims=True)
        o_ref[:, g * gw:(g + 1) * gw] = (seg * lax.rsqrt(ms + NORM_EPS) * nw_ref[:, g * gw:(g + 1) * gw]).astype(o_ref.dtype)


def _ssd(z, xbc, dt, conv_w, conv_b, dt_bias, a_log, d_skip, ssm_norm, bsz, seq):
    t, width = z.shape
    xw = xbc.shape[1]
    n_heads = width // SSM_P
    ns = seq // CHUNK
    pad = lambda v: jnp.zeros((1, LANES), F32).at[0, :n_heads].set(v)
    row = lambda w: pl.BlockSpec((CHUNK, w), lambda b, s: (b * ns + s, 0))
    const = lambda r, w: pl.BlockSpec((r, w), lambda b, s: (0, 0))
    return pl.pallas_call(
        _ssd_kernel,
        grid=(bsz, ns),
        in_specs=[row(width), row(xw), row(LANES), const(SSM_CONV, xw), const(1, xw), const(1, LANES),
                  const(1, LANES), const(1, width), const(1, width)],
        out_specs=row(width),
        out_shape=jax.ShapeDtypeStruct((t, width), BF16),
        scratch_shapes=[pltpu.VMEM((SUBLANES, xw), F32), pltpu.VMEM((CHUNK + SUBLANES, xw), F32),
                        pltpu.VMEM((SSM_N, width), F32)],
        compiler_params=_cparams(("arbitrary", "arbitrary")),
        name="ssd_scan",
    )(z, xbc, dt, conv_w, conv_b.reshape(1, xw), pad(dt_bias), pad(-jnp.exp(a_log)),
      jnp.repeat(d_skip, SSM_P).reshape(1, width), ssm_norm.reshape(1, width))


def _route(probs):
    n_exp = N_GROUPS_MOE * 4
    p = [probs[e:e + 1, :] for e in range(n_exp)]
    gs = []
    for g in range(N_GROUPS_MOE):
        a, b, c, d = p[4 * g:4 * g + 4]
        gs.append(jnp.maximum(jnp.maximum(jnp.maximum(a + b, a + c), jnp.maximum(a + d, b + c)),
                              jnp.maximum(b + d, c + d)))
    best = jnp.zeros_like(gs[0]).astype(I32)
    bs = gs[0]
    for g in range(1, N_GROUPS_MOE):
        upd = gs[g] > bs
        best = jnp.where(upd, g, best)
        bs = jnp.where(upd, gs[g], bs)
    q = [jnp.where(best == 0, p[i], jnp.where(best == 1, p[4 + i], jnp.where(best == 2, p[8 + i], p[12 + i])))
         for i in range(4)]
    i1 = jnp.zeros_like(best)
    v1 = q[0]
    for i in range(1, 4):
        upd = q[i] > v1
        i1 = jnp.where(upd, i, i1)
        v1 = jnp.where(upd, q[i], v1)
    i2 = jnp.zeros_like(best)
    v2 = jnp.full_like(v1, -1.0)
    for i in range(4):
        upd = (i1 != i) & (q[i] > v2)
        i2 = jnp.where(upd, i, i2)
        v2 = jnp.where(upd, q[i], v2)
    den = v1 + v2
    return best * 4 + i1, best * 4 + i2, v1 / den, v2 / den


def _head_stats_expand(stack, hs_ref, he_ref):
    return _dot_rsel(_dot_rsel(stack, hs_ref[...]), he_ref[...])


def _epilogue_kernel(*refs, n_in, rwkv):
    ins = refs[:n_in]
    pos = n_in
    w_refs = refs[pos:pos + n_in]
    pos += n_in
    if rwkv:
        g_ref, bonus_ref, lnw_ref, lnb_ref, hs_ref, he_ref = refs[pos:pos + 6]
        pos += 6
    x_ref, gate_ref, nw_ref, sh_ref, sc_ref, rw_ref, rb_ref = refs[pos:pos + 7]
    pos += 7
    xo_ref, h_ref, ti_ref, tw_ref = refs[pos:pos + 4]

    if rwkv:
        o = ins[0][...]
        tm = o.shape[0]
        inv_n = 1.0 / RW_N
        st = _head_stats_expand(jnp.concatenate([o, o * o], axis=0), hs_ref, he_ref) * inv_n
        mean = st[:tm]
        var = jnp.maximum(st[tm:] - mean * mean, 0.0)
        o = (o - mean) * lax.rsqrt(var + RW_GN_EPS) * lnw_ref[...] + lnb_ref[...]
        o = (o + bonus_ref[...].astype(F32)) * g_ref[...].astype(F32)
        y = _dot(o.astype(BF16), w_refs[0][...])
    else:
        y = _dot(ins[0][...], w_refs[0][...])
        for a_ref, w_ref in zip(ins[1:], w_refs[1:]):
            y = y + _dot(a_ref[...], w_ref[...])
    x_new = x_ref[...] + gate_ref[0] * y
    xo_ref[...] = x_new
    h = _rms_mod(x_new, nw_ref[...], sh_ref[0], sc_ref[0])
    h_ref[...] = h
    h_hi, h_lo = _split(h)
    rw = rw_ref[...]
    rw_hi, rw_lo = _split(rw)
    logits = _dot_nt(rw_hi, h_hi) + _dot_nt(rw_hi, h_lo) + _dot_nt(rw_lo, h_hi) + rb_ref[...]
    mx = jnp.max(logits, axis=0, keepdims=True)
    ex = jnp.exp(logits - mx)
    probs = ex / jnp.sum(ex, axis=0, keepdims=True)
    e1, e2, w1, w2 = _route(probs)
    tmn = e1.shape[1]
    zi = jnp.zeros((SUBLANES - TOP_K, tmn), I32)
    ti_ref[...] = jnp.concatenate([e1, e2, zi], axis=0)
    tw_ref[...] = jnp.concatenate([w1, w2, zi.astype(F32)], axis=0)


def _epilogue(ins, ws, x2, gate, nw, shift, scale, router_w, router_b, bsz, seq, rwkv_extra=None):
    t, d = x2.shape
    tm = ROW_TM
    ns = seq // tm
    n_exp = router_w.shape[1]
    rw_t = jnp.zeros((LANES, d), F32).at[:n_exp].set(router_w.T)
    rb = jnp.full((LANES, 1), -1e30, F32).at[:n_exp, 0].set(router_b)
    row = lambda w: pl.BlockSpec((tm, w), lambda b, s: (b * ns + s, 0))
    const = lambda a: pl.BlockSpec(a.shape, lambda b, s: (0,) * a.ndim)
    per_b = pl.BlockSpec((1, 1, d), lambda b, s: (b, 0, 0))
    args = list(ins) + list(ws)
    specs = [row(a.shape[1]) for a in ins] + [const(w) for w in ws]
    if rwkv_extra is not None:
        g, bonus, lnw, lnb, hs, he = rwkv_extra
        args += [g, bonus, lnw.reshape(1, d), lnb.reshape(1, d), hs, he]
        specs += [row(d), row(d), pl.BlockSpec((1, d), lambda b, s: (0, 0)), pl.BlockSpec((1, d), lambda b, s: (0, 0)),
                  const(hs), const(he)]
    args += [x2, gate.reshape(bsz, 1, d), nw.reshape(1, d), shift.reshape(bsz, 1, d), scale.reshape(bsz, 1, d), rw_t, rb]
    specs += [row(d), per_b, pl.BlockSpec((1, d), lambda b, s: (0, 0)), per_b, per_b, const(rw_t), const(rb)]
    lane_row = pl.BlockSpec((SUBLANES, tm), lambda b, s: (0, b * ns + s))
    return pl.pallas_call(
        functools.partial(_epilogue_kernel, n_in=len(ins), rwkv=rwkv_extra is not None),
        grid=(bsz, ns),
        in_specs=specs,
        out_specs=[row(d), row(d), lane_row, lane_row],
        out_shape=[jax.ShapeDtypeStruct((t, d), F32), jax.ShapeDtypeStruct((t, d), F32),
                   jax.ShapeDtypeStruct((SUBLANES, t), I32), jax.ShapeDtypeStruct((SUBLANES, t), F32)],
        compiler_params=_cparams(("arbitrary", "arbitrary")),
        name="mixer_epilogue",
    )(*args)


def _route_plan(top_i, n_exp, tm):
    t = top_i.shape[1]
    n_pairs = TOP_K * t
    n_tiles = n_pairs // tm + n_exp
    e_flat = top_i[:TOP_K].reshape(-1)
    onehot = (e_flat[:, None] == jnp.arange(n_exp, dtype=I32)[None, :]).astype(I32)
    csum = jnp.cumsum(onehot, axis=0)
    counts = csum[-1]
    rank = jnp.sum((csum - 1) * onehot, axis=1)
    padded = ((counts + tm - 1) // tm) * tm
    ends = jnp.cumsum(padded)
    pos = (ends - padded)[e_flat] + rank
    row_tok = jnp.zeros((n_tiles * tm,), I32).at[pos].set(jnp.arange(n_pairs, dtype=I32) % t)
    tile_start = jnp.arange(n_tiles, dtype=I32) * tm
    tile_exp = jnp.minimum(jnp.sum((tile_start[:, None] >= ends[None, :]).astype(I32), axis=1), n_exp - 1)
    n_used = (ends[-1] // tm).reshape(1).astype(I32)
    return pos.astype(I32), row_tok, tile_exp.astype(I32), n_used


def _moe_kernel(texp_ref, nused_ref, rtok_ref, h_hbm, w1_ref, w3_ref, w2_ref, y_ref, xbuf, sem):
    tm = xbuf.shape[1]
    i = pl.program_id(0)
    n_used = nused_ref[0]
    slot = i % 2

    def start_gather(tile, sl):
        def body(r, c):
            tok = rtok_ref[tile * tm + r]
            pltpu.make_async_copy(h_hbm.at[pl.ds(tok, 1)], xbuf.at[sl, pl.ds(r, 1)], sem.at[sl]).start()
            return c
        lax.fori_loop(0, tm, body, 0)

    @pl.when(i == 0)
    def _():
        start_gather(0, 0)

    @pl.when(i + 1 < n_used)
    def _():
        start_gather(i + 1, 1 - slot)

    @pl.when(i < n_used)
    def _():
        pltpu.make_async_copy(h_hbm.at[pl.ds(0, tm)], xbuf.at[slot], sem.at[slot]).wait()
        x = xbuf[slot].astype(BF16)
        a = _dot(x, w1_ref[0])
        b = _dot(x, w3_ref[0])
        hid = (_silu(a) * b).astype(BF16)
        y_ref[...] = _dot(hid, w2_ref[0])

    @pl.when(i >= n_used)
    def _():
        y_ref[...] = jnp.zeros_like(y_ref)


def _moe(h, w1, w3, w2, row_tok, tile_exp, n_used):
    t, d = h.shape
    n_exp, _, dff = w1.shape
    tm = MOE_TM
    n_tiles = tile_exp.shape[0]
    grid_spec = pltpu.PrefetchScalarGridSpec(
        num_scalar_prefetch=3,
        grid=(n_tiles,),
        in_specs=[pl.BlockSpec(memory_space=pl.ANY),
                  pl.BlockSpec((1, d, dff), lambda i, te, nu, rt: (te[i], 0, 0)),
                  pl.BlockSpec((1, d, dff), lambda i, te, nu, rt: (te[i], 0, 0)),
                  pl.BlockSpec((1, dff, d), lambda i, te, nu, rt: (te[i], 0, 0))],
        out_specs=pl.BlockSpec((tm, d), lambda i, te, nu, rt: (i, 0)),
        scratch_shapes=[pltpu.VMEM((2, tm, d), F32), pltpu.SemaphoreType.DMA((2,))],
    )
    return pl.pallas_call(
        _moe_kernel,
        grid_spec=grid_spec,
        out_shape=jax.ShapeDtypeStruct((n_tiles * tm, d), F32),
        compiler_params=_cparams(("arbitrary",)),
        name="moe_experts",
    )(tile_exp, n_used, row_tok, h, w1, w3, w2)


def _combine_kernel(pos_ref, y_hbm, x_ref, tw_ref, gate_ref, nw_ref, sh_ref, sc_ref, *out_and_scratch, final, n_tok):
    if final:
        o_ref, ybuf, sem = out_and_scratch
    else:
        xo_ref, h_ref, ybuf, sem = out_and_scratch
    tm = ybuf.shape[2]
    i = pl.program_id(0)
    n_steps = pl.num_programs(0)
    slot = i % 2

    def start_gather(tile, sl):
        def body(r, c):
            for k in range(TOP_K):
                p = pos_ref[k * n_tok + tile * tm + r]
                pltpu.make_async_copy(y_hbm.at[pl.ds(p, 1)], ybuf.at[sl, k, pl.ds(r, 1)], sem.at[sl]).start()
            return c
        lax.fori_loop(0, tm, body, 0)

    @pl.when(i == 0)
    def _():
        start_gather(0, 0)

    @pl.when(i + 1 < n_steps)
    def _():
        start_gather(i + 1, 1 - slot)

    for k in range(TOP_K):
        pltpu.make_async_copy(y_hbm.at[pl.ds(0, tm)], ybuf.at[slot, k], sem.at[slot]).wait()
    tw = tw_ref[...]
    moe = tw[:, 0:1] * ybuf[slot, 0] + tw[:, 1:2] * ybuf[slot, 1]
    x_new = x_ref[...] + gate_ref[0] * moe
    if final:
        o_ref[...] = x_new * lax.rsqrt(jnp.mean(x_new * x_new, axis=-1, keepdims=True) + NORM_EPS) * nw_ref[...]
    else:
        xo_ref[...] = x_new
        h_ref[...] = _rms_mod(x_new, nw_ref[...], sh_ref[0], sc_ref[0]).astype(h_ref.dtype)


def _combine(pos, y_sorted, x2, top_w, gate, nw, shift, scale, bsz, seq, final):
    t, d = x2.shape
    tm = ROW_TM
    ns = seq // tm
    tw = top_w.T
    row = pl.BlockSpec((tm, d), lambda i, p: (i, 0))
    per_b = pl.BlockSpec((1, 1, d), lambda i, p: (i // ns, 0, 0))
    grid_spec = pltpu.PrefetchScalarGridSpec(
        num_scalar_prefetch=1,
        grid=(t // tm,),
        in_specs=[pl.BlockSpec(memory_space=pl.ANY), row,
                  pl.BlockSpec((tm, SUBLANES), lambda i, p: (i, 0)),
                  per_b, pl.BlockSpec((1, d), lambda i, p: (0, 0)), per_b, per_b],
        out_specs=row if final else [row, row],
        scratch_shapes=[pltpu.VMEM((2, TOP_K, tm, d), F32), pltpu.SemaphoreType.DMA((2,))],
    )
    out_shape = (jax.ShapeDtypeStruct((t, d), F32) if final else
                 [jax.ShapeDtypeStruct((t, d), F32), jax.ShapeDtypeStruct((t, d), BF16)])
    return pl.pallas_call(
        functools.partial(_combine_kernel, final=final, n_tok=t),
        grid_spec=grid_spec,
        out_shape=out_shape,
        compiler_params=_cparams(("arbitrary",)),
        name="moe_combine",
    )(pos, y_sorted, x2, tw, gate.reshape(bsz, 1, d), nw.reshape(1, d), shift.reshape(bsz, 1, d), scale.reshape(bsz, 1, d))


def _rwkv_prep_kernel(h_ref, mu_ref, wr_ref, wk_ref, wv_ref, dec0_ref, dec1_ref, dec2_ref, a0_ref, a1_ref, a2_ref,
                      g1_ref, g2_ref, kk_ref, ka_ref, rk_ref, hs_ref, he_ref,
                      r_out, lw_out, k_out, v_out, kkn_out, a_out, g_out, bonus_out, carry_ref, hp_ref):
    tm = h_ref.shape[0]

    @pl.when(pl.program_id(1) == 0)
    def _():
        carry_ref[...] = jnp.zeros_like(carry_ref)

    h = h_ref[...].astype(F32)
    hp_ref[0:SUBLANES, :] = carry_ref[...]
    hp_ref[SUBLANES:SUBLANES + tm, :] = h
    carry_ref[...] = h[tm - SUBLANES:tm, :]
    xx = hp_ref[pl.ds(SUBLANES - 1, tm), :] - h
    mix = lambda i: (h + xx * mu_ref[i:i + 1, :]).astype(BF16)
    r = _dot(mix(0), wr_ref[...])
    k = _dot(mix(1), wk_ref[...])
    v = _dot(mix(2), wv_ref[...])
    wl = dec0_ref[...] + _dot(jnp.tanh(_dot(mix(3), dec1_ref[...])).astype(BF16), dec2_ref[...])
    lw = -jnp.exp(-_softplus(-wl) - 0.5)
    a = _sigmoid(a0_ref[...] + _dot(_dot(mix(4), a1_ref[...]).astype(BF16), a2_ref[...]))
    g = _dot(_sigmoid(_dot(mix(5), g1_ref[...])).astype(BF16), g2_ref[...])
    kk = k * kk_ref[...]
    k_h = k * (1.0 + (a - 1.0) * ka_ref[...])
    st = _head_stats_expand(jnp.concatenate([kk * kk, r * k_h * rk_ref[...]], axis=0), hs_ref, he_ref)
    kkn = kk / jnp.maximum(jnp.sqrt(st[:tm]), 1e-12)
    r_out[...] = r.astype(r_out.dtype)
    lw_out[...] = lw
    k_out[...] = k_h.astype(k_out.dtype)
    v_out[...] = v.astype(v_out.dtype)
    kkn_out[...] = kkn.astype(kkn_out.dtype)
    a_out[...] = a.astype(a_out.dtype)
    g_out[...] = g.astype(g_out.dtype)
    bonus_out[...] = (st[tm:] * v).astype(bonus_out.dtype)


def _rwkv_prep(h, mu, w_rkv, dec0, dec1, dec2, a0, a1, a2, g1, g2, k_k, k_a, r_k, hs, he, bsz, seq):
    t, d = h.shape
    tm = ROW_TM
    ns = seq // tm
    padc = lambda w: jnp.zeros((d, LANES), F32).at[:, :w.shape[1]].set(w).astype(BF16)
    padr = lambda w: jnp.zeros((LANES, d), F32).at[:w.shape[0]].set(w).astype(BF16)
    vec = lambda v: v.reshape(1, d)
    args = [h, mu, w_rkv[0].astype(BF16), w_rkv[1].astype(BF16), w_rkv[2].astype(BF16), vec(dec0), padc(dec1), padr(dec2),
            vec(a0), padc(a1), padr(a2), padc(g1), padr(g2), vec(k_k), vec(k_a), vec(r_k), hs, he]
    row = pl.BlockSpec((tm, d), lambda b, s: (b * ns + s, 0))
    const = lambda a: pl.BlockSpec(a.shape, lambda b, s: (0,) * a.ndim)
    outs = [BF16, F32, BF16, BF16, BF16, BF16, BF16, BF16]
    return pl.pallas_call(
        _rwkv_prep_kernel,
        grid=(bsz, ns),
        in_specs=[row] + [const(a) for a in args[1:]],
        out_specs=[row] * len(outs),
        out_shape=[jax.ShapeDtypeStruct((t, d), dt) for dt in outs],
        scratch_shapes=[pltpu.VMEM((SUBLANES, d), F32), pltpu.VMEM((tm + SUBLANES, d), F32)],
        compiler_params=_cparams(("arbitrary", "arbitrary")),
        name="rwkv_prep",
    )(*args)


def _rwkv_scan_kernel(r_ref, lw_ref, k_ref, v_ref, kk_ref, a_ref, y_ref, st_ref):
    L = RW_CHUNK
    pw = RW_PACK * RW_N
    n_packs = st_ref.shape[0]
    sh = RW_N.bit_length() - 1

    @pl.when(pl.program_id(1) == 0)
    def _():
        st_ref[...] = jnp.zeros_like(st_ref)

    t_i = _iota((L, L), 0)
    s_i = _iota((L, L), 1)
    tril = (t_i >= s_i).astype(BF16)
    wc_all = _dot_sel(tril, lw_ref[...])
    lane_head = _iota((L, pw), 1) >> sh
    s_loc = _iota((L, pw), 1) & (RW_N - 1)
    t_loc = _iota((L, pw), 0)
    strict = s_loc < t_loc
    incl = s_loc <= t_loc
    bd_mask = (_iota((pw, pw), 0) >> sh) == (_iota((pw, pw), 1) >> sh)

    def bdiag(x):
        return jnp.where(bd_mask, jnp.concatenate([x] * RW_PACK, axis=0), 0.0).astype(BF16)

    for p in range(n_packs):
        sl = slice(p * pw, (p + 1) * pw)
        r = r_ref[:, sl].astype(F32)
        lw = lw_ref[:, sl]
        k = k_ref[:, sl].astype(F32)
        v = v_ref[:, sl].astype(F32)
        kk = kk_ref[:, sl].astype(F32)
        a = a_ref[:, sl].astype(F32)
        wc = wc_all[:, sl]
        w_last = wc[L - 1:L, :]
        e_inv = jnp.exp(-wc)
        e_last = jnp.exp(w_last - wc)
        kka = kk * a
        al = -kk * jnp.exp(wc - lw)
        rb = r * jnp.exp(wc)
        bt = kka * e_inv
        kt = k * e_inv
        lhs = jnp.concatenate([al, rb], axis=0).astype(BF16)
        rows = [jnp.where(lane_head == hh, x, 0.0) for x in (bt, kt) for hh in range(RW_PACK)]
        st = st_ref[p]
        m = jnp.concatenate(rows + [st], axis=0).astype(BF16)
        pr = _dot_nt(lhs, m)
        a_ab = jnp.where(strict, pr[:L, 0:pw], 0.0)
        a_ak = jnp.where(strict, pr[:L, pw:2 * pw], 0.0)
        a_rb = jnp.where(incl, pr[L:, 0:pw], 0.0)
        a_rk = jnp.where(incl, pr[L:, pw:2 * pw], 0.0)
        bd_v = bdiag(v)
        u = pr[:L, 2 * pw:] + _dot(a_ak.astype(BF16), bd_v)
        nmat = a_ab
        n_steps = L.bit_length() - 1
        for it in range(n_steps):
            u = u + _dot(nmat.astype(BF16), bdiag(u))
            if it + 1 < n_steps:
                nmat = _dot(nmat.astype(BF16), bdiag(nmat))
        y = pr[L:, 2 * pw:] + _dot(jnp.concatenate([a_rb, a_rk], axis=1).astype(BF16),
                                   jnp.concatenate([bdiag(u), bd_v], axis=0))
        y_ref[:, sl] = y
        upd = _dot_tn(jnp.concatenate([u, v], axis=0).astype(BF16),
                      jnp.concatenate([kka * e_last, k * e_last], axis=0).astype(BF16))
        st_ref[p] = jnp.where(bd_mask, st * jnp.exp(w_last) + upd, 0.0)


def _rwkv_scan(r, lw, k, v, kk, a, bsz, seq):
    t, d = r.shape
    ns = seq // RW_CHUNK
    pw = RW_PACK * RW_N
    row = pl.BlockSpec((RW_CHUNK, d), lambda b, s: (b * ns + s, 0))
    return pl.pallas_call(
        _rwkv_scan_kernel,
        grid=(bsz, ns),
        in_specs=[row] * 6,
        out_specs=row,
        out_shape=jax.ShapeDtypeStruct((t, d), F32),
        scratch_shapes=[pltpu.VMEM((d // pw, pw, pw), F32)],
        compiler_params=_cparams(("arbitrary", "arbitrary")),
        name="rwkv_scan",
    )(r, lw, k, v, kk, a)


def _moe_block(h, top_i, top_w, x2, gate, w1, w3, w2, nw, shift, scale, bsz, seq, final):
    n_exp = w1.shape[0]
    pos, row_tok, tile_exp, n_used = _route_plan(top_i, n_exp, MOE_TM)
    y_sorted = _moe(h, w1.astype(BF16), w3.astype(BF16), w2.astype(BF16), row_tok, tile_exp, n_used)
    return _combine(pos, y_sorted, x2, top_w, gate, nw, shift, scale, bsz, seq, final)


def kernel(x, c, mod_w, mod_b, norm_w, hg_lb_logits, ev_w_in, ev_hg_norm, ev_conv_w, ev_conv_b, ev_dt_bias, ev_a_log, ev_d_skip, ev_ssm_norm, ev_w_out, od_mu, od_w_rkv, od_w_dec0, od_w_dec1, od_w_dec2, od_a0, od_a1, od_a2, od_g1, od_g2, od_k_k, od_k_a, od_r_k, od_ln_w, od_ln_b, od_w_o, router_w, router_b, moe_w1, moe_w3, moe_w2, final_norm_w):
    bsz, seq, d = x.shape
    depth = mod_w.shape[0]
    t = bsz * seq
    x2 = x.reshape(t, d)
    mod = _adaln_mod(c, mod_w, mod_b)
    gamma = jax.nn.softmax(hg_lb_logits.astype(F32), axis=0)
    lower_bounds = jnp.cumsum(gamma, axis=0) - gamma[0]
    n_rw_heads = d // RW_N
    head_of_lane = jnp.arange(d, dtype=I32) // RW_N
    hs = (head_of_lane[:, None] == jnp.arange(LANES, dtype=I32)[None, :]).astype(BF16)
    he = hs.T

    h = None
    out = None
    for l in range(depth):
        sh_m, sc_m, gt_m, sh_f, sc_f, gt_f = [mod[l, :, i * d:(i + 1) * d] for i in range(6)]
        j = l // 2
        if h is None:
            h = _norm_mod(x2, norm_w[l, 0], sh_m, sc_m, bsz, seq)
        if l % 2 == 0:
            w_in = ev_w_in[j]
            hgw = ev_hg_norm.shape[1]
            sw = ev_ssm_norm.shape[1]
            xbw = ev_conv_w.shape[2]
            nh = ev_dt_bias.shape[1]
            c0 = 4 * hgw
            proj_hg = _matmul(h, w_in[:, :c0].astype(BF16), F32, 512, 512)
            z = _matmul(h, w_in[:, c0:c0 + sw].astype(BF16), F32, 512, 512)
            xbc = _matmul(h, w_in[:, c0 + sw:c0 + sw + xbw].astype(BF16), F32, 512, 512)
            w_dt = jnp.zeros((d, LANES), F32).at[:, :nh].set(w_in[:, c0 + sw + xbw:]).astype(BF16)
            dt = _matmul(h, w_dt, F32, 512, LANES)
            o_a = _hgrn2(proj_hg, lower_bounds[l + 1], ev_hg_norm[j], bsz, seq)
            o_b = _ssd(z, xbc, dt, ev_conv_w[j], ev_conv_b[j], ev_dt_bias[j], ev_a_log[j], ev_d_skip[j],
                       ev_ssm_norm[j], bsz, seq)
            w_out = ev_w_out[j].astype(BF16)
            x2, hf, top_i, top_w = _epilogue([o_a, o_b], [w_out[:hgw], w_out[hgw:]], x2, gt_m, norm_w[l, 1], sh_f, sc_f,
                                             router_w, router_b, bsz, seq)
        else:
            r, lw, k, v, kk, a, g, bonus = _rwkv_prep(h, od_mu[j], od_w_rkv[j], od_w_dec0[j], od_w_dec1[j], od_w_dec2[j],
                                                      od_a0[j], od_a1[j], od_a2[j], od_g1[j], od_g2[j], od_k_k[j],
                                                      od_k_a[j], od_r_k[j].reshape(-1), hs, he, bsz, seq)
            y = _rwkv_scan(r, lw, k, v, kk, a, bsz, seq)
            x2, hf, top_i, top_w = _epilogue([y], [od_w_o[j].astype(BF16)], x2, gt_m, norm_w[l, 1], sh_f, sc_f,
                                             router_w, router_b, bsz, seq,
                                             rwkv_extra=(g, bonus, od_ln_w[j], od_ln_b[j], hs, he))
        final = l == depth - 1
        if final:
            nw_next, sh_next, sc_next = final_norm_w, sh_f, sc_f
        else:
            nxt = [mod[l + 1, :, i * d:(i + 1) * d] for i in range(2)]
            nw_next, sh_next, sc_next = norm_w[l + 1, 0], nxt[0], nxt[1]
        res = _moe_block(hf, top_i, top_w, x2, gt_f, moe_w1[l], moe_w3[l], moe_w2[l], nw_next, sh_next, sc_next,
                         bsz, seq, final)
        if final:
            out = res
        else:
            x2, h = res
    return out.reshape(bsz, seq, d)
```

```python
import functools

import jax
import jax.numpy as jnp
from jax import lax
from jax.experimental import pallas as pl
from jax.experimental.pallas import tpu as pltpu

F32 = jnp.float32
BF16 = jnp.bfloat16
I32 = jnp.int32

NORM_EPS = 1e-6
RW_GN_EPS = 64e-5
LANES = 128
SUBLANES = 8
VMEM_LIMIT = 56 * 1024 * 1024

HG_DK = 128
SSM_P = 64
SSM_N = 128
SSM_GROUPS = 2
SSM_CONV = 4
RW_N = 64
N_GROUPS_MOE = 4
TOP_K = 2

CHUNK = 128
RW_CHUNK = 64
RW_PACK = 4
MOE_TM = 256
ROW_TM = 256
GATHER_UNROLL = 8


def _cparams(sem):
    return pltpu.CompilerParams(dimension_semantics=sem, vmem_limit_bytes=VMEM_LIMIT)


def _dot(a, b):
    return lax.dot_general(a, b, (((1,), (0,)), ((), ())), preferred_element_type=F32)


def _dot_nt(a, b):
    return lax.dot_general(a, b, (((1,), (1,)), ((), ())), preferred_element_type=F32)


def _dot_tn(a, b):
    return lax.dot_general(a, b, (((0,), (0,)), ((), ())), preferred_element_type=F32)


def _split(x):
    hi = x.astype(BF16)
    return hi, (x - hi.astype(F32)).astype(BF16)


def _dot_sel(sel, x):
    hi, lo = _split(x)
    return _dot(sel, hi) + _dot(sel, lo)


def _dot_rsel(x, sel):
    hi, lo = _split(x)
    return _dot(hi, sel) + _dot(lo, sel)


def _sigmoid(x):
    return 1.0 / (1.0 + jnp.exp(-x))


def _silu(x):
    return x * _sigmoid(x)


def _softplus(x):
    return jnp.maximum(x, 0.0) + jnp.log(1.0 + jnp.exp(-jnp.abs(x)))


def _iota(shape, dim):
    return lax.broadcasted_iota(I32, shape, dim)


def _store_slabs(ref, val):
    for j in range(val.shape[1] // LANES):
        ref[:, j, :] = val[:, j * LANES:(j + 1) * LANES].astype(ref.dtype)


def _load_slabs(ref, lead=()):
    n_slabs = ref.shape[-2]
    return jnp.concatenate([ref[lead + (slice(None), j, slice(None))] for j in range(n_slabs)], axis=1)


def _mod_kernel(c_ref, w_ref, b_ref, o_ref):
    c = c_ref[...]
    o_ref[0] = _dot(_silu(c).astype(BF16), w_ref[0].astype(BF16)) + b_ref[0]


def _adaln_mod(c, mod_w, mod_b):
    depth, d, width = mod_w.shape
    bsz = c.shape[0]
    c_pad = jnp.zeros((SUBLANES, d), F32).at[:bsz].set(c)
    tn = 1536
    out = pl.pallas_call(
        _mod_kernel,
        grid=(depth, width // tn),
        in_specs=[pl.BlockSpec((SUBLANES, d), lambda l, j: (0, 0)),
                  pl.BlockSpec((1, d, tn), lambda l, j: (l, 0, j)),
                  pl.BlockSpec((1, 1, tn), lambda l, j: (l, 0, j))],
        out_specs=pl.BlockSpec((1, SUBLANES, tn), lambda l, j: (l, 0, j)),
        out_shape=jax.ShapeDtypeStruct((depth, SUBLANES, width), F32),
        compiler_params=_cparams(("arbitrary", "arbitrary")),
        name="adaln_mod",
    )(c_pad, mod_w, mod_b.reshape(depth, 1, width))
    return out[:, :bsz]


def _rms_mod(x, nw, shift, scale):
    y = x * lax.rsqrt(jnp.mean(x * x, axis=-1, keepdims=True) + NORM_EPS) * nw
    return y * (1.0 + scale) + shift


def _normmod_kernel(x_ref, nw_ref, sh_ref, sc_ref, h_ref):
    h_ref[...] = _rms_mod(x_ref[...], nw_ref[...], sh_ref[0], sc_ref[0]).astype(h_ref.dtype)


def _norm_mod(x2, nw, shift, scale, bsz, seq):
    t, d = x2.shape
    tm = ROW_TM
    ns = seq // tm
    return pl.pallas_call(
        _normmod_kernel,
        grid=(bsz, ns),
        in_specs=[pl.BlockSpec((tm, d), lambda b, s: (b * ns + s, 0)),
                  pl.BlockSpec((1, d), lambda b, s: (0, 0)),
                  pl.BlockSpec((1, 1, d), lambda b, s: (b, 0, 0)),
                  pl.BlockSpec((1, 1, d), lambda b, s: (b, 0, 0))],
        out_specs=pl.BlockSpec((tm, d), lambda b, s: (b * ns + s, 0)),
        out_shape=jax.ShapeDtypeStruct((t, d), BF16),
        compiler_params=_cparams(("arbitrary", "arbitrary")),
        name="norm_mod",
    )(x2, nw.reshape(1, d), shift.reshape(bsz, 1, d), scale.reshape(bsz, 1, d))


def _mm_kernel(x_ref, w_ref, o_ref):
    o_ref[...] = _dot(x_ref[...], w_ref[...]).astype(o_ref.dtype)


def _matmul(x, w, out_dtype, tm, tn):
    t, k = x.shape
    n = w.shape[1]
    return pl.pallas_call(
        _mm_kernel,
        grid=(n // tn, t // tm),
        in_specs=[pl.BlockSpec((tm, k), lambda j, i: (i, 0)),
                  pl.BlockSpec((k, tn), lambda j, i: (0, j))],
        out_specs=pl.BlockSpec((tm, tn), lambda j, i: (i, j)),
        out_shape=jax.ShapeDtypeStruct((t, n), out_dtype),
        compiler_params=_cparams(("arbitrary", "arbitrary")),
        name="matmul",
    )(x, w)


def _hgrn2_kernel(q_ref, f_ref, i_ref, g_ref, lb_ref, nw_ref, o_ref, st_ref):
    L = CHUNK
    dk = HG_DK
    n_heads = st_ref.shape[0]

    @pl.when(pl.program_id(1) == 0)
    def _():
        st_ref[...] = jnp.zeros_like(st_ref)

    lb = lb_ref[...]
    f = lb + (1.0 - lb) * _sigmoid(f_ref[...])
    logf = jnp.log(f)
    t_i = _iota((L, L), 0)
    s_i = _iota((L, L), 1)
    tril = (t_i >= s_i).astype(BF16)
    b_all = _dot_sel(tril, logf)

    levels = []
    n2 = L
    while n2 >= 2 * SUBLANES:
        n = n2 // 2
        sh = n2.bit_length() - 1
        m = ((t_i >> sh) == (s_i >> sh)) & ((t_i & (n2 - 1)) >= n) & ((s_i & (n2 - 1)) < n)
        levels.append((n2, m))
        n2 = n
    nb = L // SUBLANES
    blk3 = _iota((nb, SUBLANES, L), 0)
    sub3 = _iota((nb, SUBLANES, L), 1)
    lane3 = _iota((nb, SUBLANES, L), 2)

    for h in range(n_heads):
        sl = slice(h * dk, (h + 1) * dk)
        q = q_ref[:, sl]
        k = 1.0 - f[:, sl]
        v = i_ref[:, sl]
        b = b_all[:, sl]
        st = st_ref[h]
        b_last = b[L - 1:L, :]
        o = _dot_nt((q * jnp.exp(b)).astype(BF16), st.astype(BF16))
        a = jnp.zeros((L, L), F32)
        for n2, m in levels:
            n = n2 // 2
            r = jnp.broadcast_to(b.reshape(L // n2, n2, dk)[:, n - 1:n, :], (L // n2, n2, dk)).reshape(L, dk)
            qe = q * jnp.exp(jnp.minimum(b - r, 0.0))
            ke = k * jnp.exp(jnp.minimum(r - b, 0.0))
            a = a + jnp.where(m, _dot_nt(qe.astype(BF16), ke.astype(BF16)), 0.0)
        b3 = b.reshape(nb, SUBLANES, dk)
        q3 = q.reshape(nb, SUBLANES, dk)
        k3 = k.reshape(nb, SUBLANES, dk)
        a8 = jnp.zeros((nb, SUBLANES, L), F32)
        for s in range(SUBLANES):
            x = q3 * k3[:, s:s + 1, :] * jnp.exp(jnp.minimum(b3 - b3[:, s:s + 1, :], 0.0))
            rs = jnp.sum(x, axis=-1, keepdims=True)
            a8 = jnp.where((lane3 == blk3 * SUBLANES + s) & (sub3 >= s), rs, a8)
        a = a + a8.reshape(L, L)
        o = o + _dot(a.astype(BF16), v.astype(BF16))
        ke = k * jnp.exp(b_last - b)
        st_ref[h] = st * jnp.exp(b_last) + _dot_tn(v.astype(BF16), ke.astype(BF16))
        g = g_ref[:, sl]
        ms = jnp.mean(o * o, axis=-1, keepdims=True)
        o_ref[:, sl] = (o * lax.rsqrt(ms + NORM_EPS) * nw_ref[:, sl] * _silu(g)).astype(o_ref.dtype)


def _hgrn2(proj, lb, hg_norm, bsz, seq):
    t = proj.shape[0]
    width = lb.shape[0]
    n_heads = width // HG_DK
    ns = seq // CHUNK
    blk = lambda j: pl.BlockSpec((CHUNK, width), lambda b, s, j=j: (b * ns + s, j))
    return pl.pallas_call(
        _hgrn2_kernel,
        grid=(bsz, ns),
        in_specs=[blk(0), blk(1), blk(2), blk(3),
                  pl.BlockSpec((1, width), lambda b, s: (0, 0)),
                  pl.BlockSpec((1, width), lambda b, s: (0, 0))],
        out_specs=pl.BlockSpec((CHUNK, width), lambda b, s: (b * ns + s, 0)),
        out_shape=jax.ShapeDtypeStruct((t, width), BF16),
        scratch_shapes=[pltpu.VMEM((n_heads, HG_DK, HG_DK), F32)],
        compiler_params=_cparams(("arbitrary", "arbitrary")),
        name="hgrn2_scan",
    )(proj, proj, proj, proj, lb.reshape(1, width), hg_norm.reshape(1, width))


def _ssd_kernel(z_ref, xbc_ref, dt_ref, cw_ref, cb_ref, dtb_ref, a_ref, dsk_ref, nw_ref, o_ref,
                carry_ref, xpad_ref, st_ref):
    L = CHUNK
    width = z_ref.shape[1]
    n_heads = width // SSM_P
    gw = width // SSM_GROUPS
    heads_per_group = n_heads // SSM_GROUPS

    @pl.when(pl.program_id(1) == 0)
    def _():
        carry_ref[...] = jnp.zeros_like(carry_ref)
        st_ref[...] = jnp.zeros_like(st_ref)

    xraw = xbc_ref[...]
    xpad_ref[0:SUBLANES, :] = carry_ref[...]
    xpad_ref[SUBLANES:SUBLANES + L, :] = xraw
    carry_ref[...] = xraw[L - SUBLANES:L, :]
    acc = cb_ref[...] + jnp.zeros_like(xraw)
    for j in range(SSM_CONV):
        acc = acc + cw_ref[j:j + 1, :] * xpad_ref[pl.ds(SUBLANES - (SSM_CONV - 1) + j, L), :]
    xc = _silu(acc)
    xs = xc[:, :width]
    bm = xc[:, width:width + SSM_GROUPS * SSM_N]
    cm = xc[:, width + SSM_GROUPS * SSM_N:]

    dt = _softplus(dt_ref[...] + dtb_ref[...])
    da = dt * a_ref[...]
    da_t = da.T
    dt_t = dt.T
    t_i = _iota((L, L), 0)
    s_i = _iota((L, L), 1)
    triu = (t_i <= s_i).astype(BF16)
    acs_t = _dot_rsel(da_t, triu)
    causal = t_i >= s_i
    diag = t_i == s_i
    lane_lo = _iota((L, 2 * SSM_P), 1) < SSM_P
    bd_mask = (_iota((2 * L, 2 * SSM_P), 0) < L) == (_iota((2 * L, 2 * SSM_P), 1) < SSM_P)

    y_pairs = []
    for g in range(SSM_GROUPS):
        bg = bm[:, g * SSM_N:(g + 1) * SSM_N]
        cg = cm[:, g * SSM_N:(g + 1) * SSM_N]
        cb = _dot_nt(cg.astype(BF16), bg.astype(BF16))
        hg = st_ref[:, g * gw:(g + 1) * gw]
        yoff_g = _dot(cg.astype(BF16), hg.astype(BF16))
        xsc_parts, decay_parts = [], []
        for pr in range(heads_per_group // 2):
            j0 = g * heads_per_group + 2 * pr
            gs, ds, ecol, elast = [], [], [], []
            for j in (j0, j0 + 1):
                row_b = jnp.broadcast_to(acs_t[j:j + 1, :], (L, L))
                col_b = row_b.T
                dt_row = jnp.broadcast_to(dt_t[j:j + 1, :], (L, L))
                lmat = jnp.exp(jnp.minimum(col_b - row_b, 0.0))
                gs.append(jnp.where(causal, cb * lmat * dt_row, 0.0))
                a_last = acs_t[j:j + 1, L - 1:L]
                ds.append(jnp.where(diag, jnp.exp(a_last - row_b) * dt_row, 0.0))
                ecol.append(jnp.exp(col_b))
                elast.append(jnp.exp(a_last))
            lhs = jnp.concatenate([jnp.concatenate(gs, axis=1), jnp.concatenate(ds, axis=1)], axis=0)
            xs_pair = xs[:, j0 * SSM_P:(j0 + 2) * SSM_P]
            bd = jnp.where(bd_mask, jnp.concatenate([xs_pair, xs_pair], axis=0), 0.0)
            res = _dot(lhs.astype(BF16), bd.astype(BF16))
            yoff = yoff_g[:, pr * 2 * SSM_P:(pr + 1) * 2 * SSM_P] * jnp.where(lane_lo, ecol[0], ecol[1])
            y_pairs.append(res[:L] + yoff)
            xsc_parts.append(res[L:])
            decay_parts.append(jnp.where(lane_lo[0:1], elast[0], elast[1]))
        xsc_g = jnp.concatenate(xsc_parts, axis=1)
        decay_g = jnp.concatenate(decay_parts, axis=1)
        st_ref[:, g * gw:(g + 1) * gw] = hg * decay_g + _dot_tn(bg.astype(BF16), xsc_g.astype(BF16))
    y = jnp.concatenate(y_pairs, axis=1) + dsk_ref[...] * xs
    yz = y * _silu(z_ref[...])
    for g in range(SSM_GROUPS):
        seg = yz[:, g * gw:(g + 1) * gw]
        ms = jnp.mean(seg * seg, axis=-1, keepdims=True)
        o_ref[:, g * gw:(g + 1) * gw] = (seg * lax.rsqrt(ms + NORM_EPS) * nw_ref[:, g * gw:(g + 1) * gw]).astype(o_ref.dtype)


def _ssd(z, xbc, dt, conv_w, conv_b, dt_bias, a_log, d_skip, ssm_norm, bsz, seq):
    t, width = z.shape
    xw = xbc.shape[1]
    n_heads = width // SSM_P
    ns = seq // CHUNK
    pad = lambda v: jnp.zeros((1, LANES), F32).at[0, :n_heads].set(v)
    row = lambda w: pl.BlockSpec((CHUNK, w), lambda b, s: (b * ns + s, 0))
    const = lambda r, w: pl.BlockSpec((r, w), lambda b, s: (0, 0))
    return pl.pallas_call(
        _ssd_kernel,
        grid=(bsz, ns),
        in_specs=[row(width), row(xw), row(LANES), const(SSM_CONV, xw), const(1, xw), const(1, LANES),
                  const(1, LANES), const(1, width), const(1, width)],
        out_specs=row(width),
        out_shape=jax.ShapeDtypeStruct((t, width), BF16),
        scratch_shapes=[pltpu.VMEM((SUBLANES, xw), F32), pltpu.VMEM((CHUNK + SUBLANES, xw), F32),
                        pltpu.VMEM((SSM_N, width), F32)],
        compiler_params=_cparams(("arbitrary", "arbitrary")),
        name="ssd_scan",
    )(z, xbc, dt, conv_w, conv_b.reshape(1, xw), pad(dt_bias), pad(-jnp.exp(a_log)),
      jnp.repeat(d_skip, SSM_P).reshape(1, width), ssm_norm.reshape(1, width))


def _route(probs):
    n_exp = N_GROUPS_MOE * 4
    p = [probs[e:e + 1, :] for e in range(n_exp)]
    gs = []
    for g in range(N_GROUPS_MOE):
        a, b, c, d = p[4 * g:4 * g + 4]
        gs.append(jnp.maximum(jnp.maximum(jnp.maximum(a + b, a + c), jnp.maximum(a + d, b + c)),
                              jnp.maximum(b + d, c + d)))
    best = jnp.zeros_like(gs[0]).astype(I32)
    bs = gs[0]
    for g in range(1, N_GROUPS_MOE):
        upd = gs[g] > bs
        best = jnp.where(upd, g, best)
        bs = jnp.where(upd, gs[g], bs)
    q = [jnp.where(best == 0, p[i], jnp.where(best == 1, p[4 + i], jnp.where(best == 2, p[8 + i], p[12 + i])))
         for i in range(4)]
    i1 = jnp.zeros_like(best)
    v1 = q[0]
    for i in range(1, 4):
        upd = q[i] > v1
        i1 = jnp.where(upd, i, i1)
        v1 = jnp.where(upd, q[i], v1)
    i2 = jnp.zeros_like(best)
    v2 = jnp.full_like(v1, -1.0)
    for i in range(4):
        upd = (i1 != i) & (q[i] > v2)
        i2 = jnp.where(upd, i, i2)
        v2 = jnp.where(upd, q[i], v2)
    den = v1 + v2
    return best * 4 + i1, best * 4 + i2, v1 / den, v2 / den


def _head_stats_expand(stack, hs_ref, he_ref):
    return _dot_rsel(_dot_rsel(stack, hs_ref[...]), he_ref[...])


def _epilogue_kernel(*refs, n_in, rwkv):
    ins = refs[:n_in]
    pos = n_in
    w_refs = refs[pos:pos + n_in]
    pos += n_in
    if rwkv:
        g_ref, bonus_ref, lnw_ref, lnb_ref, hs_ref, he_ref = refs[pos:pos + 6]
        pos += 6
    x_ref, gate_ref, nw_ref, sh_ref, sc_ref, rw_ref, rb_ref = refs[pos:pos + 7]
    pos += 7
    xo_ref, h_ref, ti_ref, tw_ref = refs[pos:pos + 4]

    if rwkv:
        o = ins[0][...]
        tm = o.shape[0]
        inv_n = 1.0 / RW_N
        st = _head_stats_expand(jnp.concatenate([o, o * o], axis=0), hs_ref, he_ref) * inv_n
        mean = st[:tm]
        var = jnp.maximum(st[tm:] - mean * mean, 0.0)
        o = (o - mean) * lax.rsqrt(var + RW_GN_EPS) * lnw_ref[...] + lnb_ref[...]
        o = (o + bonus_ref[...].astype(F32)) * g_ref[...].astype(F32)
        y = _dot(o.astype(BF16), w_refs[0][...])
    else:
        y = _dot(ins[0][...], w_refs[0][...])
        for a_ref, w_ref in zip(ins[1:], w_refs[1:]):
            y = y + _dot(a_ref[...], w_ref[...])
    x_new = x_ref[...] + gate_ref[0] * y
    xo_ref[...] = x_new
    h = _rms_mod(x_new, nw_ref[...], sh_ref[0], sc_ref[0])
    _store_slabs(h_ref, h)
    h_hi, h_lo = _split(h)
    rw = rw_ref[...]
    rw_hi, rw_lo = _split(rw)
    logits = _dot_nt(rw_hi, h_hi) + _dot_nt(rw_hi, h_lo) + _dot_nt(rw_lo, h_hi) + rb_ref[...]
    mx = jnp.max(logits, axis=0, keepdims=True)
    ex = jnp.exp(logits - mx)
    probs = ex / jnp.sum(ex, axis=0, keepdims=True)
    e1, e2, w1, w2 = _route(probs)
    tmn = e1.shape[1]
    zi = jnp.zeros((SUBLANES - TOP_K, tmn), I32)
    ti_ref[...] = jnp.concatenate([e1, e2, zi], axis=0)
    tw_ref[...] = jnp.concatenate([w1, w2, zi.astype(F32)], axis=0)


def _epilogue(ins, ws, x2, gate, nw, shift, scale, router_w, router_b, bsz, seq, rwkv_extra=None):
    t, d = x2.shape
    tm = ROW_TM
    ns = seq // tm
    n_exp = router_w.shape[1]
    rw_t = jnp.zeros((LANES, d), F32).at[:n_exp].set(router_w.T)
    rb = jnp.full((LANES, 1), -1e30, F32).at[:n_exp, 0].set(router_b)
    row = lambda w: pl.BlockSpec((tm, w), lambda b, s: (b * ns + s, 0))
    const = lambda a: pl.BlockSpec(a.shape, lambda b, s: (0,) * a.ndim)
    per_b = pl.BlockSpec((1, 1, d), lambda b, s: (b, 0, 0))
    args = list(ins) + list(ws)
    specs = [row(a.shape[1]) for a in ins] + [const(w) for w in ws]
    if rwkv_extra is not None:
        g, bonus, lnw, lnb, hs, he = rwkv_extra
        args += [g, bonus, lnw.reshape(1, d), lnb.reshape(1, d), hs, he]
        specs += [row(d), row(d), pl.BlockSpec((1, d), lambda b, s: (0, 0)), pl.BlockSpec((1, d), lambda b, s: (0, 0)),
                  const(hs), const(he)]
    args += [x2, gate.reshape(bsz, 1, d), nw.reshape(1, d), shift.reshape(bsz, 1, d), scale.reshape(bsz, 1, d), rw_t, rb]
    specs += [row(d), per_b, pl.BlockSpec((1, d), lambda b, s: (0, 0)), per_b, per_b, const(rw_t), const(rb)]
    lane_row = pl.BlockSpec((SUBLANES, tm), lambda b, s: (0, b * ns + s))
    return pl.pallas_call(
        functools.partial(_epilogue_kernel, n_in=len(ins), rwkv=rwkv_extra is not None),
        grid=(bsz, ns),
        in_specs=specs,
        out_specs=[row(d), pl.BlockSpec((tm, d // LANES, LANES), lambda b, s: (b * ns + s, 0, 0)), lane_row, lane_row],
        out_shape=[jax.ShapeDtypeStruct((t, d), F32), jax.ShapeDtypeStruct((t, d // LANES, LANES), F32),
                   jax.ShapeDtypeStruct((SUBLANES, t), I32), jax.ShapeDtypeStruct((SUBLANES, t), F32)],
        compiler_params=_cparams(("arbitrary", "arbitrary")),
        name="mixer_epilogue",
    )(*args)


def _route_plan(top_i, n_exp, tm):
    t = top_i.shape[1]
    n_pairs = TOP_K * t
    n_tiles = n_pairs // tm + n_exp
    e_flat = top_i[:TOP_K].reshape(-1)
    onehot = (e_flat[:, None] == jnp.arange(n_exp, dtype=I32)[None, :]).astype(I32)
    csum = jnp.cumsum(onehot, axis=0)
    counts = csum[-1]
    rank = jnp.sum((csum - 1) * onehot, axis=1)
    padded = ((counts + tm - 1) // tm) * tm
    ends = jnp.cumsum(padded)
    pos = (ends - padded)[e_flat] + rank
    row_tok = jnp.zeros((n_tiles * tm,), I32).at[pos].set(jnp.arange(n_pairs, dtype=I32) % t)
    tile_start = jnp.arange(n_tiles, dtype=I32) * tm
    tile_exp = jnp.minimum(jnp.sum((tile_start[:, None] >= ends[None, :]).astype(I32), axis=1), n_exp - 1)
    n_used = (ends[-1] // tm).reshape(1).astype(I32)
    return pos.astype(I32), row_tok, tile_exp.astype(I32), n_used


def _moe_kernel(texp_ref, nused_ref, rtok_ref, h_hbm, w1_ref, w3_ref, w2_ref, y_ref, xbuf, sem):
    tm = xbuf.shape[1]
    i = pl.program_id(0)
    n_used = nused_ref[0]
    slot = i % 2

    def start_gather(tile, sl):
        def body(r, c):
            tok = rtok_ref[tile * tm + r]
            pltpu.make_async_copy(h_hbm.at[pl.ds(tok, 1)], xbuf.at[sl, pl.ds(r, 1)], sem.at[sl]).start()
            return c
        lax.fori_loop(0, tm, body, 0, unroll=GATHER_UNROLL)

    @pl.when(i == 0)
    def _():
        start_gather(0, 0)

    @pl.when(i + 1 < n_used)
    def _():
        start_gather(i + 1, 1 - slot)

    @pl.when(i < n_used)
    def _():
        pltpu.make_async_copy(h_hbm.at[pl.ds(0, tm)], xbuf.at[slot], sem.at[slot]).wait()
        x = _load_slabs(xbuf, (slot,)).astype(BF16)
        a = _dot(x, w1_ref[0])
        b = _dot(x, w3_ref[0])
        hid = (_silu(a) * b).astype(BF16)
        _store_slabs(y_ref, _dot(hid, w2_ref[0]))

    @pl.when(i >= n_used)
    def _():
        y_ref[...] = jnp.zeros_like(y_ref)


def _moe(h, w1, w3, w2, row_tok, tile_exp, n_used):
    t, n_slabs, _ = h.shape
    n_exp, d, dff = w1.shape
    tm = MOE_TM
    n_tiles = tile_exp.shape[0]
    grid_spec = pltpu.PrefetchScalarGridSpec(
        num_scalar_prefetch=3,
        grid=(n_tiles,),
        in_specs=[pl.BlockSpec(memory_space=pl.ANY),
                  pl.BlockSpec((1, d, dff), lambda i, te, nu, rt: (te[i], 0, 0)),
                  pl.BlockSpec((1, d, dff), lambda i, te, nu, rt: (te[i], 0, 0)),
                  pl.BlockSpec((1, dff, d), lambda i, te, nu, rt: (te[i], 0, 0))],
        out_specs=pl.BlockSpec((tm, n_slabs, LANES), lambda i, te, nu, rt: (i, 0, 0)),
        scratch_shapes=[pltpu.VMEM((2, tm, n_slabs, LANES), F32), pltpu.SemaphoreType.DMA((2,))],
    )
    return pl.pallas_call(
        _moe_kernel,
        grid_spec=grid_spec,
        out_shape=jax.ShapeDtypeStruct((n_tiles * tm, n_slabs, LANES), F32),
        compiler_params=_cparams(("arbitrary",)),
        name="moe_experts",
    )(tile_exp, n_used, row_tok, h, w1, w3, w2)


def _combine_kernel(pos_ref, y_hbm, x_ref, tw_ref, gate_ref, nw_ref, sh_ref, sc_ref, *out_and_scratch, final, n_tok):
    if final:
        o_ref, ybuf, sem = out_and_scratch
    else:
        xo_ref, h_ref, ybuf, sem = out_and_scratch
    tm = ybuf.shape[2]
    i = pl.program_id(0)
    n_steps = pl.num_programs(0)
    slot = i % 2

    def start_gather(tile, sl):
        def body(r, c):
            for k in range(TOP_K):
                p = pos_ref[k * n_tok + tile * tm + r]
                pltpu.make_async_copy(y_hbm.at[pl.ds(p, 1)], ybuf.at[sl, k, pl.ds(r, 1)], sem.at[sl]).start()
            return c
        lax.fori_loop(0, tm, body, 0, unroll=GATHER_UNROLL)

    @pl.when(i == 0)
    def _():
        start_gather(0, 0)

    @pl.when(i + 1 < n_steps)
    def _():
        start_gather(i + 1, 1 - slot)

    for k in range(TOP_K):
        pltpu.make_async_copy(y_hbm.at[pl.ds(0, tm)], ybuf.at[slot, k], sem.at[slot]).wait()
    tw = tw_ref[...]
    moe = tw[:, 0:1] * _load_slabs(ybuf, (slot, 0)) + tw[:, 1:2] * _load_slabs(ybuf, (slot, 1))
    x_new = x_ref[...] + gate_ref[0] * moe
    if final:
        o_ref[...] = x_new * lax.rsqrt(jnp.mean(x_new * x_new, axis=-1, keepdims=True) + NORM_EPS) * nw_ref[...]
    else:
        xo_ref[...] = x_new
        h_ref[...] = _rms_mod(x_new, nw_ref[...], sh_ref[0], sc_ref[0]).astype(h_ref.dtype)


def _combine(pos, y_sorted, x2, top_w, gate, nw, shift, scale, bsz, seq, final):
    t, d = x2.shape
    tm = ROW_TM
    ns = seq // tm
    tw = top_w.T
    row = pl.BlockSpec((tm, d), lambda i, p: (i, 0))
    per_b = pl.BlockSpec((1, 1, d), lambda i, p: (i // ns, 0, 0))
    grid_spec = pltpu.PrefetchScalarGridSpec(
        num_scalar_prefetch=1,
        grid=(t // tm,),
        in_specs=[pl.BlockSpec(memory_space=pl.ANY), row,
                  pl.BlockSpec((tm, SUBLANES), lambda i, p: (i, 0)),
                  per_b, pl.BlockSpec((1, d), lambda i, p: (0, 0)), per_b, per_b],
        out_specs=row if final else [row, row],
        scratch_shapes=[pltpu.VMEM((2, TOP_K, tm, d // LANES, LANES), F32), pltpu.SemaphoreType.DMA((2,))],
    )
    out_shape = (jax.ShapeDtypeStruct((t, d), F32) if final else
                 [jax.ShapeDtypeStruct((t, d), F32), jax.ShapeDtypeStruct((t, d), BF16)])
    return pl.pallas_call(
        functools.partial(_combine_kernel, final=final, n_tok=t),
        grid_spec=grid_spec,
        out_shape=out_shape,
        compiler_params=_cparams(("arbitrary",)),
        name="moe_combine",
    )(pos, y_sorted, x2, tw, gate.reshape(bsz, 1, d), nw.reshape(1, d), shift.reshape(bsz, 1, d), scale.reshape(bsz, 1, d))


def _rwkv_prep_kernel(h_ref, mu_ref, wr_ref, wk_ref, wv_ref, dec0_ref, dec1_ref, dec2_ref, a0_ref, a1_ref, a2_ref,
                      g1_ref, g2_ref, kk_ref, ka_ref, rk_ref, hs_ref, he_ref,
                      r_out, lw_out, k_out, v_out, kkn_out, a_out, g_out, bonus_out, carry_ref, hp_ref):
    tm = h_ref.shape[0]

    @pl.when(pl.program_id(1) == 0)
    def _():
        carry_ref[...] = jnp.zeros_like(carry_ref)

    h = h_ref[...].astype(F32)
    hp_ref[0:SUBLANES, :] = carry_ref[...]
    hp_ref[SUBLANES:SUBLANES + tm, :] = h
    carry_ref[...] = h[tm - SUBLANES:tm, :]
    xx = hp_ref[pl.ds(SUBLANES - 1, tm), :] - h
    mix = lambda i: (h + xx * mu_ref[i:i + 1, :]).astype(BF16)
    r = _dot(mix(0), wr_ref[...])
    k = _dot(mix(1), wk_ref[...])
    v = _dot(mix(2), wv_ref[...])
    wl = dec0_ref[...] + _dot(jnp.tanh(_dot(mix(3), dec1_ref[...])).astype(BF16), dec2_ref[...])
    lw = -jnp.exp(-_softplus(-wl) - 0.5)
    a = _sigmoid(a0_ref[...] + _dot(_dot(mix(4), a1_ref[...]).astype(BF16), a2_ref[...]))
    g = _dot(_sigmoid(_dot(mix(5), g1_ref[...])).astype(BF16), g2_ref[...])
    kk = k * kk_ref[...]
    k_h = k * (1.0 + (a - 1.0) * ka_ref[...])
    st = _head_stats_expand(jnp.concatenate([kk * kk, r * k_h * rk_ref[...]], axis=0), hs_ref, he_ref)
    kkn = kk / jnp.maximum(jnp.sqrt(st[:tm]), 1e-12)
    r_out[...] = r.astype(r_out.dtype)
    lw_out[...] = lw
    k_out[...] = k_h.astype(k_out.dtype)
    v_out[...] = v.astype(v_out.dtype)
    kkn_out[...] = kkn.astype(kkn_out.dtype)
    a_out[...] = a.astype(a_out.dtype)
    g_out[...] = g.astype(g_out.dtype)
    bonus_out[...] = (st[tm:] * v).astype(bonus_out.dtype)


def _rwkv_prep(h, mu, w_rkv, dec0, dec1, dec2, a0, a1, a2, g1, g2, k_k, k_a, r_k, hs, he, bsz, seq):
    t, d = h.shape
    tm = ROW_TM
    ns = seq // tm
    padc = lambda w: jnp.zeros((d, LANES), F32).at[:, :w.shape[1]].set(w).astype(BF16)
    padr = lambda w: jnp.zeros((LANES, d), F32).at[:w.shape[0]].set(w).astype(BF16)
    vec = lambda v: v.reshape(1, d)
    args = [h, mu, w_rkv[0].astype(BF16), w_rkv[1].astype(BF16), w_rkv[2].astype(BF16), vec(dec0), padc(dec1), padr(dec2),
            vec(a0), padc(a1), padr(a2), padc(g1), padr(g2), vec(k_k), vec(k_a), vec(r_k), hs, he]
    row = pl.BlockSpec((tm, d), lambda b, s: (b * ns + s, 0))
    const = lambda a: pl.BlockSpec(a.shape, lambda b, s: (0,) * a.ndim)
    outs = [BF16, F32, BF16, BF16, BF16, BF16, BF16, BF16]
    return pl.pallas_call(
        _rwkv_prep_kernel,
        grid=(bsz, ns),
        in_specs=[row] + [const(a) for a in args[1:]],
        out_specs=[row] * len(outs),
        out_shape=[jax.ShapeDtypeStruct((t, d), dt) for dt in outs],
        scratch_shapes=[pltpu.VMEM((SUBLANES, d), F32), pltpu.VMEM((tm + SUBLANES, d), F32)],
        compiler_params=_cparams(("arbitrary", "arbitrary")),
        name="rwkv_prep",
    )(*args)


def _rwkv_scan_kernel(r_ref, lw_ref, k_ref, v_ref, kk_ref, a_ref, y_ref, st_ref):
    L = RW_CHUNK
    pw = RW_PACK * RW_N
    n_packs = st_ref.shape[0]
    sh = RW_N.bit_length() - 1

    @pl.when(pl.program_id(1) == 0)
    def _():
        st_ref[...] = jnp.zeros_like(st_ref)

    t_i = _iota((L, L), 0)
    s_i = _iota((L, L), 1)
    tril = (t_i >= s_i).astype(BF16)
    wc_all = _dot_sel(tril, lw_ref[...])
    lane_head = _iota((L, pw), 1) >> sh
    s_loc = _iota((L, pw), 1) & (RW_N - 1)
    t_loc = _iota((L, pw), 0)
    strict = s_loc < t_loc
    incl = s_loc <= t_loc
    bd_mask = (_iota((pw, pw), 0) >> sh) == (_iota((pw, pw), 1) >> sh)

    def bdiag(x):
        return jnp.where(bd_mask, jnp.concatenate([x] * RW_PACK, axis=0), 0.0).astype(BF16)

    packs = range(n_packs)
    sls = [slice(p * pw, (p + 1) * pw) for p in packs]
    pr, vs, sts, kkas, ks, wcs = [], [], [], [], [], []
    for p in packs:
        sl = sls[p]
        r = r_ref[:, sl].astype(F32)
        lw = lw_ref[:, sl]
        k = k_ref[:, sl].astype(F32)
        kk = kk_ref[:, sl].astype(F32)
        a = a_ref[:, sl].astype(F32)
        wc = wc_all[:, sl]
        e_inv = jnp.exp(-wc)
        kka = kk * a
        al = -kk * jnp.exp(wc - lw)
        rb = r * jnp.exp(wc)
        bt = kka * e_inv
        kt = k * e_inv
        lhs = jnp.concatenate([al, rb], axis=0).astype(BF16)
        rows = [jnp.where(lane_head == hh, x, 0.0) for x in (bt, kt) for hh in range(RW_PACK)]
        st = st_ref[p]
        m = jnp.concatenate(rows + [st], axis=0).astype(BF16)
        pr.append(_dot_nt(lhs, m))
        vs.append(v_ref[:, sl].astype(F32))
        sts.append(st)
        kkas.append(kka)
        ks.append(k)
        wcs.append(wc)
    bd_vs = [bdiag(vs[p]) for p in packs]
    us = [pr[p][:L, 2 * pw:] + _dot(jnp.where(strict, pr[p][:L, pw:2 * pw], 0.0).astype(BF16), bd_vs[p]) for p in packs]
    nmats = [jnp.where(strict, pr[p][:L, 0:pw], 0.0) for p in packs]
    n_steps = L.bit_length() - 1
    for it in range(n_steps):
        us = [us[p] + _dot(nmats[p].astype(BF16), bdiag(us[p])) for p in packs]
        if it + 1 < n_steps:
            nmats = [_dot(nmats[p].astype(BF16), bdiag(nmats[p])) for p in packs]
    for p in packs:
        a_rb = jnp.where(incl, pr[p][L:, 0:pw], 0.0)
        a_rk = jnp.where(incl, pr[p][L:, pw:2 * pw], 0.0)
        y_ref[:, sls[p]] = pr[p][L:, 2 * pw:] + _dot(jnp.concatenate([a_rb, a_rk], axis=1).astype(BF16),
                                                     jnp.concatenate([bdiag(us[p]), bd_vs[p]], axis=0))
    for p in packs:
        w_last = wcs[p][L - 1:L, :]
        e_last = jnp.exp(w_last - wcs[p])
        upd = _dot_tn(jnp.concatenate([us[p], vs[p]], axis=0).astype(BF16),
                      jnp.concatenate([kkas[p] * e_last, ks[p] * e_last], axis=0).astype(BF16))
        st_ref[p] = jnp.where(bd_mask, sts[p] * jnp.exp(w_last) + upd, 0.0)


def _rwkv_scan(r, lw, k, v, kk, a, bsz, seq):
    t, d = r.shape
    ns = seq // RW_CHUNK
    pw = RW_PACK * RW_N
    row = pl.BlockSpec((RW_CHUNK, d), lambda b, s: (b * ns + s, 0))
    return pl.pallas_call(
        _rwkv_scan_kernel,
        grid=(bsz, ns),
        in_specs=[row] * 6,
        out_specs=row,
        out_shape=jax.ShapeDtypeStruct((t, d), F32),
        scratch_shapes=[pltpu.VMEM((d // pw, pw, pw), F32)],
        compiler_params=_cparams(("arbitrary", "arbitrary")),
        name="rwkv_scan",
    )(r, lw, k, v, kk, a)


def _moe_block(h, top_i, top_w, x2, gate, w1, w3, w2, nw, shift, scale, bsz, seq, final):
    n_exp = w1.shape[0]
    pos, row_tok, tile_exp, n_used = _route_plan(top_i, n_exp, MOE_TM)
    y_sorted = _moe(h, w1.astype(BF16), w3.astype(BF16), w2.astype(BF16), row_tok, tile_exp, n_used)
    return _combine(pos, y_sorted, x2, top_w, gate, nw, shift, scale, bsz, seq, final)


def kernel(x, c, mod_w, mod_b, norm_w, hg_lb_logits, ev_w_in, ev_hg_norm, ev_conv_w, ev_conv_b, ev_dt_bias, ev_a_log, ev_d_skip, ev_ssm_norm, ev_w_out, od_mu, od_w_rkv, od_w_dec0, od_w_dec1, od_w_dec2, od_a0, od_a1, od_a2, od_g1, od_g2, od_k_k, od_k_a, od_r_k, od_ln_w, od_ln_b, od_w_o, router_w, router_b, moe_w1, moe_w3, moe_w2, final_norm_w):
    bsz, seq, d = x.shape
    depth = mod_w.shape[0]
    t = bsz * seq
    x2 = x.reshape(t, d)
    mod = _adaln_mod(c, mod_w, mod_b)
    gamma = jax.nn.softmax(hg_lb_logits.astype(F32), axis=0)
    lower_bounds = jnp.cumsum(gamma, axis=0) - gamma[0]
    n_rw_heads = d // RW_N
    head_of_lane = jnp.arange(d, dtype=I32) // RW_N
    hs = (head_of_lane[:, None] == jnp.arange(LANES, dtype=I32)[None, :]).astype(BF16)
    he = hs.T

    h = None
    out = None
    for l in range(depth):
        sh_m, sc_m, gt_m, sh_f, sc_f, gt_f = [mod[l, :, i * d:(i + 1) * d] for i in range(6)]
        j = l // 2
        if h is None:
            h = _norm_mod(x2, norm_w[l, 0], sh_m, sc_m, bsz, seq)
        if l % 2 == 0:
            w_in = ev_w_in[j]
            hgw = ev_hg_norm.shape[1]
            sw = ev_ssm_norm.shape[1]
            xbw = ev_conv_w.shape[2]
            nh = ev_dt_bias.shape[1]
            c0 = 4 * hgw
            proj_hg = _matmul(h, w_in[:, :c0].astype(BF16), F32, 512, 512)
            z = _matmul(h, w_in[:, c0:c0 + sw].astype(BF16), F32, 512, 512)
            xbc = _matmul(h, w_in[:, c0 + sw:c0 + sw + xbw].astype(BF16), F32, 512, 512)
            w_dt = jnp.zeros((d, LANES), F32).at[:, :nh].set(w_in[:, c0 + sw + xbw:]).astype(BF16)
            dt = _matmul(h, w_dt, F32, 512, LANES)
            o_a = _hgrn2(proj_hg, lower_bounds[l + 1], ev_hg_norm[j], bsz, seq)
            o_b = _ssd(z, xbc, dt, ev_conv_w[j], ev_conv_b[j], ev_dt_bias[j], ev_a_log[j], ev_d_skip[j],
                       ev_ssm_norm[j], bsz, seq)
            w_out = ev_w_out[j].astype(BF16)
            x2, hf, top_i, top_w = _epilogue([o_a, o_b], [w_out[:hgw], w_out[hgw:]], x2, gt_m, norm_w[l, 1], sh_f, sc_f,
                                             router_w, router_b, bsz, seq)
        else:
            r, lw, k, v, kk, a, g, bonus = _rwkv_prep(h, od_mu[j], od_w_rkv[j], od_w_dec0[j], od_w_dec1[j], od_w_dec2[j],
                                                      od_a0[j], od_a1[j], od_a2[j], od_g1[j], od_g2[j], od_k_k[j],
                                                      od_k_a[j], od_r_k[j].reshape(-1), hs, he, bsz, seq)
            y = _rwkv_scan(r, lw, k, v, kk, a, bsz, seq)
            x2, hf, top_i, top_w = _epilogue([y], [od_w_o[j].astype(BF16)], x2, gt_m, norm_w[l, 1], sh_f, sc_f,
                                             router_w, router_b, bsz, seq,
                                             rwkv_extra=(g, bonus, od_ln_w[j], od_ln_b[j], hs, he))
        final = l == depth - 1
        if final:
            nw_next, sh_next, sc_next = final_norm_w, sh_f, sc_f
        else:
            nxt = [mod[l + 1, :, i * d:(i + 1) * d] for i in range(2)]
            nw_next, sh_next, sc_next = norm_w[l + 1, 0], nxt[0], nxt[1]
        res = _moe_block(hf, top_i, top_w, x2, gt_f, moe_w1[l], moe_w3[l], moe_w2[l], nw_next, sh_next, sc_next,
                         bsz, seq, final)
        if final:
            out = res
        else:
            x2, h = res
    return out.reshape(bsz, seq, d)
```

```python
import functools

import jax
import jax.numpy as jnp
from jax import lax
from jax.experimental import pallas as pl
from jax.experimental.pallas import tpu as pltpu

F32 = jnp.float32
BF16 = jnp.bfloat16
I32 = jnp.int32

NORM_EPS = 1e-6
RW_GN_EPS = 64e-5
LANES = 128
SUBLANES = 8
VMEM_LIMIT = 56 * 1024 * 1024

HG_DK = 128
SSM_P = 64
SSM_N = 128
SSM_GROUPS = 2
SSM_CONV = 4
RW_N = 64
N_GROUPS_MOE = 4
TOP_K = 2

CHUNK = 128
RW_CHUNK = 64
RW_PACK = 4
MOE_TM = 256
ROW_TM = 256
GATHER_UNROLL = 8


def _cparams(sem):
    return pltpu.CompilerParams(dimension_semantics=sem, vmem_limit_bytes=VMEM_LIMIT)


def _dot(a, b):
    return lax.dot_general(a, b, (((1,), (0,)), ((), ())), preferred_element_type=F32)


def _dot_nt(a, b):
    return lax.dot_general(a, b, (((1,), (1,)), ((), ())), preferred_element_type=F32)


def _dot_tn(a, b):
    return lax.dot_general(a, b, (((0,), (0,)), ((), ())), preferred_element_type=F32)


def _split(x):
    hi = x.astype(BF16)
    return hi, (x - hi.astype(F32)).astype(BF16)


def _dot_sel(sel, x):
    hi, lo = _split(x)
    return _dot(sel, hi) + _dot(sel, lo)


def _dot_rsel(x, sel):
    hi, lo = _split(x)
    return _dot(hi, sel) + _dot(lo, sel)


def _sigmoid(x):
    return 1.0 / (1.0 + jnp.exp(-x))


def _silu(x):
    return x * _sigmoid(x)


def _softplus(x):
    return jnp.maximum(x, 0.0) + jnp.log(1.0 + jnp.exp(-jnp.abs(x)))


def _iota(shape, dim):
    return lax.broadcasted_iota(I32, shape, dim)


def _store_slabs(ref, val):
    for j in range(val.shape[1] // LANES):
        ref[:, j, :] = val[:, j * LANES:(j + 1) * LANES].astype(ref.dtype)


def _load_slabs(ref, lead=()):
    n_slabs = ref.shape[-2]
    return jnp.concatenate([ref[lead + (slice(None), j, slice(None))] for j in range(n_slabs)], axis=1)


def _mod_kernel(c_ref, w_ref, b_ref, o_ref):
    c = c_ref[...]
    o_ref[0] = _dot(_silu(c).astype(BF16), w_ref[0].astype(BF16)) + b_ref[0]


def _adaln_mod(c, mod_w, mod_b):
    depth, d, width = mod_w.shape
    bsz = c.shape[0]
    c_pad = jnp.zeros((SUBLANES, d), F32).at[:bsz].set(c)
    tn = 1536
    out = pl.pallas_call(
        _mod_kernel,
        grid=(depth, width // tn),
        in_specs=[pl.BlockSpec((SUBLANES, d), lambda l, j: (0, 0)),
                  pl.BlockSpec((1, d, tn), lambda l, j: (l, 0, j)),
                  pl.BlockSpec((1, 1, tn), lambda l, j: (l, 0, j))],
        out_specs=pl.BlockSpec((1, SUBLANES, tn), lambda l, j: (l, 0, j)),
        out_shape=jax.ShapeDtypeStruct((depth, SUBLANES, width), F32),
        compiler_params=_cparams(("arbitrary", "arbitrary")),
        name="adaln_mod",
    )(c_pad, mod_w, mod_b.reshape(depth, 1, width))
    return out[:, :bsz]


def _rms_mod(x, nw, shift, scale):
    y = x * lax.rsqrt(jnp.mean(x * x, axis=-1, keepdims=True) + NORM_EPS) * nw
    return y * (1.0 + scale) + shift


def _normmod_kernel(x_ref, nw_ref, sh_ref, sc_ref, h_ref):
    h_ref[...] = _rms_mod(x_ref[...], nw_ref[...], sh_ref[0], sc_ref[0]).astype(h_ref.dtype)


def _norm_mod(x2, nw, shift, scale, bsz, seq):
    t, d = x2.shape
    tm = ROW_TM
    ns = seq // tm
    return pl.pallas_call(
        _normmod_kernel,
        grid=(bsz, ns),
        in_specs=[pl.BlockSpec((tm, d), lambda b, s: (b * ns + s, 0)),
                  pl.BlockSpec((1, d), lambda b, s: (0, 0)),
                  pl.BlockSpec((1, 1, d), lambda b, s: (b, 0, 0)),
                  pl.BlockSpec((1, 1, d), lambda b, s: (b, 0, 0))],
        out_specs=pl.BlockSpec((tm, d), lambda b, s: (b * ns + s, 0)),
        out_shape=jax.ShapeDtypeStruct((t, d), BF16),
        compiler_params=_cparams(("arbitrary", "arbitrary")),
        name="norm_mod",
    )(x2, nw.reshape(1, d), shift.reshape(bsz, 1, d), scale.reshape(bsz, 1, d))


def _block_mid_ref(b, n2):
    rows, width = b.shape
    n = n2 // 2
    if n2 >= 2 * SUBLANES:
        b3 = b.reshape(rows // n2, n2, width)
        return jnp.broadcast_to(b3[:, n - 1:n, :], b3.shape).reshape(rows, width)
    b3 = b.reshape(rows // SUBLANES, SUBLANES, width)
    sub = _iota(b3.shape, 1)
    r3 = jnp.broadcast_to(b3[:, SUBLANES - n2 + n - 1:SUBLANES - n2 + n, :], b3.shape)
    for g in range(SUBLANES // n2 - 2, -1, -1):
        r3 = jnp.where(sub < (g + 1) * n2, b3[:, g * n2 + n - 1:g * n2 + n, :], r3)
    return r3.reshape(rows, width)


def _hgrn2_kernel(h_ref, w_ref, lb_ref, nw_ref, o_ref, proj_ref, st_ref):
    L = CHUNK
    dk = HG_DK
    n_heads = st_ref.shape[0]
    width = n_heads * dk

    @pl.when(pl.program_id(1) == 0)
    def _():
        st_ref[...] = jnp.zeros_like(st_ref)

    proj_ref[...] = _dot(h_ref[...], w_ref[...])
    lb = lb_ref[...]
    f = lb + (1.0 - lb) * _sigmoid(proj_ref[:, width:2 * width])
    logf = jnp.log(f)
    t_i = _iota((L, L), 0)
    s_i = _iota((L, L), 1)
    tril = (t_i >= s_i).astype(BF16)
    b_all = _dot_sel(tril, logf)
    eye = t_i == s_i

    levels = []
    n2 = L
    while n2 >= 2:
        n = n2 // 2
        sh = n2.bit_length() - 1
        m = ((t_i >> sh) == (s_i >> sh)) & ((t_i & (n2 - 1)) >= n) & ((s_i & (n2 - 1)) < n)
        levels.append((n2, m))
        n2 = n

    for h in range(n_heads):
        sl = slice(h * dk, (h + 1) * dk)
        q = proj_ref[:, sl]
        k = 1.0 - f[:, sl]
        v = proj_ref[:, 2 * width + h * dk:2 * width + (h + 1) * dk]
        b = b_all[:, sl]
        st = st_ref[h]
        b_last = b[L - 1:L, :]
        o = _dot_nt((q * jnp.exp(b)).astype(BF16), st.astype(BF16))
        a = jnp.where(eye, jnp.sum(q * k, axis=-1, keepdims=True), 0.0)
        for n2, m in levels:
            e = jnp.exp(-jnp.abs(b - _block_mid_ref(b, n2)))
            a = a + jnp.where(m, _dot_nt((q * e).astype(BF16), (k * e).astype(BF16)), 0.0)
        o = o + _dot(a.astype(BF16), v.astype(BF16))
        ke = k * jnp.exp(b_last - b)
        st_ref[h] = st * jnp.exp(b_last) + _dot_tn(v.astype(BF16), ke.astype(BF16))
        g = proj_ref[:, 3 * width + h * dk:3 * width + (h + 1) * dk]
        ms = jnp.mean(o * o, axis=-1, keepdims=True)
        o_ref[:, sl] = (o * lax.rsqrt(ms + NORM_EPS) * nw_ref[:, sl] * _silu(g)).astype(o_ref.dtype)


def _hgrn2(h, w_hg, lb, hg_norm, bsz, seq):
    t, d = h.shape
    width = lb.shape[0]
    n_heads = width // HG_DK
    ns = seq // CHUNK
    return pl.pallas_call(
        _hgrn2_kernel,
        grid=(bsz, ns),
        in_specs=[pl.BlockSpec((CHUNK, d), lambda b, s: (b * ns + s, 0)),
                  pl.BlockSpec(w_hg.shape, lambda b, s: (0, 0)),
                  pl.BlockSpec((1, width), lambda b, s: (0, 0)),
                  pl.BlockSpec((1, width), lambda b, s: (0, 0))],
        out_specs=pl.BlockSpec((CHUNK, width), lambda b, s: (b * ns + s, 0)),
        out_shape=jax.ShapeDtypeStruct((t, width), BF16),
        scratch_shapes=[pltpu.VMEM((CHUNK, w_hg.shape[1]), F32), pltpu.VMEM((n_heads, HG_DK, HG_DK), F32)],
        compiler_params=_cparams(("arbitrary", "arbitrary")),
        name="hgrn2_scan",
    )(h, w_hg, lb.reshape(1, width), hg_norm.reshape(1, width))


def _ssd_kernel(h_ref, w_ref, cw_ref, cb_ref, dtb_ref, a_ref, dsk_ref, nw_ref, o_ref,
                proj_ref, carry_ref, xpad_ref, st_ref):
    L = CHUNK
    width = o_ref.shape[1]
    xw = cw_ref.shape[1]
    n_heads = width // SSM_P
    gw = width // SSM_GROUPS
    heads_per_group = n_heads // SSM_GROUPS

    @pl.when(pl.program_id(1) == 0)
    def _():
        carry_ref[...] = jnp.zeros_like(carry_ref)
        st_ref[...] = jnp.zeros_like(st_ref)

    proj_ref[...] = _dot(h_ref[...], w_ref[...])
    z_ref = proj_ref.at[:, 0:width]
    dt_ref = proj_ref.at[:, width + xw:]
    xraw = proj_ref[:, width:width + xw]
    xpad_ref[0:SUBLANES, :] = carry_ref[...]
    xpad_ref[SUBLANES:SUBLANES + L, :] = xraw
    carry_ref[...] = xraw[L - SUBLANES:L, :]
    acc = cb_ref[...] + jnp.zeros_like(xraw)
    for j in range(SSM_CONV):
        acc = acc + cw_ref[j:j + 1, :] * xpad_ref[pl.ds(SUBLANES - (SSM_CONV - 1) + j, L), :]
    xc = _silu(acc)
    xs = xc[:, :width]
    bm = xc[:, width:width + SSM_GROUPS * SSM_N]
    cm = xc[:, width + SSM_GROUPS * SSM_N:]

    dt = _softplus(dt_ref[...] + dtb_ref[...])
    da = dt * a_ref[...]
    da_t = da.T
    dt_t = dt.T
    t_i = _iota((L, L), 0)
    s_i = _iota((L, L), 1)
    triu = (t_i <= s_i).astype(BF16)
    acs_t = _dot_rsel(da_t, triu)
    causal = t_i >= s_i
    diag = t_i == s_i
    lane_lo = _iota((L, 2 * SSM_P), 1) < SSM_P
    bd_mask = (_iota((2 * L, 2 * SSM_P), 0) < L) == (_iota((2 * L, 2 * SSM_P), 1) < SSM_P)

    y_pairs = []
    for g in range(SSM_GROUPS):
        bg = bm[:, g * SSM_N:(g + 1) * SSM_N]
        cg = cm[:, g * SSM_N:(g + 1) * SSM_N]
        cb = _dot_nt(cg.astype(BF16), bg.astype(BF16))
        hg = st_ref[:, g * gw:(g + 1) * gw]
        yoff_g = _dot(cg.astype(BF16), hg.astype(BF16))
        xsc_parts, decay_parts = [], []
        for pr in range(heads_per_group // 2):
            j0 = g * heads_per_group + 2 * pr
            gs, ds, ecol, elast = [], [], [], []
            for j in (j0, j0 + 1):
                row_b = jnp.broadcast_to(acs_t[j:j + 1, :], (L, L))
                col_b = row_b.T
                dt_row = jnp.broadcast_to(dt_t[j:j + 1, :], (L, L))
                lmat = jnp.exp(jnp.minimum(col_b - row_b, 0.0))
                gs.append(jnp.where(causal, cb * lmat * dt_row, 0.0))
                a_last = acs_t[j:j + 1, L - 1:L]
                ds.append(jnp.where(diag, jnp.exp(a_last - row_b) * dt_row, 0.0))
                ecol.append(jnp.exp(col_b))
                elast.append(jnp.exp(a_last))
            lhs = jnp.concatenate([jnp.concatenate(gs, axis=1), jnp.concatenate(ds, axis=1)], axis=0)
            xs_pair = xs[:, j0 * SSM_P:(j0 + 2) * SSM_P]
            bd = jnp.where(bd_mask, jnp.concatenate([xs_pair, xs_pair], axis=0), 0.0)
            res = _dot(lhs.astype(BF16), bd.astype(BF16))
            yoff = yoff_g[:, pr * 2 * SSM_P:(pr + 1) * 2 * SSM_P] * jnp.where(lane_lo, ecol[0], ecol[1])
            y_pairs.append(res[:L] + yoff)
            xsc_parts.append(res[L:])
            decay_parts.append(jnp.where(lane_lo[0:1], elast[0], elast[1]))
        xsc_g = jnp.concatenate(xsc_parts, axis=1)
        decay_g = jnp.concatenate(decay_parts, axis=1)
        st_ref[:, g * gw:(g + 1) * gw] = hg * decay_g + _dot_tn(bg.astype(BF16), xsc_g.astype(BF16))
    y = jnp.concatenate(y_pairs, axis=1) + dsk_ref[...] * xs
    yz = y * _silu(z_ref[...])
    for g in range(SSM_GROUPS):
        seg = yz[:, g * gw:(g + 1) * gw]
        ms = jnp.mean(seg * seg, axis=-1, keepdims=True)
        o_ref[:, g * gw:(g + 1) * gw] = (seg * lax.rsqrt(ms + NORM_EPS) * nw_ref[:, g * gw:(g + 1) * gw]).astype(o_ref.dtype)


def _ssd(h, w_ssd, conv_w, conv_b, dt_bias, a_log, d_skip, ssm_norm, bsz, seq):
    t, d = h.shape
    width = ssm_norm.shape[0]
    xw = conv_w.shape[1]
    n_heads = width // SSM_P
    ns = seq // CHUNK
    pad = lambda v: jnp.zeros((1, LANES), F32).at[0, :n_heads].set(v)
    row = lambda w: pl.BlockSpec((CHUNK, w), lambda b, s: (b * ns + s, 0))
    const = lambda r, w: pl.BlockSpec((r, w), lambda b, s: (0, 0))
    return pl.pallas_call(
        _ssd_kernel,
        grid=(bsz, ns),
        in_specs=[row(d), const(*w_ssd.shape), const(SSM_CONV, xw), const(1, xw), const(1, LANES),
                  const(1, LANES), const(1, width), const(1, width)],
        out_specs=row(width),
        out_shape=jax.ShapeDtypeStruct((t, width), BF16),
        scratch_shapes=[pltpu.VMEM((CHUNK, w_ssd.shape[1]), F32), pltpu.VMEM((SUBLANES, xw), F32),
                        pltpu.VMEM((CHUNK + SUBLANES, xw), F32), pltpu.VMEM((SSM_N, width), F32)],
        compiler_params=_cparams(("arbitrary", "arbitrary")),
        name="ssd_scan",
    )(h, w_ssd, conv_w, conv_b.reshape(1, xw), pad(dt_bias), pad(-jnp.exp(a_log)),
      jnp.repeat(d_skip, SSM_P).reshape(1, width), ssm_norm.reshape(1, width))


def _route(probs):
    n_exp = N_GROUPS_MOE * 4
    p = [probs[e:e + 1, :] for e in range(n_exp)]
    gs = []
    for g in range(N_GROUPS_MOE):
        a, b, c, d = p[4 * g:4 * g + 4]
        gs.append(jnp.maximum(jnp.maximum(jnp.maximum(a + b, a + c), jnp.maximum(a + d, b + c)),
                              jnp.maximum(b + d, c + d)))
    best = jnp.zeros_like(gs[0]).astype(I32)
    bs = gs[0]
    for g in range(1, N_GROUPS_MOE):
        upd = gs[g] > bs
        best = jnp.where(upd, g, best)
        bs = jnp.where(upd, gs[g], bs)
    q = [jnp.where(best == 0, p[i], jnp.where(best == 1, p[4 + i], jnp.where(best == 2, p[8 + i], p[12 + i])))
         for i in range(4)]
    i1 = jnp.zeros_like(best)
    v1 = q[0]
    for i in range(1, 4):
        upd = q[i] > v1
        i1 = jnp.where(upd, i, i1)
        v1 = jnp.where(upd, q[i], v1)
    i2 = jnp.zeros_like(best)
    v2 = jnp.full_like(v1, -1.0)
    for i in range(4):
        upd = (i1 != i) & (q[i] > v2)
        i2 = jnp.where(upd, i, i2)
        v2 = jnp.where(upd, q[i], v2)
    den = v1 + v2
    return best * 4 + i1, best * 4 + i2, v1 / den, v2 / den


def _head_stats_expand(stack, hs_ref, he_ref):
    return _dot_rsel(_dot_rsel(stack, hs_ref[...]), he_ref[...])


def _epilogue_kernel(*refs, n_in, rwkv):
    ins = refs[:n_in]
    pos = n_in
    w_refs = refs[pos:pos + n_in]
    pos += n_in
    if rwkv:
        g_ref, bonus_ref, lnw_ref, lnb_ref, hs_ref, he_ref = refs[pos:pos + 6]
        pos += 6
    x_ref, gate_ref, nw_ref, sh_ref, sc_ref, rw_ref, rb_ref = refs[pos:pos + 7]
    pos += 7
    xo_ref, h_ref, ti_ref, tw_ref = refs[pos:pos + 4]

    if rwkv:
        o = ins[0][...]
        tm = o.shape[0]
        inv_n = 1.0 / RW_N
        st = _head_stats_expand(jnp.concatenate([o, o * o], axis=0), hs_ref, he_ref) * inv_n
        mean = st[:tm]
        var = jnp.maximum(st[tm:] - mean * mean, 0.0)
        o = (o - mean) * lax.rsqrt(var + RW_GN_EPS) * lnw_ref[...] + lnb_ref[...]
        o = (o + bonus_ref[...].astype(F32)) * g_ref[...].astype(F32)
        y = _dot(o.astype(BF16), w_refs[0][...])
    else:
        y = _dot(ins[0][...], w_refs[0][...])
        for a_ref, w_ref in zip(ins[1:], w_refs[1:]):
            y = y + _dot(a_ref[...], w_ref[...])
    x_new = x_ref[...] + gate_ref[0] * y
    xo_ref[...] = x_new
    h = _rms_mod(x_new, nw_ref[...], sh_ref[0], sc_ref[0])
    _store_slabs(h_ref, h)
    h_hi, h_lo = _split(h)
    rw = rw_ref[...]
    rw_hi, rw_lo = _split(rw)
    logits = _dot_nt(rw_hi, h_hi) + _dot_nt(rw_hi, h_lo) + _dot_nt(rw_lo, h_hi) + rb_ref[...]
    mx = jnp.max(logits, axis=0, keepdims=True)
    ex = jnp.exp(logits - mx)
    probs = ex / jnp.sum(ex, axis=0, keepdims=True)
    e1, e2, w1, w2 = _route(probs)
    tmn = e1.shape[1]
    zi = jnp.zeros((SUBLANES - TOP_K, tmn), I32)
    ti_ref[...] = jnp.concatenate([e1, e2, zi], axis=0)
    tw_ref[...] = jnp.concatenate([w1, w2, zi.astype(F32)], axis=0)


def _epilogue(ins, ws, x2, gate, nw, shift, scale, router_w, router_b, bsz, seq, rwkv_extra=None):
    t, d = x2.shape
    tm = ROW_TM
    ns = seq // tm
    n_exp = router_w.shape[1]
    rw_t = jnp.zeros((LANES, d), F32).at[:n_exp].set(router_w.T)
    rb = jnp.full((LANES, 1), -1e30, F32).at[:n_exp, 0].set(router_b)
    row = lambda w: pl.BlockSpec((tm, w), lambda b, s: (b * ns + s, 0))
    const = lambda a: pl.BlockSpec(a.shape, lambda b, s: (0,) * a.ndim)
    per_b = pl.BlockSpec((1, 1, d), lambda b, s: (b, 0, 0))
    args = list(ins) + list(ws)
    specs = [row(a.shape[1]) for a in ins] + [const(w) for w in ws]
    if rwkv_extra is not None:
        g, bonus, lnw, lnb, hs, he = rwkv_extra
        args += [g, bonus, lnw.reshape(1, d), lnb.reshape(1, d), hs, he]
        specs += [row(d), row(d), pl.BlockSpec((1, d), lambda b, s: (0, 0)), pl.BlockSpec((1, d), lambda b, s: (0, 0)),
                  const(hs), const(he)]
    args += [x2, gate.reshape(bsz, 1, d), nw.reshape(1, d), shift.reshape(bsz, 1, d), scale.reshape(bsz, 1, d), rw_t, rb]
    specs += [row(d), per_b, pl.BlockSpec((1, d), lambda b, s: (0, 0)), per_b, per_b, const(rw_t), const(rb)]
    lane_row = pl.BlockSpec((SUBLANES, tm), lambda b, s: (0, b * ns + s))
    return pl.pallas_call(
        functools.partial(_epilogue_kernel, n_in=len(ins), rwkv=rwkv_extra is not None),
        grid=(bsz, ns),
        in_specs=specs,
        out_specs=[row(d), pl.BlockSpec((tm, d // LANES, LANES), lambda b, s: (b * ns + s, 0, 0)), lane_row, lane_row],
        out_shape=[jax.ShapeDtypeStruct((t, d), F32), jax.ShapeDtypeStruct((t, d // LANES, LANES), F32),
                   jax.ShapeDtypeStruct((SUBLANES, t), I32), jax.ShapeDtypeStruct((SUBLANES, t), F32)],
        compiler_params=_cparams(("arbitrary", "arbitrary")),
        name="mixer_epilogue",
    )(*args)


def _route_plan(top_i, n_exp, tm):
    t = top_i.shape[1]
    n_pairs = TOP_K * t
    n_tiles = n_pairs // tm + n_exp
    e_flat = top_i[:TOP_K].reshape(-1)
    onehot = (e_flat[:, None] == jnp.arange(n_exp, dtype=I32)[None, :]).astype(I32)
    csum = jnp.cumsum(onehot, axis=0)
    counts = csum[-1]
    rank = jnp.sum((csum - 1) * onehot, axis=1)
    padded = ((counts + tm - 1) // tm) * tm
    ends = jnp.cumsum(padded)
    pos = (ends - padded)[e_flat] + rank
    row_tok = jnp.zeros((n_tiles * tm,), I32).at[pos].set(jnp.arange(n_pairs, dtype=I32) % t)
    tile_start = jnp.arange(n_tiles, dtype=I32) * tm
    tile_exp = jnp.minimum(jnp.sum((tile_start[:, None] >= ends[None, :]).astype(I32), axis=1), n_exp - 1)
    n_used = (ends[-1] // tm).reshape(1).astype(I32)
    return pos.astype(I32), row_tok, tile_exp.astype(I32), n_used


def _moe_kernel(texp_ref, nused_ref, rtok_ref, h_hbm, w1_ref, w3_ref, w2_ref, y_ref, xbuf, wb1, wb3, wb2, sem):
    tm = xbuf.shape[1]
    i = pl.program_id(0)
    n_used = nused_ref[0]
    slot = i % 2

    def start_gather(tile, sl):
        def body(r, c):
            tok = rtok_ref[tile * tm + r]
            pltpu.make_async_copy(h_hbm.at[pl.ds(tok, 1)], xbuf.at[sl, pl.ds(r, 1)], sem.at[sl]).start()
            return c
        lax.fori_loop(0, tm, body, 0, unroll=GATHER_UNROLL)

    @pl.when(i == 0)
    def _():
        start_gather(0, 0)

    @pl.when(i + 1 < n_used)
    def _():
        start_gather(i + 1, 1 - slot)

    @pl.when((i == 0) | (texp_ref[i] != texp_ref[jnp.maximum(i - 1, 0)]))
    def _():
        wb1[...] = w1_ref[0].astype(BF16)
        wb3[...] = w3_ref[0].astype(BF16)
        wb2[...] = w2_ref[0].astype(BF16)

    @pl.when(i < n_used)
    def _():
        pltpu.make_async_copy(h_hbm.at[pl.ds(0, tm)], xbuf.at[slot], sem.at[slot]).wait()
        x = _load_slabs(xbuf, (slot,)).astype(BF16)
        a = _dot(x, wb1[...])
        b = _dot(x, wb3[...])
        hid = (_silu(a) * b).astype(BF16)
        _store_slabs(y_ref, _dot(hid, wb2[...]))

    @pl.when(i >= n_used)
    def _():
        y_ref[...] = jnp.zeros_like(y_ref)


def _moe(h, w1, w3, w2, row_tok, tile_exp, n_used):
    t, n_slabs, _ = h.shape
    n_exp, d, dff = w1.shape
    tm = MOE_TM
    n_tiles = tile_exp.shape[0]
    grid_spec = pltpu.PrefetchScalarGridSpec(
        num_scalar_prefetch=3,
        grid=(n_tiles,),
        in_specs=[pl.BlockSpec(memory_space=pl.ANY),
                  pl.BlockSpec((1, d, dff), lambda i, te, nu, rt: (te[i], 0, 0)),
                  pl.BlockSpec((1, d, dff), lambda i, te, nu, rt: (te[i], 0, 0)),
                  pl.BlockSpec((1, dff, d), lambda i, te, nu, rt: (te[i], 0, 0))],
        out_specs=pl.BlockSpec((tm, n_slabs, LANES), lambda i, te, nu, rt: (i, 0, 0)),
        scratch_shapes=[pltpu.VMEM((2, tm, n_slabs, LANES), F32), pltpu.VMEM((d, dff), BF16), pltpu.VMEM((d, dff), BF16),
                        pltpu.VMEM((dff, d), BF16), pltpu.SemaphoreType.DMA((2,))],
    )
    return pl.pallas_call(
        _moe_kernel,
        grid_spec=grid_spec,
        out_shape=jax.ShapeDtypeStruct((n_tiles * tm, n_slabs, LANES), F32),
        compiler_params=_cparams(("arbitrary",)),
        name="moe_experts",
    )(tile_exp, n_used, row_tok, h, w1, w3, w2)


def _combine_kernel(pos_ref, y_hbm, x_ref, tw_ref, gate_ref, nw_ref, sh_ref, sc_ref, *out_and_scratch, final, n_tok):
    if final:
        o_ref, ybuf, sem = out_and_scratch
    else:
        xo_ref, h_ref, ybuf, sem = out_and_scratch
    tm = ybuf.shape[2]
    i = pl.program_id(0)
    n_steps = pl.num_programs(0)
    slot = i % 2

    def start_gather(tile, sl):
        def body(r, c):
            for k in range(TOP_K):
                p = pos_ref[k * n_tok + tile * tm + r]
                pltpu.make_async_copy(y_hbm.at[pl.ds(p, 1)], ybuf.at[sl, k, pl.ds(r, 1)], sem.at[sl]).start()
            return c
        lax.fori_loop(0, tm, body, 0, unroll=GATHER_UNROLL)

    @pl.when(i == 0)
    def _():
        start_gather(0, 0)

    @pl.when(i + 1 < n_steps)
    def _():
        start_gather(i + 1, 1 - slot)

    for k in range(TOP_K):
        pltpu.make_async_copy(y_hbm.at[pl.ds(0, tm)], ybuf.at[slot, k], sem.at[slot]).wait()
    tw = tw_ref[...]
    moe = tw[:, 0:1] * _load_slabs(ybuf, (slot, 0)) + tw[:, 1:2] * _load_slabs(ybuf, (slot, 1))
    x_new = x_ref[...] + gate_ref[0] * moe
    if final:
        o_ref[...] = x_new * lax.rsqrt(jnp.mean(x_new * x_new, axis=-1, keepdims=True) + NORM_EPS) * nw_ref[...]
    else:
        xo_ref[...] = x_new
        h_ref[...] = _rms_mod(x_new, nw_ref[...], sh_ref[0], sc_ref[0]).astype(h_ref.dtype)


def _combine(pos, y_sorted, x2, top_w, gate, nw, shift, scale, bsz, seq, final):
    t, d = x2.shape
    tm = ROW_TM
    ns = seq // tm
    tw = top_w.T
    row = pl.BlockSpec((tm, d), lambda i, p: (i, 0))
    per_b = pl.BlockSpec((1, 1, d), lambda i, p: (i // ns, 0, 0))
    grid_spec = pltpu.PrefetchScalarGridSpec(
        num_scalar_prefetch=1,
        grid=(t // tm,),
        in_specs=[pl.BlockSpec(memory_space=pl.ANY), row,
                  pl.BlockSpec((tm, SUBLANES), lambda i, p: (i, 0)),
                  per_b, pl.BlockSpec((1, d), lambda i, p: (0, 0)), per_b, per_b],
        out_specs=row if final else [row, row],
        scratch_shapes=[pltpu.VMEM((2, TOP_K, tm, d // LANES, LANES), F32), pltpu.SemaphoreType.DMA((2,))],
    )
    out_shape = (jax.ShapeDtypeStruct((t, d), F32) if final else
                 [jax.ShapeDtypeStruct((t, d), F32), jax.ShapeDtypeStruct((t, d), BF16)])
    return pl.pallas_call(
        functools.partial(_combine_kernel, final=final, n_tok=t),
        grid_spec=grid_spec,
        out_shape=out_shape,
        compiler_params=_cparams(("arbitrary",)),
        name="moe_combine",
    )(pos, y_sorted, x2, tw, gate.reshape(bsz, 1, d), nw.reshape(1, d), shift.reshape(bsz, 1, d), scale.reshape(bsz, 1, d))


def _rwkv_prep_kernel(h_ref, mu_ref, wr_ref, wk_ref, wv_ref, dec0_ref, dec1_ref, dec2_ref, a0_ref, a1_ref, a2_ref,
                      g1_ref, g2_ref, kk_ref, ka_ref, rk_ref, hs_ref, he_ref,
                      r_out, lw_out, k_out, v_out, kkn_out, a_out, g_out, bonus_out, carry_ref, hp_ref):
    tm = h_ref.shape[0]

    @pl.when(pl.program_id(1) == 0)
    def _():
        carry_ref[...] = jnp.zeros_like(carry_ref)

    h = h_ref[...].astype(F32)
    hp_ref[0:SUBLANES, :] = carry_ref[...]
    hp_ref[SUBLANES:SUBLANES + tm, :] = h
    carry_ref[...] = h[tm - SUBLANES:tm, :]
    xx = hp_ref[pl.ds(SUBLANES - 1, tm), :] - h
    mix = lambda i: (h + xx * mu_ref[i:i + 1, :]).astype(BF16)
    r = _dot(mix(0), wr_ref[...])
    k = _dot(mix(1), wk_ref[...])
    v = _dot(mix(2), wv_ref[...])
    wl = dec0_ref[...] + _dot(jnp.tanh(_dot(mix(3), dec1_ref[...])).astype(BF16), dec2_ref[...])
    lw = -jnp.exp(-_softplus(-wl) - 0.5)
    a = _sigmoid(a0_ref[...] + _dot(_dot(mix(4), a1_ref[...]).astype(BF16), a2_ref[...]))
    g = _dot(_sigmoid(_dot(mix(5), g1_ref[...])).astype(BF16), g2_ref[...])
    kk = k * kk_ref[...]
    k_h = k * (1.0 + (a - 1.0) * ka_ref[...])
    st = _head_stats_expand(jnp.concatenate([kk * kk, r * k_h * rk_ref[...]], axis=0), hs_ref, he_ref)
    kkn = kk / jnp.maximum(jnp.sqrt(st[:tm]), 1e-12)
    r_out[...] = r.astype(r_out.dtype)
    lw_out[...] = lw
    k_out[...] = k_h.astype(k_out.dtype)
    v_out[...] = v.astype(v_out.dtype)
    kkn_out[...] = kkn.astype(kkn_out.dtype)
    a_out[...] = a.astype(a_out.dtype)
    g_out[...] = g.astype(g_out.dtype)
    bonus_out[...] = (st[tm:] * v).astype(bonus_out.dtype)


def _rwkv_prep(h, mu, w_rkv, dec0, dec1, dec2, a0, a1, a2, g1, g2, k_k, k_a, r_k, hs, he, bsz, seq):
    t, d = h.shape
    tm = ROW_TM
    ns = seq // tm
    padc = lambda w: jnp.zeros((d, LANES), F32).at[:, :w.shape[1]].set(w).astype(BF16)
    padr = lambda w: jnp.zeros((LANES, d), F32).at[:w.shape[0]].set(w).astype(BF16)
    vec = lambda v: v.reshape(1, d)
    args = [h, mu, w_rkv[0].astype(BF16), w_rkv[1].astype(BF16), w_rkv[2].astype(BF16), vec(dec0), padc(dec1), padr(dec2),
            vec(a0), padc(a1), padr(a2), padc(g1), padr(g2), vec(k_k), vec(k_a), vec(r_k), hs, he]
    row = pl.BlockSpec((tm, d), lambda b, s: (b * ns + s, 0))
    const = lambda a: pl.BlockSpec(a.shape, lambda b, s: (0,) * a.ndim)
    outs = [BF16, F32, BF16, BF16, BF16, BF16, BF16, BF16]
    return pl.pallas_call(
        _rwkv_prep_kernel,
        grid=(bsz, ns),
        in_specs=[row] + [const(a) for a in args[1:]],
        out_specs=[row] * len(outs),
        out_shape=[jax.ShapeDtypeStruct((t, d), dt) for dt in outs],
        scratch_shapes=[pltpu.VMEM((SUBLANES, d), F32), pltpu.VMEM((tm + SUBLANES, d), F32)],
        compiler_params=_cparams(("arbitrary", "arbitrary")),
        name="rwkv_prep",
    )(*args)


def _rwkv_scan_kernel(r_ref, lw_ref, k_ref, v_ref, kk_ref, a_ref, y_ref, st_ref):
    L = RW_CHUNK
    pw = RW_PACK * RW_N
    n_packs = st_ref.shape[0]
    sh = RW_N.bit_length() - 1

    @pl.when(pl.program_id(1) == 0)
    def _():
        st_ref[...] = jnp.zeros_like(st_ref)

    t_i = _iota((L, L), 0)
    s_i = _iota((L, L), 1)
    tril = (t_i >= s_i).astype(BF16)
    wc_all = _dot_sel(tril, lw_ref[...])
    lane_head = _iota((L, pw), 1) >> sh
    s_loc = _iota((L, pw), 1) & (RW_N - 1)
    t_loc = _iota((L, pw), 0)
    strict = s_loc < t_loc
    incl = s_loc <= t_loc
    bd_mask = (_iota((pw, pw), 0) >> sh) == (_iota((pw, pw), 1) >> sh)

    def bdiag(x):
        return jnp.where(bd_mask, jnp.concatenate([x] * RW_PACK, axis=0), 0.0).astype(BF16)

    packs = range(n_packs)
    sls = [slice(p * pw, (p + 1) * pw) for p in packs]
    pr, vs, sts, kkas, ks, wcs = [], [], [], [], [], []
    for p in packs:
        sl = sls[p]
        r = r_ref[:, sl].astype(F32)
        lw = lw_ref[:, sl]
        k = k_ref[:, sl].astype(F32)
        kk = kk_ref[:, sl].astype(F32)
        a = a_ref[:, sl].astype(F32)
        wc = wc_all[:, sl]
        e_inv = jnp.exp(-wc)
        kka = kk * a
        al = -kk * jnp.exp(wc - lw)
        rb = r * jnp.exp(wc)
        bt = kka * e_inv
        kt = k * e_inv
        lhs = jnp.concatenate([al, rb], axis=0).astype(BF16)
        rows = [jnp.where(lane_head == hh, x, 0.0) for x in (bt, kt) for hh in range(RW_PACK)]
        st = st_ref[p]
        m = jnp.concatenate(rows + [st], axis=0).astype(BF16)
        pr.append(_dot_nt(lhs, m))
        vs.append(v_ref[:, sl].astype(F32))
        sts.append(st)
        kkas.append(kka)
        ks.append(k)
        wcs.append(wc)
    bd_vs = [bdiag(vs[p]) for p in packs]
    us = [pr[p][:L, 2 * pw:] + _dot(jnp.where(strict, pr[p][:L, pw:2 * pw], 0.0).astype(BF16), bd_vs[p]) for p in packs]
    nmats = [jnp.where(strict, pr[p][:L, 0:pw], 0.0) for p in packs]
    n_steps = L.bit_length() - 1
    for it in range(n_steps):
        us = [us[p] + _dot(nmats[p].astype(BF16), bdiag(us[p])) for p in packs]
        if it + 1 < n_steps:
            nmats = [_dot(nmats[p].astype(BF16), bdiag(nmats[p])) for p in packs]
    for p in packs:
        a_rb = jnp.where(incl, pr[p][L:, 0:pw], 0.0)
        a_rk = jnp.where(incl, pr[p][L:, pw:2 * pw], 0.0)
        y_ref[:, sls[p]] = pr[p][L:, 2 * pw:] + _dot(jnp.concatenate([a_rb, a_rk], axis=1).astype(BF16),
                                                     jnp.concatenate([bdiag(us[p]), bd_vs[p]], axis=0))
    for p in packs:
        w_last = wcs[p][L - 1:L, :]
        e_last = jnp.exp(w_last - wcs[p])
        upd = _dot_tn(jnp.concatenate([us[p], vs[p]], axis=0).astype(BF16),
                      jnp.concatenate([kkas[p] * e_last, ks[p] * e_last], axis=0).astype(BF16))
        st_ref[p] = jnp.where(bd_mask, sts[p] * jnp.exp(w_last) + upd, 0.0)


def _rwkv_scan(r, lw, k, v, kk, a, bsz, seq):
    t, d = r.shape
    ns = seq // RW_CHUNK
    pw = RW_PACK * RW_N
    row = pl.BlockSpec((RW_CHUNK, d), lambda b, s: (b * ns + s, 0))
    return pl.pallas_call(
        _rwkv_scan_kernel,
        grid=(bsz, ns),
        in_specs=[row] * 6,
        out_specs=row,
        out_shape=jax.ShapeDtypeStruct((t, d), F32),
        scratch_shapes=[pltpu.VMEM((d // pw, pw, pw), F32)],
        compiler_params=_cparams(("arbitrary", "arbitrary")),
        name="rwkv_scan",
    )(r, lw, k, v, kk, a)


def _moe_block(h, top_i, top_w, x2, gate, w1, w3, w2, nw, shift, scale, bsz, seq, final):
    n_exp = w1.shape[0]
    pos, row_tok, tile_exp, n_used = _route_plan(top_i, n_exp, MOE_TM)
    y_sorted = _moe(h, w1, w3, w2, row_tok, tile_exp, n_used)
    return _combine(pos, y_sorted, x2, top_w, gate, nw, shift, scale, bsz, seq, final)


def kernel(x, c, mod_w, mod_b, norm_w, hg_lb_logits, ev_w_in, ev_hg_norm, ev_conv_w, ev_conv_b, ev_dt_bias, ev_a_log, ev_d_skip, ev_ssm_norm, ev_w_out, od_mu, od_w_rkv, od_w_dec0, od_w_dec1, od_w_dec2, od_a0, od_a1, od_a2, od_g1, od_g2, od_k_k, od_k_a, od_r_k, od_ln_w, od_ln_b, od_w_o, router_w, router_b, moe_w1, moe_w3, moe_w2, final_norm_w):
    bsz, seq, d = x.shape
    depth = mod_w.shape[0]
    t = bsz * seq
    x2 = x.reshape(t, d)
    mod = _adaln_mod(c, mod_w, mod_b)
    gamma = jax.nn.softmax(hg_lb_logits.astype(F32), axis=0)
    lower_bounds = jnp.cumsum(gamma, axis=0) - gamma[0]
    n_rw_heads = d // RW_N
    head_of_lane = jnp.arange(d, dtype=I32) // RW_N
    hs = (head_of_lane[:, None] == jnp.arange(LANES, dtype=I32)[None, :]).astype(BF16)
    he = hs.T

    h = None
    out = None
    for l in range(depth):
        sh_m, sc_m, gt_m, sh_f, sc_f, gt_f = [mod[l, :, i * d:(i + 1) * d] for i in range(6)]
        j = l // 2
        if h is None:
            h = _norm_mod(x2, norm_w[l, 0], sh_m, sc_m, bsz, seq)
        if l % 2 == 0:
            w_in = ev_w_in[j]
            hgw = ev_hg_norm.shape[1]
            sw = ev_ssm_norm.shape[1]
            xbw = ev_conv_w.shape[2]
            nh = ev_dt_bias.shape[1]
            c0 = 4 * hgw
            w_hg = w_in[:, :c0].astype(BF16)
            w_ssd = jnp.zeros((d, sw + xbw + LANES), F32).at[:, :sw + xbw + nh].set(w_in[:, c0:]).astype(BF16)
            o_a = _hgrn2(h, w_hg, lower_bounds[l + 1], ev_hg_norm[j], bsz, seq)
            o_b = _ssd(h, w_ssd, ev_conv_w[j], ev_conv_b[j], ev_dt_bias[j], ev_a_log[j], ev_d_skip[j],
                       ev_ssm_norm[j], bsz, seq)
            w_out = ev_w_out[j].astype(BF16)
            x2, hf, top_i, top_w = _epilogue([o_a, o_b], [w_out[:hgw], w_out[hgw:]], x2, gt_m, norm_w[l, 1], sh_f, sc_f,
                                             router_w, router_b, bsz, seq)
        else:
            r, lw, k, v, kk, a, g, bonus = _rwkv_prep(h, od_mu[j], od_w_rkv[j], od_w_dec0[j], od_w_dec1[j], od_w_dec2[j],
                                                      od_a0[j], od_a1[j], od_a2[j], od_g1[j], od_g2[j], od_k_k[j],
                                                      od_k_a[j], od_r_k[j].reshape(-1), hs, he, bsz, seq)
            y = _rwkv_scan(r, lw, k, v, kk, a, bsz, seq)
            x2, hf, top_i, top_w = _epilogue([y], [od_w_o[j].astype(BF16)], x2, gt_m, norm_w[l, 1], sh_f, sc_f,
                                             router_w, router_b, bsz, seq,
                                             rwkv_extra=(g, bonus, od_ln_w[j], od_ln_b[j], hs, he))
        final = l == depth - 1
        if final:
            nw_next, sh_next, sc_next = final_norm_w, sh_f, sc_f
        else:
            nxt = [mod[l + 1, :, i * d:(i + 1) * d] for i in range(2)]
            nw_next, sh_next, sc_next = norm_w[l + 1, 0], nxt[0], nxt[1]
        res = _moe_block(hf, top_i, top_w, x2, gt_f, moe_w1[l], moe_w3[l], moe_w2[l], nw_next, sh_next, sc_next,
                         bsz, seq, final)
        if final:
            out = res
        else:
            x2, h = res
    return out.reshape(bsz, seq, d)
```

```python
import functools

import jax
import jax.numpy as jnp
from jax import lax
from jax.experimental import pallas as pl
from jax.experimental.pallas import tpu as pltpu

F32 = jnp.float32
BF16 = jnp.bfloat16
I32 = jnp.int32

NORM_EPS = 1e-6
RW_GN_EPS = 64e-5
LANES = 128
SUBLANES = 8
VMEM_LIMIT = 56 * 1024 * 1024

HG_DK = 128
SSM_P = 64
SSM_N = 128
SSM_GROUPS = 2
SSM_CONV = 4
RW_N = 64
N_GROUPS_MOE = 4
TOP_K = 2

CHUNK = 128
RW_CHUNK = 64
RW_PACK = 4
MOE_TM = 512
ROW_TM = 256
EPILOGUE_SUBTILES = 2
GATHER_UNROLL = 8
ROW_PITCH = 9


def _cparams(sem):
    return pltpu.CompilerParams(dimension_semantics=sem, vmem_limit_bytes=VMEM_LIMIT)


def _dot(a, b):
    return lax.dot_general(a, b, (((1,), (0,)), ((), ())), preferred_element_type=F32)


def _dot_nt(a, b):
    return lax.dot_general(a, b, (((1,), (1,)), ((), ())), preferred_element_type=F32)


def _dot_tn(a, b):
    return lax.dot_general(a, b, (((0,), (0,)), ((), ())), preferred_element_type=F32)


def _split(x):
    hi = x.astype(BF16)
    return hi, (x - hi.astype(F32)).astype(BF16)


def _dot_sel(sel, x):
    hi, lo = _split(x)
    return _dot(sel, hi) + _dot(sel, lo)


def _dot_rsel(x, sel):
    hi, lo = _split(x)
    return _dot(hi, sel) + _dot(lo, sel)


def _sigmoid(x):
    return 1.0 / (1.0 + jnp.exp(-x))


def _silu(x):
    return x * _sigmoid(x)


def _softplus(x):
    return jnp.maximum(x, 0.0) + jnp.log(1.0 + jnp.exp(-jnp.abs(x)))


def _iota(shape, dim):
    return lax.broadcasted_iota(I32, shape, dim)


def _store_rows(ref, val):
    tm, width = val.shape
    for j in range(width // LANES):
        ref[pl.ds(j, tm, stride=ROW_PITCH), :] = val[:, j * LANES:(j + 1) * LANES]
    for j in range(width // LANES, ROW_PITCH):
        ref[pl.ds(j, tm, stride=ROW_PITCH), :] = jnp.zeros((tm, LANES), ref.dtype)


def _load_rows(ref, tm, width):
    return jnp.concatenate([ref[pl.ds(j, tm, stride=ROW_PITCH), :] for j in range(width // LANES)], axis=1)


def _mod_kernel(c_ref, w_ref, b_ref, o_ref):
    c = c_ref[...]
    o_ref[0] = _dot(_silu(c).astype(BF16), w_ref[0].astype(BF16)) + b_ref[0]


def _adaln_mod(c, mod_w, mod_b):
    depth, d, width = mod_w.shape
    bsz = c.shape[0]
    c_pad = jnp.zeros((SUBLANES, d), F32).at[:bsz].set(c)
    tn = 1536
    out = pl.pallas_call(
        _mod_kernel,
        grid=(depth, width // tn),
        in_specs=[pl.BlockSpec((SUBLANES, d), lambda l, j: (0, 0)),
                  pl.BlockSpec((1, d, tn), lambda l, j: (l, 0, j)),
                  pl.BlockSpec((1, 1, tn), lambda l, j: (l, 0, j))],
        out_specs=pl.BlockSpec((1, SUBLANES, tn), lambda l, j: (l, 0, j)),
        out_shape=jax.ShapeDtypeStruct((depth, SUBLANES, width), F32),
        compiler_params=_cparams(("arbitrary", "arbitrary")),
        name="adaln_mod",
    )(c_pad, mod_w, mod_b.reshape(depth, 1, width))
    return out[:, :bsz]


def _rms_mod(x, nw, shift, scale):
    y = x * lax.rsqrt(jnp.mean(x * x, axis=-1, keepdims=True) + NORM_EPS) * nw
    return y * (1.0 + scale) + shift


def _normmod_kernel(x_ref, nw_ref, sh_ref, sc_ref, h_ref):
    h_ref[...] = _rms_mod(x_ref[...], nw_ref[...], sh_ref[0], sc_ref[0]).astype(h_ref.dtype)


def _norm_mod(x2, nw, shift, scale, bsz, seq):
    t, d = x2.shape
    tm = ROW_TM
    ns = seq // tm
    return pl.pallas_call(
        _normmod_kernel,
        grid=(bsz, ns),
        in_specs=[pl.BlockSpec((tm, d), lambda b, s: (b * ns + s, 0)),
                  pl.BlockSpec((1, d), lambda b, s: (0, 0)),
                  pl.BlockSpec((1, 1, d), lambda b, s: (b, 0, 0)),
                  pl.BlockSpec((1, 1, d), lambda b, s: (b, 0, 0))],
        out_specs=pl.BlockSpec((tm, d), lambda b, s: (b * ns + s, 0)),
        out_shape=jax.ShapeDtypeStruct((t, d), BF16),
        compiler_params=_cparams(("arbitrary", "arbitrary")),
        name="norm_mod",
    )(x2, nw.reshape(1, d), shift.reshape(bsz, 1, d), scale.reshape(bsz, 1, d))


def _block_mid_ref(b, n2):
    rows, width = b.shape
    n = n2 // 2
    if n2 >= 2 * SUBLANES:
        b3 = b.reshape(rows // n2, n2, width)
        return jnp.broadcast_to(b3[:, n - 1:n, :], b3.shape).reshape(rows, width)
    b3 = b.reshape(rows // SUBLANES, SUBLANES, width)
    sub = _iota(b3.shape, 1)
    r3 = jnp.broadcast_to(b3[:, SUBLANES - n2 + n - 1:SUBLANES - n2 + n, :], b3.shape)
    for g in range(SUBLANES // n2 - 2, -1, -1):
        r3 = jnp.where(sub < (g + 1) * n2, b3[:, g * n2 + n - 1:g * n2 + n, :], r3)
    return r3.reshape(rows, width)


def _hgrn2_kernel(h_ref, w_ref, lb_ref, nw_ref, o_ref, proj_ref, st_ref):
    L = CHUNK
    dk = HG_DK
    n_heads = st_ref.shape[0]
    width = n_heads * dk

    @pl.when(pl.program_id(1) == 0)
    def _():
        st_ref[...] = jnp.zeros_like(st_ref)

    proj_ref[...] = _dot(h_ref[...], w_ref[...])
    lb = lb_ref[...]
    f = lb + (1.0 - lb) * _sigmoid(proj_ref[:, width:2 * width])
    logf = jnp.log(f)
    t_i = _iota((L, L), 0)
    s_i = _iota((L, L), 1)
    tril = (t_i >= s_i).astype(BF16)
    b_all = _dot_sel(tril, logf)
    eye = t_i == s_i

    levels = []
    n2 = L
    while n2 >= 2:
        n = n2 // 2
        sh = n2.bit_length() - 1
        m = ((t_i >> sh) == (s_i >> sh)) & ((t_i & (n2 - 1)) >= n) & ((s_i & (n2 - 1)) < n)
        levels.append((n2, m))
        n2 = n

    for h in range(n_heads):
        sl = slice(h * dk, (h + 1) * dk)
        q = proj_ref[:, sl]
        k = 1.0 - f[:, sl]
        v = proj_ref[:, 2 * width + h * dk:2 * width + (h + 1) * dk]
        b = b_all[:, sl]
        st = st_ref[h]
        b_last = b[L - 1:L, :]
        o = _dot_nt((q * jnp.exp(b)).astype(BF16), st.astype(BF16))
        a = jnp.where(eye, jnp.sum(q * k, axis=-1, keepdims=True), 0.0)
        for n2, m in levels:
            e = jnp.exp(-jnp.abs(b - _block_mid_ref(b, n2)))
            a = jnp.where(m, _dot_nt((q * e).astype(BF16), (k * e).astype(BF16)), a)
        o = o + _dot(a.astype(BF16), v.astype(BF16))
        ke = k * jnp.exp(b_last - b)
        st_ref[h] = st * jnp.exp(b_last) + _dot_tn(v.astype(BF16), ke.astype(BF16))
        g = proj_ref[:, 3 * width + h * dk:3 * width + (h + 1) * dk]
        ms = jnp.mean(o * o, axis=-1, keepdims=True)
        o_ref[:, sl] = (o * lax.rsqrt(ms + NORM_EPS) * nw_ref[:, sl] * _silu(g)).astype(o_ref.dtype)


def _hgrn2(h, w_hg, lb, hg_norm, bsz, seq):
    t, d = h.shape
    width = lb.shape[0]
    n_heads = width // HG_DK
    ns = seq // CHUNK
    return pl.pallas_call(
        _hgrn2_kernel,
        grid=(bsz, ns),
        in_specs=[pl.BlockSpec((CHUNK, d), lambda b, s: (b * ns + s, 0)),
                  pl.BlockSpec(w_hg.shape, lambda b, s: (0, 0)),
                  pl.BlockSpec((1, width), lambda b, s: (0, 0)),
                  pl.BlockSpec((1, width), lambda b, s: (0, 0))],
        out_specs=pl.BlockSpec((CHUNK, width), lambda b, s: (b * ns + s, 0)),
        out_shape=jax.ShapeDtypeStruct((t, width), BF16),
        scratch_shapes=[pltpu.VMEM((CHUNK, w_hg.shape[1]), F32), pltpu.VMEM((n_heads, HG_DK, HG_DK), F32)],
        compiler_params=_cparams(("arbitrary", "arbitrary")),
        name="hgrn2_scan",
    )(h, w_hg, lb.reshape(1, width), hg_norm.reshape(1, width))


def _ssd_kernel(h_ref, w_ref, cw_ref, cb_ref, dtb_ref, a_ref, dsk_ref, nw_ref, o_ref,
                proj_ref, carry_ref, xpad_ref, st_ref):
    L = CHUNK
    width = o_ref.shape[1]
    xw = cw_ref.shape[1]
    n_heads = width // SSM_P
    gw = width // SSM_GROUPS
    heads_per_group = n_heads // SSM_GROUPS

    @pl.when(pl.program_id(1) == 0)
    def _():
        carry_ref[...] = jnp.zeros_like(carry_ref)
        st_ref[...] = jnp.zeros_like(st_ref)

    proj_ref[...] = _dot(h_ref[...], w_ref[...])
    z_ref = proj_ref.at[:, 0:width]
    dt_ref = proj_ref.at[:, width + xw:]
    xraw = proj_ref[:, width:width + xw]
    xpad_ref[0:SUBLANES, :] = carry_ref[...]
    xpad_ref[SUBLANES:SUBLANES + L, :] = xraw
    carry_ref[...] = xraw[L - SUBLANES:L, :]
    acc = cb_ref[...] + jnp.zeros_like(xraw)
    for j in range(SSM_CONV):
        acc = acc + cw_ref[j:j + 1, :] * xpad_ref[pl.ds(SUBLANES - (SSM_CONV - 1) + j, L), :]
    xc = _silu(acc)
    xs = xc[:, :width]
    bm = xc[:, width:width + SSM_GROUPS * SSM_N]
    cm = xc[:, width + SSM_GROUPS * SSM_N:]

    dt = _softplus(dt_ref[...] + dtb_ref[...])
    da = dt * a_ref[...]
    da_t = da.T
    dt_t = dt.T
    t_i = _iota((L, L), 0)
    s_i = _iota((L, L), 1)
    triu = (t_i <= s_i).astype(BF16)
    acs_t = _dot_rsel(da_t, triu)
    causal = t_i >= s_i
    diag = t_i == s_i
    lane_lo = _iota((L, 2 * SSM_P), 1) < SSM_P
    bd_mask = (_iota((2 * L, 2 * SSM_P), 0) < L) == (_iota((2 * L, 2 * SSM_P), 1) < SSM_P)

    y_pairs = []
    for g in range(SSM_GROUPS):
        bg = bm[:, g * SSM_N:(g + 1) * SSM_N]
        cg = cm[:, g * SSM_N:(g + 1) * SSM_N]
        cb = _dot_nt(cg.astype(BF16), bg.astype(BF16))
        hg = st_ref[:, g * gw:(g + 1) * gw]
        yoff_g = _dot(cg.astype(BF16), hg.astype(BF16))
        xsc_parts, decay_parts = [], []
        for pr in range(heads_per_group // 2):
            j0 = g * heads_per_group + 2 * pr
            gs, ds, ecol, elast = [], [], [], []
            for j in (j0, j0 + 1):
                row_b = jnp.broadcast_to(acs_t[j:j + 1, :], (L, L))
                col_b = row_b.T
                dt_row = jnp.broadcast_to(dt_t[j:j + 1, :], (L, L))
                lmat = jnp.exp(jnp.minimum(col_b - row_b, 0.0))
                gs.append(jnp.where(causal, cb * lmat * dt_row, 0.0))
                a_last = acs_t[j:j + 1, L - 1:L]
                ds.append(jnp.where(diag, jnp.exp(a_last - row_b) * dt_row, 0.0))
                ecol.append(jnp.exp(col_b))
                elast.append(jnp.exp(a_last))
            lhs = jnp.concatenate([jnp.concatenate(gs, axis=1), jnp.concatenate(ds, axis=1)], axis=0)
            xs_pair = xs[:, j0 * SSM_P:(j0 + 2) * SSM_P]
            bd = jnp.where(bd_mask, jnp.concatenate([xs_pair, xs_pair], axis=0), 0.0)
            res = _dot(lhs.astype(BF16), bd.astype(BF16))
            yoff = yoff_g[:, pr * 2 * SSM_P:(pr + 1) * 2 * SSM_P] * jnp.where(lane_lo, ecol[0], ecol[1])
            y_pairs.append(res[:L] + yoff)
            xsc_parts.append(res[L:])
            decay_parts.append(jnp.where(lane_lo[0:1], elast[0], elast[1]))
        xsc_g = jnp.concatenate(xsc_parts, axis=1)
        decay_g = jnp.concatenate(decay_parts, axis=1)
        st_ref[:, g * gw:(g + 1) * gw] = hg * decay_g + _dot_tn(bg.astype(BF16), xsc_g.astype(BF16))
    y = jnp.concatenate(y_pairs, axis=1) + dsk_ref[...] * xs
    yz = y * _silu(z_ref[...])
    for g in range(SSM_GROUPS):
        seg = yz[:, g * gw:(g + 1) * gw]
        ms = jnp.mean(seg * seg, axis=-1, keepdims=True)
        o_ref[:, g * gw:(g + 1) * gw] = (seg * lax.rsqrt(ms + NORM_EPS) * nw_ref[:, g * gw:(g + 1) * gw]).astype(o_ref.dtype)


def _ssd(h, w_ssd, conv_w, conv_b, dt_bias, a_log, d_skip, ssm_norm, bsz, seq):
    t, d = h.shape
    width = ssm_norm.shape[0]
    xw = conv_w.shape[1]
    n_heads = width // SSM_P
    ns = seq // CHUNK
    pad = lambda v: jnp.zeros((1, LANES), F32).at[0, :n_heads].set(v)
    row = lambda w: pl.BlockSpec((CHUNK, w), lambda b, s: (b * ns + s, 0))
    const = lambda r, w: pl.BlockSpec((r, w), lambda b, s: (0, 0))
    return pl.pallas_call(
        _ssd_kernel,
        grid=(bsz, ns),
        in_specs=[row(d), const(*w_ssd.shape), const(SSM_CONV, xw), const(1, xw), const(1, LANES),
                  const(1, LANES), const(1, width), const(1, width)],
        out_specs=row(width),
        out_shape=jax.ShapeDtypeStruct((t, width), BF16),
        scratch_shapes=[pltpu.VMEM((CHUNK, w_ssd.shape[1]), F32), pltpu.VMEM((SUBLANES, xw), F32),
                        pltpu.VMEM((CHUNK + SUBLANES, xw), F32), pltpu.VMEM((SSM_N, width), F32)],
        compiler_params=_cparams(("arbitrary", "arbitrary")),
        name="ssd_scan",
    )(h, w_ssd, conv_w, conv_b.reshape(1, xw), pad(dt_bias), pad(-jnp.exp(a_log)),
      jnp.repeat(d_skip, SSM_P).reshape(1, width), ssm_norm.reshape(1, width))


def _route(probs):
    n_exp = N_GROUPS_MOE * 4
    p = [probs[e:e + 1, :] for e in range(n_exp)]
    gs = []
    for g in range(N_GROUPS_MOE):
        a, b, c, d = p[4 * g:4 * g + 4]
        gs.append(jnp.maximum(jnp.maximum(jnp.maximum(a + b, a + c), jnp.maximum(a + d, b + c)),
                              jnp.maximum(b + d, c + d)))
    best = jnp.zeros_like(gs[0]).astype(I32)
    bs = gs[0]
    for g in range(1, N_GROUPS_MOE):
        upd = gs[g] > bs
        best = jnp.where(upd, g, best)
        bs = jnp.where(upd, gs[g], bs)
    q = [jnp.where(best == 0, p[i], jnp.where(best == 1, p[4 + i], jnp.where(best == 2, p[8 + i], p[12 + i])))
         for i in range(4)]
    i1 = jnp.zeros_like(best)
    v1 = q[0]
    for i in range(1, 4):
        upd = q[i] > v1
        i1 = jnp.where(upd, i, i1)
        v1 = jnp.where(upd, q[i], v1)
    i2 = jnp.zeros_like(best)
    v2 = jnp.full_like(v1, -1.0)
    for i in range(4):
        upd = (i1 != i) & (q[i] > v2)
        i2 = jnp.where(upd, i, i2)
        v2 = jnp.where(upd, q[i], v2)
    den = v1 + v2
    return best * 4 + i1, best * 4 + i2, v1 / den, v2 / den


def _head_stats_expand(stack, seg_ref):
    pw = seg_ref.shape[0]
    seg = seg_ref[...]
    return jnp.concatenate([_dot_rsel(stack[:, p * pw:(p + 1) * pw], seg) for p in range(stack.shape[1] // pw)], axis=1)


def _epilogue_kernel(*refs, n_in, rwkv):
    ins = refs[:n_in]
    pos = n_in
    w_refs = refs[pos:pos + n_in]
    pos += n_in
    if rwkv:
        g_ref, bonus_ref, lnw_ref, lnb_ref, seg_ref = refs[pos:pos + 5]
        pos += 5
    x_ref, gate_ref, nw_ref, sh_ref, sc_ref, rw_ref, rb_ref = refs[pos:pos + 7]
    pos += 7
    xo_ref, h_ref, ti_ref, tw_ref = refs[pos:pos + 4]

    tm = ROW_TM
    rw_hi, rw_lo = _split(rw_ref[...])
    for sub in range(x_ref.shape[0] // tm):
        rs = pl.ds(sub * tm, tm)
        if rwkv:
            o = ins[0][rs, :]
            inv_n = 1.0 / RW_N
            st = _head_stats_expand(jnp.concatenate([o, o * o], axis=0), seg_ref) * inv_n
            mean = st[:tm]
            var = jnp.maximum(st[tm:] - mean * mean, 0.0)
            o = (o - mean) * lax.rsqrt(var + RW_GN_EPS) * lnw_ref[...] + lnb_ref[...]
            o = (o + bonus_ref[rs, :].astype(F32)) * g_ref[rs, :].astype(F32)
            y = _dot(o.astype(BF16), w_refs[0][...])
        else:
            y = _dot(ins[0][rs, :], w_refs[0][...])
            for a_ref, w_ref in zip(ins[1:], w_refs[1:]):
                y = y + _dot(a_ref[rs, :], w_ref[...])
        x_new = x_ref[rs, :] + gate_ref[0] * y
        xo_ref[rs, :] = x_new
        h = _rms_mod(x_new, nw_ref[...], sh_ref[0], sc_ref[0])
        _store_rows(h_ref.at[pl.ds(sub * tm * ROW_PITCH, tm * ROW_PITCH)], h)
        h_hi, h_lo = _split(h)
        logits = _dot_nt(rw_hi, h_hi) + _dot_nt(rw_hi, h_lo) + _dot_nt(rw_lo, h_hi) + rb_ref[...]
        mx = jnp.max(logits, axis=0, keepdims=True)
        ex = jnp.exp(logits - mx)
        probs = ex / jnp.sum(ex, axis=0, keepdims=True)
        e1, e2, w1, w2 = _route(probs)
        zi = jnp.zeros((SUBLANES - TOP_K, tm), I32)
        ti_ref[:, sub * tm:(sub + 1) * tm] = jnp.concatenate([e1, e2, zi], axis=0)
        tw_ref[:, sub * tm:(sub + 1) * tm] = jnp.concatenate([w1, w2, zi.astype(F32)], axis=0)


def _epilogue(ins, ws, x2, gate, nw, shift, scale, router_w, router_b, bsz, seq, rwkv_extra=None):
    t, d = x2.shape
    tm = EPILOGUE_SUBTILES * ROW_TM
    ns = seq // tm
    n_exp = router_w.shape[1]
    rw_t = jnp.zeros((LANES, d), F32).at[:n_exp].set(router_w.T)
    rb = jnp.full((LANES, 1), -1e30, F32).at[:n_exp, 0].set(router_b)
    row = lambda w: pl.BlockSpec((tm, w), lambda b, s: (b * ns + s, 0))
    const = lambda a: pl.BlockSpec(a.shape, lambda b, s: (0,) * a.ndim)
    per_b = pl.BlockSpec((1, 1, d), lambda b, s: (b, 0, 0))
    args = list(ins) + list(ws)
    specs = [row(a.shape[1]) for a in ins] + [const(w) for w in ws]
    if rwkv_extra is not None:
        g, bonus, lnw, lnb, seg = rwkv_extra
        args += [g, bonus, lnw.reshape(1, d), lnb.reshape(1, d), seg]
        specs += [row(d), row(d), pl.BlockSpec((1, d), lambda b, s: (0, 0)), pl.BlockSpec((1, d), lambda b, s: (0, 0)),
                  const(seg)]
    args += [x2, gate.reshape(bsz, 1, d), nw.reshape(1, d), shift.reshape(bsz, 1, d), scale.reshape(bsz, 1, d), rw_t, rb]
    specs += [row(d), per_b, pl.BlockSpec((1, d), lambda b, s: (0, 0)), per_b, per_b, const(rw_t), const(rb)]
    lane_row = pl.BlockSpec((SUBLANES, tm), lambda b, s: (0, b * ns + s))
    return pl.pallas_call(
        functools.partial(_epilogue_kernel, n_in=len(ins), rwkv=rwkv_extra is not None),
        grid=(bsz, ns),
        in_specs=specs,
        out_specs=[row(d), pl.BlockSpec((tm * ROW_PITCH, LANES), lambda b, s: (b * ns + s, 0)), lane_row, lane_row],
        out_shape=[jax.ShapeDtypeStruct((t, d), F32), jax.ShapeDtypeStruct((t * ROW_PITCH, LANES), F32),
                   jax.ShapeDtypeStruct((SUBLANES, t), I32), jax.ShapeDtypeStruct((SUBLANES, t), F32)],
        compiler_params=_cparams(("arbitrary", "arbitrary")),
        name="mixer_epilogue",
    )(*args)


def _route_plan(top_i, n_exp, tm):
    t = top_i.shape[1]
    n_pairs = TOP_K * t
    n_tiles = n_pairs // tm + n_exp
    e_flat = top_i[:TOP_K].reshape(-1)
    onehot = (e_flat[:, None] == jnp.arange(n_exp, dtype=I32)[None, :]).astype(I32)
    csum = jnp.cumsum(onehot, axis=0)
    counts = csum[-1]
    rank = jnp.sum((csum - 1) * onehot, axis=1)
    padded = ((counts + tm - 1) // tm) * tm
    ends = jnp.cumsum(padded)
    pos = (ends - padded)[e_flat] + rank
    tile_start = jnp.arange(n_tiles, dtype=I32) * tm
    tile_exp = jnp.minimum(jnp.sum((tile_start[:, None] >= ends[None, :]).astype(I32), axis=1), n_exp - 1)
    n_used = (ends[-1] // tm).reshape(1).astype(I32)
    return pos.astype(I32), tile_exp.astype(I32), n_used


def _moe_kernel(texp_ref, nused_ref, pos_ref, h_hbm, w1_ref, w3_ref, w2_ref, y_ref, xbuf, wb1, wb3, wb2, rtok_ref, sem):
    tm = MOE_TM
    d = wb1.shape[0]
    n_sub = d // LANES
    n_tok = pos_ref.shape[0] // TOP_K
    i = pl.program_id(0)
    n_used = nused_ref[0]
    slot = i % 2

    @pl.when(i == 0)
    def _():
        def zero(j, c):
            rtok_ref[j] = 0
            return c
        lax.fori_loop(0, rtok_ref.shape[0], zero, 0, unroll=16)

        for k in range(TOP_K):
            def fill(t, c):
                rtok_ref[pos_ref[k * n_tok + t]] = t * ROW_PITCH
                return c
            lax.fori_loop(0, n_tok, fill, 0, unroll=8)

    def start_gather(tile, sl):
        def body(r8, c):
            for u in range(GATHER_UNROLL):
                r = r8 * GATHER_UNROLL + u
                pltpu.make_async_copy(h_hbm.at[pl.ds(rtok_ref[tile * tm + r], n_sub)],
                                      xbuf.at[sl, pl.ds(r * ROW_PITCH, n_sub)], sem.at[sl]).start(priority=u % 2)
            return c
        lax.fori_loop(0, tm // GATHER_UNROLL, body, 0)

    @pl.when(i == 0)
    def _():
        start_gather(0, 0)

    @pl.when(i + 1 < n_used)
    def _():
        start_gather(i + 1, 1 - slot)

    @pl.when((i == 0) | (texp_ref[i] != texp_ref[jnp.maximum(i - 1, 0)]))
    def _():
        wb1[...] = w1_ref[0, 0].astype(BF16)
        wb3[...] = w3_ref[0, 0].astype(BF16)
        wb2[...] = w2_ref[0, 0].astype(BF16)

    @pl.when(i < n_used)
    def _():
        pltpu.make_async_copy(h_hbm.at[pl.ds(0, tm * n_sub)], xbuf.at[slot, pl.ds(0, tm * n_sub)], sem.at[slot]).wait()
        x = _load_rows(xbuf.at[slot], tm, d).astype(BF16)
        a = _dot(x, wb1[...])
        b = _dot(x, wb3[...])
        hid = (_silu(a) * b).astype(BF16)
        _store_rows(y_ref, _dot(hid, wb2[...]))

    @pl.when(i >= n_used)
    def _():
        y_ref[...] = jnp.zeros_like(y_ref)


def _moe(h, w1, w3, w2, layer, pos, tile_exp, n_used):
    _, n_exp, d, dff = w1.shape
    tm = MOE_TM
    n_tiles = tile_exp.shape[0]
    grid_spec = pltpu.PrefetchScalarGridSpec(
        num_scalar_prefetch=3,
        grid=(n_tiles,),
        in_specs=[pl.BlockSpec(memory_space=pl.ANY),
                  pl.BlockSpec((1, 1, d, dff), lambda i, te, nu, rt: (layer, te[i], 0, 0)),
                  pl.BlockSpec((1, 1, d, dff), lambda i, te, nu, rt: (layer, te[i], 0, 0)),
                  pl.BlockSpec((1, 1, dff, d), lambda i, te, nu, rt: (layer, te[i], 0, 0))],
        out_specs=pl.BlockSpec((tm * ROW_PITCH, LANES), lambda i, te, nu, rt: (i, 0)),
        scratch_shapes=[pltpu.VMEM((2, tm * ROW_PITCH, LANES), F32), pltpu.VMEM((d, dff), BF16), pltpu.VMEM((d, dff), BF16),
                        pltpu.VMEM((dff, d), BF16), pltpu.SMEM((n_tiles * tm,), I32), pltpu.SemaphoreType.DMA((2,))],
    )
    return pl.pallas_call(
        _moe_kernel,
        grid_spec=grid_spec,
        out_shape=jax.ShapeDtypeStruct((n_tiles * tm * ROW_PITCH, LANES), F32),
        compiler_params=_cparams(("arbitrary",)),
        name="moe_experts",
    )(tile_exp, n_used, pos, h, w1, w3, w2)


def _combine_kernel(pos_ref, y_hbm, x_ref, tw_ref, gate_ref, nw_ref, sh_ref, sc_ref, *out_and_scratch, final, n_tok):
    if final:
        o_ref, ybuf, sem = out_and_scratch
    else:
        xo_ref, h_ref, ybuf, sem = out_and_scratch
    tm, d = x_ref.shape
    n_sub = d // LANES
    i = pl.program_id(0)
    n_steps = pl.num_programs(0)
    slot = i % 2

    def start_gather(tile, sl):
        def body(r8, c):
            for u in range(GATHER_UNROLL):
                r = r8 * GATHER_UNROLL + u
                for k in range(TOP_K):
                    src = pos_ref[k * n_tok + tile * tm + r] * ROW_PITCH
                    pltpu.make_async_copy(y_hbm.at[pl.ds(src, n_sub)], ybuf.at[sl, k, pl.ds(r * ROW_PITCH, n_sub)],
                                          sem.at[sl]).start(priority=k)
            return c
        lax.fori_loop(0, tm // GATHER_UNROLL, body, 0)

    @pl.when(i == 0)
    def _():
        start_gather(0, 0)

    @pl.when(i + 1 < n_steps)
    def _():
        start_gather(i + 1, 1 - slot)

    for k in range(TOP_K):
        pltpu.make_async_copy(y_hbm.at[pl.ds(0, tm * n_sub)], ybuf.at[slot, k, pl.ds(0, tm * n_sub)], sem.at[slot]).wait()
    tw = tw_ref[...]
    moe = tw[:, 0:1] * _load_rows(ybuf.at[slot, 0], tm, d) + tw[:, 1:2] * _load_rows(ybuf.at[slot, 1], tm, d)
    x_new = x_ref[...] + gate_ref[0] * moe
    if final:
        o_ref[...] = x_new * lax.rsqrt(jnp.mean(x_new * x_new, axis=-1, keepdims=True) + NORM_EPS) * nw_ref[...]
    else:
        xo_ref[...] = x_new
        h_ref[...] = _rms_mod(x_new, nw_ref[...], sh_ref[0], sc_ref[0]).astype(h_ref.dtype)


def _combine(pos, y_sorted, x2, top_w, gate, nw, shift, scale, bsz, seq, final):
    t, d = x2.shape
    tm = ROW_TM
    ns = seq // tm
    tw = top_w.T
    row = pl.BlockSpec((tm, d), lambda i, p: (i, 0))
    per_b = pl.BlockSpec((1, 1, d), lambda i, p: (i // ns, 0, 0))
    grid_spec = pltpu.PrefetchScalarGridSpec(
        num_scalar_prefetch=1,
        grid=(t // tm,),
        in_specs=[pl.BlockSpec(memory_space=pl.ANY), row,
                  pl.BlockSpec((tm, SUBLANES), lambda i, p: (i, 0)),
                  per_b, pl.BlockSpec((1, d), lambda i, p: (0, 0)), per_b, per_b],
        out_specs=row if final else [row, row],
        scratch_shapes=[pltpu.VMEM((2, TOP_K, tm * ROW_PITCH, LANES), F32), pltpu.SemaphoreType.DMA((2,))],
    )
    out_shape = (jax.ShapeDtypeStruct((t, d), F32) if final else
                 [jax.ShapeDtypeStruct((t, d), F32), jax.ShapeDtypeStruct((t, d), BF16)])
    return pl.pallas_call(
        functools.partial(_combine_kernel, final=final, n_tok=t),
        grid_spec=grid_spec,
        out_shape=out_shape,
        compiler_params=_cparams(("arbitrary",)),
        name="moe_combine",
    )(pos, y_sorted, x2, tw, gate.reshape(bsz, 1, d), nw.reshape(1, d), shift.reshape(bsz, 1, d), scale.reshape(bsz, 1, d))


def _rwkv_prep_kernel(h_ref, mu_ref, wr_ref, wk_ref, wv_ref, dec0_ref, dec1_ref, dec2_ref, a0_ref, a1_ref, a2_ref,
                      g1_ref, g2_ref, kk_ref, ka_ref, rk_ref, seg_ref,
                      r_out, lw_out, k_out, v_out, kkn_out, a_out, g_out, bonus_out, carry_ref, hp_ref):
    tm = h_ref.shape[0]

    @pl.when(pl.program_id(1) == 0)
    def _():
        carry_ref[...] = jnp.zeros_like(carry_ref)

    h = h_ref[...].astype(F32)
    hp_ref[0:SUBLANES, :] = carry_ref[...]
    hp_ref[SUBLANES:SUBLANES + tm, :] = h
    carry_ref[...] = h[tm - SUBLANES:tm, :]
    xx = hp_ref[pl.ds(SUBLANES - 1, tm), :] - h
    mix = lambda i: (h + xx * mu_ref[i:i + 1, :]).astype(BF16)
    r = _dot(mix(0), wr_ref[...])
    k = _dot(mix(1), wk_ref[...])
    v = _dot(mix(2), wv_ref[...])
    wl = dec0_ref[...] + _dot(jnp.tanh(_dot(mix(3), dec1_ref[...])).astype(BF16), dec2_ref[...])
    lw = -jnp.exp(-_softplus(-wl) - 0.5)
    a = _sigmoid(a0_ref[...] + _dot(_dot(mix(4), a1_ref[...]).astype(BF16), a2_ref[...]))
    g = _dot(_sigmoid(_dot(mix(5), g1_ref[...])).astype(BF16), g2_ref[...])
    kk = k * kk_ref[...]
    k_h = k * (1.0 + (a - 1.0) * ka_ref[...])
    st = _head_stats_expand(jnp.concatenate([kk * kk, r * k_h * rk_ref[...]], axis=0), seg_ref)
    kkn = kk / jnp.maximum(jnp.sqrt(st[:tm]), 1e-12)
    r_out[...] = r.astype(r_out.dtype)
    lw_out[...] = lw
    k_out[...] = k_h.astype(k_out.dtype)
    v_out[...] = v.astype(v_out.dtype)
    kkn_out[...] = kkn.astype(kkn_out.dtype)
    a_out[...] = a.astype(a_out.dtype)
    g_out[...] = g.astype(g_out.dtype)
    bonus_out[...] = (st[tm:] * v).astype(bonus_out.dtype)


def _rwkv_prep(h, mu, w_rkv, dec0, dec1, dec2, a0, a1, a2, g1, g2, k_k, k_a, r_k, seg, bsz, seq):
    t, d = h.shape
    tm = ROW_TM
    ns = seq // tm
    padc = lambda w: jnp.zeros((d, LANES), F32).at[:, :w.shape[1]].set(w).astype(BF16)
    padr = lambda w: jnp.zeros((LANES, d), F32).at[:w.shape[0]].set(w).astype(BF16)
    vec = lambda v: v.reshape(1, d)
    args = [h, mu, w_rkv[0].astype(BF16), w_rkv[1].astype(BF16), w_rkv[2].astype(BF16), vec(dec0), padc(dec1), padr(dec2),
            vec(a0), padc(a1), padr(a2), padc(g1), padr(g2), vec(k_k), vec(k_a), vec(r_k), seg]
    row = pl.BlockSpec((tm, d), lambda b, s: (b * ns + s, 0))
    const = lambda a: pl.BlockSpec(a.shape, lambda b, s: (0,) * a.ndim)
    outs = [BF16, F32, BF16, BF16, BF16, BF16, BF16, BF16]
    return pl.pallas_call(
        _rwkv_prep_kernel,
        grid=(bsz, ns),
        in_specs=[row] + [const(a) for a in args[1:]],
        out_specs=[row] * len(outs),
        out_shape=[jax.ShapeDtypeStruct((t, d), dt) for dt in outs],
        scratch_shapes=[pltpu.VMEM((SUBLANES, d), F32), pltpu.VMEM((tm + SUBLANES, d), F32)],
        compiler_params=_cparams(("arbitrary", "arbitrary")),
        name="rwkv_prep",
    )(*args)


def _rwkv_scan_kernel(r_ref, lw_ref, k_ref, v_ref, kk_ref, a_ref, y_ref, st_ref):
    L = RW_CHUNK
    pw = RW_PACK * RW_N
    n_packs = st_ref.shape[0]
    sh = RW_N.bit_length() - 1

    @pl.when(pl.program_id(1) == 0)
    def _():
        st_ref[...] = jnp.zeros_like(st_ref)

    t_i = _iota((L, L), 0)
    s_i = _iota((L, L), 1)
    tril = (t_i >= s_i).astype(BF16)
    wc_all = _dot_sel(tril, lw_ref[...])
    lane_head = _iota((L, pw), 1) >> sh
    s_loc = _iota((L, pw), 1) & (RW_N - 1)
    t_loc = _iota((L, pw), 0)
    strict = s_loc < t_loc
    incl = s_loc <= t_loc
    bd_mask = (_iota((pw, pw), 0) >> sh) == (_iota((pw, pw), 1) >> sh)

    def bdiag(x):
        return jnp.where(bd_mask, jnp.concatenate([x] * RW_PACK, axis=0), 0.0).astype(BF16)

    packs = range(n_packs)
    sls = [slice(p * pw, (p + 1) * pw) for p in packs]
    pr, vs, sts, kkas, ks, wcs = [], [], [], [], [], []
    for p in packs:
        sl = sls[p]
        r = r_ref[:, sl].astype(F32)
        lw = lw_ref[:, sl]
        k = k_ref[:, sl].astype(F32)
        kk = kk_ref[:, sl].astype(F32)
        a = a_ref[:, sl].astype(F32)
        wc = wc_all[:, sl]
        e_inv = jnp.exp(-wc)
        kka = kk * a
        al = -kk * jnp.exp(wc - lw)
        rb = r * jnp.exp(wc)
        bt = kka * e_inv
        kt = k * e_inv
        lhs = jnp.concatenate([al, rb], axis=0).astype(BF16)
        rows = [jnp.where(lane_head == hh, x, 0.0) for x in (bt, kt) for hh in range(RW_PACK)]
        st = st_ref[p]
        m = jnp.concatenate(rows + [st], axis=0).astype(BF16)
        pr.append(_dot_nt(lhs, m))
        vs.append(v_ref[:, sl].astype(F32))
        sts.append(st)
        kkas.append(kka)
        ks.append(k)
        wcs.append(wc)
    bd_vs = [bdiag(vs[p]) for p in packs]
    us = [pr[p][:L, 2 * pw:] + _dot(jnp.where(strict, pr[p][:L, pw:2 * pw], 0.0).astype(BF16), bd_vs[p]) for p in packs]
    nmats = [jnp.where(strict, pr[p][:L, 0:pw], 0.0) for p in packs]
    n_steps = L.bit_length() - 1
    for it in range(n_steps):
        us = [us[p] + _dot(nmats[p].astype(BF16), bdiag(us[p])) for p in packs]
        if it + 1 < n_steps:
            nmats = [_dot(nmats[p].astype(BF16), bdiag(nmats[p])) for p in packs]
    for p in packs:
        a_rb = jnp.where(incl, pr[p][L:, 0:pw], 0.0)
        a_rk = jnp.where(incl, pr[p][L:, pw:2 * pw], 0.0)
        y_ref[:, sls[p]] = pr[p][L:, 2 * pw:] + _dot(jnp.concatenate([a_rb, a_rk], axis=1).astype(BF16),
                                                     jnp.concatenate([bdiag(us[p]), bd_vs[p]], axis=0))
    for p in packs:
        w_last = wcs[p][L - 1:L, :]
        e_last = jnp.exp(w_last - wcs[p])
        upd = _dot_tn(jnp.concatenate([us[p], vs[p]], axis=0).astype(BF16),
                      jnp.concatenate([kkas[p] * e_last, ks[p] * e_last], axis=0).astype(BF16))
        st_ref[p] = jnp.where(bd_mask, sts[p] * jnp.exp(w_last) + upd, 0.0)


def _rwkv_scan(r, lw, k, v, kk, a, bsz, seq):
    t, d = r.shape
    ns = seq // RW_CHUNK
    pw = RW_PACK * RW_N
    row = pl.BlockSpec((RW_CHUNK, d), lambda b, s: (b * ns + s, 0))
    return pl.pallas_call(
        _rwkv_scan_kernel,
        grid=(bsz, ns),
        in_specs=[row] * 6,
        out_specs=row,
        out_shape=jax.ShapeDtypeStruct((t, d), F32),
        scratch_shapes=[pltpu.VMEM((d // pw, pw, pw), F32)],
        compiler_params=_cparams(("arbitrary", "arbitrary")),
        name="rwkv_scan",
    )(r, lw, k, v, kk, a)


def _moe_block(h, top_i, top_w, x2, gate, w1, w3, w2, layer, nw, shift, scale, bsz, seq, final):
    n_exp = w1.shape[1]
    pos, tile_exp, n_used = _route_plan(top_i, n_exp, MOE_TM)
    y_sorted = _moe(h, w1, w3, w2, layer, pos, tile_exp, n_used)
    return _combine(pos, y_sorted, x2, top_w, gate, nw, shift, scale, bsz, seq, final)


def kernel(x, c, mod_w, mod_b, norm_w, hg_lb_logits, ev_w_in, ev_hg_norm, ev_conv_w, ev_conv_b, ev_dt_bias, ev_a_log, ev_d_skip, ev_ssm_norm, ev_w_out, od_mu, od_w_rkv, od_w_dec0, od_w_dec1, od_w_dec2, od_a0, od_a1, od_a2, od_g1, od_g2, od_k_k, od_k_a, od_r_k, od_ln_w, od_ln_b, od_w_o, router_w, router_b, moe_w1, moe_w3, moe_w2, final_norm_w):
    bsz, seq, d = x.shape
    depth = mod_w.shape[0]
    t = bsz * seq
    x2 = x.reshape(t, d)
    mod = _adaln_mod(c, mod_w, mod_b)
    gamma = jax.nn.softmax(hg_lb_logits.astype(F32), axis=0)
    lower_bounds = jnp.cumsum(gamma, axis=0) - gamma[0]
    head_of_lane = jnp.arange(RW_PACK * RW_N, dtype=I32) // RW_N
    seg = (head_of_lane[:, None] == head_of_lane[None, :]).astype(BF16)

    h = None
    out = None
    for l in range(depth):
        sh_m, sc_m, gt_m, sh_f, sc_f, gt_f = [mod[l, :, i * d:(i + 1) * d] for i in range(6)]
        j = l // 2
        if h is None:
            h = _norm_mod(x2, norm_w[l, 0], sh_m, sc_m, bsz, seq)
        if l % 2 == 0:
            w_in = ev_w_in[j]
            hgw = ev_hg_norm.shape[1]
            sw = ev_ssm_norm.shape[1]
            xbw = ev_conv_w.shape[2]
            nh = ev_dt_bias.shape[1]
            c0 = 4 * hgw
            w_hg = w_in[:, :c0].astype(BF16)
            w_ssd = jnp.zeros((d, sw + xbw + LANES), F32).at[:, :sw + xbw + nh].set(w_in[:, c0:]).astype(BF16)
            o_a = _hgrn2(h, w_hg, lower_bounds[l + 1], ev_hg_norm[j], bsz, seq)
            o_b = _ssd(h, w_ssd, ev_conv_w[j], ev_conv_b[j], ev_dt_bias[j], ev_a_log[j], ev_d_skip[j],
                       ev_ssm_norm[j], bsz, seq)
            w_out = ev_w_out[j].astype(BF16)
            x2, hf, top_i, top_w = _epilogue([o_a, o_b], [w_out[:hgw], w_out[hgw:]], x2, gt_m, norm_w[l, 1], sh_f, sc_f,
                                             router_w, router_b, bsz, seq)
        else:
            r, lw, k, v, kk, a, g, bonus = _rwkv_prep(h, od_mu[j], od_w_rkv[j], od_w_dec0[j], od_w_dec1[j], od_w_dec2[j],
                                                      od_a0[j], od_a1[j], od_a2[j], od_g1[j], od_g2[j], od_k_k[j],
                                                      od_k_a[j], od_r_k[j].reshape(-1), seg, bsz, seq)
            y = _rwkv_scan(r, lw, k, v, kk, a, bsz, seq)
            x2, hf, top_i, top_w = _epilogue([y], [od_w_o[j].astype(BF16)], x2, gt_m, norm_w[l, 1], sh_f, sc_f,
                                             router_w, router_b, bsz, seq,
                                             rwkv_extra=(g, bonus, od_ln_w[j], od_ln_b[j], seg))
        final = l == depth - 1
        if final:
            nw_next, sh_next, sc_next = final_norm_w, sh_f, sc_f
        else:
            nxt = [mod[l + 1, :, i * d:(i + 1) * d] for i in range(2)]
            nw_next, sh_next, sc_next = norm_w[l + 1, 0], nxt[0], nxt[1]
        res = _moe_block(hf, top_i, top_w, x2, gt_f, moe_w1, moe_w3, moe_w2, l, nw_next, sh_next, sc_next,
                         bsz, seq, final)
        if final:
            out = res
        else:
            x2, h = res
    return out.reshape(bsz, seq, d)
```

```python
import functools

import jax
import jax.numpy as jnp
from jax import lax
from jax.experimental import pallas as pl
from jax.experimental.pallas import tpu as pltpu

F32 = jnp.float32
BF16 = jnp.bfloat16
I32 = jnp.int32

NORM_EPS = 1e-6
RW_GN_EPS = 64e-5
LANES = 128
SUBLANES = 8
MXU_N = 256
VMEM_LIMIT = 56 * 1024 * 1024

HG_DK = 128
SSM_P = 64
SSM_N = 128
SSM_GROUPS = 2
SSM_CONV = 4
RW_N = 64
N_GROUPS_MOE = 4
TOP_K = 2

CHUNK = 128
RW_CHUNK = 64
RW_PACK = 4
MOE_TM = 512
ROW_TM = 256
EPILOGUE_SUBTILES = 2
GATHER_UNROLL = 8
ROW_PITCH = 9


def _cparams(sem):
    return pltpu.CompilerParams(dimension_semantics=sem, vmem_limit_bytes=VMEM_LIMIT)


def _dot(a, b):
    return lax.dot_general(a, b, (((1,), (0,)), ((), ())), preferred_element_type=F32)


def _dot_nt(a, b):
    return lax.dot_general(a, b, (((1,), (1,)), ((), ())), preferred_element_type=F32)


def _dot_tn(a, b):
    return lax.dot_general(a, b, (((0,), (0,)), ((), ())), preferred_element_type=F32)


def _split(x):
    hi = x.astype(BF16)
    return hi, (x - hi.astype(F32)).astype(BF16)


def _dot_sel(sel, x):
    hi, lo = _split(x)
    return _dot(sel, hi) + _dot(sel, lo)


def _dot_rsel(x, sel):
    hi, lo = _split(x)
    return _dot(hi, sel) + _dot(lo, sel)


def _sigmoid(x):
    return 1.0 / (1.0 + jnp.exp(-x))


def _silu(x):
    return x * _sigmoid(x)


def _softplus(x):
    return jnp.maximum(x, 0.0) + jnp.log(1.0 + jnp.exp(-jnp.abs(x)))


def _iota(shape, dim):
    return lax.broadcasted_iota(I32, shape, dim)


def _store_rows(ref, val):
    tm, width = val.shape
    for j in range(width // LANES):
        ref[pl.ds(j, tm, stride=ROW_PITCH), :] = val[:, j * LANES:(j + 1) * LANES]
    for j in range(width // LANES, ROW_PITCH):
        ref[pl.ds(j, tm, stride=ROW_PITCH), :] = jnp.zeros((tm, LANES), ref.dtype)


def _load_rows(ref, tm, width):
    return jnp.concatenate([ref[pl.ds(j, tm, stride=ROW_PITCH), :] for j in range(width // LANES)], axis=1)


def _mod_kernel(c_ref, w_ref, b_ref, o_ref):
    c = c_ref[...]
    o_ref[0] = _dot(_silu(c).astype(BF16), w_ref[0].astype(BF16)) + b_ref[0]


def _adaln_mod(c, mod_w, mod_b):
    depth, d, width = mod_w.shape
    bsz = c.shape[0]
    c_pad = jnp.zeros((SUBLANES, d), F32).at[:bsz].set(c)
    tn = 1536
    out = pl.pallas_call(
        _mod_kernel,
        grid=(depth, width // tn),
        in_specs=[pl.BlockSpec((SUBLANES, d), lambda l, j: (0, 0)),
                  pl.BlockSpec((1, d, tn), lambda l, j: (l, 0, j)),
                  pl.BlockSpec((1, 1, tn), lambda l, j: (l, 0, j))],
        out_specs=pl.BlockSpec((1, SUBLANES, tn), lambda l, j: (l, 0, j)),
        out_shape=jax.ShapeDtypeStruct((depth, SUBLANES, width), F32),
        compiler_params=_cparams(("arbitrary", "arbitrary")),
        name="adaln_mod",
    )(c_pad, mod_w, mod_b.reshape(depth, 1, width))
    return out[:, :bsz]


def _rms_mod(x, nw, shift, scale):
    y = x * lax.rsqrt(jnp.mean(x * x, axis=-1, keepdims=True) + NORM_EPS) * nw
    return y * (1.0 + scale) + shift


def _normmod_kernel(x_ref, nw_ref, sh_ref, sc_ref, h_ref):
    h_ref[...] = _rms_mod(x_ref[...], nw_ref[...], sh_ref[0], sc_ref[0]).astype(h_ref.dtype)


def _norm_mod(x2, nw, shift, scale, bsz, seq):
    t, d = x2.shape
    tm = ROW_TM
    ns = seq // tm
    return pl.pallas_call(
        _normmod_kernel,
        grid=(bsz, ns),
        in_specs=[pl.BlockSpec((tm, d), lambda b, s: (b * ns + s, 0)),
                  pl.BlockSpec((1, d), lambda b, s: (0, 0)),
                  pl.BlockSpec((1, 1, d), lambda b, s: (b, 0, 0)),
                  pl.BlockSpec((1, 1, d), lambda b, s: (b, 0, 0))],
        out_specs=pl.BlockSpec((tm, d), lambda b, s: (b * ns + s, 0)),
        out_shape=jax.ShapeDtypeStruct((t, d), BF16),
        compiler_params=_cparams(("arbitrary", "arbitrary")),
        name="norm_mod",
    )(x2, nw.reshape(1, d), shift.reshape(bsz, 1, d), scale.reshape(bsz, 1, d))


def _block_mid_ref(b, n2):
    rows, width = b.shape
    n = n2 // 2
    if n2 >= 2 * SUBLANES:
        b3 = b.reshape(rows // n2, n2, width)
        return jnp.broadcast_to(b3[:, n - 1:n, :], b3.shape).reshape(rows, width)
    b3 = b.reshape(rows // SUBLANES, SUBLANES, width)
    sub = _iota(b3.shape, 1)
    r3 = jnp.broadcast_to(b3[:, SUBLANES - n2 + n - 1:SUBLANES - n2 + n, :], b3.shape)
    for g in range(SUBLANES // n2 - 2, -1, -1):
        r3 = jnp.where(sub < (g + 1) * n2, b3[:, g * n2 + n - 1:g * n2 + n, :], r3)
    return r3.reshape(rows, width)


def _hgrn2_kernel(h_ref, w_ref, lb_ref, nw_ref, o_ref, proj_ref, st_ref):
    L = CHUNK
    dk = HG_DK
    n_heads = st_ref.shape[0]
    width = n_heads * dk

    @pl.when(pl.program_id(1) == 0)
    def _():
        st_ref[...] = jnp.zeros_like(st_ref)

    proj_ref[...] = _dot(h_ref[...], w_ref[...])
    lb = lb_ref[...]
    f = lb + (1.0 - lb) * _sigmoid(proj_ref[:, width:2 * width])
    logf = jnp.log(f)
    t_i = _iota((L, L), 0)
    s_i = _iota((L, L), 1)
    tril = (t_i >= s_i).astype(BF16)
    b_all = _dot_sel(tril, logf)
    eye = t_i == s_i

    levels = []
    n2 = L
    while n2 >= 2:
        n = n2 // 2
        sh = n2.bit_length() - 1
        m = ((t_i >> sh) == (s_i >> sh)) & ((t_i & (n2 - 1)) >= n) & ((s_i & (n2 - 1)) < n)
        levels.append((n2, m))
        n2 = n

    for h in range(n_heads):
        sl = slice(h * dk, (h + 1) * dk)
        q = proj_ref[:, sl]
        k = 1.0 - f[:, sl]
        v = proj_ref[:, 2 * width + h * dk:2 * width + (h + 1) * dk]
        b = b_all[:, sl]
        st = st_ref[h]
        b_last = b[L - 1:L, :]
        o = _dot_nt((q * jnp.exp(b)).astype(BF16), st.astype(BF16))
        a = jnp.where(eye, jnp.sum(q * k, axis=-1, keepdims=True), 0.0)
        for n2, m in levels:
            e = jnp.exp(-jnp.abs(b - _block_mid_ref(b, n2)))
            a = jnp.where(m, _dot_nt((q * e).astype(BF16), (k * e).astype(BF16)), a)
        o = o + _dot(a.astype(BF16), v.astype(BF16))
        ke = k * jnp.exp(b_last - b)
        st_ref[h] = st * jnp.exp(b_last) + _dot_tn(v.astype(BF16), ke.astype(BF16))
        g = proj_ref[:, 3 * width + h * dk:3 * width + (h + 1) * dk]
        ms = jnp.mean(o * o, axis=-1, keepdims=True)
        o_ref[:, sl] = (o * lax.rsqrt(ms + NORM_EPS) * nw_ref[:, sl] * _silu(g)).astype(o_ref.dtype)


def _hgrn2(h, w_hg, lb, hg_norm, bsz, seq):
    t, d = h.shape
    width = lb.shape[0]
    n_heads = width // HG_DK
    ns = seq // CHUNK
    return pl.pallas_call(
        _hgrn2_kernel,
        grid=(bsz, ns),
        in_specs=[pl.BlockSpec((CHUNK, d), lambda b, s: (b * ns + s, 0)),
                  pl.BlockSpec(w_hg.shape, lambda b, s: (0, 0)),
                  pl.BlockSpec((1, width), lambda b, s: (0, 0)),
                  pl.BlockSpec((1, width), lambda b, s: (0, 0))],
        out_specs=pl.BlockSpec((CHUNK, width), lambda b, s: (b * ns + s, 0)),
        out_shape=jax.ShapeDtypeStruct((t, width), BF16),
        scratch_shapes=[pltpu.VMEM((CHUNK, w_hg.shape[1]), F32), pltpu.VMEM((n_heads, HG_DK, HG_DK), F32)],
        compiler_params=_cparams(("arbitrary", "arbitrary")),
        name="hgrn2_scan",
    )(h, w_hg, lb.reshape(1, width), hg_norm.reshape(1, width))


def _ssd_kernel(h_ref, w_ref, cw_ref, cb_ref, dtb_ref, a_ref, dsk_ref, nw_ref, o_ref,
                proj_ref, carry_ref, xpad_ref, st_ref):
    L = CHUNK
    width = o_ref.shape[1]
    xw = cw_ref.shape[1]
    n_heads = width // SSM_P
    gw = width // SSM_GROUPS
    heads_per_group = n_heads // SSM_GROUPS

    @pl.when(pl.program_id(1) == 0)
    def _():
        carry_ref[...] = jnp.zeros_like(carry_ref)
        st_ref[...] = jnp.zeros_like(st_ref)

    proj_ref[...] = _dot(h_ref[...], w_ref[...])
    z_ref = proj_ref.at[:, 0:width]
    dt_ref = proj_ref.at[:, width + xw:]
    xraw = proj_ref[:, width:width + xw]
    xpad_ref[0:SUBLANES, :] = carry_ref[...]
    xpad_ref[SUBLANES:SUBLANES + L, :] = xraw
    carry_ref[...] = xraw[L - SUBLANES:L, :]
    acc = cb_ref[...] + jnp.zeros_like(xraw)
    for j in range(SSM_CONV):
        acc = acc + cw_ref[j:j + 1, :] * xpad_ref[pl.ds(SUBLANES - (SSM_CONV - 1) + j, L), :]
    xc = _silu(acc)
    xs = xc[:, :width]
    bm = xc[:, width:width + SSM_GROUPS * SSM_N]
    cm = xc[:, width + SSM_GROUPS * SSM_N:]

    dt = _softplus(dt_ref[...] + dtb_ref[...])
    da = dt * a_ref[...]
    da_t = da.T
    dt_t = dt.T
    t_i = _iota((L, L), 0)
    s_i = _iota((L, L), 1)
    triu = (t_i <= s_i).astype(BF16)
    acs_t = _dot_rsel(da_t, triu)
    causal = t_i >= s_i
    diag = t_i == s_i
    lane_lo = _iota((L, 2 * SSM_P), 1) < SSM_P
    bd_mask = (_iota((2 * L, 2 * SSM_P), 0) < L) == (_iota((2 * L, 2 * SSM_P), 1) < SSM_P)

    y_pairs = []
    for g in range(SSM_GROUPS):
        bg = bm[:, g * SSM_N:(g + 1) * SSM_N]
        cg = cm[:, g * SSM_N:(g + 1) * SSM_N]
        cb = _dot_nt(cg.astype(BF16), bg.astype(BF16))
        hg = st_ref[:, g * gw:(g + 1) * gw]
        yoff_g = _dot(cg.astype(BF16), hg.astype(BF16))
        xsc_parts, decay_parts = [], []
        for pr in range(heads_per_group // 2):
            j0 = g * heads_per_group + 2 * pr
            gs, ds, ecol, elast = [], [], [], []
            for j in (j0, j0 + 1):
                row_b = jnp.broadcast_to(acs_t[j:j + 1, :], (L, L))
                col_b = row_b.T
                dt_row = jnp.broadcast_to(dt_t[j:j + 1, :], (L, L))
                lmat = jnp.exp(jnp.minimum(col_b - row_b, 0.0))
                gs.append(jnp.where(causal, cb * lmat * dt_row, 0.0))
                a_last = acs_t[j:j + 1, L - 1:L]
                ds.append(jnp.where(diag, jnp.exp(a_last - row_b) * dt_row, 0.0))
                ecol.append(jnp.exp(col_b))
                elast.append(jnp.exp(a_last))
            lhs = jnp.concatenate([jnp.concatenate(gs, axis=1), jnp.concatenate(ds, axis=1)], axis=0)
            xs_pair = xs[:, j0 * SSM_P:(j0 + 2) * SSM_P]
            bd = jnp.where(bd_mask, jnp.concatenate([xs_pair, xs_pair], axis=0), 0.0)
            res = _dot(lhs.astype(BF16), bd.astype(BF16))
            yoff = yoff_g[:, pr * 2 * SSM_P:(pr + 1) * 2 * SSM_P] * jnp.where(lane_lo, ecol[0], ecol[1])
            y_pairs.append(res[:L] + yoff)
            xsc_parts.append(res[L:])
            decay_parts.append(jnp.where(lane_lo[0:1], elast[0], elast[1]))
        xsc_g = jnp.concatenate(xsc_parts, axis=1)
        decay_g = jnp.concatenate(decay_parts, axis=1)
        st_ref[:, g * gw:(g + 1) * gw] = hg * decay_g + _dot_tn(bg.astype(BF16), xsc_g.astype(BF16))
    y = jnp.concatenate(y_pairs, axis=1) + dsk_ref[...] * xs
    yz = y * _silu(z_ref[...])
    for g in range(SSM_GROUPS):
        seg = yz[:, g * gw:(g + 1) * gw]
        ms = jnp.mean(seg * seg, axis=-1, keepdims=True)
        o_ref[:, g * gw:(g + 1) * gw] = (seg * lax.rsqrt(ms + NORM_EPS) * nw_ref[:, g * gw:(g + 1) * gw]).astype(o_ref.dtype)


def _ssd(h, w_ssd, conv_w, conv_b, dt_bias, a_log, d_skip, ssm_norm, bsz, seq):
    t, d = h.shape
    width = ssm_norm.shape[0]
    xw = conv_w.shape[1]
    n_heads = width // SSM_P
    ns = seq // CHUNK
    pad = lambda v: jnp.zeros((1, LANES), F32).at[0, :n_heads].set(v)
    row = lambda w: pl.BlockSpec((CHUNK, w), lambda b, s: (b * ns + s, 0))
    const = lambda r, w: pl.BlockSpec((r, w), lambda b, s: (0, 0))
    return pl.pallas_call(
        _ssd_kernel,
        grid=(bsz, ns),
        in_specs=[row(d), const(*w_ssd.shape), const(SSM_CONV, xw), const(1, xw), const(1, LANES),
                  const(1, LANES), const(1, width), const(1, width)],
        out_specs=row(width),
        out_shape=jax.ShapeDtypeStruct((t, width), BF16),
        scratch_shapes=[pltpu.VMEM((CHUNK, w_ssd.shape[1]), F32), pltpu.VMEM((SUBLANES, xw), F32),
                        pltpu.VMEM((CHUNK + SUBLANES, xw), F32), pltpu.VMEM((SSM_N, width), F32)],
        compiler_params=_cparams(("arbitrary", "arbitrary")),
        name="ssd_scan",
    )(h, w_ssd, conv_w, conv_b.reshape(1, xw), pad(dt_bias), pad(-jnp.exp(a_log)),
      jnp.repeat(d_skip, SSM_P).reshape(1, width), ssm_norm.reshape(1, width))


def _route(probs):
    n_exp = N_GROUPS_MOE * 4
    p = [probs[e:e + 1, :] for e in range(n_exp)]
    gs = []
    for g in range(N_GROUPS_MOE):
        a, b, c, d = p[4 * g:4 * g + 4]
        gs.append(jnp.maximum(jnp.maximum(jnp.maximum(a + b, a + c), jnp.maximum(a + d, b + c)),
                              jnp.maximum(b + d, c + d)))
    best = jnp.zeros_like(gs[0]).astype(I32)
    bs = gs[0]
    for g in range(1, N_GROUPS_MOE):
        upd = gs[g] > bs
        best = jnp.where(upd, g, best)
        bs = jnp.where(upd, gs[g], bs)
    q = [jnp.where(best == 0, p[i], jnp.where(best == 1, p[4 + i], jnp.where(best == 2, p[8 + i], p[12 + i])))
         for i in range(4)]
    i1 = jnp.zeros_like(best)
    v1 = q[0]
    for i in range(1, 4):
        upd = q[i] > v1
        i1 = jnp.where(upd, i, i1)
        v1 = jnp.where(upd, q[i], v1)
    i2 = jnp.zeros_like(best)
    v2 = jnp.full_like(v1, -1.0)
    for i in range(4):
        upd = (i1 != i) & (q[i] > v2)
        i2 = jnp.where(upd, i, i2)
        v2 = jnp.where(upd, q[i], v2)
    den = v1 + v2
    return best * 4 + i1, best * 4 + i2, v1 / den, v2 / den


def _head_stats_expand(stack, seg_ref):
    pw = seg_ref.shape[0]
    seg = seg_ref[...]
    return jnp.concatenate([_dot_rsel(stack[:, p * pw:(p + 1) * pw], seg) for p in range(stack.shape[1] // pw)], axis=1)


def _epilogue_kernel(*refs, n_in, rwkv):
    ins = refs[:n_in]
    pos = n_in
    w_refs = refs[pos:pos + n_in]
    pos += n_in
    if rwkv:
        g_ref, bonus_ref, lnw_ref, lnb_ref, seg_ref = refs[pos:pos + 5]
        pos += 5
    x_ref, gate_ref, nw_ref, sh_ref, sc_ref, rw_ref, rb_ref = refs[pos:pos + 7]
    pos += 7
    xo_ref, h_ref, ti_ref, tw_ref = refs[pos:pos + 4]

    tm = ROW_TM
    rw_hi, rw_lo = _split(rw_ref[...])
    for sub in range(x_ref.shape[0] // tm):
        rs = pl.ds(sub * tm, tm)
        if rwkv:
            o = ins[0][rs, :]
            inv_n = 1.0 / RW_N
            st = _head_stats_expand(jnp.concatenate([o, o * o], axis=0), seg_ref) * inv_n
            mean = st[:tm]
            var = jnp.maximum(st[tm:] - mean * mean, 0.0)
            o = (o - mean) * lax.rsqrt(var + RW_GN_EPS) * lnw_ref[...] + lnb_ref[...]
            o = (o + bonus_ref[rs, :].astype(F32)) * g_ref[rs, :].astype(F32)
            y = _dot(o.astype(BF16), w_refs[0][...])
        else:
            y = _dot(ins[0][rs, :], w_refs[0][...])
            for a_ref, w_ref in zip(ins[1:], w_refs[1:]):
                y = y + _dot(a_ref[rs, :], w_ref[...])
        x_new = x_ref[rs, :] + gate_ref[0] * y
        xo_ref[rs, :] = x_new
        h = _rms_mod(x_new, nw_ref[...], sh_ref[0], sc_ref[0])
        _store_rows(h_ref.at[pl.ds(sub * tm * ROW_PITCH, tm * ROW_PITCH)], h)
        h_hi, h_lo = _split(h)
        logits = _dot_nt(rw_hi, h_hi) + _dot_nt(rw_hi, h_lo) + _dot_nt(rw_lo, h_hi) + rb_ref[...]
        mx = jnp.max(logits, axis=0, keepdims=True)
        ex = jnp.exp(logits - mx)
        probs = ex / jnp.sum(ex, axis=0, keepdims=True)
        e1, e2, w1, w2 = _route(probs)
        zi = jnp.zeros((SUBLANES - TOP_K, tm), I32)
        ti_ref[:, sub * tm:(sub + 1) * tm] = jnp.concatenate([e1, e2, zi], axis=0)
        tw_ref[:, sub * tm:(sub + 1) * tm] = jnp.concatenate([w1, w2, zi.astype(F32)], axis=0)


def _epilogue(ins, ws, x2, gate, nw, shift, scale, router_w, router_b, bsz, seq, rwkv_extra=None):
    t, d = x2.shape
    tm = EPILOGUE_SUBTILES * ROW_TM
    ns = seq // tm
    n_exp = router_w.shape[1]
    rw_t = jnp.zeros((LANES, d), F32).at[:n_exp].set(router_w.T)
    rb = jnp.full((LANES, 1), -1e30, F32).at[:n_exp, 0].set(router_b)
    row = lambda w: pl.BlockSpec((tm, w), lambda b, s: (b * ns + s, 0))
    const = lambda a: pl.BlockSpec(a.shape, lambda b, s: (0,) * a.ndim)
    per_b = pl.BlockSpec((1, 1, d), lambda b, s: (b, 0, 0))
    args = list(ins) + list(ws)
    specs = [row(a.shape[1]) for a in ins] + [const(w) for w in ws]
    if rwkv_extra is not None:
        g, bonus, lnw, lnb, seg = rwkv_extra
        args += [g, bonus, lnw.reshape(1, d), lnb.reshape(1, d), seg]
        specs += [row(d), row(d), pl.BlockSpec((1, d), lambda b, s: (0, 0)), pl.BlockSpec((1, d), lambda b, s: (0, 0)),
                  const(seg)]
    args += [x2, gate.reshape(bsz, 1, d), nw.reshape(1, d), shift.reshape(bsz, 1, d), scale.reshape(bsz, 1, d), rw_t, rb]
    specs += [row(d), per_b, pl.BlockSpec((1, d), lambda b, s: (0, 0)), per_b, per_b, const(rw_t), const(rb)]
    lane_row = pl.BlockSpec((SUBLANES, tm), lambda b, s: (0, b * ns + s))
    return pl.pallas_call(
        functools.partial(_epilogue_kernel, n_in=len(ins), rwkv=rwkv_extra is not None),
        grid=(bsz, ns),
        in_specs=specs,
        out_specs=[row(d), pl.BlockSpec((tm * ROW_PITCH, LANES), lambda b, s: (b * ns + s, 0)), lane_row, lane_row],
        out_shape=[jax.ShapeDtypeStruct((t, d), F32), jax.ShapeDtypeStruct((t * ROW_PITCH, LANES), F32),
                   jax.ShapeDtypeStruct((SUBLANES, t), I32), jax.ShapeDtypeStruct((SUBLANES, t), F32)],
        compiler_params=_cparams(("arbitrary", "arbitrary")),
        name="mixer_epilogue",
    )(*args)


def _route_plan(top_i, n_exp, tm):
    t = top_i.shape[1]
    n_pairs = TOP_K * t
    n_tiles = n_pairs // tm + n_exp
    e_flat = top_i[:TOP_K].reshape(-1)
    onehot = (e_flat[:, None] == jnp.arange(n_exp, dtype=I32)[None, :]).astype(I32)
    csum = jnp.cumsum(onehot, axis=0)
    counts = csum[-1]
    rank = jnp.sum((csum - 1) * onehot, axis=1)
    padded = ((counts + tm - 1) // tm) * tm
    ends = jnp.cumsum(padded)
    pos = (ends - padded)[e_flat] + rank
    tile_start = jnp.arange(n_tiles, dtype=I32) * tm
    tile_exp = jnp.minimum(jnp.sum((tile_start[:, None] >= ends[None, :]).astype(I32), axis=1), n_exp - 1)
    n_used = (ends[-1] // tm).reshape(1).astype(I32)
    return pos.astype(I32), tile_exp.astype(I32), n_used


def _moe_kernel(texp_ref, nused_ref, pos_ref, h_hbm, w1_ref, w3_ref, w2_ref, y_ref, xbuf, wb1, wb3, wb2, rtok_ref, sem):
    tm = MOE_TM
    d = wb1.shape[0]
    n_sub = d // LANES
    n_tok = pos_ref.shape[0] // TOP_K
    i = pl.program_id(0)
    n_used = nused_ref[0]
    slot = i % 2

    @pl.when(i == 0)
    def _():
        def zero(j, c):
            rtok_ref[j] = 0
            return c
        lax.fori_loop(0, rtok_ref.shape[0], zero, 0, unroll=16)

        for k in range(TOP_K):
            def fill(t, c):
                rtok_ref[pos_ref[k * n_tok + t]] = t * ROW_PITCH
                return c
            lax.fori_loop(0, n_tok, fill, 0, unroll=8)

    def start_gather(tile, sl):
        def body(r8, c):
            for u in range(GATHER_UNROLL):
                r = r8 * GATHER_UNROLL + u
                pltpu.make_async_copy(h_hbm.at[pl.ds(rtok_ref[tile * tm + r], n_sub)],
                                      xbuf.at[sl, pl.ds(r * ROW_PITCH, n_sub)], sem.at[sl]).start(priority=u % 2)
            return c
        lax.fori_loop(0, tm // GATHER_UNROLL, body, 0)

    def wait_gather(sl):
        pltpu.make_async_copy(h_hbm.at[pl.ds(0, tm * n_sub)], xbuf.at[sl, pl.ds(0, tm * n_sub)], sem.at[sl]).wait()

    @pl.when(i == 0)
    def _():
        start_gather(0, 0)

    @pl.when((i == 0) | (texp_ref[i] != texp_ref[jnp.maximum(i - 1, 0)]))
    def _():
        wb1[...] = w1_ref[0, 0].astype(BF16)
        wb3[...] = w3_ref[0, 0].astype(BF16)
        wb2[...] = w2_ref[0, 0].astype(BF16)

    @pl.when(i < n_used)
    def _():
        wait_gather(slot)
        nxt = jnp.minimum(i + 1, n_used - 1)
        dff = wb1.shape[1]
        pieces = 2 * (dff // MXU_N) + d // MXU_N
        per = tm // pieces
        issued = [0]

        def issue_slice():
            for r in range(issued[0], issued[0] + per):
                pltpu.make_async_copy(h_hbm.at[pl.ds(rtok_ref[nxt * tm + r], n_sub)],
                                      xbuf.at[1 - slot, pl.ds(r * ROW_PITCH, n_sub)],
                                      sem.at[1 - slot]).start(priority=r % 2)
            issued[0] += per

        x = _load_rows(xbuf.at[slot], tm, d).astype(BF16)
        hid = []
        for c in range(dff // MXU_N):
            cs = slice(c * MXU_N, (c + 1) * MXU_N)
            a = _dot(x, wb1[:, cs])
            issue_slice()
            b = _dot(x, wb3[:, cs])
            issue_slice()
            hid.append((_silu(a) * b).astype(BF16))
        hid = jnp.concatenate(hid, axis=1)
        for c in range(d // MXU_N):
            y = _dot(hid, wb2[:, c * MXU_N:(c + 1) * MXU_N])
            for j in range(MXU_N // LANES):
                y_ref[pl.ds(c * (MXU_N // LANES) + j, tm, stride=ROW_PITCH), :] = y[:, j * LANES:(j + 1) * LANES]
            issue_slice()
        for j in range(d // LANES, ROW_PITCH):
            y_ref[pl.ds(j, tm, stride=ROW_PITCH), :] = jnp.zeros((tm, LANES), F32)

    @pl.when(i == n_used - 1)
    def _():
        wait_gather(1 - slot)

    @pl.when(i >= n_used)
    def _():
        y_ref[...] = jnp.zeros_like(y_ref)


def _moe(h, w1, w3, w2, layer, pos, tile_exp, n_used):
    _, n_exp, d, dff = w1.shape
    tm = MOE_TM
    n_tiles = tile_exp.shape[0]
    grid_spec = pltpu.PrefetchScalarGridSpec(
        num_scalar_prefetch=3,
        grid=(n_tiles,),
        in_specs=[pl.BlockSpec(memory_space=pl.ANY),
                  pl.BlockSpec((1, 1, d, dff), lambda i, te, nu, rt: (layer, te[i], 0, 0)),
                  pl.BlockSpec((1, 1, d, dff), lambda i, te, nu, rt: (layer, te[i], 0, 0)),
                  pl.BlockSpec((1, 1, dff, d), lambda i, te, nu, rt: (layer, te[i], 0, 0))],
        out_specs=pl.BlockSpec((tm * ROW_PITCH, LANES), lambda i, te, nu, rt: (i, 0)),
        scratch_shapes=[pltpu.VMEM((2, tm * ROW_PITCH, LANES), F32), pltpu.VMEM((d, dff), BF16), pltpu.VMEM((d, dff), BF16),
                        pltpu.VMEM((dff, d), BF16), pltpu.SMEM((n_tiles * tm,), I32), pltpu.SemaphoreType.DMA((2,))],
    )
    return pl.pallas_call(
        _moe_kernel,
        grid_spec=grid_spec,
        out_shape=jax.ShapeDtypeStruct((n_tiles * tm * ROW_PITCH, LANES), F32),
        compiler_params=_cparams(("arbitrary",)),
        name="moe_experts",
    )(tile_exp, n_used, pos, h, w1, w3, w2)


def _combine_kernel(pos_ref, y_hbm, x_ref, tw_ref, gate_ref, nw_ref, sh_ref, sc_ref, *out_and_scratch, final, n_tok):
    if final:
        o_ref, ybuf, sem = out_and_scratch
    else:
        xo_ref, h_ref, ybuf, sem = out_and_scratch
    tm, d = x_ref.shape
    n_sub = d // LANES
    i = pl.program_id(0)
    n_steps = pl.num_programs(0)
    slot = i % 2

    def start_gather(tile, sl):
        def body(r8, c):
            for u in range(GATHER_UNROLL):
                r = r8 * GATHER_UNROLL + u
                for k in range(TOP_K):
                    src = pos_ref[k * n_tok + tile * tm + r] * ROW_PITCH
                    pltpu.make_async_copy(y_hbm.at[pl.ds(src, n_sub)], ybuf.at[sl, k, pl.ds(r * ROW_PITCH, n_sub)],
                                          sem.at[sl]).start(priority=k)
            return c
        lax.fori_loop(0, tm // GATHER_UNROLL, body, 0)

    @pl.when(i == 0)
    def _():
        start_gather(0, 0)

    @pl.when(i + 1 < n_steps)
    def _():
        start_gather(i + 1, 1 - slot)

    for k in range(TOP_K):
        pltpu.make_async_copy(y_hbm.at[pl.ds(0, tm * n_sub)], ybuf.at[slot, k, pl.ds(0, tm * n_sub)], sem.at[slot]).wait()
    tw = tw_ref[...]
    moe = tw[:, 0:1] * _load_rows(ybuf.at[slot, 0], tm, d) + tw[:, 1:2] * _load_rows(ybuf.at[slot, 1], tm, d)
    x_new = x_ref[...] + gate_ref[0] * moe
    if final:
        o_ref[...] = x_new * lax.rsqrt(jnp.mean(x_new * x_new, axis=-1, keepdims=True) + NORM_EPS) * nw_ref[...]
    else:
        xo_ref[...] = x_new
        h_ref[...] = _rms_mod(x_new, nw_ref[...], sh_ref[0], sc_ref[0]).astype(h_ref.dtype)


def _combine(pos, y_sorted, x2, top_w, gate, nw, shift, scale, bsz, seq, final):
    t, d = x2.shape
    tm = ROW_TM
    ns = seq // tm
    tw = top_w.T
    row = pl.BlockSpec((tm, d), lambda i, p: (i, 0))
    per_b = pl.BlockSpec((1, 1, d), lambda i, p: (i // ns, 0, 0))
    grid_spec = pltpu.PrefetchScalarGridSpec(
        num_scalar_prefetch=1,
        grid=(t // tm,),
        in_specs=[pl.BlockSpec(memory_space=pl.ANY), row,
                  pl.BlockSpec((tm, SUBLANES), lambda i, p: (i, 0)),
                  per_b, pl.BlockSpec((1, d), lambda i, p: (0, 0)), per_b, per_b],
        out_specs=row if final else [row, row],
        scratch_shapes=[pltpu.VMEM((2, TOP_K, tm * ROW_PITCH, LANES), F32), pltpu.SemaphoreType.DMA((2,))],
    )
    out_shape = (jax.ShapeDtypeStruct((t, d), F32) if final else
                 [jax.ShapeDtypeStruct((t, d), F32), jax.ShapeDtypeStruct((t, d), BF16)])
    return pl.pallas_call(
        functools.partial(_combine_kernel, final=final, n_tok=t),
        grid_spec=grid_spec,
        out_shape=out_shape,
        compiler_params=_cparams(("arbitrary",)),
        name="moe_combine",
    )(pos, y_sorted, x2, tw, gate.reshape(bsz, 1, d), nw.reshape(1, d), shift.reshape(bsz, 1, d), scale.reshape(bsz, 1, d))


def _rwkv_prep_kernel(h_ref, mu_ref, wr_ref, wk_ref, wv_ref, dec0_ref, dec1_ref, dec2_ref, a0_ref, a1_ref, a2_ref,
                      g1_ref, g2_ref, kk_ref, ka_ref, rk_ref, seg_ref,
                      r_out, lw_out, k_out, v_out, kkn_out, a_out, g_out, bonus_out, carry_ref, hp_ref):
    tm = h_ref.shape[0]

    @pl.when(pl.program_id(1) == 0)
    def _():
        carry_ref[...] = jnp.zeros_like(carry_ref)

    h = h_ref[...].astype(F32)
    hp_ref[0:SUBLANES, :] = carry_ref[...]
    hp_ref[SUBLANES:SUBLANES + tm, :] = h
    carry_ref[...] = h[tm - SUBLANES:tm, :]
    xx = hp_ref[pl.ds(SUBLANES - 1, tm), :] - h
    mix = lambda i: (h + xx * mu_ref[i:i + 1, :]).astype(BF16)
    r = _dot(mix(0), wr_ref[...])
    k = _dot(mix(1), wk_ref[...])
    v = _dot(mix(2), wv_ref[...])
    wl = dec0_ref[...] + _dot(jnp.tanh(_dot(mix(3), dec1_ref[...])).astype(BF16), dec2_ref[...])
    lw = -jnp.exp(-_softplus(-wl) - 0.5)
    a = _sigmoid(a0_ref[...] + _dot(_dot(mix(4), a1_ref[...]).astype(BF16), a2_ref[...]))
    g = _dot(_sigmoid(_dot(mix(5), g1_ref[...])).astype(BF16), g2_ref[...])
    kk = k * kk_ref[...]
    k_h = k * (1.0 + (a - 1.0) * ka_ref[...])
    st = _head_stats_expand(jnp.concatenate([kk * kk, r * k_h * rk_ref[...]], axis=0), seg_ref)
    kkn = kk / jnp.maximum(jnp.sqrt(st[:tm]), 1e-12)
    r_out[...] = r.astype(r_out.dtype)
    lw_out[...] = lw
    k_out[...] = k_h.astype(k_out.dtype)
    v_out[...] = v.astype(v_out.dtype)
    kkn_out[...] = kkn.astype(kkn_out.dtype)
    a_out[...] = a.astype(a_out.dtype)
    g_out[...] = g.astype(g_out.dtype)
    bonus_out[...] = (st[tm:] * v).astype(bonus_out.dtype)


def _rwkv_prep(h, mu, w_rkv, dec0, dec1, dec2, a0, a1, a2, g1, g2, k_k, k_a, r_k, seg, bsz, seq):
    t, d = h.shape
    tm = ROW_TM
    ns = seq // tm
    padc = lambda w: jnp.zeros((d, LANES), F32).at[:, :w.shape[1]].set(w).astype(BF16)
    padr = lambda w: jnp.zeros((LANES, d), F32).at[:w.shape[0]].set(w).astype(BF16)
    vec = lambda v: v.reshape(1, d)
    args = [h, mu, w_rkv[0].astype(BF16), w_rkv[1].astype(BF16), w_rkv[2].astype(BF16), vec(dec0), padc(dec1), padr(dec2),
            vec(a0), padc(a1), padr(a2), padc(g1), padr(g2), vec(k_k), vec(k_a), vec(r_k), seg]
    row = pl.BlockSpec((tm, d), lambda b, s: (b * ns + s, 0))
    const = lambda a: pl.BlockSpec(a.shape, lambda b, s: (0,) * a.ndim)
    outs = [BF16, F32, BF16, BF16, BF16, BF16, BF16, BF16]
    return pl.pallas_call(
        _rwkv_prep_kernel,
        grid=(bsz, ns),
        in_specs=[row] + [const(a) for a in args[1:]],
        out_specs=[row] * len(outs),
        out_shape=[jax.ShapeDtypeStruct((t, d), dt) for dt in outs],
        scratch_shapes=[pltpu.VMEM((SUBLANES, d), F32), pltpu.VMEM((tm + SUBLANES, d), F32)],
        compiler_params=_cparams(("arbitrary", "arbitrary")),
        name="rwkv_prep",
    )(*args)


def _rwkv_scan_kernel(r_ref, lw_ref, k_ref, v_ref, kk_ref, a_ref, y_ref, st_ref):
    L = RW_CHUNK
    pw = RW_PACK * RW_N
    n_packs = st_ref.shape[0]
    sh = RW_N.bit_length() - 1

    @pl.when(pl.program_id(1) == 0)
    def _():
        st_ref[...] = jnp.zeros_like(st_ref)

    t_i = _iota((L, L), 0)
    s_i = _iota((L, L), 1)
    tril = (t_i >= s_i).astype(BF16)
    wc_all = _dot_sel(tril, lw_ref[...])
    lane_head = _iota((L, pw), 1) >> sh
    s_loc = _iota((L, pw), 1) & (RW_N - 1)
    t_loc = _iota((L, pw), 0)
    strict = s_loc < t_loc
    incl = s_loc <= t_loc
    bd_mask = (_iota((pw, pw), 0) >> sh) == (_iota((pw, pw), 1) >> sh)

    def bdiag(x):
        return jnp.where(bd_mask, jnp.concatenate([x] * RW_PACK, axis=0), 0.0).astype(BF16)

    packs = range(n_packs)
    sls = [slice(p * pw, (p + 1) * pw) for p in packs]
    pr, vs, sts, kkas, ks, wcs = [], [], [], [], [], []
    for p in packs:
        sl = sls[p]
        r = r_ref[:, sl].astype(F32)
        lw = lw_ref[:, sl]
        k = k_ref[:, sl].astype(F32)
        kk = kk_ref[:, sl].astype(F32)
        a = a_ref[:, sl].astype(F32)
        wc = wc_all[:, sl]
        e_inv = jnp.exp(-wc)
        kka = kk * a
        al = -kk * jnp.exp(wc - lw)
        rb = r * jnp.exp(wc)
        bt = kka * e_inv
        kt = k * e_inv
        lhs = jnp.concatenate([al, rb], axis=0).astype(BF16)
        rows = [jnp.where(lane_head == hh, x, 0.0) for x in (bt, kt) for hh in range(RW_PACK)]
        st = st_ref[p]
        m = jnp.concatenate(rows + [st], axis=0).astype(BF16)
        pr.append(_dot_nt(lhs, m))
        vs.append(v_ref[:, sl].astype(F32))
        sts.append(st)
        kkas.append(kka)
        ks.append(k)
        wcs.append(wc)
    bd_vs = [bdiag(vs[p]) for p in packs]
    us = [pr[p][:L, 2 * pw:] + _dot(jnp.where(strict, pr[p][:L, pw:2 * pw], 0.0).astype(BF16), bd_vs[p]) for p in packs]
    nmats = [jnp.where(strict, pr[p][:L, 0:pw], 0.0) for p in packs]
    n_steps = L.bit_length() - 1
    for it in range(n_steps):
        us = [us[p] + _dot(nmats[p].astype(BF16), bdiag(us[p])) for p in packs]
        if it + 1 < n_steps:
            nmats = [_dot(nmats[p].astype(BF16), bdiag(nmats[p])) for p in packs]
    for p in packs:
        a_rb = jnp.where(incl, pr[p][L:, 0:pw], 0.0)
        a_rk = jnp.where(incl, pr[p][L:, pw:2 * pw], 0.0)
        y_ref[:, sls[p]] = pr[p][L:, 2 * pw:] + _dot(jnp.concatenate([a_rb, a_rk], axis=1).astype(BF16),
                                                     jnp.concatenate([bdiag(us[p]), bd_vs[p]], axis=0))
    for p in packs:
        w_last = wcs[p][L - 1:L, :]
        e_last = jnp.exp(w_last - wcs[p])
        upd = _dot_tn(jnp.concatenate([us[p], vs[p]], axis=0).astype(BF16),
                      jnp.concatenate([kkas[p] * e_last, ks[p] * e_last], axis=0).astype(BF16))
        st_ref[p] = jnp.where(bd_mask, sts[p] * jnp.exp(w_last) + upd, 0.0)


def _rwkv_scan(r, lw, k, v, kk, a, bsz, seq):
    t, d = r.shape
    ns = seq // RW_CHUNK
    pw = RW_PACK * RW_N
    row = pl.BlockSpec((RW_CHUNK, d), lambda b, s: (b * ns + s, 0))
    return pl.pallas_call(
        _rwkv_scan_kernel,
        grid=(bsz, ns),
        in_specs=[row] * 6,
        out_specs=row,
        out_shape=jax.ShapeDtypeStruct((t, d), F32),
        scratch_shapes=[pltpu.VMEM((d // pw, pw, pw), F32)],
        compiler_params=_cparams(("arbitrary", "arbitrary")),
        name="rwkv_scan",
    )(r, lw, k, v, kk, a)


def _moe_block(h, top_i, top_w, x2, gate, w1, w3, w2, layer, nw, shift, scale, bsz, seq, final):
    n_exp = w1.shape[1]
    pos, tile_exp, n_used = _route_plan(top_i, n_exp, MOE_TM)
    y_sorted = _moe(h, w1, w3, w2, layer, pos, tile_exp, n_used)
    return _combine(pos, y_sorted, x2, top_w, gate, nw, shift, scale, bsz, seq, final)


def kernel(x, c, mod_w, mod_b, norm_w, hg_lb_logits, ev_w_in, ev_hg_norm, ev_conv_w, ev_conv_b, ev_dt_bias, ev_a_log, ev_d_skip, ev_ssm_norm, ev_w_out, od_mu, od_w_rkv, od_w_dec0, od_w_dec1, od_w_dec2, od_a0, od_a1, od_a2, od_g1, od_g2, od_k_k, od_k_a, od_r_k, od_ln_w, od_ln_b, od_w_o, router_w, router_b, moe_w1, moe_w3, moe_w2, final_norm_w):
    bsz, seq, d = x.shape
    depth = mod_w.shape[0]
    t = bsz * seq
    x2 = x.reshape(t, d)
    mod = _adaln_mod(c, mod_w, mod_b)
    gamma = jax.nn.softmax(hg_lb_logits.astype(F32), axis=0)
    lower_bounds = jnp.cumsum(gamma, axis=0) - gamma[0]
    head_of_lane = jnp.arange(RW_PACK * RW_N, dtype=I32) // RW_N
    seg = (head_of_lane[:, None] == head_of_lane[None, :]).astype(BF16)

    h = None
    out = None
    for l in range(depth):
        sh_m, sc_m, gt_m, sh_f, sc_f, gt_f = [mod[l, :, i * d:(i + 1) * d] for i in range(6)]
        j = l // 2
        if h is None:
            h = _norm_mod(x2, norm_w[l, 0], sh_m, sc_m, bsz, seq)
        if l % 2 == 0:
            w_in = ev_w_in[j]
            hgw = ev_hg_norm.shape[1]
            sw = ev_ssm_norm.shape[1]
            xbw = ev_conv_w.shape[2]
            nh = ev_dt_bias.shape[1]
            c0 = 4 * hgw
            w_hg = w_in[:, :c0].astype(BF16)
            w_ssd = jnp.zeros((d, sw + xbw + LANES), F32).at[:, :sw + xbw + nh].set(w_in[:, c0:]).astype(BF16)
            o_a = _hgrn2(h, w_hg, lower_bounds[l + 1], ev_hg_norm[j], bsz, seq)
            o_b = _ssd(h, w_ssd, ev_conv_w[j], ev_conv_b[j], ev_dt_bias[j], ev_a_log[j], ev_d_skip[j],
                       ev_ssm_norm[j], bsz, seq)
            w_out = ev_w_out[j].astype(BF16)
            x2, hf, top_i, top_w = _epilogue([o_a, o_b], [w_out[:hgw], w_out[hgw:]], x2, gt_m, norm_w[l, 1], sh_f, sc_f,
                                             router_w, router_b, bsz, seq)
        else:
            r, lw, k, v, kk, a, g, bonus = _rwkv_prep(h, od_mu[j], od_w_rkv[j], od_w_dec0[j], od_w_dec1[j], od_w_dec2[j],
                                                      od_a0[j], od_a1[j], od_a2[j], od_g1[j], od_g2[j], od_k_k[j],
                                                      od_k_a[j], od_r_k[j].reshape(-1), seg, bsz, seq)
            y = _rwkv_scan(r, lw, k, v, kk, a, bsz, seq)
            x2, hf, top_i, top_w = _epilogue([y], [od_w_o[j].astype(BF16)], x2, gt_m, norm_w[l, 1], sh_f, sc_f,
                                             router_w, router_b, bsz, seq,
                                             rwkv_extra=(g, bonus, od_ln_w[j], od_ln_b[j], seg))
        final = l == depth - 1
        if final:
            nw_next, sh_next, sc_next = final_norm_w, sh_f, sc_f
        else:
            nxt = [mod[l + 1, :, i * d:(i + 1) * d] for i in range(2)]
            nw_next, sh_next, sc_next = norm_w[l + 1, 0], nxt[0], nxt[1]
        res = _moe_block(hf, top_i, top_w, x2, gt_f, moe_w1, moe_w3, moe_w2, l, nw_next, sh_next, sc_next,
                         bsz, seq, final)
        if final:
            out = res
        else:
            x2, h = res
    return out.reshape(bsz, seq, d)
```

```python
import functools

import jax
import jax.numpy as jnp
from jax import lax
from jax.experimental import pallas as pl
from jax.experimental.pallas import tpu as pltpu

F32 = jnp.float32
BF16 = jnp.bfloat16
I32 = jnp.int32

NORM_EPS = 1e-6
RW_GN_EPS = 64e-5
LANES = 128
SUBLANES = 8
VMEM_LIMIT = 56 * 1024 * 1024

HG_DK = 128
SSM_P = 64
SSM_N = 128
SSM_GROUPS = 2
SSM_CONV = 4
RW_N = 64
N_GROUPS_MOE = 4
TOP_K = 2

CHUNK = 128
RW_CHUNK = 64
RW_PACK = 4
MOE_TM = 512
ROW_TM = 256
EPILOGUE_SUBTILES = 2
DISPATCH_TM = 512
ZERO_CHUNK = 64
GATHER_UNROLL = 8
ROW_PITCH = 9


def _cparams(sem):
    return pltpu.CompilerParams(dimension_semantics=sem, vmem_limit_bytes=VMEM_LIMIT)


def _dot(a, b):
    return lax.dot_general(a, b, (((1,), (0,)), ((), ())), preferred_element_type=F32)


def _dot_nt(a, b):
    return lax.dot_general(a, b, (((1,), (1,)), ((), ())), preferred_element_type=F32)


def _dot_tn(a, b):
    return lax.dot_general(a, b, (((0,), (0,)), ((), ())), preferred_element_type=F32)


def _split(x):
    hi = x.astype(BF16)
    return hi, (x - hi.astype(F32)).astype(BF16)


def _dot_sel(sel, x):
    hi, lo = _split(x)
    return _dot(sel, hi) + _dot(sel, lo)


def _dot_rsel(x, sel):
    hi, lo = _split(x)
    return _dot(hi, sel) + _dot(lo, sel)


def _sigmoid(x):
    return 1.0 / (1.0 + jnp.exp(-x))


def _silu(x):
    return x * _sigmoid(x)


def _softplus(x):
    return jnp.maximum(x, 0.0) + jnp.log(1.0 + jnp.exp(-jnp.abs(x)))


def _iota(shape, dim):
    return lax.broadcasted_iota(I32, shape, dim)


def _store_rows(ref, val):
    tm, width = val.shape
    for j in range(width // LANES):
        ref[pl.ds(j, tm, stride=ROW_PITCH), :] = val[:, j * LANES:(j + 1) * LANES]
    for j in range(width // LANES, ROW_PITCH):
        ref[pl.ds(j, tm, stride=ROW_PITCH), :] = jnp.zeros((tm, LANES), ref.dtype)


def _load_rows(ref, tm, width):
    return jnp.concatenate([ref[pl.ds(j, tm, stride=ROW_PITCH), :] for j in range(width // LANES)], axis=1)


def _mod_kernel(c_ref, w_ref, b_ref, o_ref):
    c = c_ref[...]
    o_ref[0] = _dot(_silu(c).astype(BF16), w_ref[0].astype(BF16)) + b_ref[0]


def _adaln_mod(c, mod_w, mod_b):
    depth, d, width = mod_w.shape
    bsz = c.shape[0]
    c_pad = jnp.zeros((SUBLANES, d), F32).at[:bsz].set(c)
    tn = 1536
    out = pl.pallas_call(
        _mod_kernel,
        grid=(depth, width // tn),
        in_specs=[pl.BlockSpec((SUBLANES, d), lambda l, j: (0, 0)),
                  pl.BlockSpec((1, d, tn), lambda l, j: (l, 0, j)),
                  pl.BlockSpec((1, 1, tn), lambda l, j: (l, 0, j))],
        out_specs=pl.BlockSpec((1, SUBLANES, tn), lambda l, j: (l, 0, j)),
        out_shape=jax.ShapeDtypeStruct((depth, SUBLANES, width), F32),
        compiler_params=_cparams(("arbitrary", "arbitrary")),
        name="adaln_mod",
    )(c_pad, mod_w, mod_b.reshape(depth, 1, width))
    return out[:, :bsz]


def _rms_mod(x, nw, shift, scale):
    y = x * lax.rsqrt(jnp.mean(x * x, axis=-1, keepdims=True) + NORM_EPS) * nw
    return y * (1.0 + scale) + shift


def _normmod_kernel(x_ref, nw_ref, sh_ref, sc_ref, h_ref):
    h_ref[...] = _rms_mod(x_ref[...], nw_ref[...], sh_ref[0], sc_ref[0]).astype(h_ref.dtype)


def _norm_mod(x2, nw, shift, scale, bsz, seq):
    t, d = x2.shape
    tm = ROW_TM
    ns = seq // tm
    return pl.pallas_call(
        _normmod_kernel,
        grid=(bsz, ns),
        in_specs=[pl.BlockSpec((tm, d), lambda b, s: (b * ns + s, 0)),
                  pl.BlockSpec((1, d), lambda b, s: (0, 0)),
                  pl.BlockSpec((1, 1, d), lambda b, s: (b, 0, 0)),
                  pl.BlockSpec((1, 1, d), lambda b, s: (b, 0, 0))],
        out_specs=pl.BlockSpec((tm, d), lambda b, s: (b * ns + s, 0)),
        out_shape=jax.ShapeDtypeStruct((t, d), BF16),
        compiler_params=_cparams(("arbitrary", "arbitrary")),
        name="norm_mod",
    )(x2, nw.reshape(1, d), shift.reshape(bsz, 1, d), scale.reshape(bsz, 1, d))


def _block_mid_ref(b, n2):
    rows, width = b.shape
    n = n2 // 2
    if n2 >= 2 * SUBLANES:
        b3 = b.reshape(rows // n2, n2, width)
        return jnp.broadcast_to(b3[:, n - 1:n, :], b3.shape).reshape(rows, width)
    b3 = b.reshape(rows // SUBLANES, SUBLANES, width)
    sub = _iota(b3.shape, 1)
    r3 = jnp.broadcast_to(b3[:, SUBLANES - n2 + n - 1:SUBLANES - n2 + n, :], b3.shape)
    for g in range(SUBLANES // n2 - 2, -1, -1):
        r3 = jnp.where(sub < (g + 1) * n2, b3[:, g * n2 + n - 1:g * n2 + n, :], r3)
    return r3.reshape(rows, width)


def _hgrn2_kernel(h_ref, w_ref, lb_ref, nw_ref, o_ref, proj_ref, st_ref):
    L = CHUNK
    dk = HG_DK
    n_heads = st_ref.shape[0]
    width = n_heads * dk

    @pl.when(pl.program_id(1) == 0)
    def _():
        st_ref[...] = jnp.zeros_like(st_ref)

    proj_ref[...] = _dot(h_ref[...], w_ref[...])
    lb = lb_ref[...]
    f = lb + (1.0 - lb) * _sigmoid(proj_ref[:, width:2 * width])
    logf = jnp.log(f)
    t_i = _iota((L, L), 0)
    s_i = _iota((L, L), 1)
    tril = (t_i >= s_i).astype(BF16)
    b_all = _dot_sel(tril, logf)
    eye = t_i == s_i

    levels = []
    n2 = L
    while n2 >= 2:
        n = n2 // 2
        sh = n2.bit_length() - 1
        m = ((t_i >> sh) == (s_i >> sh)) & ((t_i & (n2 - 1)) >= n) & ((s_i & (n2 - 1)) < n)
        levels.append((n2, m))
        n2 = n

    for h in range(n_heads):
        sl = slice(h * dk, (h + 1) * dk)
        q = proj_ref[:, sl]
        k = 1.0 - f[:, sl]
        v = proj_ref[:, 2 * width + h * dk:2 * width + (h + 1) * dk]
        b = b_all[:, sl]
        st = st_ref[h]
        b_last = b[L - 1:L, :]
        o = _dot_nt((q * jnp.exp(b)).astype(BF16), st.astype(BF16))
        a = jnp.where(eye, jnp.sum(q * k, axis=-1, keepdims=True), 0.0)
        for n2, m in levels:
            e = jnp.exp(-jnp.abs(b - _block_mid_ref(b, n2)))
            a = jnp.where(m, _dot_nt((q * e).astype(BF16), (k * e).astype(BF16)), a)
        o = o + _dot(a.astype(BF16), v.astype(BF16))
        ke = k * jnp.exp(b_last - b)
        st_ref[h] = st * jnp.exp(b_last) + _dot_tn(v.astype(BF16), ke.astype(BF16))
        g = proj_ref[:, 3 * width + h * dk:3 * width + (h + 1) * dk]
        ms = jnp.mean(o * o, axis=-1, keepdims=True)
        o_ref[:, sl] = (o * lax.rsqrt(ms + NORM_EPS) * nw_ref[:, sl] * _silu(g)).astype(o_ref.dtype)


def _hgrn2(h, w_hg, lb, hg_norm, bsz, seq):
    t, d = h.shape
    width = lb.shape[0]
    n_heads = width // HG_DK
    ns = seq // CHUNK
    return pl.pallas_call(
        _hgrn2_kernel,
        grid=(bsz, ns),
        in_specs=[pl.BlockSpec((CHUNK, d), lambda b, s: (b * ns + s, 0)),
                  pl.BlockSpec(w_hg.shape, lambda b, s: (0, 0)),
                  pl.BlockSpec((1, width), lambda b, s: (0, 0)),
                  pl.BlockSpec((1, width), lambda b, s: (0, 0))],
        out_specs=pl.BlockSpec((CHUNK, width), lambda b, s: (b * ns + s, 0)),
        out_shape=jax.ShapeDtypeStruct((t, width), BF16),
        scratch_shapes=[pltpu.VMEM((CHUNK, w_hg.shape[1]), F32), pltpu.VMEM((n_heads, HG_DK, HG_DK), F32)],
        compiler_params=_cparams(("arbitrary", "arbitrary")),
        name="hgrn2_scan",
    )(h, w_hg, lb.reshape(1, width), hg_norm.reshape(1, width))


def _ssd_kernel(h_ref, w_ref, cw_ref, cb_ref, dtb_ref, a_ref, dsk_ref, nw_ref, o_ref,
                proj_ref, carry_ref, xpad_ref, st_ref):
    L = CHUNK
    width = o_ref.shape[1]
    xw = cw_ref.shape[1]
    n_heads = width // SSM_P
    gw = width // SSM_GROUPS
    heads_per_group = n_heads // SSM_GROUPS

    @pl.when(pl.program_id(1) == 0)
    def _():
        carry_ref[...] = jnp.zeros_like(carry_ref)
        st_ref[...] = jnp.zeros_like(st_ref)

    proj_ref[...] = _dot(h_ref[...], w_ref[...])
    z_ref = proj_ref.at[:, 0:width]
    dt_ref = proj_ref.at[:, width + xw:]
    xraw = proj_ref[:, width:width + xw]
    xpad_ref[0:SUBLANES, :] = carry_ref[...]
    xpad_ref[SUBLANES:SUBLANES + L, :] = xraw
    carry_ref[...] = xraw[L - SUBLANES:L, :]
    acc = cb_ref[...] + jnp.zeros_like(xraw)
    for j in range(SSM_CONV):
        acc = acc + cw_ref[j:j + 1, :] * xpad_ref[pl.ds(SUBLANES - (SSM_CONV - 1) + j, L), :]
    xc = _silu(acc)
    xs = xc[:, :width]
    bm = xc[:, width:width + SSM_GROUPS * SSM_N]
    cm = xc[:, width + SSM_GROUPS * SSM_N:]

    dt = _softplus(dt_ref[...] + dtb_ref[...])
    da = dt * a_ref[...]
    da_t = da.T
    dt_t = dt.T
    t_i = _iota((L, L), 0)
    s_i = _iota((L, L), 1)
    triu = (t_i <= s_i).astype(BF16)
    acs_t = _dot_rsel(da_t, triu)
    causal = t_i >= s_i
    diag = t_i == s_i
    lane_lo = _iota((L, 2 * SSM_P), 1) < SSM_P
    bd_mask = (_iota((2 * L, 2 * SSM_P), 0) < L) == (_iota((2 * L, 2 * SSM_P), 1) < SSM_P)

    y_pairs = []
    for g in range(SSM_GROUPS):
        bg = bm[:, g * SSM_N:(g + 1) * SSM_N]
        cg = cm[:, g * SSM_N:(g + 1) * SSM_N]
        cb = _dot_nt(cg.astype(BF16), bg.astype(BF16))
        hg = st_ref[:, g * gw:(g + 1) * gw]
        yoff_g = _dot(cg.astype(BF16), hg.astype(BF16))
        xsc_parts, decay_parts = [], []
        for pr in range(heads_per_group // 2):
            j0 = g * heads_per_group + 2 * pr
            gs, ds, ecol, elast = [], [], [], []
            for j in (j0, j0 + 1):
                row_b = jnp.broadcast_to(acs_t[j:j + 1, :], (L, L))
                col_b = row_b.T
                dt_row = jnp.broadcast_to(dt_t[j:j + 1, :], (L, L))
                lmat = jnp.exp(jnp.minimum(col_b - row_b, 0.0))
                gs.append(jnp.where(causal, cb * lmat * dt_row, 0.0))
                a_last = acs_t[j:j + 1, L - 1:L]
                ds.append(jnp.where(diag, jnp.exp(a_last - row_b) * dt_row, 0.0))
                ecol.append(jnp.exp(col_b))
                elast.append(jnp.exp(a_last))
            lhs = jnp.concatenate([jnp.concatenate(gs, axis=1), jnp.concatenate(ds, axis=1)], axis=0)
            xs_pair = xs[:, j0 * SSM_P:(j0 + 2) * SSM_P]
            bd = jnp.where(bd_mask, jnp.concatenate([xs_pair, xs_pair], axis=0), 0.0)
            res = _dot(lhs.astype(BF16), bd.astype(BF16))
            yoff = yoff_g[:, pr * 2 * SSM_P:(pr + 1) * 2 * SSM_P] * jnp.where(lane_lo, ecol[0], ecol[1])
            y_pairs.append(res[:L] + yoff)
            xsc_parts.append(res[L:])
            decay_parts.append(jnp.where(lane_lo[0:1], elast[0], elast[1]))
        xsc_g = jnp.concatenate(xsc_parts, axis=1)
        decay_g = jnp.concatenate(decay_parts, axis=1)
        st_ref[:, g * gw:(g + 1) * gw] = hg * decay_g + _dot_tn(bg.astype(BF16), xsc_g.astype(BF16))
    y = jnp.concatenate(y_pairs, axis=1) + dsk_ref[...] * xs
    yz = y * _silu(z_ref[...])
    for g in range(SSM_GROUPS):
        seg = yz[:, g * gw:(g + 1) * gw]
        ms = jnp.mean(seg * seg, axis=-1, keepdims=True)
        o_ref[:, g * gw:(g + 1) * gw] = (seg * lax.rsqrt(ms + NORM_EPS) * nw_ref[:, g * gw:(g + 1) * gw]).astype(o_ref.dtype)


def _ssd(h, w_ssd, conv_w, conv_b, dt_bias, a_log, d_skip, ssm_norm, bsz, seq):
    t, d = h.shape
    width = ssm_norm.shape[0]
    xw = conv_w.shape[1]
    n_heads = width // SSM_P
    ns = seq // CHUNK
    pad = lambda v: jnp.zeros((1, LANES), F32).at[0, :n_heads].set(v)
    row = lambda w: pl.BlockSpec((CHUNK, w), lambda b, s: (b * ns + s, 0))
    const = lambda r, w: pl.BlockSpec((r, w), lambda b, s: (0, 0))
    return pl.pallas_call(
        _ssd_kernel,
        grid=(bsz, ns),
        in_specs=[row(d), const(*w_ssd.shape), const(SSM_CONV, xw), const(1, xw), const(1, LANES),
                  const(1, LANES), const(1, width), const(1, width)],
        out_specs=row(width),
        out_shape=jax.ShapeDtypeStruct((t, width), BF16),
        scratch_shapes=[pltpu.VMEM((CHUNK, w_ssd.shape[1]), F32), pltpu.VMEM((SUBLANES, xw), F32),
                        pltpu.VMEM((CHUNK + SUBLANES, xw), F32), pltpu.VMEM((SSM_N, width), F32)],
        compiler_params=_cparams(("arbitrary", "arbitrary")),
        name="ssd_scan",
    )(h, w_ssd, conv_w, conv_b.reshape(1, xw), pad(dt_bias), pad(-jnp.exp(a_log)),
      jnp.repeat(d_skip, SSM_P).reshape(1, width), ssm_norm.reshape(1, width))


def _route(probs):
    n_exp = N_GROUPS_MOE * 4
    p = [probs[e:e + 1, :] for e in range(n_exp)]
    gs = []
    for g in range(N_GROUPS_MOE):
        a, b, c, d = p[4 * g:4 * g + 4]
        gs.append(jnp.maximum(jnp.maximum(jnp.maximum(a + b, a + c), jnp.maximum(a + d, b + c)),
                              jnp.maximum(b + d, c + d)))
    best = jnp.zeros_like(gs[0]).astype(I32)
    bs = gs[0]
    for g in range(1, N_GROUPS_MOE):
        upd = gs[g] > bs
        best = jnp.where(upd, g, best)
        bs = jnp.where(upd, gs[g], bs)
    q = [jnp.where(best == 0, p[i], jnp.where(best == 1, p[4 + i], jnp.where(best == 2, p[8 + i], p[12 + i])))
         for i in range(4)]
    i1 = jnp.zeros_like(best)
    v1 = q[0]
    for i in range(1, 4):
        upd = q[i] > v1
        i1 = jnp.where(upd, i, i1)
        v1 = jnp.where(upd, q[i], v1)
    i2 = jnp.zeros_like(best)
    v2 = jnp.full_like(v1, -1.0)
    for i in range(4):
        upd = (i1 != i) & (q[i] > v2)
        i2 = jnp.where(upd, i, i2)
        v2 = jnp.where(upd, q[i], v2)
    den = v1 + v2
    return best * 4 + i1, best * 4 + i2, v1 / den, v2 / den


def _head_stats_expand(stack, seg_ref):
    pw = seg_ref.shape[0]
    seg = seg_ref[...]
    return jnp.concatenate([_dot_rsel(stack[:, p * pw:(p + 1) * pw], seg) for p in range(stack.shape[1] // pw)], axis=1)


def _epilogue_kernel(*refs, n_in, rwkv):
    ins = refs[:n_in]
    pos = n_in
    w_refs = refs[pos:pos + n_in]
    pos += n_in
    if rwkv:
        g_ref, bonus_ref, lnw_ref, lnb_ref, seg_ref = refs[pos:pos + 5]
        pos += 5
    x_ref, gate_ref, nw_ref, sh_ref, sc_ref, rw_ref, rb_ref = refs[pos:pos + 7]
    pos += 7
    xo_ref, h_ref, ti_ref, tw_ref = refs[pos:pos + 4]

    tm = ROW_TM
    rw_hi, rw_lo = _split(rw_ref[...])
    for sub in range(x_ref.shape[0] // tm):
        rs = pl.ds(sub * tm, tm)
        if rwkv:
            o = ins[0][rs, :]
            inv_n = 1.0 / RW_N
            st = _head_stats_expand(jnp.concatenate([o, o * o], axis=0), seg_ref) * inv_n
            mean = st[:tm]
            var = jnp.maximum(st[tm:] - mean * mean, 0.0)
            o = (o - mean) * lax.rsqrt(var + RW_GN_EPS) * lnw_ref[...] + lnb_ref[...]
            o = (o + bonus_ref[rs, :].astype(F32)) * g_ref[rs, :].astype(F32)
            y = _dot(o.astype(BF16), w_refs[0][...])
        else:
            y = _dot(ins[0][rs, :], w_refs[0][...])
            for a_ref, w_ref in zip(ins[1:], w_refs[1:]):
                y = y + _dot(a_ref[rs, :], w_ref[...])
        x_new = x_ref[rs, :] + gate_ref[0] * y
        xo_ref[rs, :] = x_new
        h = _rms_mod(x_new, nw_ref[...], sh_ref[0], sc_ref[0])
        _store_rows(h_ref.at[pl.ds(sub * tm * ROW_PITCH, tm * ROW_PITCH)], h)
        h_hi, h_lo = _split(h)
        logits = _dot_nt(rw_hi, h_hi) + _dot_nt(rw_hi, h_lo) + _dot_nt(rw_lo, h_hi) + rb_ref[...]
        mx = jnp.max(logits, axis=0, keepdims=True)
        ex = jnp.exp(logits - mx)
        probs = ex / jnp.sum(ex, axis=0, keepdims=True)
        e1, e2, w1, w2 = _route(probs)
        zi = jnp.zeros((SUBLANES - TOP_K, tm), I32)
        ti_ref[:, sub * tm:(sub + 1) * tm] = jnp.concatenate([e1, e2, zi], axis=0)
        tw_ref[:, sub * tm:(sub + 1) * tm] = jnp.concatenate([w1, w2, zi.astype(F32)], axis=0)


def _epilogue(ins, ws, x2, gate, nw, shift, scale, router_w, router_b, bsz, seq, rwkv_extra=None):
    t, d = x2.shape
    tm = EPILOGUE_SUBTILES * ROW_TM
    ns = seq // tm
    n_exp = router_w.shape[1]
    rw_t = jnp.zeros((LANES, d), F32).at[:n_exp].set(router_w.T)
    rb = jnp.full((LANES, 1), -1e30, F32).at[:n_exp, 0].set(router_b)
    row = lambda w: pl.BlockSpec((tm, w), lambda b, s: (b * ns + s, 0))
    const = lambda a: pl.BlockSpec(a.shape, lambda b, s: (0,) * a.ndim)
    per_b = pl.BlockSpec((1, 1, d), lambda b, s: (b, 0, 0))
    args = list(ins) + list(ws)
    specs = [row(a.shape[1]) for a in ins] + [const(w) for w in ws]
    if rwkv_extra is not None:
        g, bonus, lnw, lnb, seg = rwkv_extra
        args += [g, bonus, lnw.reshape(1, d), lnb.reshape(1, d), seg]
        specs += [row(d), row(d), pl.BlockSpec((1, d), lambda b, s: (0, 0)), pl.BlockSpec((1, d), lambda b, s: (0, 0)),
                  const(seg)]
    args += [x2, gate.reshape(bsz, 1, d), nw.reshape(1, d), shift.reshape(bsz, 1, d), scale.reshape(bsz, 1, d), rw_t, rb]
    specs += [row(d), per_b, pl.BlockSpec((1, d), lambda b, s: (0, 0)), per_b, per_b, const(rw_t), const(rb)]
    lane_row = pl.BlockSpec((SUBLANES, tm), lambda b, s: (0, b * ns + s))
    return pl.pallas_call(
        functools.partial(_epilogue_kernel, n_in=len(ins), rwkv=rwkv_extra is not None),
        grid=(bsz, ns),
        in_specs=specs,
        out_specs=[row(d), pl.BlockSpec((tm * ROW_PITCH, LANES), lambda b, s: (b * ns + s, 0)), lane_row, lane_row],
        out_shape=[jax.ShapeDtypeStruct((t, d), F32), jax.ShapeDtypeStruct((t * ROW_PITCH, LANES), F32),
                   jax.ShapeDtypeStruct((SUBLANES, t), I32), jax.ShapeDtypeStruct((SUBLANES, t), F32)],
        compiler_params=_cparams(("arbitrary", "arbitrary")),
        name="mixer_epilogue",
    )(*args)


def _route_plan(top_i, n_exp, tm):
    t = top_i.shape[1]
    n_pairs = TOP_K * t
    n_tiles = n_pairs // tm + n_exp
    e_flat = top_i[:TOP_K].reshape(-1)
    onehot = (e_flat[:, None] == jnp.arange(n_exp, dtype=I32)[None, :]).astype(I32)
    csum = jnp.cumsum(onehot, axis=0)
    counts = csum[-1]
    rank = jnp.sum((csum - 1) * onehot, axis=1)
    padded = ((counts + tm - 1) // tm) * tm
    ends = jnp.cumsum(padded)
    pos = (ends - padded)[e_flat] + rank
    tile_start = jnp.arange(n_tiles, dtype=I32) * tm
    tile_exp = jnp.minimum(jnp.sum((tile_start[:, None] >= ends[None, :]).astype(I32), axis=1), n_exp - 1)
    n_used = (ends[-1] // tm).reshape(1).astype(I32)
    gap_lo = jnp.concatenate([ends - padded + counts, ends[-1:]]).astype(I32)
    gap_hi = jnp.concatenate([ends, jnp.full((1,), n_tiles * tm, I32)]).astype(I32)
    return pos.astype(I32), tile_exp.astype(I32), n_used, gap_lo, gap_hi


def _dispatch_kernel(pos_ref, glo_ref, ghi_ref, h_ref, xs_hbm, zbuf, sem, zsem):
    tm = DISPATCH_TM
    n_tok = pos_ref.shape[0] // TOP_K
    i = pl.program_id(0)

    @pl.when(i == 0)
    def _():
        zbuf[...] = jnp.zeros_like(zbuf)

        def chunk_copy(row):
            return pltpu.make_async_copy(zbuf, xs_hbm.at[pl.ds(row * ROW_PITCH, ZERO_CHUNK * ROW_PITCH)], zsem)

        def row_copy(row):
            return pltpu.make_async_copy(zbuf.at[pl.ds(0, ROW_PITCH)], xs_hbm.at[pl.ds(row * ROW_PITCH, ROW_PITCH)], zsem)

        for e in range(glo_ref.shape[0]):
            lo = glo_ref[e]
            hi = ghi_ref[e]
            n_chunks = lax.div(hi - lo, ZERO_CHUNK)
            mid = lo + n_chunks * ZERO_CHUNK

            def start_chunk(c, carry):
                chunk_copy(lo + c * ZERO_CHUNK).start()
                return carry

            def start_row(r, carry):
                row_copy(r).start()
                return carry

            def wait_chunk(c, carry):
                chunk_copy(lo).wait()
                return carry

            def wait_row(r, carry):
                row_copy(lo).wait()
                return carry

            lax.fori_loop(0, n_chunks, start_chunk, 0)
            lax.fori_loop(mid, hi, start_row, 0)
            lax.fori_loop(0, n_chunks, wait_chunk, 0)
            lax.fori_loop(mid, hi, wait_row, 0)

    def body(r8, c):
        for u in range(GATHER_UNROLL):
            r = r8 * GATHER_UNROLL + u
            for k in range(TOP_K):
                p = pos_ref[k * n_tok + i * tm + r]
                pltpu.make_async_copy(h_ref.at[pl.ds(r * ROW_PITCH, ROW_PITCH)],
                                      xs_hbm.at[pl.ds(p * ROW_PITCH, ROW_PITCH)], sem).start(priority=k)
        return c
    lax.fori_loop(0, tm // GATHER_UNROLL, body, 0)
    for k in range(TOP_K):
        pltpu.make_async_copy(h_ref, xs_hbm.at[pl.ds(0, tm * ROW_PITCH)], sem).wait()


def _dispatch(h, pos, gap_lo, gap_hi, n_rows):
    tm = DISPATCH_TM
    t = h.shape[0] // ROW_PITCH
    grid_spec = pltpu.PrefetchScalarGridSpec(
        num_scalar_prefetch=3,
        grid=(t // tm,),
        in_specs=[pl.BlockSpec((tm * ROW_PITCH, LANES), lambda i, p, lo, hi: (i, 0))],
        out_specs=pl.BlockSpec(memory_space=pl.ANY),
        scratch_shapes=[pltpu.VMEM((ZERO_CHUNK * ROW_PITCH, LANES), F32), pltpu.SemaphoreType.DMA(()),
                        pltpu.SemaphoreType.DMA(())],
    )
    return pl.pallas_call(
        _dispatch_kernel,
        grid_spec=grid_spec,
        out_shape=jax.ShapeDtypeStruct((n_rows * ROW_PITCH, LANES), F32),
        compiler_params=_cparams(("arbitrary",)),
        name="moe_dispatch",
    )(pos, gap_lo, gap_hi, h)


def _moe_kernel(texp_ref, nused_ref, x_ref, w1_ref, w3_ref, w2_ref, y_ref, wb1, wb3, wb2):
    tm = MOE_TM
    d = wb1.shape[0]
    i = pl.program_id(0)
    n_used = nused_ref[0]

    @pl.when((i == 0) | (texp_ref[i] != texp_ref[jnp.maximum(i - 1, 0)]))
    def _():
        wb1[...] = w1_ref[0, 0].astype(BF16)
        wb3[...] = w3_ref[0, 0].astype(BF16)
        wb2[...] = w2_ref[0, 0].astype(BF16)

    @pl.when(i < n_used)
    def _():
        x = _load_rows(x_ref, tm, d).astype(BF16)
        a = _dot(x, wb1[...])
        b = _dot(x, wb3[...])
        hid = (_silu(a) * b).astype(BF16)
        _store_rows(y_ref, _dot(hid, wb2[...]))

    @pl.when(i >= n_used)
    def _():
        y_ref[...] = jnp.zeros_like(y_ref)


def _moe(xs, w1, w3, w2, layer, tile_exp, n_used):
    _, n_exp, d, dff = w1.shape
    tm = MOE_TM
    n_tiles = tile_exp.shape[0]
    rows = pl.BlockSpec((tm * ROW_PITCH, LANES), lambda i, te, nu: (i, 0))
    grid_spec = pltpu.PrefetchScalarGridSpec(
        num_scalar_prefetch=2,
        grid=(n_tiles,),
        in_specs=[rows,
                  pl.BlockSpec((1, 1, d, dff), lambda i, te, nu: (layer, te[i], 0, 0)),
                  pl.BlockSpec((1, 1, d, dff), lambda i, te, nu: (layer, te[i], 0, 0)),
                  pl.BlockSpec((1, 1, dff, d), lambda i, te, nu: (layer, te[i], 0, 0))],
        out_specs=rows,
        scratch_shapes=[pltpu.VMEM((d, dff), BF16), pltpu.VMEM((d, dff), BF16), pltpu.VMEM((dff, d), BF16)],
    )
    return pl.pallas_call(
        _moe_kernel,
        grid_spec=grid_spec,
        out_shape=jax.ShapeDtypeStruct((n_tiles * tm * ROW_PITCH, LANES), F32),
        compiler_params=_cparams(("arbitrary",)),
        name="moe_experts",
    )(tile_exp, n_used, xs, w1, w3, w2)


def _combine_kernel(pos_ref, y_hbm, x_ref, tw_ref, gate_ref, nw_ref, sh_ref, sc_ref, *out_and_scratch, final, n_tok):
    if final:
        o_ref, ybuf, sem = out_and_scratch
    else:
        xo_ref, h_ref, ybuf, sem = out_and_scratch
    tm, d = x_ref.shape
    n_sub = d // LANES
    i = pl.program_id(0)
    n_steps = pl.num_programs(0)
    slot = i % 2

    def start_gather(tile, sl):
        def body(r8, c):
            for u in range(GATHER_UNROLL):
                r = r8 * GATHER_UNROLL + u
                for k in range(TOP_K):
                    src = pos_ref[k * n_tok + tile * tm + r] * ROW_PITCH
                    pltpu.make_async_copy(y_hbm.at[pl.ds(src, n_sub)], ybuf.at[sl, k, pl.ds(r * ROW_PITCH, n_sub)],
                                          sem.at[sl]).start(priority=k)
            return c
        lax.fori_loop(0, tm // GATHER_UNROLL, body, 0)

    @pl.when(i == 0)
    def _():
        start_gather(0, 0)

    @pl.when(i + 1 < n_steps)
    def _():
        start_gather(i + 1, 1 - slot)

    for k in range(TOP_K):
        pltpu.make_async_copy(y_hbm.at[pl.ds(0, tm * n_sub)], ybuf.at[slot, k, pl.ds(0, tm * n_sub)], sem.at[slot]).wait()
    tw = tw_ref[...]
    moe = tw[:, 0:1] * _load_rows(ybuf.at[slot, 0], tm, d) + tw[:, 1:2] * _load_rows(ybuf.at[slot, 1], tm, d)
    x_new = x_ref[...] + gate_ref[0] * moe
    if final:
        o_ref[...] = x_new * lax.rsqrt(jnp.mean(x_new * x_new, axis=-1, keepdims=True) + NORM_EPS) * nw_ref[...]
    else:
        xo_ref[...] = x_new
        h_ref[...] = _rms_mod(x_new, nw_ref[...], sh_ref[0], sc_ref[0]).astype(h_ref.dtype)


def _combine(pos, y_sorted, x2, top_w, gate, nw, shift, scale, bsz, seq, final):
    t, d = x2.shape
    tm = ROW_TM
    ns = seq // tm
    tw = top_w.T
    row = pl.BlockSpec((tm, d), lambda i, p: (i, 0))
    per_b = pl.BlockSpec((1, 1, d), lambda i, p: (i // ns, 0, 0))
    grid_spec = pltpu.PrefetchScalarGridSpec(
        num_scalar_prefetch=1,
        grid=(t // tm,),
        in_specs=[pl.BlockSpec(memory_space=pl.ANY), row,
                  pl.BlockSpec((tm, SUBLANES), lambda i, p: (i, 0)),
                  per_b, pl.BlockSpec((1, d), lambda i, p: (0, 0)), per_b, per_b],
        out_specs=row if final else [row, row],
        scratch_shapes=[pltpu.VMEM((2, TOP_K, tm * ROW_PITCH, LANES), F32), pltpu.SemaphoreType.DMA((2,))],
    )
    out_shape = (jax.ShapeDtypeStruct((t, d), F32) if final else
                 [jax.ShapeDtypeStruct((t, d), F32), jax.ShapeDtypeStruct((t, d), BF16)])
    return pl.pallas_call(
        functools.partial(_combine_kernel, final=final, n_tok=t),
        grid_spec=grid_spec,
        out_shape=out_shape,
        compiler_params=_cparams(("arbitrary",)),
        name="moe_combine",
    )(pos, y_sorted, x2, tw, gate.reshape(bsz, 1, d), nw.reshape(1, d), shift.reshape(bsz, 1, d), scale.reshape(bsz, 1, d))


def _rwkv_prep_kernel(h_ref, mu_ref, wr_ref, wk_ref, wv_ref, dec0_ref, dec1_ref, dec2_ref, a0_ref, a1_ref, a2_ref,
                      g1_ref, g2_ref, kk_ref, ka_ref, rk_ref, seg_ref,
                      r_out, lw_out, k_out, v_out, kkn_out, a_out, g_out, bonus_out, carry_ref, hp_ref):
    tm = h_ref.shape[0]

    @pl.when(pl.program_id(1) == 0)
    def _():
        carry_ref[...] = jnp.zeros_like(carry_ref)

    h = h_ref[...].astype(F32)
    hp_ref[0:SUBLANES, :] = carry_ref[...]
    hp_ref[SUBLANES:SUBLANES + tm, :] = h
    carry_ref[...] = h[tm - SUBLANES:tm, :]
    xx = hp_ref[pl.ds(SUBLANES - 1, tm), :] - h
    mix = lambda i: (h + xx * mu_ref[i:i + 1, :]).astype(BF16)
    r = _dot(mix(0), wr_ref[...])
    k = _dot(mix(1), wk_ref[...])
    v = _dot(mix(2), wv_ref[...])
    wl = dec0_ref[...] + _dot(jnp.tanh(_dot(mix(3), dec1_ref[...])).astype(BF16), dec2_ref[...])
    lw = -jnp.exp(-_softplus(-wl) - 0.5)
    a = _sigmoid(a0_ref[...] + _dot(_dot(mix(4), a1_ref[...]).astype(BF16), a2_ref[...]))
    g = _dot(_sigmoid(_dot(mix(5), g1_ref[...])).astype(BF16), g2_ref[...])
    kk = k * kk_ref[...]
    k_h = k * (1.0 + (a - 1.0) * ka_ref[...])
    st = _head_stats_expand(jnp.concatenate([kk * kk, r * k_h * rk_ref[...]], axis=0), seg_ref)
    kkn = kk / jnp.maximum(jnp.sqrt(st[:tm]), 1e-12)
    r_out[...] = r.astype(r_out.dtype)
    lw_out[...] = lw
    k_out[...] = k_h.astype(k_out.dtype)
    v_out[...] = v.astype(v_out.dtype)
    kkn_out[...] = kkn.astype(kkn_out.dtype)
    a_out[...] = a.astype(a_out.dtype)
    g_out[...] = g.astype(g_out.dtype)
    bonus_out[...] = (st[tm:] * v).astype(bonus_out.dtype)


def _rwkv_prep(h, mu, w_rkv, dec0, dec1, dec2, a0, a1, a2, g1, g2, k_k, k_a, r_k, seg, bsz, seq):
    t, d = h.shape
    tm = ROW_TM
    ns = seq // tm
    padc = lambda w: jnp.zeros((d, LANES), F32).at[:, :w.shape[1]].set(w).astype(BF16)
    padr = lambda w: jnp.zeros((LANES, d), F32).at[:w.shape[0]].set(w).astype(BF16)
    vec = lambda v: v.reshape(1, d)
    args = [h, mu, w_rkv[0].astype(BF16), w_rkv[1].astype(BF16), w_rkv[2].astype(BF16), vec(dec0), padc(dec1), padr(dec2),
            vec(a0), padc(a1), padr(a2), padc(g1), padr(g2), vec(k_k), vec(k_a), vec(r_k), seg]
    row = pl.BlockSpec((tm, d), lambda b, s: (b * ns + s, 0))
    const = lambda a: pl.BlockSpec(a.shape, lambda b, s: (0,) * a.ndim)
    outs = [BF16, F32, BF16, BF16, BF16, BF16, BF16, BF16]
    return pl.pallas_call(
        _rwkv_prep_kernel,
        grid=(bsz, ns),
        in_specs=[row] + [const(a) for a in args[1:]],
        out_specs=[row] * len(outs),
        out_shape=[jax.ShapeDtypeStruct((t, d), dt) for dt in outs],
        scratch_shapes=[pltpu.VMEM((SUBLANES, d), F32), pltpu.VMEM((tm + SUBLANES, d), F32)],
        compiler_params=_cparams(("arbitrary", "arbitrary")),
        name="rwkv_prep",
    )(*args)


def _rwkv_scan_kernel(r_ref, lw_ref, k_ref, v_ref, kk_ref, a_ref, y_ref, st_ref):
    L = RW_CHUNK
    pw = RW_PACK * RW_N
    n_packs = st_ref.shape[0]
    sh = RW_N.bit_length() - 1

    @pl.when(pl.program_id(1) == 0)
    def _():
        st_ref[...] = jnp.zeros_like(st_ref)

    t_i = _iota((L, L), 0)
    s_i = _iota((L, L), 1)
    tril = (t_i >= s_i).astype(BF16)
    wc_all = _dot_sel(tril, lw_ref[...])
    lane_head = _iota((L, pw), 1) >> sh
    s_loc = _iota((L, pw), 1) & (RW_N - 1)
    t_loc = _iota((L, pw), 0)
    strict = s_loc < t_loc
    incl = s_loc <= t_loc
    bd_mask = (_iota((pw, pw), 0) >> sh) == (_iota((pw, pw), 1) >> sh)

    def bdiag(x):
        return jnp.where(bd_mask, jnp.concatenate([x] * RW_PACK, axis=0), 0.0).astype(BF16)

    packs = range(n_packs)
    sls = [slice(p * pw, (p + 1) * pw) for p in packs]
    pr, vs, sts, kkas, ks, wcs = [], [], [], [], [], []
    for p in packs:
        sl = sls[p]
        r = r_ref[:, sl].astype(F32)
        lw = lw_ref[:, sl]
        k = k_ref[:, sl].astype(F32)
        kk = kk_ref[:, sl].astype(F32)
        a = a_ref[:, sl].astype(F32)
        wc = wc_all[:, sl]
        e_inv = jnp.exp(-wc)
        kka = kk * a
        al = -kk * jnp.exp(wc - lw)
        rb = r * jnp.exp(wc)
        bt = kka * e_inv
        kt = k * e_inv
        lhs = jnp.concatenate([al, rb], axis=0).astype(BF16)
        rows = [jnp.where(lane_head == hh, x, 0.0) for x in (bt, kt) for hh in range(RW_PACK)]
        st = st_ref[p]
        m = jnp.concatenate(rows + [st], axis=0).astype(BF16)
        pr.append(_dot_nt(lhs, m))
        vs.append(v_ref[:, sl].astype(F32))
        sts.append(st)
        kkas.append(kka)
        ks.append(k)
        wcs.append(wc)
    bd_vs = [bdiag(vs[p]) for p in packs]
    us = [pr[p][:L, 2 * pw:] + _dot(jnp.where(strict, pr[p][:L, pw:2 * pw], 0.0).astype(BF16), bd_vs[p]) for p in packs]
    nmats = [jnp.where(strict, pr[p][:L, 0:pw], 0.0) for p in packs]
    n_steps = L.bit_length() - 1
    for it in range(n_steps):
        us = [us[p] + _dot(nmats[p].astype(BF16), bdiag(us[p])) for p in packs]
        if it + 1 < n_steps:
            nmats = [_dot(nmats[p].astype(BF16), bdiag(nmats[p])) for p in packs]
    for p in packs:
        a_rb = jnp.where(incl, pr[p][L:, 0:pw], 0.0)
        a_rk = jnp.where(incl, pr[p][L:, pw:2 * pw], 0.0)
        y_ref[:, sls[p]] = pr[p][L:, 2 * pw:] + _dot(jnp.concatenate([a_rb, a_rk], axis=1).astype(BF16),
                                                     jnp.concatenate([bdiag(us[p]), bd_vs[p]], axis=0))
    for p in packs:
        w_last = wcs[p][L - 1:L, :]
        e_last = jnp.exp(w_last - wcs[p])
        upd = _dot_tn(jnp.concatenate([us[p], vs[p]], axis=0).astype(BF16),
                      jnp.concatenate([kkas[p] * e_last, ks[p] * e_last], axis=0).astype(BF16))
        st_ref[p] = jnp.where(bd_mask, sts[p] * jnp.exp(w_last) + upd, 0.0)


def _rwkv_scan(r, lw, k, v, kk, a, bsz, seq):
    t, d = r.shape
    ns = seq // RW_CHUNK
    pw = RW_PACK * RW_N
    row = pl.BlockSpec((RW_CHUNK, d), lambda b, s: (b * ns + s, 0))
    return pl.pallas_call(
        _rwkv_scan_kernel,
        grid=(bsz, ns),
        in_specs=[row] * 6,
        out_specs=row,
        out_shape=jax.ShapeDtypeStruct((t, d), F32),
        scratch_shapes=[pltpu.VMEM((d // pw, pw, pw), F32)],
        compiler_params=_cparams(("arbitrary", "arbitrary")),
        name="rwkv_scan",
    )(r, lw, k, v, kk, a)


def _moe_block(h, top_i, top_w, x2, gate, w1, w3, w2, layer, nw, shift, scale, bsz, seq, final):
    n_exp = w1.shape[1]
    pos, tile_exp, n_used, gap_lo, gap_hi = _route_plan(top_i, n_exp, MOE_TM)
    x_sorted = _dispatch(h, pos, gap_lo, gap_hi, tile_exp.shape[0] * MOE_TM)
    y_sorted = _moe(x_sorted, w1, w3, w2, layer, tile_exp, n_used)
    return _combine(pos, y_sorted, x2, top_w, gate, nw, shift, scale, bsz, seq, final)


def kernel(x, c, mod_w, mod_b, norm_w, hg_lb_logits, ev_w_in, ev_hg_norm, ev_conv_w, ev_conv_b, ev_dt_bias, ev_a_log, ev_d_skip, ev_ssm_norm, ev_w_out, od_mu, od_w_rkv, od_w_dec0, od_w_dec1, od_w_dec2, od_a0, od_a1, od_a2, od_g1, od_g2, od_k_k, od_k_a, od_r_k, od_ln_w, od_ln_b, od_w_o, router_w, router_b, moe_w1, moe_w3, moe_w2, final_norm_w):
    bsz, seq, d = x.shape
    depth = mod_w.shape[0]
    t = bsz * seq
    x2 = x.reshape(t, d)
    mod = _adaln_mod(c, mod_w, mod_b)
    gamma = jax.nn.softmax(hg_lb_logits.astype(F32), axis=0)
    lower_bounds = jnp.cumsum(gamma, axis=0) - gamma[0]
    head_of_lane = jnp.arange(RW_PACK * RW_N, dtype=I32) // RW_N
    seg = (head_of_lane[:, None] == head_of_lane[None, :]).astype(BF16)

    h = None
    out = None
    for l in range(depth):
        sh_m, sc_m, gt_m, sh_f, sc_f, gt_f = [mod[l, :, i * d:(i + 1) * d] for i in range(6)]
        j = l // 2
        if h is None:
            h = _norm_mod(x2, norm_w[l, 0], sh_m, sc_m, bsz, seq)
        if l % 2 == 0:
            w_in = ev_w_in[j]
            hgw = ev_hg_norm.shape[1]
            sw = ev_ssm_norm.shape[1]
            xbw = ev_conv_w.shape[2]
            nh = ev_dt_bias.shape[1]
            c0 = 4 * hgw
            w_hg = w_in[:, :c0].astype(BF16)
            w_ssd = jnp.zeros((d, sw + xbw + LANES), F32).at[:, :sw + xbw + nh].set(w_in[:, c0:]).astype(BF16)
            o_a = _hgrn2(h, w_hg, lower_bounds[l + 1], ev_hg_norm[j], bsz, seq)
            o_b = _ssd(h, w_ssd, ev_conv_w[j], ev_conv_b[j], ev_dt_bias[j], ev_a_log[j], ev_d_skip[j],
                       ev_ssm_norm[j], bsz, seq)
            w_out = ev_w_out[j].astype(BF16)
            x2, hf, top_i, top_w = _epilogue([o_a, o_b], [w_out[:hgw], w_out[hgw:]], x2, gt_m, norm_w[l, 1], sh_f, sc_f,
                                             router_w, router_b, bsz, seq)
        else:
            r, lw, k, v, kk, a, g, bonus = _rwkv_prep(h, od_mu[j], od_w_rkv[j], od_w_dec0[j], od_w_dec1[j], od_w_dec2[j],
                                                      od_a0[j], od_a1[j], od_a2[j], od_g1[j], od_g2[j], od_k_k[j],
                                                      od_k_a[j], od_r_k[j].reshape(-1), seg, bsz, seq)
            y = _rwkv_scan(r, lw, k, v, kk, a, bsz, seq)
            x2, hf, top_i, top_w = _epilogue([y], [od_w_o[j].astype(BF16)], x2, gt_m, norm_w[l, 1], sh_f, sc_f,
                                             router_w, router_b, bsz, seq,
                                             rwkv_extra=(g, bonus, od_ln_w[j], od_ln_b[j], seg))
        final = l == depth - 1
        if final:
            nw_next, sh_next, sc_next = final_norm_w, sh_f, sc_f
        else:
            nxt = [mod[l + 1, :, i * d:(i + 1) * d] for i in range(2)]
            nw_next, sh_next, sc_next = norm_w[l + 1, 0], nxt[0], nxt[1]
        res = _moe_block(hf, top_i, top_w, x2, gt_f, moe_w1, moe_w3, moe_w2, l, nw_next, sh_next, sc_next,
                         bsz, seq, final)
        if final:
            out = res
        else:
            x2, h = res
    return out.reshape(bsz, seq, d)
```

```python
import functools

import jax
import jax.numpy as jnp
from jax import lax
from jax.experimental import pallas as pl
from jax.experimental.pallas import tpu as pltpu

F32 = jnp.float32
BF16 = jnp.bfloat16
I32 = jnp.int32

NORM_EPS = 1e-6
RW_GN_EPS = 64e-5
RW_DECAY_SCALE = 0.6065306597126334
LANES = 128
SUBLANES = 8
VMEM_LIMIT = 56 * 1024 * 1024

HG_DK = 128
SSM_P = 64
SSM_N = 128
SSM_GROUPS = 2
SSM_CONV = 4
RW_N = 64
N_GROUPS_MOE = 4
TOP_K = 2

CHUNK = 128
RW_CHUNK = 64
RW_PACK = 4
MOE_TM = 512
ROW_TM = 256
NORM_TM = 512
EPILOGUE_SUBTILES = 2
DISPATCH_TM = 512
ZERO_CHUNK = 64
GATHER_UNROLL = 8
ROW_PITCH = 9


def _cparams(sem):
    return pltpu.CompilerParams(dimension_semantics=sem, vmem_limit_bytes=VMEM_LIMIT)


def _dot(a, b):
    return lax.dot_general(a, b, (((1,), (0,)), ((), ())), preferred_element_type=F32)


def _dot_nt(a, b):
    return lax.dot_general(a, b, (((1,), (1,)), ((), ())), preferred_element_type=F32)


def _dot_tn(a, b):
    return lax.dot_general(a, b, (((0,), (0,)), ((), ())), preferred_element_type=F32)


def _split(x):
    hi = x.astype(BF16)
    return hi, (x - hi.astype(F32)).astype(BF16)


def _dot_sel(sel, x):
    hi, lo = _split(x)
    return _dot(sel, hi) + _dot(sel, lo)


def _dot_rsel(x, sel):
    hi, lo = _split(x)
    return _dot(hi, sel) + _dot(lo, sel)


def _sigmoid(x):
    return 1.0 / (1.0 + jnp.exp(-x))


def _silu(x):
    return x * _sigmoid(x)


def _softplus(x):
    return jnp.maximum(x, 0.0) + jnp.log(1.0 + jnp.exp(-jnp.abs(x)))


def _iota(shape, dim):
    return lax.broadcasted_iota(I32, shape, dim)


def _store_rows(ref, val):
    tm, width = val.shape
    for j in range(width // LANES):
        ref[pl.ds(j, tm, stride=ROW_PITCH), :] = val[:, j * LANES:(j + 1) * LANES]
    for j in range(width // LANES, ROW_PITCH):
        ref[pl.ds(j, tm, stride=ROW_PITCH), :] = jnp.zeros((tm, LANES), ref.dtype)


def _load_rows(ref, tm, width):
    return jnp.concatenate([ref[pl.ds(j, tm, stride=ROW_PITCH), :] for j in range(width // LANES)], axis=1)


def _mod_kernel(c_ref, w_ref, b_ref, o_ref):
    c = c_ref[...]
    o_ref[0] = _dot(_silu(c).astype(BF16), w_ref[0].astype(BF16)) + b_ref[0]


def _adaln_mod(c, mod_w, mod_b):
    depth, d, width = mod_w.shape
    bsz = c.shape[0]
    c_pad = jnp.zeros((SUBLANES, d), F32).at[:bsz].set(c)
    tn = 1536
    out = pl.pallas_call(
        _mod_kernel,
        grid=(depth, width // tn),
        in_specs=[pl.BlockSpec((SUBLANES, d), lambda l, j: (0, 0)),
                  pl.BlockSpec((1, d, tn), lambda l, j: (l, 0, j)),
                  pl.BlockSpec((1, 1, tn), lambda l, j: (l, 0, j))],
        out_specs=pl.BlockSpec((1, SUBLANES, tn), lambda l, j: (l, 0, j)),
        out_shape=jax.ShapeDtypeStruct((depth, SUBLANES, width), F32),
        compiler_params=_cparams(("arbitrary", "arbitrary")),
        name="adaln_mod",
    )(c_pad, mod_w, mod_b.reshape(depth, 1, width))
    return out[:, :bsz]


def _rms_mod(x, nw, shift, scale):
    y = x * lax.rsqrt(jnp.mean(x * x, axis=-1, keepdims=True) + NORM_EPS) * nw
    return y * (1.0 + scale) + shift


def _normmod_kernel(x_ref, nw_ref, sh_ref, sc_ref, h_ref):
    h_ref[...] = _rms_mod(x_ref[...], nw_ref[...], sh_ref[0], sc_ref[0]).astype(h_ref.dtype)


def _norm_mod(x2, nw, shift, scale, bsz, seq):
    t, d = x2.shape
    tm = NORM_TM
    ns = seq // tm
    return pl.pallas_call(
        _normmod_kernel,
        grid=(bsz, ns),
        in_specs=[pl.BlockSpec((tm, d), lambda b, s: (b * ns + s, 0)),
                  pl.BlockSpec((1, d), lambda b, s: (0, 0)),
                  pl.BlockSpec((1, 1, d), lambda b, s: (b, 0, 0)),
                  pl.BlockSpec((1, 1, d), lambda b, s: (b, 0, 0))],
        out_specs=pl.BlockSpec((tm, d), lambda b, s: (b * ns + s, 0)),
        out_shape=jax.ShapeDtypeStruct((t, d), BF16),
        compiler_params=_cparams(("arbitrary", "arbitrary")),
        name="norm_mod",
    )(x2, nw.reshape(1, d), shift.reshape(bsz, 1, d), scale.reshape(bsz, 1, d))


def _block_mid_ref(b, n2):
    rows, width = b.shape
    n = n2 // 2
    if n2 >= 2 * SUBLANES:
        b3 = b.reshape(rows // n2, n2, width)
        return jnp.broadcast_to(b3[:, n - 1:n, :], b3.shape).reshape(rows, width)
    b3 = b.reshape(rows // SUBLANES, SUBLANES, width)
    sub = _iota(b3.shape, 1)
    r3 = jnp.broadcast_to(b3[:, SUBLANES - n2 + n - 1:SUBLANES - n2 + n, :], b3.shape)
    for g in range(SUBLANES // n2 - 2, -1, -1):
        r3 = jnp.where(sub < (g + 1) * n2, b3[:, g * n2 + n - 1:g * n2 + n, :], r3)
    return r3.reshape(rows, width)


def _hgrn2_kernel(h_ref, w_ref, lb_ref, nw_ref, o_ref, proj_ref, st_ref):
    L = CHUNK
    dk = HG_DK
    n_heads = st_ref.shape[0]
    width = n_heads * dk

    @pl.when(pl.program_id(1) == 0)
    def _():
        st_ref[...] = jnp.zeros_like(st_ref)

    proj_ref[...] = _dot(h_ref[...], w_ref[...])
    lb = lb_ref[...]
    f = lb + (1.0 - lb) * _sigmoid(proj_ref[:, width:2 * width])
    logf = jnp.log(f)
    t_i = _iota((L, L), 0)
    s_i = _iota((L, L), 1)
    tril = (t_i >= s_i).astype(BF16)
    b_all = _dot_sel(tril, logf)
    eye = t_i == s_i

    levels = []
    n2 = L
    while n2 >= 2:
        n = n2 // 2
        sh = n2.bit_length() - 1
        m = ((t_i >> sh) == (s_i >> sh)) & ((t_i & (n2 - 1)) >= n) & ((s_i & (n2 - 1)) < n)
        levels.append((n2, m))
        n2 = n

    for h in range(n_heads):
        sl = slice(h * dk, (h + 1) * dk)
        q = proj_ref[:, sl]
        k = 1.0 - f[:, sl]
        v = proj_ref[:, 2 * width + h * dk:2 * width + (h + 1) * dk]
        b = b_all[:, sl]
        st = st_ref[h]
        b_last = b[L - 1:L, :]
        o = _dot_nt((q * jnp.exp(b)).astype(BF16), st.astype(BF16))
        a = jnp.where(eye, jnp.sum(q * k, axis=-1, keepdims=True), 0.0)
        for n2, m in levels:
            e = jnp.exp(-jnp.abs(b - _block_mid_ref(b, n2)))
            a = jnp.where(m, _dot_nt((q * e).astype(BF16), (k * e).astype(BF16)), a)
        o = o + _dot(a.astype(BF16), v.astype(BF16))
        ke = k * jnp.exp(b_last - b)
        st_ref[h] = st * jnp.exp(b_last) + _dot_tn(v.astype(BF16), ke.astype(BF16))
        g = proj_ref[:, 3 * width + h * dk:3 * width + (h + 1) * dk]
        ms = jnp.mean(o * o, axis=-1, keepdims=True)
        o_ref[:, sl] = (o * lax.rsqrt(ms + NORM_EPS) * nw_ref[:, sl] * _silu(g)).astype(o_ref.dtype)


def _hgrn2(h, w_hg, lb, hg_norm, bsz, seq):
    t, d = h.shape
    width = lb.shape[0]
    n_heads = width // HG_DK
    ns = seq // CHUNK
    return pl.pallas_call(
        _hgrn2_kernel,
        grid=(bsz, ns),
        in_specs=[pl.BlockSpec((CHUNK, d), lambda b, s: (b * ns + s, 0)),
                  pl.BlockSpec(w_hg.shape, lambda b, s: (0, 0)),
                  pl.BlockSpec((1, width), lambda b, s: (0, 0)),
                  pl.BlockSpec((1, width), lambda b, s: (0, 0))],
        out_specs=pl.BlockSpec((CHUNK, width), lambda b, s: (b * ns + s, 0)),
        out_shape=jax.ShapeDtypeStruct((t, width), BF16),
        scratch_shapes=[pltpu.VMEM((CHUNK, w_hg.shape[1]), F32), pltpu.VMEM((n_heads, HG_DK, HG_DK), F32)],
        compiler_params=_cparams(("arbitrary", "arbitrary")),
        name="hgrn2_scan",
    )(h, w_hg, lb.reshape(1, width), hg_norm.reshape(1, width))


def _ssd_kernel(h_ref, w_ref, cw_ref, cb_ref, dtb_ref, a_ref, dsk_ref, nw_ref, o_ref,
                proj_ref, carry_ref, xpad_ref, st_ref):
    L = CHUNK
    width = o_ref.shape[1]
    xw = cw_ref.shape[1]
    n_heads = width // SSM_P
    gw = width // SSM_GROUPS
    heads_per_group = n_heads // SSM_GROUPS

    @pl.when(pl.program_id(1) == 0)
    def _():
        carry_ref[...] = jnp.zeros_like(carry_ref)
        st_ref[...] = jnp.zeros_like(st_ref)

    proj_ref[...] = _dot(h_ref[...], w_ref[...])
    z_ref = proj_ref.at[:, 0:width]
    dt_ref = proj_ref.at[:, width + xw:]
    xraw = proj_ref[:, width:width + xw]
    xpad_ref[0:SUBLANES, :] = carry_ref[...]
    xpad_ref[SUBLANES:SUBLANES + L, :] = xraw
    carry_ref[...] = xraw[L - SUBLANES:L, :]
    acc = cb_ref[...] + jnp.zeros_like(xraw)
    for j in range(SSM_CONV):
        acc = acc + cw_ref[j:j + 1, :] * xpad_ref[pl.ds(SUBLANES - (SSM_CONV - 1) + j, L), :]
    xc = _silu(acc)
    xs = xc[:, :width]
    bm = xc[:, width:width + SSM_GROUPS * SSM_N]
    cm = xc[:, width + SSM_GROUPS * SSM_N:]

    dt = _softplus(dt_ref[...] + dtb_ref[...])
    da = dt * a_ref[...]
    da_t = da.T
    dt_t = dt.T
    t_i = _iota((L, L), 0)
    s_i = _iota((L, L), 1)
    triu = (t_i <= s_i).astype(BF16)
    acs_t = _dot_rsel(da_t, triu)
    causal = t_i >= s_i
    diag = t_i == s_i
    lane_lo = _iota((L, 2 * SSM_P), 1) < SSM_P
    bd_mask = (_iota((2 * L, 2 * SSM_P), 0) < L) == (_iota((2 * L, 2 * SSM_P), 1) < SSM_P)

    y_pairs = []
    for g in range(SSM_GROUPS):
        bg = bm[:, g * SSM_N:(g + 1) * SSM_N]
        cg = cm[:, g * SSM_N:(g + 1) * SSM_N]
        cb = _dot_nt(cg.astype(BF16), bg.astype(BF16))
        hg = st_ref[:, g * gw:(g + 1) * gw]
        yoff_g = _dot(cg.astype(BF16), hg.astype(BF16))
        xsc_parts, decay_parts = [], []
        for pr in range(heads_per_group // 2):
            j0 = g * heads_per_group + 2 * pr
            gs, ds, ecol, elast = [], [], [], []
            for j in (j0, j0 + 1):
                row_b = jnp.broadcast_to(acs_t[j:j + 1, :], (L, L))
                col_b = row_b.T
                dt_row = jnp.broadcast_to(dt_t[j:j + 1, :], (L, L))
                lmat = jnp.exp(jnp.minimum(col_b - row_b, 0.0))
                gs.append(jnp.where(causal, cb * lmat * dt_row, 0.0))
                a_last = acs_t[j:j + 1, L - 1:L]
                ds.append(jnp.where(diag, jnp.exp(a_last - row_b) * dt_row, 0.0))
                ecol.append(jnp.exp(col_b))
                elast.append(jnp.exp(a_last))
            lhs = jnp.concatenate([jnp.concatenate(gs, axis=1), jnp.concatenate(ds, axis=1)], axis=0)
            xs_pair = xs[:, j0 * SSM_P:(j0 + 2) * SSM_P]
            bd = jnp.where(bd_mask, jnp.concatenate([xs_pair, xs_pair], axis=0), 0.0)
            res = _dot(lhs.astype(BF16), bd.astype(BF16))
            yoff = yoff_g[:, pr * 2 * SSM_P:(pr + 1) * 2 * SSM_P] * jnp.where(lane_lo, ecol[0], ecol[1])
            y_pairs.append(res[:L] + yoff)
            xsc_parts.append(res[L:])
            decay_parts.append(jnp.where(lane_lo[0:1], elast[0], elast[1]))
        xsc_g = jnp.concatenate(xsc_parts, axis=1)
        decay_g = jnp.concatenate(decay_parts, axis=1)
        st_ref[:, g * gw:(g + 1) * gw] = hg * decay_g + _dot_tn(bg.astype(BF16), xsc_g.astype(BF16))
    y = jnp.concatenate(y_pairs, axis=1) + dsk_ref[...] * xs
    yz = y * _silu(z_ref[...])
    for g in range(SSM_GROUPS):
        seg = yz[:, g * gw:(g + 1) * gw]
        ms = jnp.mean(seg * seg, axis=-1, keepdims=True)
        o_ref[:, g * gw:(g + 1) * gw] = (seg * lax.rsqrt(ms + NORM_EPS) * nw_ref[:, g * gw:(g + 1) * gw]).astype(o_ref.dtype)


def _ssd(h, w_ssd, conv_w, conv_b, dt_bias, a_log, d_skip, ssm_norm, bsz, seq):
    t, d = h.shape
    width = ssm_norm.shape[0]
    xw = conv_w.shape[1]
    n_heads = width // SSM_P
    ns = seq // CHUNK
    pad = lambda v: jnp.zeros((1, LANES), F32).at[0, :n_heads].set(v)
    row = lambda w: pl.BlockSpec((CHUNK, w), lambda b, s: (b * ns + s, 0))
    const = lambda r, w: pl.BlockSpec((r, w), lambda b, s: (0, 0))
    return pl.pallas_call(
        _ssd_kernel,
        grid=(bsz, ns),
        in_specs=[row(d), const(*w_ssd.shape), const(SSM_CONV, xw), const(1, xw), const(1, LANES),
                  const(1, LANES), const(1, width), const(1, width)],
        out_specs=row(width),
        out_shape=jax.ShapeDtypeStruct((t, width), BF16),
        scratch_shapes=[pltpu.VMEM((CHUNK, w_ssd.shape[1]), F32), pltpu.VMEM((SUBLANES, xw), F32),
                        pltpu.VMEM((CHUNK + SUBLANES, xw), F32), pltpu.VMEM((SSM_N, width), F32)],
        compiler_params=_cparams(("arbitrary", "arbitrary")),
        name="ssd_scan",
    )(h, w_ssd, conv_w, conv_b.reshape(1, xw), pad(dt_bias), pad(-jnp.exp(a_log)),
      jnp.repeat(d_skip, SSM_P).reshape(1, width), ssm_norm.reshape(1, width))


def _route(probs):
    n_exp = N_GROUPS_MOE * 4
    p = [probs[e:e + 1, :] for e in range(n_exp)]
    gs = []
    for g in range(N_GROUPS_MOE):
        a, b, c, d = p[4 * g:4 * g + 4]
        gs.append(jnp.maximum(jnp.maximum(jnp.maximum(a + b, a + c), jnp.maximum(a + d, b + c)),
                              jnp.maximum(b + d, c + d)))
    best = jnp.zeros_like(gs[0]).astype(I32)
    bs = gs[0]
    for g in range(1, N_GROUPS_MOE):
        upd = gs[g] > bs
        best = jnp.where(upd, g, best)
        bs = jnp.where(upd, gs[g], bs)
    q = [jnp.where(best == 0, p[i], jnp.where(best == 1, p[4 + i], jnp.where(best == 2, p[8 + i], p[12 + i])))
         for i in range(4)]
    i1 = jnp.zeros_like(best)
    v1 = q[0]
    for i in range(1, 4):
        upd = q[i] > v1
        i1 = jnp.where(upd, i, i1)
        v1 = jnp.where(upd, q[i], v1)
    i2 = jnp.zeros_like(best)
    v2 = jnp.full_like(v1, -1.0)
    for i in range(4):
        upd = (i1 != i) & (q[i] > v2)
        i2 = jnp.where(upd, i, i2)
        v2 = jnp.where(upd, q[i], v2)
    den = v1 + v2
    return best * 4 + i1, best * 4 + i2, v1 / den, v2 / den


def _head_stats_expand(stack, seg_ref):
    pw = seg_ref.shape[0]
    seg = seg_ref[...]
    return jnp.concatenate([_dot_rsel(stack[:, p * pw:(p + 1) * pw], seg) for p in range(stack.shape[1] // pw)], axis=1)


def _epilogue_kernel(*refs, n_in, rwkv):
    ins = refs[:n_in]
    pos = n_in
    w_refs = refs[pos:pos + n_in]
    pos += n_in
    if rwkv:
        g_ref, bonus_ref, lnw_ref, lnb_ref, seg_ref = refs[pos:pos + 5]
        pos += 5
    x_ref, gate_ref, nw_ref, sh_ref, sc_ref, rw_ref, rb_ref = refs[pos:pos + 7]
    pos += 7
    xo_ref, h_ref, ti_ref, tw_ref = refs[pos:pos + 4]

    tm = ROW_TM
    rw_hi, rw_lo = _split(rw_ref[...])
    for sub in range(x_ref.shape[0] // tm):
        rs = pl.ds(sub * tm, tm)
        if rwkv:
            o = ins[0][rs, :]
            inv_n = 1.0 / RW_N
            st = _head_stats_expand(jnp.concatenate([o, o * o], axis=0), seg_ref) * inv_n
            mean = st[:tm]
            var = jnp.maximum(st[tm:] - mean * mean, 0.0)
            o = (o - mean) * lax.rsqrt(var + RW_GN_EPS) * lnw_ref[...] + lnb_ref[...]
            o = (o + bonus_ref[rs, :].astype(F32)) * g_ref[rs, :].astype(F32)
            y = _dot(o.astype(BF16), w_refs[0][...])
        else:
            y = _dot(ins[0][rs, :], w_refs[0][...])
            for a_ref, w_ref in zip(ins[1:], w_refs[1:]):
                y = y + _dot(a_ref[rs, :], w_ref[...])
        x_new = x_ref[rs, :] + gate_ref[0] * y
        xo_ref[rs, :] = x_new
        h = _rms_mod(x_new, nw_ref[...], sh_ref[0], sc_ref[0])
        _store_rows(h_ref.at[pl.ds(sub * tm * ROW_PITCH, tm * ROW_PITCH)], h)
        h_hi, h_lo = _split(h)
        logits = _dot_nt(rw_hi, h_hi) + _dot_nt(rw_hi, h_lo) + _dot_nt(rw_lo, h_hi) + rb_ref[...]
        mx = jnp.max(logits, axis=0, keepdims=True)
        ex = jnp.exp(logits - mx)
        probs = ex / jnp.sum(ex, axis=0, keepdims=True)
        e1, e2, w1, w2 = _route(probs)
        zi = jnp.zeros((SUBLANES - TOP_K, tm), I32)
        ti_ref[:, sub * tm:(sub + 1) * tm] = jnp.concatenate([e1, e2, zi], axis=0)
        tw_ref[:, sub * tm:(sub + 1) * tm] = jnp.concatenate([w1, w2, zi.astype(F32)], axis=0)


def _epilogue(ins, ws, x2, gate, nw, shift, scale, router_w, router_b, bsz, seq, rwkv_extra=None):
    t, d = x2.shape
    tm = EPILOGUE_SUBTILES * ROW_TM
    ns = seq // tm
    n_exp = router_w.shape[1]
    rw_t = jnp.zeros((LANES, d), F32).at[:n_exp].set(router_w.T)
    rb = jnp.full((LANES, 1), -1e30, F32).at[:n_exp, 0].set(router_b)
    row = lambda w: pl.BlockSpec((tm, w), lambda b, s: (b * ns + s, 0))
    const = lambda a: pl.BlockSpec(a.shape, lambda b, s: (0,) * a.ndim)
    per_b = pl.BlockSpec((1, 1, d), lambda b, s: (b, 0, 0))
    args = list(ins) + list(ws)
    specs = [row(a.shape[1]) for a in ins] + [const(w) for w in ws]
    if rwkv_extra is not None:
        g, bonus, lnw, lnb, seg = rwkv_extra
        args += [g, bonus, lnw.reshape(1, d), lnb.reshape(1, d), seg]
        specs += [row(d), row(d), pl.BlockSpec((1, d), lambda b, s: (0, 0)), pl.BlockSpec((1, d), lambda b, s: (0, 0)),
                  const(seg)]
    args += [x2, gate.reshape(bsz, 1, d), nw.reshape(1, d), shift.reshape(bsz, 1, d), scale.reshape(bsz, 1, d), rw_t, rb]
    specs += [row(d), per_b, pl.BlockSpec((1, d), lambda b, s: (0, 0)), per_b, per_b, const(rw_t), const(rb)]
    lane_row = pl.BlockSpec((SUBLANES, tm), lambda b, s: (0, b * ns + s))
    return pl.pallas_call(
        functools.partial(_epilogue_kernel, n_in=len(ins), rwkv=rwkv_extra is not None),
        grid=(bsz, ns),
        in_specs=specs,
        out_specs=[row(d), pl.BlockSpec((tm * ROW_PITCH, LANES), lambda b, s: (b * ns + s, 0)), lane_row, lane_row],
        out_shape=[jax.ShapeDtypeStruct((t, d), F32), jax.ShapeDtypeStruct((t * ROW_PITCH, LANES), F32),
                   jax.ShapeDtypeStruct((SUBLANES, t), I32), jax.ShapeDtypeStruct((SUBLANES, t), F32)],
        compiler_params=_cparams(("arbitrary", "arbitrary")),
        name="mixer_epilogue",
    )(*args)


def _route_plan(top_i, n_exp, tm):
    t = top_i.shape[1]
    n_pairs = TOP_K * t
    n_tiles = n_pairs // tm + n_exp
    e_flat = top_i[:TOP_K].reshape(-1)
    onehot = (e_flat[:, None] == jnp.arange(n_exp, dtype=I32)[None, :]).astype(I32)
    csum = jnp.cumsum(onehot, axis=0)
    counts = csum[-1]
    rank = jnp.sum((csum - 1) * onehot, axis=1)
    padded = ((counts + tm - 1) // tm) * tm
    ends = jnp.cumsum(padded)
    pos = (ends - padded)[e_flat] + rank
    tile_start = jnp.arange(n_tiles, dtype=I32) * tm
    tile_exp = jnp.minimum(jnp.sum((tile_start[:, None] >= ends[None, :]).astype(I32), axis=1), n_exp - 1)
    n_used = (ends[-1] // tm).reshape(1).astype(I32)
    gap_lo = jnp.concatenate([ends - padded + counts, ends[-1:]]).astype(I32)
    gap_hi = jnp.concatenate([ends, jnp.full((1,), n_tiles * tm, I32)]).astype(I32)
    return pos.astype(I32), tile_exp.astype(I32), n_used, gap_lo, gap_hi


def _dispatch_kernel(pos_ref, glo_ref, ghi_ref, h_ref, xs_hbm, zbuf, sem, zsem):
    tm = DISPATCH_TM
    n_tok = pos_ref.shape[0] // TOP_K
    i = pl.program_id(0)

    @pl.when(i == 0)
    def _():
        zbuf[...] = jnp.zeros_like(zbuf)

        def chunk_copy(row):
            return pltpu.make_async_copy(zbuf, xs_hbm.at[pl.ds(row * ROW_PITCH, ZERO_CHUNK * ROW_PITCH)], zsem)

        def row_copy(row):
            return pltpu.make_async_copy(zbuf.at[pl.ds(0, ROW_PITCH)], xs_hbm.at[pl.ds(row * ROW_PITCH, ROW_PITCH)], zsem)

        for e in range(glo_ref.shape[0]):
            lo = glo_ref[e]
            hi = ghi_ref[e]
            n_chunks = lax.div(hi - lo, ZERO_CHUNK)
            mid = lo + n_chunks * ZERO_CHUNK

            def start_chunk(c, carry):
                chunk_copy(lo + c * ZERO_CHUNK).start()
                return carry

            def start_row(r, carry):
                row_copy(r).start()
                return carry

            def wait_chunk(c, carry):
                chunk_copy(lo).wait()
                return carry

            def wait_row(r, carry):
                row_copy(lo).wait()
                return carry

            lax.fori_loop(0, n_chunks, start_chunk, 0)
            lax.fori_loop(mid, hi, start_row, 0)
            lax.fori_loop(0, n_chunks, wait_chunk, 0)
            lax.fori_loop(mid, hi, wait_row, 0)

    def body(r8, c):
        for u in range(GATHER_UNROLL):
            r = r8 * GATHER_UNROLL + u
            for k in range(TOP_K):
                p = pos_ref[k * n_tok + i * tm + r]
                pltpu.make_async_copy(h_ref.at[pl.ds(r * ROW_PITCH, ROW_PITCH)],
                                      xs_hbm.at[pl.ds(p, ROW_PITCH)], sem).start(priority=k)
        return c
    lax.fori_loop(0, tm // GATHER_UNROLL, body, 0)
    for k in range(TOP_K):
        pltpu.make_async_copy(h_ref, xs_hbm.at[pl.ds(0, tm * ROW_PITCH)], sem).wait()


def _dispatch(h, pos, gap_lo, gap_hi, n_rows):
    tm = DISPATCH_TM
    t = h.shape[0] // ROW_PITCH
    grid_spec = pltpu.PrefetchScalarGridSpec(
        num_scalar_prefetch=3,
        grid=(t // tm,),
        in_specs=[pl.BlockSpec((tm * ROW_PITCH, LANES), lambda i, p, lo, hi: (i, 0))],
        out_specs=pl.BlockSpec(memory_space=pl.ANY),
        scratch_shapes=[pltpu.VMEM((ZERO_CHUNK * ROW_PITCH, LANES), F32), pltpu.SemaphoreType.DMA(()),
                        pltpu.SemaphoreType.DMA(())],
    )
    return pl.pallas_call(
        _dispatch_kernel,
        grid_spec=grid_spec,
        out_shape=jax.ShapeDtypeStruct((n_rows * ROW_PITCH, LANES), F32),
        compiler_params=_cparams(("arbitrary",)),
        name="moe_dispatch",
    )(pos, gap_lo, gap_hi, h)


def _moe_kernel(texp_ref, nused_ref, x_ref, w1_ref, w3_ref, w2_ref, y_ref, wb1, wb3, wb2):
    tm = MOE_TM
    d = wb1.shape[0]
    i = pl.program_id(0)
    n_used = nused_ref[0]

    @pl.when((i == 0) | (texp_ref[i] != texp_ref[jnp.maximum(i - 1, 0)]))
    def _():
        wb1[...] = w1_ref[0, 0].astype(BF16)
        wb3[...] = w3_ref[0, 0].astype(BF16)
        wb2[...] = w2_ref[0, 0].astype(BF16)

    @pl.when(i < n_used)
    def _():
        x = _load_rows(x_ref, tm, d).astype(BF16)
        a = _dot(x, wb1[...])
        b = _dot(x, wb3[...])
        hid = (_silu(a) * b).astype(BF16)
        _store_rows(y_ref, _dot(hid, wb2[...]))

    @pl.when(i >= n_used)
    def _():
        y_ref[...] = jnp.zeros_like(y_ref)


def _moe(xs, w1, w3, w2, layer, tile_exp, n_used):
    _, n_exp, d, dff = w1.shape
    tm = MOE_TM
    n_tiles = tile_exp.shape[0]
    rows = pl.BlockSpec((tm * ROW_PITCH, LANES), lambda i, te, nu: (i, 0))
    grid_spec = pltpu.PrefetchScalarGridSpec(
        num_scalar_prefetch=2,
        grid=(n_tiles,),
        in_specs=[rows,
                  pl.BlockSpec((1, 1, d, dff), lambda i, te, nu: (layer, te[i], 0, 0)),
                  pl.BlockSpec((1, 1, d, dff), lambda i, te, nu: (layer, te[i], 0, 0)),
                  pl.BlockSpec((1, 1, dff, d), lambda i, te, nu: (layer, te[i], 0, 0))],
        out_specs=rows,
        scratch_shapes=[pltpu.VMEM((d, dff), BF16), pltpu.VMEM((d, dff), BF16), pltpu.VMEM((dff, d), BF16)],
    )
    return pl.pallas_call(
        _moe_kernel,
        grid_spec=grid_spec,
        out_shape=jax.ShapeDtypeStruct((n_tiles * tm * ROW_PITCH, LANES), F32),
        compiler_params=_cparams(("arbitrary",)),
        name="moe_experts",
    )(tile_exp, n_used, xs, w1, w3, w2)


def _combine_kernel(pos_ref, y_hbm, x_ref, tw_ref, gate_ref, nw_ref, sh_ref, sc_ref, *out_and_scratch, final, n_tok):
    if final:
        o_ref, ybuf, sem = out_and_scratch
    else:
        xo_ref, h_ref, ybuf, sem = out_and_scratch
    tm, d = x_ref.shape
    n_sub = d // LANES
    i = pl.program_id(0)
    n_steps = pl.num_programs(0)
    slot = i % 2

    def start_gather(tile, sl):
        def body(r8, c):
            for u in range(GATHER_UNROLL):
                r = r8 * GATHER_UNROLL + u
                for k in range(TOP_K):
                    src = pos_ref[k * n_tok + tile * tm + r]
                    pltpu.make_async_copy(y_hbm.at[pl.ds(src, n_sub)], ybuf.at[sl, k, pl.ds(r * ROW_PITCH, n_sub)],
                                          sem.at[sl]).start(priority=k)
            return c
        lax.fori_loop(0, tm // GATHER_UNROLL, body, 0)

    def wait_gather(sl):
        for k in range(TOP_K):
            pltpu.make_async_copy(y_hbm.at[pl.ds(0, tm * n_sub)], ybuf.at[sl, k, pl.ds(0, tm * n_sub)], sem.at[sl]).wait()

    @pl.when(i == 0)
    def _():
        start_gather(0, 0)

    wait_gather(slot)
    nxt = jnp.minimum(i + 1, n_steps - 1)
    per = tm // (2 * n_sub)
    issued = [0]

    def issue_slice():
        for r in range(issued[0], issued[0] + per):
            for k in range(TOP_K):
                src = pos_ref[k * n_tok + nxt * tm + r]
                pltpu.make_async_copy(y_hbm.at[pl.ds(src, n_sub)], ybuf.at[1 - slot, k, pl.ds(r * ROW_PITCH, n_sub)],
                                      sem.at[1 - slot]).start(priority=k)
        issued[0] += per

    tw = tw_ref[...]
    gate = gate_ref[0]
    cols = []
    ssq = jnp.zeros((tm, 1), F32)
    for j in range(n_sub):
        cs = slice(j * LANES, (j + 1) * LANES)
        rows = pl.ds(j, tm, stride=ROW_PITCH)
        moe = tw[:, 0:1] * ybuf.at[slot, 0][rows, :] + tw[:, 1:2] * ybuf.at[slot, 1][rows, :]
        xj = x_ref[:, cs] + gate[:, cs] * moe
        if not final:
            xo_ref[:, cs] = xj
        ssq = ssq + jnp.sum(xj * xj, axis=-1, keepdims=True)
        cols.append(xj)
        issue_slice()
    rs = lax.rsqrt(ssq * (1.0 / d) + NORM_EPS)
    for j in range(n_sub):
        cs = slice(j * LANES, (j + 1) * LANES)
        y = cols[j] * rs * nw_ref[:, cs]
        if final:
            o_ref[:, cs] = y
        else:
            h_ref[:, cs] = (y * (1.0 + sc_ref[0][:, cs]) + sh_ref[0][:, cs]).astype(h_ref.dtype)
        issue_slice()

    @pl.when(i == n_steps - 1)
    def _():
        wait_gather(1 - slot)


def _combine(pos, y_sorted, x2, top_w, gate, nw, shift, scale, bsz, seq, final):
    t, d = x2.shape
    tm = ROW_TM
    ns = seq // tm
    tw = top_w.T
    row = pl.BlockSpec((tm, d), lambda i, p: (i, 0))
    per_b = pl.BlockSpec((1, 1, d), lambda i, p: (i // ns, 0, 0))
    grid_spec = pltpu.PrefetchScalarGridSpec(
        num_scalar_prefetch=1,
        grid=(t // tm,),
        in_specs=[pl.BlockSpec(memory_space=pl.ANY), row,
                  pl.BlockSpec((tm, SUBLANES), lambda i, p: (i, 0)),
                  per_b, pl.BlockSpec((1, d), lambda i, p: (0, 0)), per_b, per_b],
        out_specs=row if final else [row, row],
        scratch_shapes=[pltpu.VMEM((2, TOP_K, tm * ROW_PITCH, LANES), F32), pltpu.SemaphoreType.DMA((2,))],
    )
    out_shape = (jax.ShapeDtypeStruct((t, d), F32) if final else
                 [jax.ShapeDtypeStruct((t, d), F32), jax.ShapeDtypeStruct((t, d), BF16)])
    return pl.pallas_call(
        functools.partial(_combine_kernel, final=final, n_tok=t),
        grid_spec=grid_spec,
        out_shape=out_shape,
        compiler_params=_cparams(("arbitrary",)),
        name="moe_combine",
    )(pos, y_sorted, x2, tw, gate.reshape(bsz, 1, d), nw.reshape(1, d), shift.reshape(bsz, 1, d), scale.reshape(bsz, 1, d))


def _rwkv_prep_kernel(h_ref, mu_ref, wr_ref, wk_ref, wv_ref, dec0_ref, dec1_ref, dec2_ref, a0_ref, a1_ref, a2_ref,
                      g1_ref, g2_ref, kk_ref, ka_ref, rk_ref, seg_ref,
                      r_out, lw_out, k_out, v_out, kkn_out, a_out, g_out, bonus_out, carry_ref, hp_ref):
    tm = h_ref.shape[0]

    @pl.when(pl.program_id(1) == 0)
    def _():
        carry_ref[...] = jnp.zeros_like(carry_ref)

    hb = h_ref[...]
    h = hb.astype(F32)
    hp_ref[0:SUBLANES, :] = carry_ref[...]
    hp_ref[SUBLANES:SUBLANES + tm, :] = h
    carry_ref[...] = h[tm - SUBLANES:tm, :]
    xxb = (hp_ref[pl.ds(SUBLANES - 1, tm), :] - h).astype(BF16)
    mix = lambda i: hb + xxb * mu_ref[i:i + 1, :].astype(BF16)
    r = _dot(mix(0), wr_ref[...])
    k = _dot(mix(1), wk_ref[...])
    v = _dot(mix(2), wv_ref[...])
    wl = dec0_ref[...] + _dot(jnp.tanh(_dot(mix(3), dec1_ref[...])).astype(BF16), dec2_ref[...])
    lw = -RW_DECAY_SCALE * _sigmoid(wl)
    a = _sigmoid(a0_ref[...] + _dot(_dot(mix(4), a1_ref[...]).astype(BF16), a2_ref[...]))
    g = _dot(_sigmoid(_dot(mix(5), g1_ref[...])).astype(BF16), g2_ref[...])
    kk = k * kk_ref[...]
    k_h = k * (1.0 + (a - 1.0) * ka_ref[...])
    st = _head_stats_expand(jnp.concatenate([kk * kk, r * k_h * rk_ref[...]], axis=0), seg_ref)
    kkn = kk * lax.rsqrt(jnp.maximum(st[:tm], 1e-24))
    r_out[...] = r.astype(r_out.dtype)
    lw_out[...] = lw
    k_out[...] = k_h.astype(k_out.dtype)
    v_out[...] = v.astype(v_out.dtype)
    kkn_out[...] = kkn.astype(kkn_out.dtype)
    a_out[...] = a.astype(a_out.dtype)
    g_out[...] = g.astype(g_out.dtype)
    bonus_out[...] = (st[tm:] * v).astype(bonus_out.dtype)


def _rwkv_prep(h, mu, w_rkv, dec0, dec1, dec2, a0, a1, a2, g1, g2, k_k, k_a, r_k, seg, bsz, seq):
    t, d = h.shape
    tm = ROW_TM
    ns = seq // tm
    padc = lambda w: jnp.zeros((d, LANES), F32).at[:, :w.shape[1]].set(w).astype(BF16)
    padr = lambda w: jnp.zeros((LANES, d), F32).at[:w.shape[0]].set(w).astype(BF16)
    vec = lambda v: v.reshape(1, d)
    args = [h, mu, w_rkv[0].astype(BF16), w_rkv[1].astype(BF16), w_rkv[2].astype(BF16), vec(dec0), padc(dec1), padr(dec2),
            vec(a0), padc(a1), padr(a2), padc(g1), padr(g2), vec(k_k), vec(k_a), vec(r_k), seg]
    row = pl.BlockSpec((tm, d), lambda b, s: (b * ns + s, 0))
    const = lambda a: pl.BlockSpec(a.shape, lambda b, s: (0,) * a.ndim)
    outs = [BF16, F32, BF16, BF16, BF16, BF16, BF16, BF16]
    return pl.pallas_call(
        _rwkv_prep_kernel,
        grid=(bsz, ns),
        in_specs=[row] + [const(a) for a in args[1:]],
        out_specs=[row] * len(outs),
        out_shape=[jax.ShapeDtypeStruct((t, d), dt) for dt in outs],
        scratch_shapes=[pltpu.VMEM((SUBLANES, d), F32), pltpu.VMEM((tm + SUBLANES, d), F32)],
        compiler_params=_cparams(("arbitrary", "arbitrary")),
        name="rwkv_prep",
    )(*args)


def _rwkv_scan_kernel(r_ref, lw_ref, k_ref, v_ref, kk_ref, a_ref, y_ref, st_ref):
    L = RW_CHUNK
    pw = RW_PACK * RW_N
    n_packs = st_ref.shape[0]
    sh = RW_N.bit_length() - 1

    @pl.when(pl.program_id(1) == 0)
    def _():
        st_ref[...] = jnp.zeros_like(st_ref)

    t_i = _iota((L, L), 0)
    s_i = _iota((L, L), 1)
    tril = (t_i >= s_i).astype(BF16)
    wc_all = _dot_sel(tril, lw_ref[...])
    lane_head = _iota((L, pw), 1) >> sh
    s_loc = _iota((L, pw), 1) & (RW_N - 1)
    t_loc = _iota((L, pw), 0)
    strict = s_loc < t_loc
    incl = s_loc <= t_loc
    bd_mask = (_iota((pw, pw), 0) >> sh) == (_iota((pw, pw), 1) >> sh)

    def bdiag(x):
        return jnp.where(bd_mask, jnp.concatenate([x] * RW_PACK, axis=0), 0.0).astype(BF16)

    packs = range(n_packs)
    sls = [slice(p * pw, (p + 1) * pw) for p in packs]
    pr, vs, sts, kkas, ks, wcs = [], [], [], [], [], []
    for p in packs:
        sl = sls[p]
        r = r_ref[:, sl].astype(F32)
        lw = lw_ref[:, sl]
        k = k_ref[:, sl].astype(F32)
        kk = kk_ref[:, sl].astype(F32)
        a = a_ref[:, sl].astype(F32)
        wc = wc_all[:, sl]
        e_inv = jnp.exp(-wc)
        kka = kk * a
        al = -kk * jnp.exp(wc - lw)
        rb = r * jnp.exp(wc)
        bt = kka * e_inv
        kt = k * e_inv
        lhs = jnp.concatenate([al, rb], axis=0).astype(BF16)
        rows = [jnp.where(lane_head == hh, x, 0.0) for x in (bt, kt) for hh in range(RW_PACK)]
        st = st_ref[p]
        m = jnp.concatenate(rows + [st], axis=0).astype(BF16)
        pr.append(_dot_nt(lhs, m))
        vs.append(v_ref[:, sl].astype(F32))
        sts.append(st)
        kkas.append(kka)
        ks.append(k)
        wcs.append(wc)
    bd_vs = [bdiag(vs[p]) for p in packs]
    us = [pr[p][:L, 2 * pw:] + _dot(jnp.where(strict, pr[p][:L, pw:2 * pw], 0.0).astype(BF16), bd_vs[p]) for p in packs]
    nmats = [jnp.where(strict, pr[p][:L, 0:pw], 0.0) for p in packs]
    n_steps = L.bit_length() - 1
    for it in range(n_steps):
        us = [us[p] + _dot(nmats[p].astype(BF16), bdiag(us[p])) for p in packs]
        if it + 1 < n_steps:
            nmats = [_dot(nmats[p].astype(BF16), bdiag(nmats[p])) for p in packs]
    for p in packs:
        a_rb = jnp.where(incl, pr[p][L:, 0:pw], 0.0)
        a_rk = jnp.where(incl, pr[p][L:, pw:2 * pw], 0.0)
        y_ref[:, sls[p]] = pr[p][L:, 2 * pw:] + _dot(jnp.concatenate([a_rb, a_rk], axis=1).astype(BF16),
                                                     jnp.concatenate([bdiag(us[p]), bd_vs[p]], axis=0))
    for p in packs:
        w_last = wcs[p][L - 1:L, :]
        e_last = jnp.exp(w_last - wcs[p])
        upd = _dot_tn(jnp.concatenate([us[p], vs[p]], axis=0).astype(BF16),
                      jnp.concatenate([kkas[p] * e_last, ks[p] * e_last], axis=0).astype(BF16))
        st_ref[p] = jnp.where(bd_mask, sts[p] * jnp.exp(w_last) + upd, 0.0)


def _rwkv_scan(r, lw, k, v, kk, a, bsz, seq):
    t, d = r.shape
    ns = seq // RW_CHUNK
    pw = RW_PACK * RW_N
    row = pl.BlockSpec((RW_CHUNK, d), lambda b, s: (b * ns + s, 0))
    return pl.pallas_call(
        _rwkv_scan_kernel,
        grid=(bsz, ns),
        in_specs=[row] * 6,
        out_specs=row,
        out_shape=jax.ShapeDtypeStruct((t, d), F32),
        scratch_shapes=[pltpu.VMEM((d // pw, pw, pw), F32)],
        compiler_params=_cparams(("arbitrary", "arbitrary")),
        name="rwkv_scan",
    )(r, lw, k, v, kk, a)


def _moe_block(h, top_i, top_w, x2, gate, w1, w3, w2, layer, nw, shift, scale, bsz, seq, final):
    n_exp = w1.shape[1]
    pos, tile_exp, n_used, gap_lo, gap_hi = _route_plan(top_i, n_exp, MOE_TM)
    pos = pos * ROW_PITCH
    x_sorted = _dispatch(h, pos, gap_lo, gap_hi, tile_exp.shape[0] * MOE_TM)
    y_sorted = _moe(x_sorted, w1, w3, w2, layer, tile_exp, n_used)
    return _combine(pos, y_sorted, x2, top_w, gate, nw, shift, scale, bsz, seq, final)


def kernel(x, c, mod_w, mod_b, norm_w, hg_lb_logits, ev_w_in, ev_hg_norm, ev_conv_w, ev_conv_b, ev_dt_bias, ev_a_log, ev_d_skip, ev_ssm_norm, ev_w_out, od_mu, od_w_rkv, od_w_dec0, od_w_dec1, od_w_dec2, od_a0, od_a1, od_a2, od_g1, od_g2, od_k_k, od_k_a, od_r_k, od_ln_w, od_ln_b, od_w_o, router_w, router_b, moe_w1, moe_w3, moe_w2, final_norm_w):
    bsz, seq, d = x.shape
    depth = mod_w.shape[0]
    t = bsz * seq
    x2 = x.reshape(t, d)
    mod = _adaln_mod(c, mod_w, mod_b)
    gamma = jax.nn.softmax(hg_lb_logits.astype(F32), axis=0)
    lower_bounds = jnp.cumsum(gamma, axis=0) - gamma[0]
    head_of_lane = jnp.arange(RW_PACK * RW_N, dtype=I32) // RW_N
    seg = (head_of_lane[:, None] == head_of_lane[None, :]).astype(BF16)

    h = None
    out = None
    for l in range(depth):
        sh_m, sc_m, gt_m, sh_f, sc_f, gt_f = [mod[l, :, i * d:(i + 1) * d] for i in range(6)]
        j = l // 2
        if h is None:
            h = _norm_mod(x2, norm_w[l, 0], sh_m, sc_m, bsz, seq)
        if l % 2 == 0:
            w_in = ev_w_in[j]
            hgw = ev_hg_norm.shape[1]
            sw = ev_ssm_norm.shape[1]
            xbw = ev_conv_w.shape[2]
            nh = ev_dt_bias.shape[1]
            c0 = 4 * hgw
            w_hg = w_in[:, :c0].astype(BF16)
            w_ssd = jnp.zeros((d, sw + xbw + LANES), F32).at[:, :sw + xbw + nh].set(w_in[:, c0:]).astype(BF16)
            o_a = _hgrn2(h, w_hg, lower_bounds[l + 1], ev_hg_norm[j], bsz, seq)
            o_b = _ssd(h, w_ssd, ev_conv_w[j], ev_conv_b[j], ev_dt_bias[j], ev_a_log[j], ev_d_skip[j],
                       ev_ssm_norm[j], bsz, seq)
            w_out = ev_w_out[j].astype(BF16)
            x2, hf, top_i, top_w = _epilogue([o_a, o_b], [w_out[:hgw], w_out[hgw:]], x2, gt_m, norm_w[l, 1], sh_f, sc_f,
                                             router_w, router_b, bsz, seq)
        else:
            r, lw, k, v, kk, a, g, bonus = _rwkv_prep(h, od_mu[j], od_w_rkv[j], od_w_dec0[j], od_w_dec1[j], od_w_dec2[j],
                                                      od_a0[j], od_a1[j], od_a2[j], od_g1[j], od_g2[j], od_k_k[j],
                                                      od_k_a[j], od_r_k[j].reshape(-1), seg, bsz, seq)
            y = _rwkv_scan(r, lw, k, v, kk, a, bsz, seq)
            x2, hf, top_i, top_w = _epilogue([y], [od_w_o[j].astype(BF16)], x2, gt_m, norm_w[l, 1], sh_f, sc_f,
                                             router_w, router_b, bsz, seq,
                                             rwkv_extra=(g, bonus, od_ln_w[j], od_ln_b[j], seg))
        final = l == depth - 1
        if final:
            nw_next, sh_next, sc_next = final_norm_w, sh_f, sc_f
        else:
            nxt = [mod[l + 1, :, i * d:(i + 1) * d] for i in range(2)]
            nw_next, sh_next, sc_next = norm_w[l + 1, 0], nxt[0], nxt[1]
        res = _moe_block(hf, top_i, top_w, x2, gt_f, moe_w1, moe_w3, moe_w2, l, nw_next, sh_next, sc_next,
                         bsz, seq, final)
        if final:
            out = res
        else:
            x2, h = res
    return out.reshape(bsz, seq, d)
```

```python
import functools

import jax
import jax.numpy as jnp
from jax import lax
from jax.experimental import pallas as pl
from jax.experimental.pallas import tpu as pltpu

F32 = jnp.float32
BF16 = jnp.bfloat16
I32 = jnp.int32

NORM_EPS = 1e-6
RW_GN_EPS = 64e-5
RW_DECAY_SCALE = 0.6065306597126334
LANES = 128
SUBLANES = 8
VMEM_LIMIT = 56 * 1024 * 1024

HG_DK = 128
SSM_P = 64
SSM_N = 128
SSM_GROUPS = 2
SSM_CONV = 4
RW_N = 64
N_GROUPS_MOE = 4
TOP_K = 2

CHUNK = 128
RW_CHUNK = 64
RW_PACK = 4
RW_BATCH_ROWS = 2
MOE_TM = 512
ROW_TM = 256
NORM_TM = 512
EPILOGUE_SUBTILES = 2
DISPATCH_TM = 512
ZERO_CHUNK = 64
GATHER_UNROLL = 16
ROW_PITCH = 9


def _cparams(sem):
    return pltpu.CompilerParams(dimension_semantics=sem, vmem_limit_bytes=VMEM_LIMIT)


def _dot(a, b):
    return lax.dot_general(a, b, (((1,), (0,)), ((), ())), preferred_element_type=F32)


def _dot_nt(a, b):
    return lax.dot_general(a, b, (((1,), (1,)), ((), ())), preferred_element_type=F32)


def _dot_tn(a, b):
    return lax.dot_general(a, b, (((0,), (0,)), ((), ())), preferred_element_type=F32)


def _split(x):
    hi = x.astype(BF16)
    return hi, (x - hi.astype(F32)).astype(BF16)


def _dot_sel(sel, x):
    hi, lo = _split(x)
    return _dot(sel, hi) + _dot(sel, lo)


def _dot_rsel(x, sel):
    hi, lo = _split(x)
    return _dot(hi, sel) + _dot(lo, sel)


def _sigmoid(x):
    return 1.0 / (1.0 + jnp.exp(-x))


def _silu(x):
    return x * _sigmoid(x)


def _softplus(x):
    return jnp.maximum(x, 0.0) + jnp.log(1.0 + jnp.exp(-jnp.abs(x)))


def _iota(shape, dim):
    return lax.broadcasted_iota(I32, shape, dim)


def _store_rows(ref, val):
    tm, width = val.shape
    for j in range(width // LANES):
        ref[pl.ds(j, tm, stride=ROW_PITCH), :] = val[:, j * LANES:(j + 1) * LANES]
    for j in range(width // LANES, ROW_PITCH):
        ref[pl.ds(j, tm, stride=ROW_PITCH), :] = jnp.zeros((tm, LANES), ref.dtype)


def _load_rows(ref, tm, width):
    return jnp.concatenate([ref[pl.ds(j, tm, stride=ROW_PITCH), :] for j in range(width // LANES)], axis=1)


def _mod_kernel(c_ref, w_ref, b_ref, o_ref):
    c = c_ref[...]
    o_ref[0] = _dot(_silu(c).astype(BF16), w_ref[0].astype(BF16)) + b_ref[0]


def _adaln_mod(c, mod_w, mod_b):
    depth, d, width = mod_w.shape
    bsz = c.shape[0]
    c_pad = jnp.zeros((SUBLANES, d), F32).at[:bsz].set(c)
    tn = 1536
    out = pl.pallas_call(
        _mod_kernel,
        grid=(depth, width // tn),
        in_specs=[pl.BlockSpec((SUBLANES, d), lambda l, j: (0, 0)),
                  pl.BlockSpec((1, d, tn), lambda l, j: (l, 0, j)),
                  pl.BlockSpec((1, 1, tn), lambda l, j: (l, 0, j))],
        out_specs=pl.BlockSpec((1, SUBLANES, tn), lambda l, j: (l, 0, j)),
        out_shape=jax.ShapeDtypeStruct((depth, SUBLANES, width), F32),
        compiler_params=_cparams(("arbitrary", "arbitrary")),
        name="adaln_mod",
    )(c_pad, mod_w, mod_b.reshape(depth, 1, width))
    return out[:, :bsz]


def _rms_mod(x, nw, shift, scale):
    y = x * lax.rsqrt(jnp.mean(x * x, axis=-1, keepdims=True) + NORM_EPS) * nw
    return y * (1.0 + scale) + shift


def _normmod_kernel(x_ref, nw_ref, sh_ref, sc_ref, h_ref):
    h_ref[...] = _rms_mod(x_ref[...], nw_ref[...], sh_ref[0], sc_ref[0]).astype(h_ref.dtype)


def _norm_mod(x2, nw, shift, scale, bsz, seq):
    t, d = x2.shape
    tm = NORM_TM
    ns = seq // tm
    return pl.pallas_call(
        _normmod_kernel,
        grid=(bsz, ns),
        in_specs=[pl.BlockSpec((tm, d), lambda b, s: (b * ns + s, 0)),
                  pl.BlockSpec((1, d), lambda b, s: (0, 0)),
                  pl.BlockSpec((1, 1, d), lambda b, s: (b, 0, 0)),
                  pl.BlockSpec((1, 1, d), lambda b, s: (b, 0, 0))],
        out_specs=pl.BlockSpec((tm, d), lambda b, s: (b * ns + s, 0)),
        out_shape=jax.ShapeDtypeStruct((t, d), BF16),
        compiler_params=_cparams(("arbitrary", "arbitrary")),
        name="norm_mod",
    )(x2, nw.reshape(1, d), shift.reshape(bsz, 1, d), scale.reshape(bsz, 1, d))


def _block_mid_ref(b, n2):
    rows, width = b.shape
    n = n2 // 2
    if n2 >= 2 * SUBLANES:
        b3 = b.reshape(rows // n2, n2, width)
        return jnp.broadcast_to(b3[:, n - 1:n, :], b3.shape).reshape(rows, width)
    b3 = b.reshape(rows // SUBLANES, SUBLANES, width)
    sub = _iota(b3.shape, 1)
    r3 = jnp.broadcast_to(b3[:, SUBLANES - n2 + n - 1:SUBLANES - n2 + n, :], b3.shape)
    for g in range(SUBLANES // n2 - 2, -1, -1):
        r3 = jnp.where(sub < (g + 1) * n2, b3[:, g * n2 + n - 1:g * n2 + n, :], r3)
    return r3.reshape(rows, width)


def _hgrn2_kernel(h_ref, w_ref, lb_ref, nw_ref, o_ref, proj_ref, st_ref):
    L = CHUNK
    dk = HG_DK
    n_heads = st_ref.shape[0]
    width = n_heads * dk

    @pl.when(pl.program_id(1) == 0)
    def _():
        st_ref[...] = jnp.zeros_like(st_ref)

    proj_ref[...] = _dot(h_ref[...], w_ref[...])
    lb = lb_ref[...]
    f = lb + (1.0 - lb) * _sigmoid(proj_ref[:, width:2 * width])
    logf = jnp.log(f)
    t_i = _iota((L, L), 0)
    s_i = _iota((L, L), 1)
    tril = (t_i >= s_i).astype(BF16)
    b_all = _dot_sel(tril, logf)
    eye = t_i == s_i

    levels = []
    n2 = L
    while n2 >= 2:
        n = n2 // 2
        sh = n2.bit_length() - 1
        m = ((t_i >> sh) == (s_i >> sh)) & ((t_i & (n2 - 1)) >= n) & ((s_i & (n2 - 1)) < n)
        levels.append((n2, m))
        n2 = n

    for h in range(n_heads):
        sl = slice(h * dk, (h + 1) * dk)
        q = proj_ref[:, sl]
        k = 1.0 - f[:, sl]
        v = proj_ref[:, 2 * width + h * dk:2 * width + (h + 1) * dk]
        b = b_all[:, sl]
        st = st_ref[h]
        b_last = b[L - 1:L, :]
        o = _dot_nt((q * jnp.exp(b)).astype(BF16), st.astype(BF16))
        a = jnp.where(eye, jnp.sum(q * k, axis=-1, keepdims=True), 0.0)
        for n2, m in levels:
            e = jnp.exp(-jnp.abs(b - _block_mid_ref(b, n2)))
            a = jnp.where(m, _dot_nt((q * e).astype(BF16), (k * e).astype(BF16)), a)
        o = o + _dot(a.astype(BF16), v.astype(BF16))
        ke = k * jnp.exp(b_last - b)
        st_ref[h] = st * jnp.exp(b_last) + _dot_tn(v.astype(BF16), ke.astype(BF16))
        g = proj_ref[:, 3 * width + h * dk:3 * width + (h + 1) * dk]
        ms = jnp.mean(o * o, axis=-1, keepdims=True)
        o_ref[:, sl] = (o * lax.rsqrt(ms + NORM_EPS) * nw_ref[:, sl] * _silu(g)).astype(o_ref.dtype)


def _hgrn2(h, w_hg, lb, hg_norm, bsz, seq):
    t, d = h.shape
    width = lb.shape[0]
    n_heads = width // HG_DK
    ns = seq // CHUNK
    return pl.pallas_call(
        _hgrn2_kernel,
        grid=(bsz, ns),
        in_specs=[pl.BlockSpec((CHUNK, d), lambda b, s: (b * ns + s, 0)),
                  pl.BlockSpec(w_hg.shape, lambda b, s: (0, 0)),
                  pl.BlockSpec((1, width), lambda b, s: (0, 0)),
                  pl.BlockSpec((1, width), lambda b, s: (0, 0))],
        out_specs=pl.BlockSpec((CHUNK, width), lambda b, s: (b * ns + s, 0)),
        out_shape=jax.ShapeDtypeStruct((t, width), BF16),
        scratch_shapes=[pltpu.VMEM((CHUNK, w_hg.shape[1]), F32), pltpu.VMEM((n_heads, HG_DK, HG_DK), F32)],
        compiler_params=_cparams(("arbitrary", "arbitrary")),
        name="hgrn2_scan",
    )(h, w_hg, lb.reshape(1, width), hg_norm.reshape(1, width))


def _ssd_kernel(h_ref, w_ref, cw_ref, cb_ref, dtb_ref, a_ref, dsk_ref, nw_ref, o_ref,
                proj_ref, carry_ref, xpad_ref, st_ref):
    L = CHUNK
    width = o_ref.shape[1]
    xw = cw_ref.shape[1]
    n_heads = width // SSM_P
    gw = width // SSM_GROUPS
    heads_per_group = n_heads // SSM_GROUPS

    @pl.when(pl.program_id(1) == 0)
    def _():
        carry_ref[...] = jnp.zeros_like(carry_ref)
        st_ref[...] = jnp.zeros_like(st_ref)

    proj_ref[...] = _dot(h_ref[...], w_ref[...])
    z_ref = proj_ref.at[:, 0:width]
    dt_ref = proj_ref.at[:, width + xw:]
    xraw = proj_ref[:, width:width + xw]
    xpad_ref[0:SUBLANES, :] = carry_ref[...]
    xpad_ref[SUBLANES:SUBLANES + L, :] = xraw
    carry_ref[...] = xraw[L - SUBLANES:L, :]
    acc = cb_ref[...] + jnp.zeros_like(xraw)
    for j in range(SSM_CONV):
        acc = acc + cw_ref[j:j + 1, :] * xpad_ref[pl.ds(SUBLANES - (SSM_CONV - 1) + j, L), :]
    xc = _silu(acc)
    xs = xc[:, :width]
    bm = xc[:, width:width + SSM_GROUPS * SSM_N]
    cm = xc[:, width + SSM_GROUPS * SSM_N:]

    dt = _softplus(dt_ref[...] + dtb_ref[...])
    da = dt * a_ref[...]
    da_t = da.T
    dt_t = dt.T
    t_i = _iota((L, L), 0)
    s_i = _iota((L, L), 1)
    triu = (t_i <= s_i).astype(BF16)
    acs_t = _dot_rsel(da_t, triu)
    causal = t_i >= s_i
    diag = t_i == s_i
    lane_lo = _iota((L, 2 * SSM_P), 1) < SSM_P
    bd_mask = (_iota((2 * L, 2 * SSM_P), 0) < L) == (_iota((2 * L, 2 * SSM_P), 1) < SSM_P)

    y_pairs = []
    for g in range(SSM_GROUPS):
        bg = bm[:, g * SSM_N:(g + 1) * SSM_N]
        cg = cm[:, g * SSM_N:(g + 1) * SSM_N]
        cb = _dot_nt(cg.astype(BF16), bg.astype(BF16))
        hg = st_ref[:, g * gw:(g + 1) * gw]
        yoff_g = _dot(cg.astype(BF16), hg.astype(BF16))
        xsc_parts, decay_parts = [], []
        for pr in range(heads_per_group // 2):
            j0 = g * heads_per_group + 2 * pr
            gs, ds, ecol, elast = [], [], [], []
            for j in (j0, j0 + 1):
                row_b = jnp.broadcast_to(acs_t[j:j + 1, :], (L, L))
                col_b = row_b.T
                dt_row = jnp.broadcast_to(dt_t[j:j + 1, :], (L, L))
                lmat = jnp.exp(jnp.minimum(col_b - row_b, 0.0))
                gs.append(jnp.where(causal, cb * lmat * dt_row, 0.0))
                a_last = acs_t[j:j + 1, L - 1:L]
                ds.append(jnp.where(diag, jnp.exp(a_last - row_b) * dt_row, 0.0))
                ecol.append(jnp.exp(col_b))
                elast.append(jnp.exp(a_last))
            lhs = jnp.concatenate([jnp.concatenate(gs, axis=1), jnp.concatenate(ds, axis=1)], axis=0)
            xs_pair = xs[:, j0 * SSM_P:(j0 + 2) * SSM_P]
            bd = jnp.where(bd_mask, jnp.concatenate([xs_pair, xs_pair], axis=0), 0.0)
            res = _dot(lhs.astype(BF16), bd.astype(BF16))
            yoff = yoff_g[:, pr * 2 * SSM_P:(pr + 1) * 2 * SSM_P] * jnp.where(lane_lo, ecol[0], ecol[1])
            y_pairs.append(res[:L] + yoff)
            xsc_parts.append(res[L:])
            decay_parts.append(jnp.where(lane_lo[0:1], elast[0], elast[1]))
        xsc_g = jnp.concatenate(xsc_parts, axis=1)
        decay_g = jnp.concatenate(decay_parts, axis=1)
        st_ref[:, g * gw:(g + 1) * gw] = hg * decay_g + _dot_tn(bg.astype(BF16), xsc_g.astype(BF16))
    y = jnp.concatenate(y_pairs, axis=1) + dsk_ref[...] * xs
    yz = y * _silu(z_ref[...])
    for g in range(SSM_GROUPS):
        seg = yz[:, g * gw:(g + 1) * gw]
        ms = jnp.mean(seg * seg, axis=-1, keepdims=True)
        o_ref[:, g * gw:(g + 1) * gw] = (seg * lax.rsqrt(ms + NORM_EPS) * nw_ref[:, g * gw:(g + 1) * gw]).astype(o_ref.dtype)


def _ssd(h, w_ssd, conv_w, conv_b, dt_bias, a_log, d_skip, ssm_norm, bsz, seq):
    t, d = h.shape
    width = ssm_norm.shape[0]
    xw = conv_w.shape[1]
    n_heads = width // SSM_P
    ns = seq // CHUNK
    pad = lambda v: jnp.zeros((1, LANES), F32).at[0, :n_heads].set(v)
    row = lambda w: pl.BlockSpec((CHUNK, w), lambda b, s: (b * ns + s, 0))
    const = lambda r, w: pl.BlockSpec((r, w), lambda b, s: (0, 0))
    return pl.pallas_call(
        _ssd_kernel,
        grid=(bsz, ns),
        in_specs=[row(d), const(*w_ssd.shape), const(SSM_CONV, xw), const(1, xw), const(1, LANES),
                  const(1, LANES), const(1, width), const(1, width)],
        out_specs=row(width),
        out_shape=jax.ShapeDtypeStruct((t, width), BF16),
        scratch_shapes=[pltpu.VMEM((CHUNK, w_ssd.shape[1]), F32), pltpu.VMEM((SUBLANES, xw), F32),
                        pltpu.VMEM((CHUNK + SUBLANES, xw), F32), pltpu.VMEM((SSM_N, width), F32)],
        compiler_params=_cparams(("arbitrary", "arbitrary")),
        name="ssd_scan",
    )(h, w_ssd, conv_w, conv_b.reshape(1, xw), pad(dt_bias), pad(-jnp.exp(a_log)),
      jnp.repeat(d_skip, SSM_P).reshape(1, width), ssm_norm.reshape(1, width))


def _route(probs):
    n_exp = N_GROUPS_MOE * 4
    p = [probs[e:e + 1, :] for e in range(n_exp)]
    gs = []
    for g in range(N_GROUPS_MOE):
        a, b, c, d = p[4 * g:4 * g + 4]
        gs.append(jnp.maximum(jnp.maximum(jnp.maximum(a + b, a + c), jnp.maximum(a + d, b + c)),
                              jnp.maximum(b + d, c + d)))
    best = jnp.zeros_like(gs[0]).astype(I32)
    bs = gs[0]
    for g in range(1, N_GROUPS_MOE):
        upd = gs[g] > bs
        best = jnp.where(upd, g, best)
        bs = jnp.where(upd, gs[g], bs)
    q = [jnp.where(best == 0, p[i], jnp.where(best == 1, p[4 + i], jnp.where(best == 2, p[8 + i], p[12 + i])))
         for i in range(4)]
    i1 = jnp.zeros_like(best)
    v1 = q[0]
    for i in range(1, 4):
        upd = q[i] > v1
        i1 = jnp.where(upd, i, i1)
        v1 = jnp.where(upd, q[i], v1)
    i2 = jnp.zeros_like(best)
    v2 = jnp.full_like(v1, -1.0)
    for i in range(4):
        upd = (i1 != i) & (q[i] > v2)
        i2 = jnp.where(upd, i, i2)
        v2 = jnp.where(upd, q[i], v2)
    den = v1 + v2
    return best * 4 + i1, best * 4 + i2, v1 / den, v2 / den


def _head_stats_expand(stack, seg_ref):
    pw = seg_ref.shape[0]
    seg = seg_ref[...]
    return jnp.concatenate([_dot_rsel(stack[:, p * pw:(p + 1) * pw], seg) for p in range(stack.shape[1] // pw)], axis=1)


def _epilogue_kernel(*refs, n_in, rwkv):
    ins = refs[:n_in]
    pos = n_in
    w_refs = refs[pos:pos + n_in]
    pos += n_in
    if rwkv:
        g_ref, bonus_ref, lnw_ref, lnb_ref, seg_ref = refs[pos:pos + 5]
        pos += 5
    x_ref, gate_ref, nw_ref, sh_ref, sc_ref, rw_ref, rb_ref = refs[pos:pos + 7]
    pos += 7
    xo_ref, h_ref, ti_ref, tw_ref = refs[pos:pos + 4]

    tm = ROW_TM
    rw_hi, rw_lo = _split(rw_ref[...])
    for sub in range(x_ref.shape[0] // tm):
        rs = pl.ds(sub * tm, tm)
        if rwkv:
            o = ins[0][rs, :]
            inv_n = 1.0 / RW_N
            st = _head_stats_expand(jnp.concatenate([o, o * o], axis=0), seg_ref) * inv_n
            mean = st[:tm]
            var = jnp.maximum(st[tm:] - mean * mean, 0.0)
            o = (o - mean) * lax.rsqrt(var + RW_GN_EPS) * lnw_ref[...] + lnb_ref[...]
            o = (o + bonus_ref[rs, :].astype(F32)) * g_ref[rs, :].astype(F32)
            y = _dot(o.astype(BF16), w_refs[0][...])
        else:
            y = _dot(ins[0][rs, :], w_refs[0][...])
            for a_ref, w_ref in zip(ins[1:], w_refs[1:]):
                y = y + _dot(a_ref[rs, :], w_ref[...])
        x_new = x_ref[rs, :] + gate_ref[0] * y
        xo_ref[rs, :] = x_new
        h = _rms_mod(x_new, nw_ref[...], sh_ref[0], sc_ref[0])
        _store_rows(h_ref.at[pl.ds(sub * tm * ROW_PITCH, tm * ROW_PITCH)], h)
        h_hi, h_lo = _split(h)
        logits = _dot_nt(rw_hi, h_hi) + _dot_nt(rw_hi, h_lo) + _dot_nt(rw_lo, h_hi) + rb_ref[...]
        mx = jnp.max(logits, axis=0, keepdims=True)
        ex = jnp.exp(logits - mx)
        probs = ex / jnp.sum(ex, axis=0, keepdims=True)
        e1, e2, w1, w2 = _route(probs)
        zi = jnp.zeros((SUBLANES - TOP_K, tm), I32)
        ti_ref[:, sub * tm:(sub + 1) * tm] = jnp.concatenate([e1, e2, zi], axis=0)
        tw_ref[:, sub * tm:(sub + 1) * tm] = jnp.concatenate([w1, w2, zi.astype(F32)], axis=0)


def _epilogue(ins, ws, x2, gate, nw, shift, scale, router_w, router_b, bsz, seq, rwkv_extra=None):
    t, d = x2.shape
    tm = EPILOGUE_SUBTILES * ROW_TM
    ns = seq // tm
    n_exp = router_w.shape[1]
    rw_t = jnp.zeros((LANES, d), F32).at[:n_exp].set(router_w.T)
    rb = jnp.full((LANES, 1), -1e30, F32).at[:n_exp, 0].set(router_b)
    row = lambda w: pl.BlockSpec((tm, w), lambda b, s: (b * ns + s, 0))
    const = lambda a: pl.BlockSpec(a.shape, lambda b, s: (0,) * a.ndim)
    per_b = pl.BlockSpec((1, 1, d), lambda b, s: (b, 0, 0))
    args = list(ins) + list(ws)
    specs = [row(a.shape[1]) for a in ins] + [const(w) for w in ws]
    if rwkv_extra is not None:
        g, bonus, lnw, lnb, seg = rwkv_extra
        args += [g, bonus, lnw.reshape(1, d), lnb.reshape(1, d), seg]
        specs += [row(d), row(d), pl.BlockSpec((1, d), lambda b, s: (0, 0)), pl.BlockSpec((1, d), lambda b, s: (0, 0)),
                  const(seg)]
    args += [x2, gate.reshape(bsz, 1, d), nw.reshape(1, d), shift.reshape(bsz, 1, d), scale.reshape(bsz, 1, d), rw_t, rb]
    specs += [row(d), per_b, pl.BlockSpec((1, d), lambda b, s: (0, 0)), per_b, per_b, const(rw_t), const(rb)]
    lane_row = pl.BlockSpec((SUBLANES, tm), lambda b, s: (0, b * ns + s))
    return pl.pallas_call(
        functools.partial(_epilogue_kernel, n_in=len(ins), rwkv=rwkv_extra is not None),
        grid=(bsz, ns),
        in_specs=specs,
        out_specs=[row(d), pl.BlockSpec((tm * ROW_PITCH, LANES), lambda b, s: (b * ns + s, 0)), lane_row, lane_row],
        out_shape=[jax.ShapeDtypeStruct((t, d), F32), jax.ShapeDtypeStruct((t * ROW_PITCH, LANES), F32),
                   jax.ShapeDtypeStruct((SUBLANES, t), I32), jax.ShapeDtypeStruct((SUBLANES, t), F32)],
        compiler_params=_cparams(("arbitrary", "arbitrary")),
        name="mixer_epilogue",
    )(*args)


def _route_plan(top_i, n_exp, tm):
    t = top_i.shape[1]
    n_pairs = TOP_K * t
    n_tiles = n_pairs // tm + n_exp
    e_flat = top_i[:TOP_K].reshape(-1)
    onehot = (e_flat[:, None] == jnp.arange(n_exp, dtype=I32)[None, :]).astype(I32)
    csum = jnp.cumsum(onehot, axis=0)
    counts = csum[-1]
    rank = jnp.sum((csum - 1) * onehot, axis=1)
    padded = ((counts + tm - 1) // tm) * tm
    ends = jnp.cumsum(padded)
    pos = (ends - padded)[e_flat] + rank
    tile_start = jnp.arange(n_tiles, dtype=I32) * tm
    tile_exp = jnp.minimum(jnp.sum((tile_start[:, None] >= ends[None, :]).astype(I32), axis=1), n_exp - 1)
    n_used = (ends[-1] // tm).reshape(1).astype(I32)
    gap_lo = jnp.concatenate([ends - padded + counts, ends[-1:]]).astype(I32)
    gap_hi = jnp.concatenate([ends, jnp.full((1,), n_tiles * tm, I32)]).astype(I32)
    return pos.astype(I32), tile_exp.astype(I32), n_used, gap_lo, gap_hi


def _dispatch_kernel(pos_ref, glo_ref, ghi_ref, h_ref, xs_hbm, zbuf, sem, zsem):
    tm = DISPATCH_TM
    n_tok = pos_ref.shape[0] // TOP_K
    i = pl.program_id(0)

    @pl.when(i == 0)
    def _():
        zbuf[...] = jnp.zeros_like(zbuf)

        def chunk_copy(row):
            return pltpu.make_async_copy(zbuf, xs_hbm.at[pl.ds(row * ROW_PITCH, ZERO_CHUNK * ROW_PITCH)], zsem)

        def row_copy(row):
            return pltpu.make_async_copy(zbuf.at[pl.ds(0, ROW_PITCH)], xs_hbm.at[pl.ds(row * ROW_PITCH, ROW_PITCH)], zsem)

        for e in range(glo_ref.shape[0]):
            lo = glo_ref[e]
            hi = ghi_ref[e]
            n_chunks = lax.div(hi - lo, ZERO_CHUNK)
            mid = lo + n_chunks * ZERO_CHUNK

            def start_chunk(c, carry):
                chunk_copy(lo + c * ZERO_CHUNK).start()
                return carry

            def start_row(r, carry):
                row_copy(r).start()
                return carry

            def wait_chunk(c, carry):
                chunk_copy(lo).wait()
                return carry

            def wait_row(r, carry):
                row_copy(lo).wait()
                return carry

            lax.fori_loop(0, n_chunks, start_chunk, 0)
            lax.fori_loop(mid, hi, start_row, 0)
            lax.fori_loop(0, n_chunks, wait_chunk, 0)
            lax.fori_loop(mid, hi, wait_row, 0)

    def body(r8, c):
        for u in range(GATHER_UNROLL):
            r = r8 * GATHER_UNROLL + u
            for k in range(TOP_K):
                p = pos_ref[k * n_tok + i * tm + r]
                pltpu.make_async_copy(h_ref.at[pl.ds(r * ROW_PITCH, ROW_PITCH)],
                                      xs_hbm.at[pl.ds(p, ROW_PITCH)], sem).start(priority=k)
        return c
    lax.fori_loop(0, tm // GATHER_UNROLL, body, 0)
    for k in range(TOP_K):
        pltpu.make_async_copy(h_ref, xs_hbm.at[pl.ds(0, tm * ROW_PITCH)], sem).wait()


def _dispatch(h, pos, gap_lo, gap_hi, n_rows):
    tm = DISPATCH_TM
    t = h.shape[0] // ROW_PITCH
    grid_spec = pltpu.PrefetchScalarGridSpec(
        num_scalar_prefetch=3,
        grid=(t // tm,),
        in_specs=[pl.BlockSpec((tm * ROW_PITCH, LANES), lambda i, p, lo, hi: (i, 0))],
        out_specs=pl.BlockSpec(memory_space=pl.ANY),
        scratch_shapes=[pltpu.VMEM((ZERO_CHUNK * ROW_PITCH, LANES), F32), pltpu.SemaphoreType.DMA(()),
                        pltpu.SemaphoreType.DMA(())],
    )
    return pl.pallas_call(
        _dispatch_kernel,
        grid_spec=grid_spec,
        out_shape=jax.ShapeDtypeStruct((n_rows * ROW_PITCH, LANES), F32),
        compiler_params=_cparams(("arbitrary",)),
        name="moe_dispatch",
    )(pos, gap_lo, gap_hi, h)


def _moe_kernel(texp_ref, nused_ref, x_ref, w1_ref, w3_ref, w2_ref, y_ref, wb1, wb3, wb2):
    tm = MOE_TM
    d = wb1.shape[0]
    i = pl.program_id(0)
    n_used = nused_ref[0]

    @pl.when((i == 0) | (texp_ref[i] != texp_ref[jnp.maximum(i - 1, 0)]))
    def _():
        wb1[...] = w1_ref[0, 0].astype(BF16)
        wb3[...] = w3_ref[0, 0].astype(BF16)
        wb2[...] = w2_ref[0, 0].astype(BF16)

    @pl.when(i < n_used)
    def _():
        x = _load_rows(x_ref, tm, d).astype(BF16)
        a = _dot(x, wb1[...])
        b = _dot(x, wb3[...])
        hid = (_silu(a) * b).astype(BF16)
        _store_rows(y_ref, _dot(hid, wb2[...]))

    @pl.when(i >= n_used)
    def _():
        y_ref[...] = jnp.zeros_like(y_ref)


def _moe(xs, w1, w3, w2, layer, tile_exp, n_used):
    _, n_exp, d, dff = w1.shape
    tm = MOE_TM
    n_tiles = tile_exp.shape[0]
    rows = pl.BlockSpec((tm * ROW_PITCH, LANES), lambda i, te, nu: (i, 0))
    grid_spec = pltpu.PrefetchScalarGridSpec(
        num_scalar_prefetch=2,
        grid=(n_tiles,),
        in_specs=[rows,
                  pl.BlockSpec((1, 1, d, dff), lambda i, te, nu: (layer, te[i], 0, 0)),
                  pl.BlockSpec((1, 1, d, dff), lambda i, te, nu: (layer, te[i], 0, 0)),
                  pl.BlockSpec((1, 1, dff, d), lambda i, te, nu: (layer, te[i], 0, 0))],
        out_specs=rows,
        scratch_shapes=[pltpu.VMEM((d, dff), BF16), pltpu.VMEM((d, dff), BF16), pltpu.VMEM((dff, d), BF16)],
    )
    return pl.pallas_call(
        _moe_kernel,
        grid_spec=grid_spec,
        out_shape=jax.ShapeDtypeStruct((n_tiles * tm * ROW_PITCH, LANES), F32),
        compiler_params=_cparams(("arbitrary",)),
        name="moe_experts",
    )(tile_exp, n_used, xs, w1, w3, w2)


def _combine_kernel(pos_ref, y_hbm, x_ref, tw_ref, gate_ref, nw_ref, sh_ref, sc_ref, *out_and_scratch, final, n_tok):
    if final:
        o_ref, ybuf, sem = out_and_scratch
    else:
        xo_ref, h_ref, ybuf, sem = out_and_scratch
    tm, d = x_ref.shape
    n_sub = d // LANES
    i = pl.program_id(0)
    n_steps = pl.num_programs(0)
    slot = i % 2

    def start_gather(tile, sl):
        def body(r8, c):
            for u in range(GATHER_UNROLL):
                r = r8 * GATHER_UNROLL + u
                for k in range(TOP_K):
                    src = pos_ref[k * n_tok + tile * tm + r]
                    pltpu.make_async_copy(y_hbm.at[pl.ds(src, n_sub)], ybuf.at[sl, k, pl.ds(r * ROW_PITCH, n_sub)],
                                          sem.at[sl]).start(priority=k)
            return c
        lax.fori_loop(0, tm // GATHER_UNROLL, body, 0)

    def wait_gather(sl):
        for k in range(TOP_K):
            pltpu.make_async_copy(y_hbm.at[pl.ds(0, tm * n_sub)], ybuf.at[sl, k, pl.ds(0, tm * n_sub)], sem.at[sl]).wait()

    @pl.when(i == 0)
    def _():
        start_gather(0, 0)

    @pl.when(i + 1 < n_steps)
    def _():
        start_gather(i + 1, 1 - slot)

    wait_gather(slot)
    tw = tw_ref[...]
    moe = tw[:, 0:1] * _load_rows(ybuf.at[slot, 0], tm, d) + tw[:, 1:2] * _load_rows(ybuf.at[slot, 1], tm, d)
    x_new = x_ref[...] + gate_ref[0] * moe
    if final:
        o_ref[...] = x_new * lax.rsqrt(jnp.mean(x_new * x_new, axis=-1, keepdims=True) + NORM_EPS) * nw_ref[...]
    else:
        xo_ref[...] = x_new
        h_ref[...] = _rms_mod(x_new, nw_ref[...], sh_ref[0], sc_ref[0]).astype(h_ref.dtype)


def _combine(pos, y_sorted, x2, top_w, gate, nw, shift, scale, bsz, seq, final):
    t, d = x2.shape
    tm = ROW_TM
    ns = seq // tm
    tw = top_w.T
    row = pl.BlockSpec((tm, d), lambda i, p: (i, 0))
    per_b = pl.BlockSpec((1, 1, d), lambda i, p: (i // ns, 0, 0))
    grid_spec = pltpu.PrefetchScalarGridSpec(
        num_scalar_prefetch=1,
        grid=(t // tm,),
        in_specs=[pl.BlockSpec(memory_space=pl.ANY), row,
                  pl.BlockSpec((tm, SUBLANES), lambda i, p: (i, 0)),
                  per_b, pl.BlockSpec((1, d), lambda i, p: (0, 0)), per_b, per_b],
        out_specs=row if final else [row, row],
        scratch_shapes=[pltpu.VMEM((2, TOP_K, tm * ROW_PITCH, LANES), F32), pltpu.SemaphoreType.DMA((2,))],
    )
    out_shape = (jax.ShapeDtypeStruct((t, d), F32) if final else
                 [jax.ShapeDtypeStruct((t, d), F32), jax.ShapeDtypeStruct((t, d), BF16)])
    return pl.pallas_call(
        functools.partial(_combine_kernel, final=final, n_tok=t),
        grid_spec=grid_spec,
        out_shape=out_shape,
        compiler_params=_cparams(("arbitrary",)),
        name="moe_combine",
    )(pos, y_sorted, x2, tw, gate.reshape(bsz, 1, d), nw.reshape(1, d), shift.reshape(bsz, 1, d), scale.reshape(bsz, 1, d))


def _rwkv_prep_kernel(h_ref, mu_ref, wr_ref, wk_ref, wv_ref, dec0_ref, dec1_ref, dec2_ref, a0_ref, a1_ref, a2_ref,
                      g1_ref, g2_ref, kk_ref, ka_ref, rk_ref, seg_ref,
                      r_out, lw_out, k_out, v_out, kkn_out, a_out, g_out, bonus_out, carry_ref, hp_ref):
    tm = h_ref.shape[0]

    @pl.when(pl.program_id(1) == 0)
    def _():
        carry_ref[...] = jnp.zeros_like(carry_ref)

    hb = h_ref[...]
    h = hb.astype(F32)
    hp_ref[0:SUBLANES, :] = carry_ref[...]
    hp_ref[SUBLANES:SUBLANES + tm, :] = h
    carry_ref[...] = h[tm - SUBLANES:tm, :]
    xxb = (hp_ref[pl.ds(SUBLANES - 1, tm), :] - h).astype(BF16)
    mix = lambda i: hb + xxb * mu_ref[i:i + 1, :].astype(BF16)
    r = _dot(mix(0), wr_ref[...])
    k = _dot(mix(1), wk_ref[...])
    v = _dot(mix(2), wv_ref[...])
    wl = dec0_ref[...] + _dot(jnp.tanh(_dot(mix(3), dec1_ref[...])).astype(BF16), dec2_ref[...])
    lw = -RW_DECAY_SCALE * _sigmoid(wl)
    a = _sigmoid(a0_ref[...] + _dot(_dot(mix(4), a1_ref[...]).astype(BF16), a2_ref[...]))
    g = _dot(_sigmoid(_dot(mix(5), g1_ref[...])).astype(BF16), g2_ref[...])
    kk = k * kk_ref[...]
    k_h = k * (1.0 + (a - 1.0) * ka_ref[...])
    st = _head_stats_expand(jnp.concatenate([kk * kk, r * k_h * rk_ref[...]], axis=0), seg_ref)
    kkn = kk * lax.rsqrt(jnp.maximum(st[:tm], 1e-24))
    r_out[...] = r.astype(r_out.dtype)
    lw_out[...] = lw
    k_out[...] = k_h.astype(k_out.dtype)
    v_out[...] = v.astype(v_out.dtype)
    kkn_out[...] = kkn.astype(kkn_out.dtype)
    a_out[...] = a.astype(a_out.dtype)
    g_out[...] = g.astype(g_out.dtype)
    bonus_out[...] = (st[tm:] * v).astype(bonus_out.dtype)


def _rwkv_prep(h, mu, w_rkv, dec0, dec1, dec2, a0, a1, a2, g1, g2, k_k, k_a, r_k, seg, bsz, seq):
    t, d = h.shape
    tm = ROW_TM
    ns = seq // tm
    padc = lambda w: jnp.zeros((d, LANES), F32).at[:, :w.shape[1]].set(w).astype(BF16)
    padr = lambda w: jnp.zeros((LANES, d), F32).at[:w.shape[0]].set(w).astype(BF16)
    vec = lambda v: v.reshape(1, d)
    args = [h, mu, w_rkv[0].astype(BF16), w_rkv[1].astype(BF16), w_rkv[2].astype(BF16), vec(dec0), padc(dec1), padr(dec2),
            vec(a0), padc(a1), padr(a2), padc(g1), padr(g2), vec(k_k), vec(k_a), vec(r_k), seg]
    row = pl.BlockSpec((tm, d), lambda b, s: (b * ns + s, 0))
    const = lambda a: pl.BlockSpec(a.shape, lambda b, s: (0,) * a.ndim)
    outs = [BF16, F32, BF16, BF16, BF16, BF16, BF16, BF16]
    return pl.pallas_call(
        _rwkv_prep_kernel,
        grid=(bsz, ns),
        in_specs=[row] + [const(a) for a in args[1:]],
        out_specs=[row] * len(outs),
        out_shape=[jax.ShapeDtypeStruct((t, d), dt) for dt in outs],
        scratch_shapes=[pltpu.VMEM((SUBLANES, d), F32), pltpu.VMEM((tm + SUBLANES, d), F32)],
        compiler_params=_cparams(("arbitrary", "arbitrary")),
        name="rwkv_prep",
    )(*args)


def _rwkv_scan_kernel(r_ref, lw_ref, k_ref, v_ref, kk_ref, a_ref, y_ref, st_ref):
    L = RW_CHUNK
    pw = RW_PACK * RW_N
    n_packs = st_ref.shape[0]
    sh = RW_N.bit_length() - 1

    @pl.when(pl.program_id(1) == 0)
    def _():
        st_ref[...] = jnp.zeros_like(st_ref)

    t_i = _iota((L, L), 0)
    s_i = _iota((L, L), 1)
    tril = (t_i >= s_i).astype(BF16)
    n_batch = lw_ref.shape[0]
    per_batch = n_packs // n_batch
    wc_all = [_dot_sel(tril, lw_ref[bb]) for bb in range(n_batch)]
    lane_head = _iota((L, pw), 1) >> sh
    s_loc = _iota((L, pw), 1) & (RW_N - 1)
    t_loc = _iota((L, pw), 0)
    strict = s_loc < t_loc
    incl = s_loc <= t_loc
    bd_mask = (_iota((pw, pw), 0) >> sh) == (_iota((pw, pw), 1) >> sh)

    def bdiag(x):
        return jnp.where(bd_mask, jnp.concatenate([x] * RW_PACK, axis=0), 0.0).astype(BF16)

    packs = range(n_packs)
    sls = [(p // per_batch, slice((p % per_batch) * pw, (p % per_batch + 1) * pw)) for p in packs]
    pr, vs, sts, kkas, ks, wcs = [], [], [], [], [], []
    for p in packs:
        bb, sl = sls[p]
        r = r_ref[bb, :, sl].astype(F32)
        lw = lw_ref[bb, :, sl]
        k = k_ref[bb, :, sl].astype(F32)
        kk = kk_ref[bb, :, sl].astype(F32)
        a = a_ref[bb, :, sl].astype(F32)
        wc = wc_all[bb][:, sl]
        e_inv = jnp.exp(-wc)
        kka = kk * a
        al = -kk * jnp.exp(wc - lw)
        rb = r * jnp.exp(wc)
        bt = kka * e_inv
        kt = k * e_inv
        lhs = jnp.concatenate([al, rb], axis=0).astype(BF16)
        rows = [jnp.where(lane_head == hh, x, 0.0) for x in (bt, kt) for hh in range(RW_PACK)]
        st = st_ref[p]
        m = jnp.concatenate(rows + [st], axis=0).astype(BF16)
        pr.append(_dot_nt(lhs, m))
        vs.append(v_ref[bb, :, sl].astype(F32))
        sts.append(st)
        kkas.append(kka)
        ks.append(k)
        wcs.append(wc)
    bd_vs = [bdiag(vs[p]) for p in packs]
    us = [pr[p][:L, 2 * pw:] + _dot(jnp.where(strict, pr[p][:L, pw:2 * pw], 0.0).astype(BF16), bd_vs[p]) for p in packs]
    nmats = [jnp.where(strict, pr[p][:L, 0:pw], 0.0) for p in packs]
    n_steps = L.bit_length() - 1
    for it in range(n_steps):
        us = [us[p] + _dot(nmats[p].astype(BF16), bdiag(us[p])) for p in packs]
        if it + 1 < n_steps:
            nmats = [_dot(nmats[p].astype(BF16), bdiag(nmats[p])) for p in packs]
    for p in packs:
        a_rb = jnp.where(incl, pr[p][L:, 0:pw], 0.0)
        a_rk = jnp.where(incl, pr[p][L:, pw:2 * pw], 0.0)
        bb, sl = sls[p]
        y_ref[bb, :, sl] = pr[p][L:, 2 * pw:] + _dot(jnp.concatenate([a_rb, a_rk], axis=1).astype(BF16),
                                                     jnp.concatenate([bdiag(us[p]), bd_vs[p]], axis=0))
    for p in packs:
        w_last = wcs[p][L - 1:L, :]
        e_last = jnp.exp(w_last - wcs[p])
        upd = _dot_tn(jnp.concatenate([us[p], vs[p]], axis=0).astype(BF16),
                      jnp.concatenate([kkas[p] * e_last, ks[p] * e_last], axis=0).astype(BF16))
        st_ref[p] = jnp.where(bd_mask, sts[p] * jnp.exp(w_last) + upd, 0.0)


def _rwkv_scan(r, lw, k, v, kk, a, bsz, seq):
    t, d = r.shape
    ns = seq // RW_CHUNK
    pw = RW_PACK * RW_N
    nb = RW_BATCH_ROWS if bsz % RW_BATCH_ROWS == 0 else 1
    row = pl.BlockSpec((nb, RW_CHUNK, d), lambda b, s: (b, s, 0))
    y = pl.pallas_call(
        _rwkv_scan_kernel,
        grid=(bsz // nb, ns),
        in_specs=[row] * 6,
        out_specs=row,
        out_shape=jax.ShapeDtypeStruct((bsz, seq, d), F32),
        scratch_shapes=[pltpu.VMEM((nb * (d // pw), pw, pw), F32)],
        compiler_params=_cparams(("arbitrary", "arbitrary")),
        name="rwkv_scan",
    )(*[z.reshape(bsz, seq, d) for z in (r, lw, k, v, kk, a)])
    return y.reshape(t, d)


def _moe_block(h, top_i, top_w, x2, gate, w1, w3, w2, layer, nw, shift, scale, bsz, seq, final):
    n_exp = w1.shape[1]
    pos, tile_exp, n_used, gap_lo, gap_hi = _route_plan(top_i, n_exp, MOE_TM)
    pos = pos * ROW_PITCH
    x_sorted = _dispatch(h, pos, gap_lo, gap_hi, tile_exp.shape[0] * MOE_TM)
    y_sorted = _moe(x_sorted, w1, w3, w2, layer, tile_exp, n_used)
    return _combine(pos, y_sorted, x2, top_w, gate, nw, shift, scale, bsz, seq, final)


def kernel(x, c, mod_w, mod_b, norm_w, hg_lb_logits, ev_w_in, ev_hg_norm, ev_conv_w, ev_conv_b, ev_dt_bias, ev_a_log, ev_d_skip, ev_ssm_norm, ev_w_out, od_mu, od_w_rkv, od_w_dec0, od_w_dec1, od_w_dec2, od_a0, od_a1, od_a2, od_g1, od_g2, od_k_k, od_k_a, od_r_k, od_ln_w, od_ln_b, od_w_o, router_w, router_b, moe_w1, moe_w3, moe_w2, final_norm_w):
    bsz, seq, d = x.shape
    depth = mod_w.shape[0]
    t = bsz * seq
    x2 = x.reshape(t, d)
    mod = _adaln_mod(c, mod_w, mod_b)
    gamma = jax.nn.softmax(hg_lb_logits.astype(F32), axis=0)
    lower_bounds = jnp.cumsum(gamma, axis=0) - gamma[0]
    head_of_lane = jnp.arange(RW_PACK * RW_N, dtype=I32) // RW_N
    seg = (head_of_lane[:, None] == head_of_lane[None, :]).astype(BF16)

    h = None
    out = None
    for l in range(depth):
        sh_m, sc_m, gt_m, sh_f, sc_f, gt_f = [mod[l, :, i * d:(i + 1) * d] for i in range(6)]
        j = l // 2
        if h is None:
            h = _norm_mod(x2, norm_w[l, 0], sh_m, sc_m, bsz, seq)
        if l % 2 == 0:
            w_in = ev_w_in[j]
            hgw = ev_hg_norm.shape[1]
            sw = ev_ssm_norm.shape[1]
            xbw = ev_conv_w.shape[2]
            nh = ev_dt_bias.shape[1]
            c0 = 4 * hgw
            w_hg = w_in[:, :c0].astype(BF16)
            w_ssd = jnp.zeros((d, sw + xbw + LANES), F32).at[:, :sw + xbw + nh].set(w_in[:, c0:]).astype(BF16)
            o_a = _hgrn2(h, w_hg, lower_bounds[l + 1], ev_hg_norm[j], bsz, seq)
            o_b = _ssd(h, w_ssd, ev_conv_w[j], ev_conv_b[j], ev_dt_bias[j], ev_a_log[j], ev_d_skip[j],
                       ev_ssm_norm[j], bsz, seq)
            w_out = ev_w_out[j].astype(BF16)
            x2, hf, top_i, top_w = _epilogue([o_a, o_b], [w_out[:hgw], w_out[hgw:]], x2, gt_m, norm_w[l, 1], sh_f, sc_f,
                                             router_w, router_b, bsz, seq)
        else:
            r, lw, k, v, kk, a, g, bonus = _rwkv_prep(h, od_mu[j], od_w_rkv[j], od_w_dec0[j], od_w_dec1[j], od_w_dec2[j],
                                                      od_a0[j], od_a1[j], od_a2[j], od_g1[j], od_g2[j], od_k_k[j],
                                                      od_k_a[j], od_r_k[j].reshape(-1), seg, bsz, seq)
            y = _rwkv_scan(r, lw, k, v, kk, a, bsz, seq)
            x2, hf, top_i, top_w = _epilogue([y], [od_w_o[j].astype(BF16)], x2, gt_m, norm_w[l, 1], sh_f, sc_f,
                                             router_w, router_b, bsz, seq,
                                             rwkv_extra=(g, bonus, od_ln_w[j], od_ln_b[j], seg))
        final = l == depth - 1
        if final:
            nw_next, sh_next, sc_next = final_norm_w, sh_f, sc_f
        else:
            nxt = [mod[l + 1, :, i * d:(i + 1) * d] for i in range(2)]
            nw_next, sh_next, sc_next = norm_w[l + 1, 0], nxt[0], nxt[1]
        res = _moe_block(hf, top_i, top_w, x2, gt_f, moe_w1, moe_w3, moe_w2, l, nw_next, sh_next, sc_next,
                         bsz, seq, final)
        if final:
            out = res
        else:
            x2, h = res
    return out.reshape(bsz, seq, d)
```

```python
import functools

import jax
import jax.numpy as jnp
from jax import lax
from jax.experimental import pallas as pl
from jax.experimental.pallas import tpu as pltpu

F32 = jnp.float32
BF16 = jnp.bfloat16
I32 = jnp.int32

NORM_EPS = 1e-6
RW_GN_EPS = 64e-5
RW_DECAY_SCALE = 0.6065306597126334
LANES = 128
SUBLANES = 8
VMEM_LIMIT = 56 * 1024 * 1024

HG_DK = 128
SSM_P = 64
SSM_N = 128
SSM_GROUPS = 2
SSM_CONV = 4
RW_N = 64
N_GROUPS_MOE = 4
TOP_K = 2

CHUNK = 128
HG_BATCH_ROWS = 2
RW_CHUNK = 64
RW_PACK = 4
RW_BATCH_ROWS = 4
MOE_TM = 512
ROW_TM = 256
NORM_TM = 512
EPILOGUE_SUBTILES = 4
DISPATCH_TM = 512
ZERO_CHUNK = 64
GATHER_UNROLL = 16
ROW_PITCH = 9


def _cparams(sem):
    return pltpu.CompilerParams(dimension_semantics=sem, vmem_limit_bytes=VMEM_LIMIT)


def _dot(a, b):
    return lax.dot_general(a, b, (((1,), (0,)), ((), ())), preferred_element_type=F32)


def _dot_nt(a, b):
    return lax.dot_general(a, b, (((1,), (1,)), ((), ())), preferred_element_type=F32)


def _dot_tn(a, b):
    return lax.dot_general(a, b, (((0,), (0,)), ((), ())), preferred_element_type=F32)


def _split(x):
    hi = x.astype(BF16)
    return hi, (x - hi.astype(F32)).astype(BF16)


def _dot_sel(sel, x):
    hi, lo = _split(x)
    return _dot(sel, hi) + _dot(sel, lo)


def _dot_rsel(x, sel):
    hi, lo = _split(x)
    return _dot(hi, sel) + _dot(lo, sel)


def _sigmoid(x):
    return 1.0 / (1.0 + jnp.exp(-x))


def _silu(x):
    return x * _sigmoid(x)


def _softplus(x):
    return jnp.maximum(x, 0.0) + jnp.log(1.0 + jnp.exp(-jnp.abs(x)))


def _iota(shape, dim):
    return lax.broadcasted_iota(I32, shape, dim)


def _store_rows(ref, val):
    tm, width = val.shape
    for j in range(width // LANES):
        ref[pl.ds(j, tm, stride=ROW_PITCH), :] = val[:, j * LANES:(j + 1) * LANES]
    for j in range(width // LANES, ROW_PITCH):
        ref[pl.ds(j, tm, stride=ROW_PITCH), :] = jnp.zeros((tm, LANES), ref.dtype)


def _load_rows(ref, tm, width):
    return jnp.concatenate([ref[pl.ds(j, tm, stride=ROW_PITCH), :] for j in range(width // LANES)], axis=1)


def _mod_kernel(c_ref, w_ref, b_ref, o_ref):
    c = c_ref[...]
    o_ref[0] = _dot(_silu(c).astype(BF16), w_ref[0].astype(BF16)) + b_ref[0]


def _adaln_mod(c, mod_w, mod_b):
    depth, d, width = mod_w.shape
    bsz = c.shape[0]
    c_pad = jnp.zeros((SUBLANES, d), F32).at[:bsz].set(c)
    tn = 1536
    out = pl.pallas_call(
        _mod_kernel,
        grid=(depth, width // tn),
        in_specs=[pl.BlockSpec((SUBLANES, d), lambda l, j: (0, 0)),
                  pl.BlockSpec((1, d, tn), lambda l, j: (l, 0, j)),
                  pl.BlockSpec((1, 1, tn), lambda l, j: (l, 0, j))],
        out_specs=pl.BlockSpec((1, SUBLANES, tn), lambda l, j: (l, 0, j)),
        out_shape=jax.ShapeDtypeStruct((depth, SUBLANES, width), F32),
        compiler_params=_cparams(("arbitrary", "arbitrary")),
        name="adaln_mod",
    )(c_pad, mod_w, mod_b.reshape(depth, 1, width))
    return out[:, :bsz]


def _rms_mod(x, nw, shift, scale):
    y = x * lax.rsqrt(jnp.mean(x * x, axis=-1, keepdims=True) + NORM_EPS) * nw
    return y * (1.0 + scale) + shift


def _normmod_kernel(x_ref, nw_ref, sh_ref, sc_ref, h_ref):
    h_ref[...] = _rms_mod(x_ref[...], nw_ref[...], sh_ref[0], sc_ref[0]).astype(h_ref.dtype)


def _norm_mod(x2, nw, shift, scale, bsz, seq):
    t, d = x2.shape
    tm = NORM_TM
    ns = seq // tm
    return pl.pallas_call(
        _normmod_kernel,
        grid=(bsz, ns),
        in_specs=[pl.BlockSpec((tm, d), lambda b, s: (b * ns + s, 0)),
                  pl.BlockSpec((1, d), lambda b, s: (0, 0)),
                  pl.BlockSpec((1, 1, d), lambda b, s: (b, 0, 0)),
                  pl.BlockSpec((1, 1, d), lambda b, s: (b, 0, 0))],
        out_specs=pl.BlockSpec((tm, d), lambda b, s: (b * ns + s, 0)),
        out_shape=jax.ShapeDtypeStruct((t, d), BF16),
        compiler_params=_cparams(("arbitrary", "arbitrary")),
        name="norm_mod",
    )(x2, nw.reshape(1, d), shift.reshape(bsz, 1, d), scale.reshape(bsz, 1, d))


def _block_mid_ref(b, n2):
    rows, width = b.shape
    n = n2 // 2
    if n2 >= 2 * SUBLANES:
        b3 = b.reshape(rows // n2, n2, width)
        return jnp.broadcast_to(b3[:, n - 1:n, :], b3.shape).reshape(rows, width)
    b3 = b.reshape(rows // SUBLANES, SUBLANES, width)
    sub = _iota(b3.shape, 1)
    r3 = jnp.broadcast_to(b3[:, SUBLANES - n2 + n - 1:SUBLANES - n2 + n, :], b3.shape)
    for g in range(SUBLANES // n2 - 2, -1, -1):
        r3 = jnp.where(sub < (g + 1) * n2, b3[:, g * n2 + n - 1:g * n2 + n, :], r3)
    return r3.reshape(rows, width)


def _hgrn2_kernel(h_ref, w_ref, lb_ref, nw_ref, o_ref, proj_ref, st_ref):
    L = CHUNK
    dk = HG_DK
    n_batch = h_ref.shape[0]
    n_heads = st_ref.shape[0] // n_batch
    width = n_heads * dk

    @pl.when(pl.program_id(1) == 0)
    def _():
        st_ref[...] = jnp.zeros_like(st_ref)

    proj_ref[...] = _dot(h_ref[...].reshape(n_batch * L, h_ref.shape[2]), w_ref[...])
    lb = lb_ref[...]
    f = lb + (1.0 - lb) * _sigmoid(proj_ref[:, width:2 * width])
    logf = jnp.log(f)
    t_i = _iota((L, L), 0)
    s_i = _iota((L, L), 1)
    tril = (t_i >= s_i).astype(BF16)
    b_all = [_dot_sel(tril, logf[bb * L:(bb + 1) * L]) for bb in range(n_batch)]
    eye = t_i == s_i

    levels = []
    n2 = L
    while n2 >= 2:
        n = n2 // 2
        sh = n2.bit_length() - 1
        m = ((t_i >> sh) == (s_i >> sh)) & ((t_i & (n2 - 1)) >= n) & ((s_i & (n2 - 1)) < n)
        levels.append((n2, m))
        n2 = n

    for h, bb in [(h, bb) for h in range(n_heads) for bb in range(n_batch)]:
        sl = slice(h * dk, (h + 1) * dk)
        rows = slice(bb * L, (bb + 1) * L)
        q = proj_ref[rows, sl]
        k = 1.0 - f[rows, sl]
        v = proj_ref[rows, 2 * width + h * dk:2 * width + (h + 1) * dk]
        b = b_all[bb][:, sl]
        st = st_ref[bb * n_heads + h]
        b_last = b[L - 1:L, :]
        o = _dot_nt((q * jnp.exp(b)).astype(BF16), st.astype(BF16))
        a = jnp.where(eye, jnp.sum(q * k, axis=-1, keepdims=True), 0.0)
        for n2, m in levels:
            e = jnp.exp(-jnp.abs(b - _block_mid_ref(b, n2)))
            a = jnp.where(m, _dot_nt((q * e).astype(BF16), (k * e).astype(BF16)), a)
        o = o + _dot(a.astype(BF16), v.astype(BF16))
        ke = k * jnp.exp(b_last - b)
        st_ref[bb * n_heads + h] = st * jnp.exp(b_last) + _dot_tn(v.astype(BF16), ke.astype(BF16))
        g = proj_ref[rows, 3 * width + h * dk:3 * width + (h + 1) * dk]
        ms = jnp.mean(o * o, axis=-1, keepdims=True)
        o_ref[bb, :, sl] = (o * lax.rsqrt(ms + NORM_EPS) * nw_ref[:, sl] * _silu(g)).astype(o_ref.dtype)


def _hgrn2(h, w_hg, lb, hg_norm, bsz, seq):
    t, d = h.shape
    width = lb.shape[0]
    n_heads = width // HG_DK
    ns = seq // CHUNK
    nb = HG_BATCH_ROWS if bsz % HG_BATCH_ROWS == 0 else 1
    out = pl.pallas_call(
        _hgrn2_kernel,
        grid=(bsz // nb, ns),
        in_specs=[pl.BlockSpec((nb, CHUNK, d), lambda b, s: (b, s, 0)),
                  pl.BlockSpec(w_hg.shape, lambda b, s: (0, 0)),
                  pl.BlockSpec((1, width), lambda b, s: (0, 0)),
                  pl.BlockSpec((1, width), lambda b, s: (0, 0))],
        out_specs=pl.BlockSpec((nb, CHUNK, width), lambda b, s: (b, s, 0)),
        out_shape=jax.ShapeDtypeStruct((bsz, seq, width), BF16),
        scratch_shapes=[pltpu.VMEM((nb * CHUNK, w_hg.shape[1]), F32), pltpu.VMEM((nb * n_heads, HG_DK, HG_DK), F32)],
        compiler_params=_cparams(("arbitrary", "arbitrary")),
        name="hgrn2_scan",
    )(h.reshape(bsz, seq, d), w_hg, lb.reshape(1, width), hg_norm.reshape(1, width))
    return out.reshape(t, width)


def _ssd_kernel(h_ref, w_ref, cw_ref, cb_ref, dtb_ref, a_ref, dsk_ref, nw_ref, o_ref,
                proj_ref, carry_ref, xpad_ref, st_ref):
    L = CHUNK
    width = o_ref.shape[1]
    xw = cw_ref.shape[1]
    n_heads = width // SSM_P
    gw = width // SSM_GROUPS
    heads_per_group = n_heads // SSM_GROUPS

    @pl.when(pl.program_id(1) == 0)
    def _():
        carry_ref[...] = jnp.zeros_like(carry_ref)
        st_ref[...] = jnp.zeros_like(st_ref)

    proj_ref[...] = _dot(h_ref[...], w_ref[...])
    z_ref = proj_ref.at[:, 0:width]
    dt_ref = proj_ref.at[:, width + xw:]
    xraw = proj_ref[:, width:width + xw]
    xpad_ref[0:SUBLANES, :] = carry_ref[...]
    xpad_ref[SUBLANES:SUBLANES + L, :] = xraw
    carry_ref[...] = xraw[L - SUBLANES:L, :]
    acc = cb_ref[...] + jnp.zeros_like(xraw)
    for j in range(SSM_CONV):
        acc = acc + cw_ref[j:j + 1, :] * xpad_ref[pl.ds(SUBLANES - (SSM_CONV - 1) + j, L), :]
    xc = _silu(acc)
    xs = xc[:, :width]
    bm = xc[:, width:width + SSM_GROUPS * SSM_N]
    cm = xc[:, width + SSM_GROUPS * SSM_N:]

    dt = _softplus(dt_ref[...] + dtb_ref[...])
    da = dt * a_ref[...]
    da_t = da.T
    dt_t = dt.T
    t_i = _iota((L, L), 0)
    s_i = _iota((L, L), 1)
    triu = (t_i <= s_i).astype(BF16)
    acs_t = _dot_rsel(da_t, triu)
    causal = t_i >= s_i
    diag = t_i == s_i
    lane_lo = _iota((L, 2 * SSM_P), 1) < SSM_P
    bd_mask = (_iota((2 * L, 2 * SSM_P), 0) < L) == (_iota((2 * L, 2 * SSM_P), 1) < SSM_P)

    y_pairs = []
    for g in range(SSM_GROUPS):
        bg = bm[:, g * SSM_N:(g + 1) * SSM_N]
        cg = cm[:, g * SSM_N:(g + 1) * SSM_N]
        cb = _dot_nt(cg.astype(BF16), bg.astype(BF16))
        hg = st_ref[:, g * gw:(g + 1) * gw]
        yoff_g = _dot(cg.astype(BF16), hg.astype(BF16))
        xsc_parts, decay_parts = [], []
        for pr in range(heads_per_group // 2):
            j0 = g * heads_per_group + 2 * pr
            gs, ds, ecol, elast = [], [], [], []
            for j in (j0, j0 + 1):
                row_b = jnp.broadcast_to(acs_t[j:j + 1, :], (L, L))
                col_b = row_b.T
                dt_row = jnp.broadcast_to(dt_t[j:j + 1, :], (L, L))
                lmat = jnp.exp(jnp.minimum(col_b - row_b, 0.0))
                gs.append(jnp.where(causal, cb * lmat * dt_row, 0.0))
                a_last = acs_t[j:j + 1, L - 1:L]
                ds.append(jnp.where(diag, jnp.exp(a_last - row_b) * dt_row, 0.0))
                ecol.append(jnp.exp(col_b))
                elast.append(jnp.exp(a_last))
            lhs = jnp.concatenate([jnp.concatenate(gs, axis=1), jnp.concatenate(ds, axis=1)], axis=0)
            xs_pair = xs[:, j0 * SSM_P:(j0 + 2) * SSM_P]
            bd = jnp.where(bd_mask, jnp.concatenate([xs_pair, xs_pair], axis=0), 0.0)
            res = _dot(lhs.astype(BF16), bd.astype(BF16))
            yoff = yoff_g[:, pr * 2 * SSM_P:(pr + 1) * 2 * SSM_P] * jnp.where(lane_lo, ecol[0], ecol[1])
            y_pairs.append(res[:L] + yoff)
            xsc_parts.append(res[L:])
            decay_parts.append(jnp.where(lane_lo[0:1], elast[0], elast[1]))
        xsc_g = jnp.concatenate(xsc_parts, axis=1)
        decay_g = jnp.concatenate(decay_parts, axis=1)
        st_ref[:, g * gw:(g + 1) * gw] = hg * decay_g + _dot_tn(bg.astype(BF16), xsc_g.astype(BF16))
    y = jnp.concatenate(y_pairs, axis=1) + dsk_ref[...] * xs
    yz = y * _silu(z_ref[...])
    for g in range(SSM_GROUPS):
        seg = yz[:, g * gw:(g + 1) * gw]
        ms = jnp.mean(seg * seg, axis=-1, keepdims=True)
        o_ref[:, g * gw:(g + 1) * gw] = (seg * lax.rsqrt(ms + NORM_EPS) * nw_ref[:, g * gw:(g + 1) * gw]).astype(o_ref.dtype)


def _ssd(h, w_ssd, conv_w, conv_b, dt_bias, a_log, d_skip, ssm_norm, bsz, seq):
    t, d = h.shape
    width = ssm_norm.shape[0]
    xw = conv_w.shape[1]
    n_heads = width // SSM_P
    ns = seq // CHUNK
    pad = lambda v: jnp.zeros((1, LANES), F32).at[0, :n_heads].set(v)
    row = lambda w: pl.BlockSpec((CHUNK, w), lambda b, s: (b * ns + s, 0))
    const = lambda r, w: pl.BlockSpec((r, w), lambda b, s: (0, 0))
    return pl.pallas_call(
        _ssd_kernel,
        grid=(bsz, ns),
        in_specs=[row(d), const(*w_ssd.shape), const(SSM_CONV, xw), const(1, xw), const(1, LANES),
                  const(1, LANES), const(1, width), const(1, width)],
        out_specs=row(width),
        out_shape=jax.ShapeDtypeStruct((t, width), BF16),
        scratch_shapes=[pltpu.VMEM((CHUNK, w_ssd.shape[1]), F32), pltpu.VMEM((SUBLANES, xw), F32),
                        pltpu.VMEM((CHUNK + SUBLANES, xw), F32), pltpu.VMEM((SSM_N, width), F32)],
        compiler_params=_cparams(("arbitrary", "arbitrary")),
        name="ssd_scan",
    )(h, w_ssd, conv_w, conv_b.reshape(1, xw), pad(dt_bias), pad(-jnp.exp(a_log)),
      jnp.repeat(d_skip, SSM_P).reshape(1, width), ssm_norm.reshape(1, width))


def _route(probs):
    n_exp = N_GROUPS_MOE * 4
    p = [probs[e:e + 1, :] for e in range(n_exp)]
    gs = []
    for g in range(N_GROUPS_MOE):
        a, b, c, d = p[4 * g:4 * g + 4]
        gs.append(jnp.maximum(jnp.maximum(jnp.maximum(a + b, a + c), jnp.maximum(a + d, b + c)),
                              jnp.maximum(b + d, c + d)))
    best = jnp.zeros_like(gs[0]).astype(I32)
    bs = gs[0]
    for g in range(1, N_GROUPS_MOE):
        upd = gs[g] > bs
        best = jnp.where(upd, g, best)
        bs = jnp.where(upd, gs[g], bs)
    q = [jnp.where(best == 0, p[i], jnp.where(best == 1, p[4 + i], jnp.where(best == 2, p[8 + i], p[12 + i])))
         for i in range(4)]
    i1 = jnp.zeros_like(best)
    v1 = q[0]
    for i in range(1, 4):
        upd = q[i] > v1
        i1 = jnp.where(upd, i, i1)
        v1 = jnp.where(upd, q[i], v1)
    i2 = jnp.zeros_like(best)
    v2 = jnp.full_like(v1, -1.0)
    for i in range(4):
        upd = (i1 != i) & (q[i] > v2)
        i2 = jnp.where(upd, i, i2)
        v2 = jnp.where(upd, q[i], v2)
    den = v1 + v2
    return best * 4 + i1, best * 4 + i2, v1 / den, v2 / den


def _head_stats_expand(stack, seg_ref):
    pw = seg_ref.shape[0]
    seg = seg_ref[...]
    return jnp.concatenate([_dot_rsel(stack[:, p * pw:(p + 1) * pw], seg) for p in range(stack.shape[1] // pw)], axis=1)


def _epilogue_kernel(*refs, n_in, rwkv):
    ins = refs[:n_in]
    pos = n_in
    w_refs = refs[pos:pos + n_in]
    pos += n_in
    if rwkv:
        g_ref, bonus_ref, lnw_ref, lnb_ref, seg_ref = refs[pos:pos + 5]
        pos += 5
    x_ref, gate_ref, nw_ref, sh_ref, sc_ref, rw_ref, rb_ref = refs[pos:pos + 7]
    pos += 7
    xo_ref, h_ref, ti_ref, tw_ref = refs[pos:pos + 4]

    tm = ROW_TM
    rw_hi, rw_lo = _split(rw_ref[...])
    for sub in range(x_ref.shape[0] // tm):
        rs = pl.ds(sub * tm, tm)
        if rwkv:
            o = ins[0][rs, :]
            inv_n = 1.0 / RW_N
            st = _head_stats_expand(jnp.concatenate([o, o * o], axis=0), seg_ref) * inv_n
            mean = st[:tm]
            var = jnp.maximum(st[tm:] - mean * mean, 0.0)
            o = (o - mean) * lax.rsqrt(var + RW_GN_EPS) * lnw_ref[...] + lnb_ref[...]
            o = (o + bonus_ref[rs, :].astype(F32)) * g_ref[rs, :].astype(F32)
            y = _dot(o.astype(BF16), w_refs[0][...])
        else:
            y = _dot(ins[0][rs, :], w_refs[0][...])
            for a_ref, w_ref in zip(ins[1:], w_refs[1:]):
                y = y + _dot(a_ref[rs, :], w_ref[...])
        x_new = x_ref[rs, :] + gate_ref[0] * y
        xo_ref[rs, :] = x_new
        h = _rms_mod(x_new, nw_ref[...], sh_ref[0], sc_ref[0])
        _store_rows(h_ref.at[pl.ds(sub * tm * ROW_PITCH, tm * ROW_PITCH)], h)
        h_hi, h_lo = _split(h)
        logits = _dot_nt(rw_hi, h_hi) + _dot_nt(rw_hi, h_lo) + _dot_nt(rw_lo, h_hi) + rb_ref[...]
        mx = jnp.max(logits, axis=0, keepdims=True)
        ex = jnp.exp(logits - mx)
        probs = ex / jnp.sum(ex, axis=0, keepdims=True)
        e1, e2, w1, w2 = _route(probs)
        zi = jnp.zeros((SUBLANES - TOP_K, tm), I32)
        ti_ref[:, sub * tm:(sub + 1) * tm] = jnp.concatenate([e1, e2, zi], axis=0)
        tw_ref[:, sub * tm:(sub + 1) * tm] = jnp.concatenate([w1, w2, zi.astype(F32)], axis=0)


def _epilogue(ins, ws, x2, gate, nw, shift, scale, router_w, router_b, bsz, seq, rwkv_extra=None):
    t, d = x2.shape
    tm = EPILOGUE_SUBTILES * ROW_TM
    ns = seq // tm
    n_exp = router_w.shape[1]
    rw_t = jnp.zeros((LANES, d), F32).at[:n_exp].set(router_w.T)
    rb = jnp.full((LANES, 1), -1e30, F32).at[:n_exp, 0].set(router_b)
    row = lambda w: pl.BlockSpec((tm, w), lambda b, s: (b * ns + s, 0))
    const = lambda a: pl.BlockSpec(a.shape, lambda b, s: (0,) * a.ndim)
    per_b = pl.BlockSpec((1, 1, d), lambda b, s: (b, 0, 0))
    args = list(ins) + list(ws)
    specs = [row(a.shape[1]) for a in ins] + [const(w) for w in ws]
    if rwkv_extra is not None:
        g, bonus, lnw, lnb, seg = rwkv_extra
        args += [g, bonus, lnw.reshape(1, d), lnb.reshape(1, d), seg]
        specs += [row(d), row(d), pl.BlockSpec((1, d), lambda b, s: (0, 0)), pl.BlockSpec((1, d), lambda b, s: (0, 0)),
                  const(seg)]
    args += [x2, gate.reshape(bsz, 1, d), nw.reshape(1, d), shift.reshape(bsz, 1, d), scale.reshape(bsz, 1, d), rw_t, rb]
    specs += [row(d), per_b, pl.BlockSpec((1, d), lambda b, s: (0, 0)), per_b, per_b, const(rw_t), const(rb)]
    lane_row = pl.BlockSpec((SUBLANES, tm), lambda b, s: (0, b * ns + s))
    return pl.pallas_call(
        functools.partial(_epilogue_kernel, n_in=len(ins), rwkv=rwkv_extra is not None),
        grid=(bsz, ns),
        in_specs=specs,
        out_specs=[row(d), pl.BlockSpec((tm * ROW_PITCH, LANES), lambda b, s: (b * ns + s, 0)), lane_row, lane_row],
        out_shape=[jax.ShapeDtypeStruct((t, d), F32), jax.ShapeDtypeStruct((t * ROW_PITCH, LANES), F32),
                   jax.ShapeDtypeStruct((SUBLANES, t), I32), jax.ShapeDtypeStruct((SUBLANES, t), F32)],
        compiler_params=_cparams(("arbitrary", "arbitrary")),
        name="mixer_epilogue",
    )(*args)


def _route_plan(top_i, n_exp, tm):
    t = top_i.shape[1]
    n_pairs = TOP_K * t
    n_tiles = n_pairs // tm + n_exp
    e_flat = top_i[:TOP_K].reshape(-1)
    onehot = (e_flat[:, None] == jnp.arange(n_exp, dtype=I32)[None, :]).astype(I32)
    csum = jnp.cumsum(onehot, axis=0)
    counts = csum[-1]
    rank = jnp.sum((csum - 1) * onehot, axis=1)
    padded = ((counts + tm - 1) // tm) * tm
    ends = jnp.cumsum(padded)
    pos = (ends - padded)[e_flat] + rank
    tile_start = jnp.arange(n_tiles, dtype=I32) * tm
    tile_exp = jnp.minimum(jnp.sum((tile_start[:, None] >= ends[None, :]).astype(I32), axis=1), n_exp - 1)
    n_used = (ends[-1] // tm).reshape(1).astype(I32)
    gap_lo = jnp.concatenate([ends - padded + counts, ends[-1:]]).astype(I32)
    gap_hi = jnp.concatenate([ends, jnp.full((1,), n_tiles * tm, I32)]).astype(I32)
    return pos.astype(I32), tile_exp.astype(I32), n_used, gap_lo, gap_hi


def _dispatch_kernel(pos_ref, glo_ref, ghi_ref, h_ref, xs_hbm, zbuf, sem, zsem):
    tm = DISPATCH_TM
    n_tok = pos_ref.shape[0] // TOP_K
    i = pl.program_id(0)

    @pl.when(i == 0)
    def _():
        zbuf[...] = jnp.zeros_like(zbuf)

        def chunk_copy(row):
            return pltpu.make_async_copy(zbuf, xs_hbm.at[pl.ds(row * ROW_PITCH, ZERO_CHUNK * ROW_PITCH)], zsem)

        def row_copy(row):
            return pltpu.make_async_copy(zbuf.at[pl.ds(0, ROW_PITCH)], xs_hbm.at[pl.ds(row * ROW_PITCH, ROW_PITCH)], zsem)

        for e in range(glo_ref.shape[0]):
            lo = glo_ref[e]
            hi = ghi_ref[e]
            n_chunks = lax.div(hi - lo, ZERO_CHUNK)
            mid = lo + n_chunks * ZERO_CHUNK

            def start_chunk(c, carry):
                chunk_copy(lo + c * ZERO_CHUNK).start()
                return carry

            def start_row(r, carry):
                row_copy(r).start()
                return carry

            def wait_chunk(c, carry):
                chunk_copy(lo).wait()
                return carry

            def wait_row(r, carry):
                row_copy(lo).wait()
                return carry

            lax.fori_loop(0, n_chunks, start_chunk, 0)
            lax.fori_loop(mid, hi, start_row, 0)
            lax.fori_loop(0, n_chunks, wait_chunk, 0)
            lax.fori_loop(mid, hi, wait_row, 0)

    def body(r8, c):
        for u in range(GATHER_UNROLL):
            r = r8 * GATHER_UNROLL + u
            for k in range(TOP_K):
                p = pos_ref[k * n_tok + i * tm + r]
                pltpu.make_async_copy(h_ref.at[pl.ds(r * ROW_PITCH, ROW_PITCH)],
                                      xs_hbm.at[pl.ds(p, ROW_PITCH)], sem).start(priority=k)
        return c
    lax.fori_loop(0, tm // GATHER_UNROLL, body, 0)
    for k in range(TOP_K):
        pltpu.make_async_copy(h_ref, xs_hbm.at[pl.ds(0, tm * ROW_PITCH)], sem).wait()


def _dispatch(h, pos, gap_lo, gap_hi, n_rows):
    tm = DISPATCH_TM
    t = h.shape[0] // ROW_PITCH
    grid_spec = pltpu.PrefetchScalarGridSpec(
        num_scalar_prefetch=3,
        grid=(t // tm,),
        in_specs=[pl.BlockSpec((tm * ROW_PITCH, LANES), lambda i, p, lo, hi: (i, 0))],
        out_specs=pl.BlockSpec(memory_space=pl.ANY),
        scratch_shapes=[pltpu.VMEM((ZERO_CHUNK * ROW_PITCH, LANES), F32), pltpu.SemaphoreType.DMA(()),
                        pltpu.SemaphoreType.DMA(())],
    )
    return pl.pallas_call(
        _dispatch_kernel,
        grid_spec=grid_spec,
        out_shape=jax.ShapeDtypeStruct((n_rows * ROW_PITCH, LANES), F32),
        compiler_params=_cparams(("arbitrary",)),
        name="moe_dispatch",
    )(pos, gap_lo, gap_hi, h)


def _moe_kernel(texp_ref, nused_ref, x_ref, w1_ref, w3_ref, w2_ref, y_ref, wb1, wb3, wb2):
    tm = MOE_TM
    d = wb1.shape[0]
    i = pl.program_id(0)
    n_used = nused_ref[0]

    @pl.when((i == 0) | (texp_ref[i] != texp_ref[jnp.maximum(i - 1, 0)]))
    def _():
        wb1[...] = w1_ref[0, 0].astype(BF16)
        wb3[...] = w3_ref[0, 0].astype(BF16)
        wb2[...] = w2_ref[0, 0].astype(BF16)

    @pl.when(i < n_used)
    def _():
        x = _load_rows(x_ref, tm, d).astype(BF16)
        a = _dot(x, wb1[...])
        b = _dot(x, wb3[...])
        hid = (_silu(a) * b).astype(BF16)
        _store_rows(y_ref, _dot(hid, wb2[...]))

    @pl.when(i >= n_used)
    def _():
        y_ref[...] = jnp.zeros_like(y_ref)


def _moe(xs, w1, w3, w2, layer, tile_exp, n_used):
    _, n_exp, d, dff = w1.shape
    tm = MOE_TM
    n_tiles = tile_exp.shape[0]
    rows = pl.BlockSpec((tm * ROW_PITCH, LANES), lambda i, te, nu: (i, 0))
    grid_spec = pltpu.PrefetchScalarGridSpec(
        num_scalar_prefetch=2,
        grid=(n_tiles,),
        in_specs=[rows,
                  pl.BlockSpec((1, 1, d, dff), lambda i, te, nu: (layer, te[i], 0, 0)),
                  pl.BlockSpec((1, 1, d, dff), lambda i, te, nu: (layer, te[i], 0, 0)),
                  pl.BlockSpec((1, 1, dff, d), lambda i, te, nu: (layer, te[i], 0, 0))],
        out_specs=rows,
        scratch_shapes=[pltpu.VMEM((d, dff), BF16), pltpu.VMEM((d, dff), BF16), pltpu.VMEM((dff, d), BF16)],
    )
    return pl.pallas_call(
        _moe_kernel,
        grid_spec=grid_spec,
        out_shape=jax.ShapeDtypeStruct((n_tiles * tm * ROW_PITCH, LANES), F32),
        compiler_params=_cparams(("arbitrary",)),
        name="moe_experts",
    )(tile_exp, n_used, xs, w1, w3, w2)


def _combine_kernel(pos_ref, y_hbm, x_ref, tw_ref, gate_ref, nw_ref, sh_ref, sc_ref, *out_and_scratch, final, n_tok):
    if final:
        o_ref, ybuf, sem = out_and_scratch
    else:
        xo_ref, h_ref, ybuf, sem = out_and_scratch
    tm, d = x_ref.shape
    n_sub = d // LANES
    i = pl.program_id(0)
    n_steps = pl.num_programs(0)
    slot = i % 2

    def start_gather(tile, sl):
        def body(r8, c):
            for u in range(GATHER_UNROLL):
                r = r8 * GATHER_UNROLL + u
                for k in range(TOP_K):
                    src = pos_ref[k * n_tok + tile * tm + r]
                    pltpu.make_async_copy(y_hbm.at[pl.ds(src, n_sub)], ybuf.at[sl, k, pl.ds(r * ROW_PITCH, n_sub)],
                                          sem.at[sl]).start(priority=k)
            return c
        lax.fori_loop(0, tm // GATHER_UNROLL, body, 0)

    def wait_gather(sl):
        for k in range(TOP_K):
            pltpu.make_async_copy(y_hbm.at[pl.ds(0, tm * n_sub)], ybuf.at[sl, k, pl.ds(0, tm * n_sub)], sem.at[sl]).wait()

    @pl.when(i == 0)
    def _():
        start_gather(0, 0)

    @pl.when(i + 1 < n_steps)
    def _():
        start_gather(i + 1, 1 - slot)

    wait_gather(slot)
    tw = tw_ref[...]
    moe = tw[:, 0:1] * _load_rows(ybuf.at[slot, 0], tm, d) + tw[:, 1:2] * _load_rows(ybuf.at[slot, 1], tm, d)
    x_new = x_ref[...] + gate_ref[0] * moe
    if final:
        o_ref[...] = x_new * lax.rsqrt(jnp.mean(x_new * x_new, axis=-1, keepdims=True) + NORM_EPS) * nw_ref[...]
    else:
        xo_ref[...] = x_new
        h_ref[...] = _rms_mod(x_new, nw_ref[...], sh_ref[0], sc_ref[0]).astype(h_ref.dtype)


def _combine(pos, y_sorted, x2, top_w, gate, nw, shift, scale, bsz, seq, final):
    t, d = x2.shape
    tm = ROW_TM
    ns = seq // tm
    tw = top_w.T
    row = pl.BlockSpec((tm, d), lambda i, p: (i, 0))
    per_b = pl.BlockSpec((1, 1, d), lambda i, p: (i // ns, 0, 0))
    grid_spec = pltpu.PrefetchScalarGridSpec(
        num_scalar_prefetch=1,
        grid=(t // tm,),
        in_specs=[pl.BlockSpec(memory_space=pl.ANY), row,
                  pl.BlockSpec((tm, SUBLANES), lambda i, p: (i, 0)),
                  per_b, pl.BlockSpec((1, d), lambda i, p: (0, 0)), per_b, per_b],
        out_specs=row if final else [row, row],
        scratch_shapes=[pltpu.VMEM((2, TOP_K, tm * ROW_PITCH, LANES), F32), pltpu.SemaphoreType.DMA((2,))],
    )
    out_shape = (jax.ShapeDtypeStruct((t, d), F32) if final else
                 [jax.ShapeDtypeStruct((t, d), F32), jax.ShapeDtypeStruct((t, d), BF16)])
    return pl.pallas_call(
        functools.partial(_combine_kernel, final=final, n_tok=t),
        grid_spec=grid_spec,
        out_shape=out_shape,
        compiler_params=_cparams(("arbitrary",)),
        name="moe_combine",
    )(pos, y_sorted, x2, tw, gate.reshape(bsz, 1, d), nw.reshape(1, d), shift.reshape(bsz, 1, d), scale.reshape(bsz, 1, d))


def _rwkv_prep_kernel(h_ref, mu_ref, wr_ref, wk_ref, wv_ref, dec0_ref, dec1_ref, dec2_ref, a0_ref, a1_ref, a2_ref,
                      g1_ref, g2_ref, kk_ref, ka_ref, rk_ref, seg_ref,
                      r_out, lw_out, k_out, v_out, kkn_out, a_out, g_out, bonus_out, carry_ref, hp_ref):
    tm = h_ref.shape[0]

    @pl.when(pl.program_id(1) == 0)
    def _():
        carry_ref[...] = jnp.zeros_like(carry_ref)

    hb = h_ref[...]
    h = hb.astype(F32)
    hp_ref[0:SUBLANES, :] = carry_ref[...]
    hp_ref[SUBLANES:SUBLANES + tm, :] = h
    carry_ref[...] = h[tm - SUBLANES:tm, :]
    xxb = (hp_ref[pl.ds(SUBLANES - 1, tm), :] - h).astype(BF16)
    mix = lambda i: hb + xxb * mu_ref[i:i + 1, :].astype(BF16)
    r = _dot(mix(0), wr_ref[...])
    k = _dot(mix(1), wk_ref[...])
    v = _dot(mix(2), wv_ref[...])
    wl = dec0_ref[...] + _dot(jnp.tanh(_dot(mix(3), dec1_ref[...])).astype(BF16), dec2_ref[...])
    lw = -RW_DECAY_SCALE * _sigmoid(wl)
    a = _sigmoid(a0_ref[...] + _dot(_dot(mix(4), a1_ref[...]).astype(BF16), a2_ref[...]))
    g = _dot(_sigmoid(_dot(mix(5), g1_ref[...])).astype(BF16), g2_ref[...])
    kk = k * kk_ref[...]
    k_h = k * (1.0 + (a - 1.0) * ka_ref[...])
    st = _head_stats_expand(jnp.concatenate([kk * kk, r * k_h * rk_ref[...]], axis=0), seg_ref)
    kkn = kk * lax.rsqrt(jnp.maximum(st[:tm], 1e-24))
    r_out[...] = r.astype(r_out.dtype)
    lw_out[...] = lw
    k_out[...] = k_h.astype(k_out.dtype)
    v_out[...] = v.astype(v_out.dtype)
    kkn_out[...] = kkn.astype(kkn_out.dtype)
    a_out[...] = a.astype(a_out.dtype)
    g_out[...] = g.astype(g_out.dtype)
    bonus_out[...] = (st[tm:] * v).astype(bonus_out.dtype)


def _rwkv_prep(h, mu, w_rkv, dec0, dec1, dec2, a0, a1, a2, g1, g2, k_k, k_a, r_k, seg, bsz, seq):
    t, d = h.shape
    tm = ROW_TM
    ns = seq // tm
    padc = lambda w: jnp.zeros((d, LANES), F32).at[:, :w.shape[1]].set(w).astype(BF16)
    padr = lambda w: jnp.zeros((LANES, d), F32).at[:w.shape[0]].set(w).astype(BF16)
    vec = lambda v: v.reshape(1, d)
    args = [h, mu, w_rkv[0].astype(BF16), w_rkv[1].astype(BF16), w_rkv[2].astype(BF16), vec(dec0), padc(dec1), padr(dec2),
            vec(a0), padc(a1), padr(a2), padc(g1), padr(g2), vec(k_k), vec(k_a), vec(r_k), seg]
    row = pl.BlockSpec((tm, d), lambda b, s: (b * ns + s, 0))
    const = lambda a: pl.BlockSpec(a.shape, lambda b, s: (0,) * a.ndim)
    outs = [BF16, F32, BF16, BF16, BF16, BF16, BF16, BF16]
    return pl.pallas_call(
        _rwkv_prep_kernel,
        grid=(bsz, ns),
        in_specs=[row] + [const(a) for a in args[1:]],
        out_specs=[row] * len(outs),
        out_shape=[jax.ShapeDtypeStruct((t, d), dt) for dt in outs],
        scratch_shapes=[pltpu.VMEM((SUBLANES, d), F32), pltpu.VMEM((tm + SUBLANES, d), F32)],
        compiler_params=_cparams(("arbitrary", "arbitrary")),
        name="rwkv_prep",
    )(*args)


def _rwkv_scan_kernel(r_ref, lw_ref, k_ref, v_ref, kk_ref, a_ref, y_ref, st_ref):
    L = RW_CHUNK
    pw = RW_PACK * RW_N
    n_packs = st_ref.shape[0]
    sh = RW_N.bit_length() - 1

    @pl.when(pl.program_id(1) == 0)
    def _():
        st_ref[...] = jnp.zeros_like(st_ref)

    t_i = _iota((L, L), 0)
    s_i = _iota((L, L), 1)
    tril = (t_i >= s_i).astype(BF16)
    n_batch = lw_ref.shape[0]
    per_batch = n_packs // n_batch
    wc_all = [_dot_sel(tril, lw_ref[bb]) for bb in range(n_batch)]
    lane_head = _iota((L, pw), 1) >> sh
    s_loc = _iota((L, pw), 1) & (RW_N - 1)
    t_loc = _iota((L, pw), 0)
    strict = s_loc < t_loc
    incl = s_loc <= t_loc
    bd_mask = (_iota((pw, pw), 0) >> sh) == (_iota((pw, pw), 1) >> sh)

    def bdiag(x):
        return jnp.where(bd_mask, jnp.concatenate([x] * RW_PACK, axis=0), 0.0).astype(BF16)

    packs = range(n_packs)
    sls = [(p // per_batch, slice((p % per_batch) * pw, (p % per_batch + 1) * pw)) for p in packs]
    pr, vs, sts, kkas, ks, wcs = [], [], [], [], [], []
    for p in packs:
        bb, sl = sls[p]
        r = r_ref[bb, :, sl].astype(F32)
        lw = lw_ref[bb, :, sl]
        k = k_ref[bb, :, sl].astype(F32)
        kk = kk_ref[bb, :, sl].astype(F32)
        a = a_ref[bb, :, sl].astype(F32)
        wc = wc_all[bb][:, sl]
        e_inv = jnp.exp(-wc)
        kka = kk * a
        al = -kk * jnp.exp(wc - lw)
        rb = r * jnp.exp(wc)
        bt = kka * e_inv
        kt = k * e_inv
        lhs = jnp.concatenate([al, rb], axis=0).astype(BF16)
        rows = [jnp.where(lane_head == hh, x, 0.0) for x in (bt, kt) for hh in range(RW_PACK)]
        st = st_ref[p]
        m = jnp.concatenate(rows + [st], axis=0).astype(BF16)
        pr.append(_dot_nt(lhs, m))
        vs.append(v_ref[bb, :, sl].astype(F32))
        sts.append(st)
        kkas.append(kka)
        ks.append(k)
        wcs.append(wc)
    bd_vs = [bdiag(vs[p]) for p in packs]
    us = [pr[p][:L, 2 * pw:] + _dot(jnp.where(strict, pr[p][:L, pw:2 * pw], 0.0).astype(BF16), bd_vs[p]) for p in packs]
    nmats = [jnp.where(strict, pr[p][:L, 0:pw], 0.0) for p in packs]
    n_steps = L.bit_length() - 1
    for it in range(n_steps):
        us = [us[p] + _dot(nmats[p].astype(BF16), bdiag(us[p])) for p in packs]
        if it + 1 < n_steps:
            nmats = [_dot(nmats[p].astype(BF16), bdiag(nmats[p])) for p in packs]
    for p in packs:
        a_rb = jnp.where(incl, pr[p][L:, 0:pw], 0.0)
        a_rk = jnp.where(incl, pr[p][L:, pw:2 * pw], 0.0)
        bb, sl = sls[p]
        y_ref[bb, :, sl] = pr[p][L:, 2 * pw:] + _dot(jnp.concatenate([a_rb, a_rk], axis=1).astype(BF16),
                                                     jnp.concatenate([bdiag(us[p]), bd_vs[p]], axis=0))
    for p in packs:
        w_last = wcs[p][L - 1:L, :]
        e_last = jnp.exp(w_last - wcs[p])
        upd = _dot_tn(jnp.concatenate([us[p], vs[p]], axis=0).astype(BF16),
                      jnp.concatenate([kkas[p] * e_last, ks[p] * e_last], axis=0).astype(BF16))
        st_ref[p] = jnp.where(bd_mask, sts[p] * jnp.exp(w_last) + upd, 0.0)


def _rwkv_scan(r, lw, k, v, kk, a, bsz, seq):
    t, d = r.shape
    ns = seq // RW_CHUNK
    pw = RW_PACK * RW_N
    nb = RW_BATCH_ROWS if bsz % RW_BATCH_ROWS == 0 else 1
    row = pl.BlockSpec((nb, RW_CHUNK, d), lambda b, s: (b, s, 0))
    y = pl.pallas_call(
        _rwkv_scan_kernel,
        grid=(bsz // nb, ns),
        in_specs=[row] * 6,
        out_specs=row,
        out_shape=jax.ShapeDtypeStruct((bsz, seq, d), F32),
        scratch_shapes=[pltpu.VMEM((nb * (d // pw), pw, pw), F32)],
        compiler_params=_cparams(("arbitrary", "arbitrary")),
        name="rwkv_scan",
    )(*[z.reshape(bsz, seq, d) for z in (r, lw, k, v, kk, a)])
    return y.reshape(t, d)


def _moe_block(h, top_i, top_w, x2, gate, w1, w3, w2, layer, nw, shift, scale, bsz, seq, final):
    n_exp = w1.shape[1]
    pos, tile_exp, n_used, gap_lo, gap_hi = _route_plan(top_i, n_exp, MOE_TM)
    pos = pos * ROW_PITCH
    x_sorted = _dispatch(h, pos, gap_lo, gap_hi, tile_exp.shape[0] * MOE_TM)
    y_sorted = _moe(x_sorted, w1, w3, w2, layer, tile_exp, n_used)
    return _combine(pos, y_sorted, x2, top_w, gate, nw, shift, scale, bsz, seq, final)


def kernel(x, c, mod_w, mod_b, norm_w, hg_lb_logits, ev_w_in, ev_hg_norm, ev_conv_w, ev_conv_b, ev_dt_bias, ev_a_log, ev_d_skip, ev_ssm_norm, ev_w_out, od_mu, od_w_rkv, od_w_dec0, od_w_dec1, od_w_dec2, od_a0, od_a1, od_a2, od_g1, od_g2, od_k_k, od_k_a, od_r_k, od_ln_w, od_ln_b, od_w_o, router_w, router_b, moe_w1, moe_w3, moe_w2, final_norm_w):
    bsz, seq, d = x.shape
    depth = mod_w.shape[0]
    t = bsz * seq
    x2 = x.reshape(t, d)
    mod = _adaln_mod(c, mod_w, mod_b)
    gamma = jax.nn.softmax(hg_lb_logits.astype(F32), axis=0)
    lower_bounds = jnp.cumsum(gamma, axis=0) - gamma[0]
    head_of_lane = jnp.arange(RW_PACK * RW_N, dtype=I32) // RW_N
    seg = (head_of_lane[:, None] == head_of_lane[None, :]).astype(BF16)

    h = None
    out = None
    for l in range(depth):
        sh_m, sc_m, gt_m, sh_f, sc_f, gt_f = [mod[l, :, i * d:(i + 1) * d] for i in range(6)]
        j = l // 2
        if h is None:
            h = _norm_mod(x2, norm_w[l, 0], sh_m, sc_m, bsz, seq)
        if l % 2 == 0:
            w_in = ev_w_in[j]
            hgw = ev_hg_norm.shape[1]
            sw = ev_ssm_norm.shape[1]
            xbw = ev_conv_w.shape[2]
            nh = ev_dt_bias.shape[1]
            c0 = 4 * hgw
            w_hg = w_in[:, :c0].astype(BF16)
            w_ssd = jnp.zeros((d, sw + xbw + LANES), F32).at[:, :sw + xbw + nh].set(w_in[:, c0:]).astype(BF16)
            o_a = _hgrn2(h, w_hg, lower_bounds[l + 1], ev_hg_norm[j], bsz, seq)
            o_b = _ssd(h, w_ssd, ev_conv_w[j], ev_conv_b[j], ev_dt_bias[j], ev_a_log[j], ev_d_skip[j],
                       ev_ssm_norm[j], bsz, seq)
            w_out = ev_w_out[j].astype(BF16)
            x2, hf, top_i, top_w = _epilogue([o_a, o_b], [w_out[:hgw], w_out[hgw:]], x2, gt_m, norm_w[l, 1], sh_f, sc_f,
                                             router_w, router_b, bsz, seq)
        else:
            r, lw, k, v, kk, a, g, bonus = _rwkv_prep(h, od_mu[j], od_w_rkv[j], od_w_dec0[j], od_w_dec1[j], od_w_dec2[j],
                                                      od_a0[j], od_a1[j], od_a2[j], od_g1[j], od_g2[j], od_k_k[j],
                                                      od_k_a[j], od_r_k[j].reshape(-1), seg, bsz, seq)
            y = _rwkv_scan(r, lw, k, v, kk, a, bsz, seq)
            x2, hf, top_i, top_w = _epilogue([y], [od_w_o[j].astype(BF16)], x2, gt_m, norm_w[l, 1], sh_f, sc_f,
                                             router_w, router_b, bsz, seq,
                                             rwkv_extra=(g, bonus, od_ln_w[j], od_ln_b[j], seg))
        final = l == depth - 1
        if final:
            nw_next, sh_next, sc_next = final_norm_w, sh_f, sc_f
        else:
            nxt = [mod[l + 1, :, i * d:(i + 1) * d] for i in range(2)]
            nw_next, sh_next, sc_next = norm_w[l + 1, 0], nxt[0], nxt[1]
        res = _moe_block(hf, top_i, top_w, x2, gt_f, moe_w1, moe_w3, moe_w2, l, nw_next, sh_next, sc_next,
                         bsz, seq, final)
        if final:
            out = res
        else:
            x2, h = res
    return out.reshape(bsz, seq, d)
```

```python
import functools

import jax
import jax.numpy as jnp
from jax import lax
from jax.experimental import pallas as pl
from jax.experimental.pallas import tpu as pltpu

F32 = jnp.float32
BF16 = jnp.bfloat16
I32 = jnp.int32

NORM_EPS = 1e-6
RW_GN_EPS = 64e-5
RW_DECAY_SCALE = 0.6065306597126334
LANES = 128
SUBLANES = 8
VMEM_LIMIT = 56 * 1024 * 1024

HG_DK = 128
SSM_P = 64
SSM_N = 128
SSM_GROUPS = 2
SSM_CONV = 4
RW_N = 64
N_GROUPS_MOE = 4
TOP_K = 2

CHUNK = 128
HG_BATCH_ROWS = 2
RW_CHUNK = 64
RW_PACK = 4
RW_BATCH_ROWS = 2
MOE_TM = 512
ROW_TM = 256
NORM_TM = 512
EPILOGUE_SUBTILES = 4
DISPATCH_TM = 512
ZERO_CHUNK = 64
GATHER_UNROLL = 16
ROW_PITCH = 9


def _cparams(sem):
    return pltpu.CompilerParams(dimension_semantics=sem, vmem_limit_bytes=VMEM_LIMIT)


def _dot(a, b):
    return lax.dot_general(a, b, (((1,), (0,)), ((), ())), preferred_element_type=F32)


def _dot_nt(a, b):
    return lax.dot_general(a, b, (((1,), (1,)), ((), ())), preferred_element_type=F32)


def _dot_tn(a, b):
    return lax.dot_general(a, b, (((0,), (0,)), ((), ())), preferred_element_type=F32)


def _split(x):
    hi = x.astype(BF16)
    return hi, (x - hi.astype(F32)).astype(BF16)


def _dot_sel(sel, x):
    hi, lo = _split(x)
    return _dot(sel, hi) + _dot(sel, lo)


def _dot_rsel(x, sel):
    hi, lo = _split(x)
    return _dot(hi, sel) + _dot(lo, sel)


def _sigmoid(x):
    return 1.0 / (1.0 + jnp.exp(-x))


def _silu(x):
    return x * _sigmoid(x)


def _softplus(x):
    return jnp.maximum(x, 0.0) + jnp.log(1.0 + jnp.exp(-jnp.abs(x)))


def _iota(shape, dim):
    return lax.broadcasted_iota(I32, shape, dim)


def _store_rows(ref, val):
    tm, width = val.shape
    for j in range(width // LANES):
        ref[pl.ds(j, tm, stride=ROW_PITCH), :] = val[:, j * LANES:(j + 1) * LANES]
    for j in range(width // LANES, ROW_PITCH):
        ref[pl.ds(j, tm, stride=ROW_PITCH), :] = jnp.zeros((tm, LANES), ref.dtype)


def _load_rows(ref, tm, width):
    return jnp.concatenate([ref[pl.ds(j, tm, stride=ROW_PITCH), :] for j in range(width // LANES)], axis=1)


def _mod_kernel(c_ref, w_ref, b_ref, o_ref):
    c = c_ref[...]
    o_ref[0] = _dot(_silu(c).astype(BF16), w_ref[0].astype(BF16)) + b_ref[0]


def _adaln_mod(c, mod_w, mod_b):
    depth, d, width = mod_w.shape
    bsz = c.shape[0]
    c_pad = jnp.zeros((SUBLANES, d), F32).at[:bsz].set(c)
    tn = 1536
    out = pl.pallas_call(
        _mod_kernel,
        grid=(depth, width // tn),
        in_specs=[pl.BlockSpec((SUBLANES, d), lambda l, j: (0, 0)),
                  pl.BlockSpec((1, d, tn), lambda l, j: (l, 0, j)),
                  pl.BlockSpec((1, 1, tn), lambda l, j: (l, 0, j))],
        out_specs=pl.BlockSpec((1, SUBLANES, tn), lambda l, j: (l, 0, j)),
        out_shape=jax.ShapeDtypeStruct((depth, SUBLANES, width), F32),
        compiler_params=_cparams(("arbitrary", "arbitrary")),
        name="adaln_mod",
    )(c_pad, mod_w, mod_b.reshape(depth, 1, width))
    return out[:, :bsz]


def _rms_mod(x, nw, shift, scale):
    y = x * lax.rsqrt(jnp.mean(x * x, axis=-1, keepdims=True) + NORM_EPS) * nw
    return y * (1.0 + scale) + shift


def _normmod_kernel(x_ref, nw_ref, sh_ref, sc_ref, h_ref):
    h_ref[...] = _rms_mod(x_ref[...], nw_ref[...], sh_ref[0], sc_ref[0]).astype(h_ref.dtype)


def _norm_mod(x2, nw, shift, scale, bsz, seq):
    t, d = x2.shape
    tm = NORM_TM
    ns = seq // tm
    return pl.pallas_call(
        _normmod_kernel,
        grid=(bsz, ns),
        in_specs=[pl.BlockSpec((tm, d), lambda b, s: (b * ns + s, 0)),
                  pl.BlockSpec((1, d), lambda b, s: (0, 0)),
                  pl.BlockSpec((1, 1, d), lambda b, s: (b, 0, 0)),
                  pl.BlockSpec((1, 1, d), lambda b, s: (b, 0, 0))],
        out_specs=pl.BlockSpec((tm, d), lambda b, s: (b * ns + s, 0)),
        out_shape=jax.ShapeDtypeStruct((t, d), BF16),
        compiler_params=_cparams(("arbitrary", "arbitrary")),
        name="norm_mod",
    )(x2, nw.reshape(1, d), shift.reshape(bsz, 1, d), scale.reshape(bsz, 1, d))


def _block_mid_ref(b, n2):
    rows, width = b.shape
    n = n2 // 2
    if n2 >= 2 * SUBLANES:
        b3 = b.reshape(rows // n2, n2, width)
        return jnp.broadcast_to(b3[:, n - 1:n, :], b3.shape).reshape(rows, width)
    b3 = b.reshape(rows // SUBLANES, SUBLANES, width)
    sub = _iota(b3.shape, 1)
    r3 = jnp.broadcast_to(b3[:, SUBLANES - n2 + n - 1:SUBLANES - n2 + n, :], b3.shape)
    for g in range(SUBLANES // n2 - 2, -1, -1):
        r3 = jnp.where(sub < (g + 1) * n2, b3[:, g * n2 + n - 1:g * n2 + n, :], r3)
    return r3.reshape(rows, width)


def _hgrn2_kernel(h_ref, w_ref, lb_ref, nw_ref, o_ref, proj_ref, st_ref):
    L = CHUNK
    dk = HG_DK
    n_batch = h_ref.shape[0]
    n_heads = st_ref.shape[0] // n_batch
    width = n_heads * dk

    @pl.when(pl.program_id(1) == 0)
    def _():
        st_ref[...] = jnp.zeros_like(st_ref)

    proj_ref[...] = _dot(h_ref[...].reshape(n_batch * L, h_ref.shape[2]), w_ref[...])
    lb = lb_ref[...]
    f = lb + (1.0 - lb) * _sigmoid(proj_ref[:, width:2 * width])
    logf = jnp.log(f)
    t_i = _iota((L, L), 0)
    s_i = _iota((L, L), 1)
    tril = (t_i >= s_i).astype(BF16)
    b_all = [_dot_sel(tril, logf[bb * L:(bb + 1) * L]) for bb in range(n_batch)]
    eye = t_i == s_i

    levels = []
    n2 = L
    while n2 >= 2:
        n = n2 // 2
        sh = n2.bit_length() - 1
        m = ((t_i >> sh) == (s_i >> sh)) & ((t_i & (n2 - 1)) >= n) & ((s_i & (n2 - 1)) < n)
        levels.append((n2, m))
        n2 = n

    for h, bb in [(h, bb) for h in range(n_heads) for bb in range(n_batch)]:
        sl = slice(h * dk, (h + 1) * dk)
        rows = slice(bb * L, (bb + 1) * L)
        q = proj_ref[rows, sl]
        k = 1.0 - f[rows, sl]
        v = proj_ref[rows, 2 * width + h * dk:2 * width + (h + 1) * dk]
        b = b_all[bb][:, sl]
        st = st_ref[bb * n_heads + h]
        b_last = b[L - 1:L, :]
        o = _dot_nt((q * jnp.exp(b)).astype(BF16), st.astype(BF16))
        a = jnp.where(eye, jnp.sum(q * k, axis=-1, keepdims=True), 0.0)
        qb = q.astype(BF16)
        kb = k.astype(BF16)
        for n2, m in levels:
            e = jnp.exp(-jnp.abs(b - _block_mid_ref(b, n2))).astype(BF16)
            a = jnp.where(m, _dot_nt(qb * e, kb * e), a)
        o = o + _dot(a.astype(BF16), v.astype(BF16))
        ke = k * jnp.exp(b_last - b)
        st_ref[bb * n_heads + h] = st * jnp.exp(b_last) + _dot_tn(v.astype(BF16), ke.astype(BF16))
        g = proj_ref[rows, 3 * width + h * dk:3 * width + (h + 1) * dk]
        ms = jnp.mean(o * o, axis=-1, keepdims=True)
        o_ref[bb, :, sl] = (o * lax.rsqrt(ms + NORM_EPS) * nw_ref[:, sl] * _silu(g)).astype(o_ref.dtype)


def _hgrn2(h, w_hg, lb, hg_norm, bsz, seq):
    t, d = h.shape
    width = lb.shape[0]
    n_heads = width // HG_DK
    ns = seq // CHUNK
    nb = HG_BATCH_ROWS if bsz % HG_BATCH_ROWS == 0 else 1
    out = pl.pallas_call(
        _hgrn2_kernel,
        grid=(bsz // nb, ns),
        in_specs=[pl.BlockSpec((nb, CHUNK, d), lambda b, s: (b, s, 0)),
                  pl.BlockSpec(w_hg.shape, lambda b, s: (0, 0)),
                  pl.BlockSpec((1, width), lambda b, s: (0, 0)),
                  pl.BlockSpec((1, width), lambda b, s: (0, 0))],
        out_specs=pl.BlockSpec((nb, CHUNK, width), lambda b, s: (b, s, 0)),
        out_shape=jax.ShapeDtypeStruct((bsz, seq, width), BF16),
        scratch_shapes=[pltpu.VMEM((nb * CHUNK, w_hg.shape[1]), F32), pltpu.VMEM((nb * n_heads, HG_DK, HG_DK), F32)],
        compiler_params=_cparams(("arbitrary", "arbitrary")),
        name="hgrn2_scan",
    )(h.reshape(bsz, seq, d), w_hg, lb.reshape(1, width), hg_norm.reshape(1, width))
    return out.reshape(t, width)


def _ssd_kernel(h_ref, w_ref, cw_ref, cb_ref, dtb_ref, a_ref, dsk_ref, nw_ref, o_ref,
                proj_ref, carry_ref, xpad_ref, st_ref):
    L = CHUNK
    width = o_ref.shape[2]
    xw = cw_ref.shape[1]
    n_heads = width // SSM_P
    gw = width // SSM_GROUPS
    heads_per_group = n_heads // SSM_GROUPS

    @pl.when(pl.program_id(1) == 0)
    def _():
        carry_ref[...] = jnp.zeros_like(carry_ref)
        st_ref[...] = jnp.zeros_like(st_ref)

    n_batch = h_ref.shape[0]
    proj_ref[...] = _dot(h_ref[...].reshape(n_batch * L, h_ref.shape[2]), w_ref[...])
    t_i = _iota((L, L), 0)
    s_i = _iota((L, L), 1)
    triu = (t_i <= s_i).astype(BF16)
    causal = t_i >= s_i
    diag = t_i == s_i
    lane_lo = _iota((L, 2 * SSM_P), 1) < SSM_P
    bd_mask = (_iota((2 * L, 2 * SSM_P), 0) < L) == (_iota((2 * L, 2 * SSM_P), 1) < SSM_P)

    for bb in range(n_batch):
        rs = slice(bb * L, (bb + 1) * L)
        z_ref = proj_ref.at[rs, 0:width]
        dt_ref = proj_ref.at[rs, width + xw:]
        xraw = proj_ref[rs, width:width + xw]
        xpad_ref[bb, 0:SUBLANES, :] = carry_ref[bb]
        xpad_ref[bb, SUBLANES:SUBLANES + L, :] = xraw
        carry_ref[bb] = xraw[L - SUBLANES:L, :]
        acc = cb_ref[...] + jnp.zeros_like(xraw)
        for j in range(SSM_CONV):
            acc = acc + cw_ref[j:j + 1, :] * xpad_ref[bb, pl.ds(SUBLANES - (SSM_CONV - 1) + j, L), :]
        xc = _silu(acc)
        xs = xc[:, :width]
        bm = xc[:, width:width + SSM_GROUPS * SSM_N]
        cm = xc[:, width + SSM_GROUPS * SSM_N:]

        dt = _softplus(dt_ref[...] + dtb_ref[...])
        da = dt * a_ref[...]
        da_t = da.T
        dt_t = dt.T
        acs_t = _dot_rsel(da_t, triu)

        y_pairs = []
        for g in range(SSM_GROUPS):
            bg = bm[:, g * SSM_N:(g + 1) * SSM_N]
            cg = cm[:, g * SSM_N:(g + 1) * SSM_N]
            cb = _dot_nt(cg.astype(BF16), bg.astype(BF16))
            hg = st_ref[bb, :, g * gw:(g + 1) * gw]
            yoff_g = _dot(cg.astype(BF16), hg.astype(BF16))
            xsc_parts, decay_parts = [], []
            for pr in range(heads_per_group // 2):
                j0 = g * heads_per_group + 2 * pr
                gs, ds, ecol, elast = [], [], [], []
                for j in (j0, j0 + 1):
                    row_b = jnp.broadcast_to(acs_t[j:j + 1, :], (L, L))
                    col_b = row_b.T
                    dt_row = jnp.broadcast_to(dt_t[j:j + 1, :], (L, L))
                    lmat = jnp.exp(jnp.minimum(col_b - row_b, 0.0))
                    gs.append(jnp.where(causal, cb * lmat * dt_row, 0.0))
                    a_last = acs_t[j:j + 1, L - 1:L]
                    ds.append(jnp.where(diag, jnp.exp(a_last - row_b) * dt_row, 0.0))
                    ecol.append(jnp.exp(col_b))
                    elast.append(jnp.exp(a_last))
                lhs = jnp.concatenate([jnp.concatenate(gs, axis=1), jnp.concatenate(ds, axis=1)], axis=0)
                xs_pair = xs[:, j0 * SSM_P:(j0 + 2) * SSM_P]
                bd = jnp.where(bd_mask, jnp.concatenate([xs_pair, xs_pair], axis=0), 0.0)
                res = _dot(lhs.astype(BF16), bd.astype(BF16))
                yoff = yoff_g[:, pr * 2 * SSM_P:(pr + 1) * 2 * SSM_P] * jnp.where(lane_lo, ecol[0], ecol[1])
                y_pairs.append(res[:L] + yoff)
                xsc_parts.append(res[L:])
                decay_parts.append(jnp.where(lane_lo[0:1], elast[0], elast[1]))
            xsc_g = jnp.concatenate(xsc_parts, axis=1)
            decay_g = jnp.concatenate(decay_parts, axis=1)
            st_ref[bb, :, g * gw:(g + 1) * gw] = hg * decay_g + _dot_tn(bg.astype(BF16), xsc_g.astype(BF16))
        y = jnp.concatenate(y_pairs, axis=1) + dsk_ref[...] * xs
        yz = y * _silu(z_ref[...])
        for g in range(SSM_GROUPS):
            seg = yz[:, g * gw:(g + 1) * gw]
            ms = jnp.mean(seg * seg, axis=-1, keepdims=True)
            o_ref[bb, :, g * gw:(g + 1) * gw] = (seg * lax.rsqrt(ms + NORM_EPS)
                                                 * nw_ref[:, g * gw:(g + 1) * gw]).astype(o_ref.dtype)


def _ssd(h, w_ssd, conv_w, conv_b, dt_bias, a_log, d_skip, ssm_norm, bsz, seq):
    t, d = h.shape
    width = ssm_norm.shape[0]
    xw = conv_w.shape[1]
    n_heads = width // SSM_P
    ns = seq // CHUNK
    nb = HG_BATCH_ROWS if bsz % HG_BATCH_ROWS == 0 else 1
    pad = lambda v: jnp.zeros((1, LANES), F32).at[0, :n_heads].set(v)
    row = lambda w: pl.BlockSpec((nb, CHUNK, w), lambda b, s: (b, s, 0))
    const = lambda r, w: pl.BlockSpec((r, w), lambda b, s: (0, 0))
    out = pl.pallas_call(
        _ssd_kernel,
        grid=(bsz // nb, ns),
        in_specs=[row(d), const(*w_ssd.shape), const(SSM_CONV, xw), const(1, xw), const(1, LANES),
                  const(1, LANES), const(1, width), const(1, width)],
        out_specs=row(width),
        out_shape=jax.ShapeDtypeStruct((bsz, seq, width), BF16),
        scratch_shapes=[pltpu.VMEM((nb * CHUNK, w_ssd.shape[1]), F32), pltpu.VMEM((nb, SUBLANES, xw), F32),
                        pltpu.VMEM((nb, CHUNK + SUBLANES, xw), F32), pltpu.VMEM((nb, SSM_N, width), F32)],
        compiler_params=_cparams(("arbitrary", "arbitrary")),
        name="ssd_scan",
    )(h.reshape(bsz, seq, d), w_ssd, conv_w, conv_b.reshape(1, xw), pad(dt_bias), pad(-jnp.exp(a_log)),
      jnp.repeat(d_skip, SSM_P).reshape(1, width), ssm_norm.reshape(1, width))
    return out.reshape(t, width)


def _route(probs):
    n_exp = N_GROUPS_MOE * 4
    p = [probs[e:e + 1, :] for e in range(n_exp)]
    gs = []
    for g in range(N_GROUPS_MOE):
        a, b, c, d = p[4 * g:4 * g + 4]
        gs.append(jnp.maximum(jnp.maximum(jnp.maximum(a + b, a + c), jnp.maximum(a + d, b + c)),
                              jnp.maximum(b + d, c + d)))
    best = jnp.zeros_like(gs[0]).astype(I32)
    bs = gs[0]
    for g in range(1, N_GROUPS_MOE):
        upd = gs[g] > bs
        best = jnp.where(upd, g, best)
        bs = jnp.where(upd, gs[g], bs)
    q = [jnp.where(best == 0, p[i], jnp.where(best == 1, p[4 + i], jnp.where(best == 2, p[8 + i], p[12 + i])))
         for i in range(4)]
    i1 = jnp.zeros_like(best)
    v1 = q[0]
    for i in range(1, 4):
        upd = q[i] > v1
        i1 = jnp.where(upd, i, i1)
        v1 = jnp.where(upd, q[i], v1)
    i2 = jnp.zeros_like(best)
    v2 = jnp.full_like(v1, -1.0)
    for i in range(4):
        upd = (i1 != i) & (q[i] > v2)
        i2 = jnp.where(upd, i, i2)
        v2 = jnp.where(upd, q[i], v2)
    den = v1 + v2
    return best * 4 + i1, best * 4 + i2, v1 / den, v2 / den


def _head_stats_expand(stack, seg_ref):
    pw = seg_ref.shape[0]
    seg = seg_ref[...]
    return jnp.concatenate([_dot_rsel(stack[:, p * pw:(p + 1) * pw], seg) for p in range(stack.shape[1] // pw)], axis=1)


def _epilogue_kernel(*refs, n_in, rwkv):
    ins = refs[:n_in]
    pos = n_in
    w_refs = refs[pos:pos + n_in]
    pos += n_in
    if rwkv:
        g_ref, bonus_ref, lnw_ref, lnb_ref, seg_ref = refs[pos:pos + 5]
        pos += 5
    x_ref, gate_ref, nw_ref, sh_ref, sc_ref, rw_ref, rb_ref = refs[pos:pos + 7]
    pos += 7
    xo_ref, h_ref, ti_ref, tw_ref = refs[pos:pos + 4]

    tm = ROW_TM
    rw_hi, rw_lo = _split(rw_ref[...])
    for sub in range(x_ref.shape[0] // tm):
        rs = pl.ds(sub * tm, tm)
        if rwkv:
            o = ins[0][rs, :]
            inv_n = 1.0 / RW_N
            st = _head_stats_expand(jnp.concatenate([o, o * o], axis=0), seg_ref) * inv_n
            mean = st[:tm]
            var = jnp.maximum(st[tm:] - mean * mean, 0.0)
            o = (o - mean) * lax.rsqrt(var + RW_GN_EPS) * lnw_ref[...] + lnb_ref[...]
            o = (o + bonus_ref[rs, :].astype(F32)) * g_ref[rs, :].astype(F32)
            y = _dot(o.astype(BF16), w_refs[0][...])
        else:
            y = _dot(ins[0][rs, :], w_refs[0][...])
            for a_ref, w_ref in zip(ins[1:], w_refs[1:]):
                y = y + _dot(a_ref[rs, :], w_ref[...])
        x_new = x_ref[rs, :] + gate_ref[0] * y
        xo_ref[rs, :] = x_new
        h = _rms_mod(x_new, nw_ref[...], sh_ref[0], sc_ref[0])
        _store_rows(h_ref.at[pl.ds(sub * tm * ROW_PITCH, tm * ROW_PITCH)], h)
        h_hi, h_lo = _split(h)
        logits = _dot_nt(rw_hi, h_hi) + _dot_nt(rw_hi, h_lo) + _dot_nt(rw_lo, h_hi) + rb_ref[...]
        mx = jnp.max(logits, axis=0, keepdims=True)
        ex = jnp.exp(logits - mx)
        probs = ex / jnp.sum(ex, axis=0, keepdims=True)
        e1, e2, w1, w2 = _route(probs)
        zi = jnp.zeros((SUBLANES - TOP_K, tm), I32)
        ti_ref[:, sub * tm:(sub + 1) * tm] = jnp.concatenate([e1, e2, zi], axis=0)
        tw_ref[:, sub * tm:(sub + 1) * tm] = jnp.concatenate([w1, w2, zi.astype(F32)], axis=0)


def _epilogue(ins, ws, x2, gate, nw, shift, scale, router_w, router_b, bsz, seq, rwkv_extra=None):
    t, d = x2.shape
    tm = EPILOGUE_SUBTILES * ROW_TM
    ns = seq // tm
    n_exp = router_w.shape[1]
    rw_t = jnp.zeros((LANES, d), F32).at[:n_exp].set(router_w.T)
    rb = jnp.full((LANES, 1), -1e30, F32).at[:n_exp, 0].set(router_b)
    row = lambda w: pl.BlockSpec((tm, w), lambda b, s: (b * ns + s, 0))
    const = lambda a: pl.BlockSpec(a.shape, lambda b, s: (0,) * a.ndim)
    per_b = pl.BlockSpec((1, 1, d), lambda b, s: (b, 0, 0))
    args = list(ins) + list(ws)
    specs = [row(a.shape[1]) for a in ins] + [const(w) for w in ws]
    if rwkv_extra is not None:
        g, bonus, lnw, lnb, seg = rwkv_extra
        args += [g, bonus, lnw.reshape(1, d), lnb.reshape(1, d), seg]
        specs += [row(d), row(d), pl.BlockSpec((1, d), lambda b, s: (0, 0)), pl.BlockSpec((1, d), lambda b, s: (0, 0)),
                  const(seg)]
    args += [x2, gate.reshape(bsz, 1, d), nw.reshape(1, d), shift.reshape(bsz, 1, d), scale.reshape(bsz, 1, d), rw_t, rb]
    specs += [row(d), per_b, pl.BlockSpec((1, d), lambda b, s: (0, 0)), per_b, per_b, const(rw_t), const(rb)]
    lane_row = pl.BlockSpec((SUBLANES, tm), lambda b, s: (0, b * ns + s))
    return pl.pallas_call(
        functools.partial(_epilogue_kernel, n_in=len(ins), rwkv=rwkv_extra is not None),
        grid=(bsz, ns),
        in_specs=specs,
        out_specs=[row(d), pl.BlockSpec((tm * ROW_PITCH, LANES), lambda b, s: (b * ns + s, 0)), lane_row, lane_row],
        out_shape=[jax.ShapeDtypeStruct((t, d), F32), jax.ShapeDtypeStruct((t * ROW_PITCH, LANES), F32),
                   jax.ShapeDtypeStruct((SUBLANES, t), I32), jax.ShapeDtypeStruct((SUBLANES, t), F32)],
        compiler_params=_cparams(("arbitrary", "arbitrary")),
        name="mixer_epilogue",
    )(*args)


def _route_plan(top_i, n_exp, tm):
    t = top_i.shape[1]
    n_pairs = TOP_K * t
    n_tiles = n_pairs // tm + n_exp
    e_flat = top_i[:TOP_K].reshape(-1)
    onehot = (e_flat[:, None] == jnp.arange(n_exp, dtype=I32)[None, :]).astype(I32)
    csum = jnp.cumsum(onehot, axis=0)
    counts = csum[-1]
    rank = jnp.sum((csum - 1) * onehot, axis=1)
    padded = ((counts + tm - 1) // tm) * tm
    ends = jnp.cumsum(padded)
    pos = (ends - padded)[e_flat] + rank
    tile_start = jnp.arange(n_tiles, dtype=I32) * tm
    tile_exp = jnp.minimum(jnp.sum((tile_start[:, None] >= ends[None, :]).astype(I32), axis=1), n_exp - 1)
    n_used = (ends[-1] // tm).reshape(1).astype(I32)
    gap_lo = jnp.concatenate([ends - padded + counts, ends[-1:]]).astype(I32)
    gap_hi = jnp.concatenate([ends, jnp.full((1,), n_tiles * tm, I32)]).astype(I32)
    return pos.astype(I32), tile_exp.astype(I32), n_used, gap_lo, gap_hi


def _dispatch_kernel(pos_ref, glo_ref, ghi_ref, h_ref, xs_hbm, zbuf, sem, zsem):
    tm = DISPATCH_TM
    n_tok = pos_ref.shape[0] // TOP_K
    i = pl.program_id(0)

    @pl.when(i == 0)
    def _():
        zbuf[...] = jnp.zeros_like(zbuf)

        def chunk_copy(row):
            return pltpu.make_async_copy(zbuf, xs_hbm.at[pl.ds(row * ROW_PITCH, ZERO_CHUNK * ROW_PITCH)], zsem)

        def row_copy(row):
            return pltpu.make_async_copy(zbuf.at[pl.ds(0, ROW_PITCH)], xs_hbm.at[pl.ds(row * ROW_PITCH, ROW_PITCH)], zsem)

        for e in range(glo_ref.shape[0]):
            lo = glo_ref[e]
            hi = ghi_ref[e]
            n_chunks = lax.div(hi - lo, ZERO_CHUNK)
            mid = lo + n_chunks * ZERO_CHUNK

            def start_chunk(c, carry):
                chunk_copy(lo + c * ZERO_CHUNK).start()
                return carry

            def start_row(r, carry):
                row_copy(r).start()
                return carry

            def wait_chunk(c, carry):
                chunk_copy(lo).wait()
                return carry

            def wait_row(r, carry):
                row_copy(lo).wait()
                return carry

            lax.fori_loop(0, n_chunks, start_chunk, 0)
            lax.fori_loop(mid, hi, start_row, 0)
            lax.fori_loop(0, n_chunks, wait_chunk, 0)
            lax.fori_loop(mid, hi, wait_row, 0)

    def body(r8, c):
        for u in range(GATHER_UNROLL):
            r = r8 * GATHER_UNROLL + u
            for k in range(TOP_K):
                p = pos_ref[k * n_tok + i * tm + r]
                pltpu.make_async_copy(h_ref.at[pl.ds(r * ROW_PITCH, ROW_PITCH)],
                                      xs_hbm.at[pl.ds(p, ROW_PITCH)], sem).start(priority=k)
        return c
    lax.fori_loop(0, tm // GATHER_UNROLL, body, 0)
    for k in range(TOP_K):
        pltpu.make_async_copy(h_ref, xs_hbm.at[pl.ds(0, tm * ROW_PITCH)], sem).wait()


def _dispatch(h, pos, gap_lo, gap_hi, n_rows):
    tm = DISPATCH_TM
    t = h.shape[0] // ROW_PITCH
    grid_spec = pltpu.PrefetchScalarGridSpec(
        num_scalar_prefetch=3,
        grid=(t // tm,),
        in_specs=[pl.BlockSpec((tm * ROW_PITCH, LANES), lambda i, p, lo, hi: (i, 0))],
        out_specs=pl.BlockSpec(memory_space=pl.ANY),
        scratch_shapes=[pltpu.VMEM((ZERO_CHUNK * ROW_PITCH, LANES), F32), pltpu.SemaphoreType.DMA(()),
                        pltpu.SemaphoreType.DMA(())],
    )
    return pl.pallas_call(
        _dispatch_kernel,
        grid_spec=grid_spec,
        out_shape=jax.ShapeDtypeStruct((n_rows * ROW_PITCH, LANES), F32),
        compiler_params=_cparams(("arbitrary",)),
        name="moe_dispatch",
    )(pos, gap_lo, gap_hi, h)


def _moe_kernel(texp_ref, nused_ref, x_ref, w1_ref, w3_ref, w2_ref, y_ref, wb1, wb3, wb2):
    tm = MOE_TM
    d = wb1.shape[0]
    i = pl.program_id(0)
    n_used = nused_ref[0]

    @pl.when((i == 0) | (texp_ref[i] != texp_ref[jnp.maximum(i - 1, 0)]))
    def _():
        wb1[...] = w1_ref[0, 0].astype(BF16)
        wb3[...] = w3_ref[0, 0].astype(BF16)
        wb2[...] = w2_ref[0, 0].astype(BF16)

    @pl.when(i < n_used)
    def _():
        x = _load_rows(x_ref, tm, d).astype(BF16)
        a = _dot(x, wb1[...])
        b = _dot(x, wb3[...])
        hid = (_silu(a) * b).astype(BF16)
        _store_rows(y_ref, _dot(hid, wb2[...]))

    @pl.when(i >= n_used)
    def _():
        y_ref[...] = jnp.zeros_like(y_ref)


def _moe(xs, w1, w3, w2, layer, tile_exp, n_used):
    _, n_exp, d, dff = w1.shape
    tm = MOE_TM
    n_tiles = tile_exp.shape[0]
    rows = pl.BlockSpec((tm * ROW_PITCH, LANES), lambda i, te, nu: (i, 0))
    grid_spec = pltpu.PrefetchScalarGridSpec(
        num_scalar_prefetch=2,
        grid=(n_tiles,),
        in_specs=[rows,
                  pl.BlockSpec((1, 1, d, dff), lambda i, te, nu: (layer, te[i], 0, 0)),
                  pl.BlockSpec((1, 1, d, dff), lambda i, te, nu: (layer, te[i], 0, 0)),
                  pl.BlockSpec((1, 1, dff, d), lambda i, te, nu: (layer, te[i], 0, 0))],
        out_specs=rows,
        scratch_shapes=[pltpu.VMEM((d, dff), BF16), pltpu.VMEM((d, dff), BF16), pltpu.VMEM((dff, d), BF16)],
    )
    return pl.pallas_call(
        _moe_kernel,
        grid_spec=grid_spec,
        out_shape=jax.ShapeDtypeStruct((n_tiles * tm * ROW_PITCH, LANES), F32),
        compiler_params=_cparams(("arbitrary",)),
        name="moe_experts",
    )(tile_exp, n_used, xs, w1, w3, w2)


def _combine_kernel(pos_ref, y_hbm, x_ref, tw_ref, gate_ref, nw_ref, sh_ref, sc_ref, *out_and_scratch, final, n_tok):
    if final:
        o_ref, ybuf, sem = out_and_scratch
    else:
        xo_ref, h_ref, ybuf, sem = out_and_scratch
    tm, d = x_ref.shape
    n_sub = d // LANES
    i = pl.program_id(0)
    n_steps = pl.num_programs(0)
    slot = i % 2

    def start_gather(tile, sl):
        def body(r8, c):
            for u in range(GATHER_UNROLL):
                r = r8 * GATHER_UNROLL + u
                for k in range(TOP_K):
                    src = pos_ref[k * n_tok + tile * tm + r]
                    pltpu.make_async_copy(y_hbm.at[pl.ds(src, n_sub)], ybuf.at[sl, k, pl.ds(r * ROW_PITCH, n_sub)],
                                          sem.at[sl]).start(priority=k)
            return c
        lax.fori_loop(0, tm // GATHER_UNROLL, body, 0)

    def wait_gather(sl):
        for k in range(TOP_K):
            pltpu.make_async_copy(y_hbm.at[pl.ds(0, tm * n_sub)], ybuf.at[sl, k, pl.ds(0, tm * n_sub)], sem.at[sl]).wait()

    @pl.when(i == 0)
    def _():
        start_gather(0, 0)

    @pl.when(i + 1 < n_steps)
    def _():
        start_gather(i + 1, 1 - slot)

    wait_gather(slot)
    tw = tw_ref[...]
    moe = tw[:, 0:1] * _load_rows(ybuf.at[slot, 0], tm, d) + tw[:, 1:2] * _load_rows(ybuf.at[slot, 1], tm, d)
    x_new = x_ref[...] + gate_ref[0] * moe
    if final:
        o_ref[...] = x_new * lax.rsqrt(jnp.mean(x_new * x_new, axis=-1, keepdims=True) + NORM_EPS) * nw_ref[...]
    else:
        xo_ref[...] = x_new
        h_ref[...] = _rms_mod(x_new, nw_ref[...], sh_ref[0], sc_ref[0]).astype(h_ref.dtype)


def _combine(pos, y_sorted, x2, top_w, gate, nw, shift, scale, bsz, seq, final):
    t, d = x2.shape
    tm = ROW_TM
    ns = seq // tm
    tw = top_w.T
    row = pl.BlockSpec((tm, d), lambda i, p: (i, 0))
    per_b = pl.BlockSpec((1, 1, d), lambda i, p: (i // ns, 0, 0))
    grid_spec = pltpu.PrefetchScalarGridSpec(
        num_scalar_prefetch=1,
        grid=(t // tm,),
        in_specs=[pl.BlockSpec(memory_space=pl.ANY), row,
                  pl.BlockSpec((tm, SUBLANES), lambda i, p: (i, 0)),
                  per_b, pl.BlockSpec((1, d), lambda i, p: (0, 0)), per_b, per_b],
        out_specs=row if final else [row, row],
        scratch_shapes=[pltpu.VMEM((2, TOP_K, tm * ROW_PITCH, LANES), F32), pltpu.SemaphoreType.DMA((2,))],
    )
    out_shape = (jax.ShapeDtypeStruct((t, d), F32) if final else
                 [jax.ShapeDtypeStruct((t, d), F32), jax.ShapeDtypeStruct((t, d), BF16)])
    return pl.pallas_call(
        functools.partial(_combine_kernel, final=final, n_tok=t),
        grid_spec=grid_spec,
        out_shape=out_shape,
        compiler_params=_cparams(("arbitrary",)),
        name="moe_combine",
    )(pos, y_sorted, x2, tw, gate.reshape(bsz, 1, d), nw.reshape(1, d), shift.reshape(bsz, 1, d), scale.reshape(bsz, 1, d))


def _rwkv_prep_kernel(h_ref, mu_ref, wr_ref, wk_ref, wv_ref, dec0_ref, dec1_ref, dec2_ref, a0_ref, a1_ref, a2_ref,
                      g1_ref, g2_ref, kk_ref, ka_ref, rk_ref, seg_ref,
                      r_out, lw_out, k_out, v_out, kkn_out, a_out, g_out, bonus_out, carry_ref, hp_ref):
    tm = h_ref.shape[0]

    @pl.when(pl.program_id(1) == 0)
    def _():
        carry_ref[...] = jnp.zeros_like(carry_ref)

    hb = h_ref[...]
    h = hb.astype(F32)
    hp_ref[0:SUBLANES, :] = carry_ref[...]
    hp_ref[SUBLANES:SUBLANES + tm, :] = h
    carry_ref[...] = h[tm - SUBLANES:tm, :]
    xxb = (hp_ref[pl.ds(SUBLANES - 1, tm), :] - h).astype(BF16)
    mix = lambda i: hb + xxb * mu_ref[i:i + 1, :].astype(BF16)
    r = _dot(mix(0), wr_ref[...])
    k = _dot(mix(1), wk_ref[...])
    v = _dot(mix(2), wv_ref[...])
    wl = dec0_ref[...] + _dot(jnp.tanh(_dot(mix(3), dec1_ref[...])).astype(BF16), dec2_ref[...])
    lw = -RW_DECAY_SCALE * _sigmoid(wl)
    a = _sigmoid(a0_ref[...] + _dot(_dot(mix(4), a1_ref[...]).astype(BF16), a2_ref[...]))
    g = _dot(_sigmoid(_dot(mix(5), g1_ref[...])).astype(BF16), g2_ref[...])
    kk = k * kk_ref[...]
    k_h = k * (1.0 + (a - 1.0) * ka_ref[...])
    st = _head_stats_expand(jnp.concatenate([kk * kk, r * k_h * rk_ref[...]], axis=0), seg_ref)
    kkn = kk * lax.rsqrt(jnp.maximum(st[:tm], 1e-24))
    r_out[...] = r.astype(r_out.dtype)
    lw_out[...] = lw
    k_out[...] = k_h.astype(k_out.dtype)
    v_out[...] = v.astype(v_out.dtype)
    kkn_out[...] = kkn.astype(kkn_out.dtype)
    a_out[...] = a.astype(a_out.dtype)
    g_out[...] = g.astype(g_out.dtype)
    bonus_out[...] = (st[tm:] * v).astype(bonus_out.dtype)


def _rwkv_prep(h, mu, w_rkv, dec0, dec1, dec2, a0, a1, a2, g1, g2, k_k, k_a, r_k, seg, bsz, seq):
    t, d = h.shape
    tm = ROW_TM
    ns = seq // tm
    padc = lambda w: jnp.zeros((d, LANES), F32).at[:, :w.shape[1]].set(w).astype(BF16)
    padr = lambda w: jnp.zeros((LANES, d), F32).at[:w.shape[0]].set(w).astype(BF16)
    vec = lambda v: v.reshape(1, d)
    args = [h, mu, w_rkv[0].astype(BF16), w_rkv[1].astype(BF16), w_rkv[2].astype(BF16), vec(dec0), padc(dec1), padr(dec2),
            vec(a0), padc(a1), padr(a2), padc(g1), padr(g2), vec(k_k), vec(k_a), vec(r_k), seg]
    row = pl.BlockSpec((tm, d), lambda b, s: (b * ns + s, 0))
    const = lambda a: pl.BlockSpec(a.shape, lambda b, s: (0,) * a.ndim)
    outs = [BF16, F32, BF16, BF16, BF16, BF16, BF16, BF16]
    return pl.pallas_call(
        _rwkv_prep_kernel,
        grid=(bsz, ns),
        in_specs=[row] + [const(a) for a in args[1:]],
        out_specs=[row] * len(outs),
        out_shape=[jax.ShapeDtypeStruct((t, d), dt) for dt in outs],
        scratch_shapes=[pltpu.VMEM((SUBLANES, d), F32), pltpu.VMEM((tm + SUBLANES, d), F32)],
        compiler_params=_cparams(("arbitrary", "arbitrary")),
        name="rwkv_prep",
    )(*args)


def _rwkv_scan_kernel(r_ref, lw_ref, k_ref, v_ref, kk_ref, a_ref, y_ref, st_ref):
    L = RW_CHUNK
    pw = RW_PACK * RW_N
    n_packs = st_ref.shape[0]
    sh = RW_N.bit_length() - 1

    @pl.when(pl.program_id(1) == 0)
    def _():
        st_ref[...] = jnp.zeros_like(st_ref)

    t_i = _iota((L, L), 0)
    s_i = _iota((L, L), 1)
    tril = (t_i >= s_i).astype(BF16)
    n_batch = lw_ref.shape[0]
    per_batch = n_packs // n_batch
    wc_all = [_dot_sel(tril, lw_ref[bb]) for bb in range(n_batch)]
    lane_head = _iota((L, pw), 1) >> sh
    s_loc = _iota((L, pw), 1) & (RW_N - 1)
    t_loc = _iota((L, pw), 0)
    strict = s_loc < t_loc
    incl = s_loc <= t_loc
    bd_mask = (_iota((pw, pw), 0) >> sh) == (_iota((pw, pw), 1) >> sh)

    def bdiag(x):
        return jnp.where(bd_mask, jnp.concatenate([x] * RW_PACK, axis=0), 0.0).astype(BF16)

    packs = range(n_packs)
    sls = [(p // per_batch, slice((p % per_batch) * pw, (p % per_batch + 1) * pw)) for p in packs]
    pr, vs, sts, kkas, ks, wcs = [], [], [], [], [], []
    for p in packs:
        bb, sl = sls[p]
        r = r_ref[bb, :, sl].astype(F32)
        lw = lw_ref[bb, :, sl]
        k = k_ref[bb, :, sl].astype(F32)
        kk = kk_ref[bb, :, sl].astype(F32)
        a = a_ref[bb, :, sl].astype(F32)
        wc = wc_all[bb][:, sl]
        e_inv = jnp.exp(-wc)
        kka = kk * a
        al = -kk * jnp.exp(wc - lw)
        rb = r * jnp.exp(wc)
        bt = kka * e_inv
        kt = k * e_inv
        lhs = jnp.concatenate([al, rb], axis=0).astype(BF16)
        rows = [jnp.where(lane_head == hh, x, 0.0) for x in (bt, kt) for hh in range(RW_PACK)]
        st = st_ref[p]
        m = jnp.concatenate(rows + [st], axis=0).astype(BF16)
        pr.append(_dot_nt(lhs, m))
        vs.append(v_ref[bb, :, sl].astype(F32))
        sts.append(st)
        kkas.append(kka)
        ks.append(k)
        wcs.append(wc)
    bd_vs = [bdiag(vs[p]) for p in packs]
    us = [pr[p][:L, 2 * pw:] + _dot(jnp.where(strict, pr[p][:L, pw:2 * pw], 0.0).astype(BF16), bd_vs[p]) for p in packs]
    nmats = [jnp.where(strict, pr[p][:L, 0:pw], 0.0) for p in packs]
    n_steps = L.bit_length() - 1
    for it in range(n_steps):
        us = [us[p] + _dot(nmats[p].astype(BF16), bdiag(us[p])) for p in packs]
        if it + 1 < n_steps:
            nmats = [_dot(nmats[p].astype(BF16), bdiag(nmats[p])) for p in packs]
    for p in packs:
        a_rb = jnp.where(incl, pr[p][L:, 0:pw], 0.0)
        a_rk = jnp.where(incl, pr[p][L:, pw:2 * pw], 0.0)
        bb, sl = sls[p]
        y_ref[bb, :, sl] = pr[p][L:, 2 * pw:] + _dot(jnp.concatenate([a_rb, a_rk], axis=1).astype(BF16),
                                                     jnp.concatenate([bdiag(us[p]), bd_vs[p]], axis=0))
    for p in packs:
        w_last = wcs[p][L - 1:L, :]
        e_last = jnp.exp(w_last - wcs[p])
        upd = _dot_tn(jnp.concatenate([us[p], vs[p]], axis=0).astype(BF16),
                      jnp.concatenate([kkas[p] * e_last, ks[p] * e_last], axis=0).astype(BF16))
        st_ref[p] = jnp.where(bd_mask, sts[p] * jnp.exp(w_last) + upd, 0.0)


def _rwkv_scan(r, lw, k, v, kk, a, bsz, seq):
    t, d = r.shape
    ns = seq // RW_CHUNK
    pw = RW_PACK * RW_N
    nb = RW_BATCH_ROWS if bsz % RW_BATCH_ROWS == 0 else 1
    row = pl.BlockSpec((nb, RW_CHUNK, d), lambda b, s: (b, s, 0))
    y = pl.pallas_call(
        _rwkv_scan_kernel,
        grid=(bsz // nb, ns),
        in_specs=[row] * 6,
        out_specs=row,
        out_shape=jax.ShapeDtypeStruct((bsz, seq, d), F32),
        scratch_shapes=[pltpu.VMEM((nb * (d // pw), pw, pw), F32)],
        compiler_params=_cparams(("arbitrary", "arbitrary")),
        name="rwkv_scan",
    )(*[z.reshape(bsz, seq, d) for z in (r, lw, k, v, kk, a)])
    return y.reshape(t, d)


def _moe_block(h, top_i, top_w, x2, gate, w1, w3, w2, layer, nw, shift, scale, bsz, seq, final):
    n_exp = w1.shape[1]
    pos, tile_exp, n_used, gap_lo, gap_hi = _route_plan(top_i, n_exp, MOE_TM)
    pos = pos * ROW_PITCH
    x_sorted = _dispatch(h, pos, gap_lo, gap_hi, tile_exp.shape[0] * MOE_TM)
    y_sorted = _moe(x_sorted, w1, w3, w2, layer, tile_exp, n_used)
    return _combine(pos, y_sorted, x2, top_w, gate, nw, shift, scale, bsz, seq, final)


def kernel(x, c, mod_w, mod_b, norm_w, hg_lb_logits, ev_w_in, ev_hg_norm, ev_conv_w, ev_conv_b, ev_dt_bias, ev_a_log, ev_d_skip, ev_ssm_norm, ev_w_out, od_mu, od_w_rkv, od_w_dec0, od_w_dec1, od_w_dec2, od_a0, od_a1, od_a2, od_g1, od_g2, od_k_k, od_k_a, od_r_k, od_ln_w, od_ln_b, od_w_o, router_w, router_b, moe_w1, moe_w3, moe_w2, final_norm_w):
    bsz, seq, d = x.shape
    depth = mod_w.shape[0]
    t = bsz * seq
    x2 = x.reshape(t, d)
    mod = _adaln_mod(c, mod_w, mod_b)
    gamma = jax.nn.softmax(hg_lb_logits.astype(F32), axis=0)
    lower_bounds = jnp.cumsum(gamma, axis=0) - gamma[0]
    head_of_lane = jnp.arange(RW_PACK * RW_N, dtype=I32) // RW_N
    seg = (head_of_lane[:, None] == head_of_lane[None, :]).astype(BF16)

    h = None
    out = None
    for l in range(depth):
        sh_m, sc_m, gt_m, sh_f, sc_f, gt_f = [mod[l, :, i * d:(i + 1) * d] for i in range(6)]
        j = l // 2
        if h is None:
            h = _norm_mod(x2, norm_w[l, 0], sh_m, sc_m, bsz, seq)
        if l % 2 == 0:
            w_in = ev_w_in[j]
            hgw = ev_hg_norm.shape[1]
            sw = ev_ssm_norm.shape[1]
            xbw = ev_conv_w.shape[2]
            nh = ev_dt_bias.shape[1]
            c0 = 4 * hgw
            w_hg = w_in[:, :c0].astype(BF16)
            w_ssd = jnp.zeros((d, sw + xbw + LANES), F32).at[:, :sw + xbw + nh].set(w_in[:, c0:]).astype(BF16)
            o_a = _hgrn2(h, w_hg, lower_bounds[l + 1], ev_hg_norm[j], bsz, seq)
            o_b = _ssd(h, w_ssd, ev_conv_w[j], ev_conv_b[j], ev_dt_bias[j], ev_a_log[j], ev_d_skip[j],
                       ev_ssm_norm[j], bsz, seq)
            w_out = ev_w_out[j].astype(BF16)
            x2, hf, top_i, top_w = _epilogue([o_a, o_b], [w_out[:hgw], w_out[hgw:]], x2, gt_m, norm_w[l, 1], sh_f, sc_f,
                                             router_w, router_b, bsz, seq)
        else:
            r, lw, k, v, kk, a, g, bonus = _rwkv_prep(h, od_mu[j], od_w_rkv[j], od_w_dec0[j], od_w_dec1[j], od_w_dec2[j],
                                                      od_a0[j], od_a1[j], od_a2[j], od_g1[j], od_g2[j], od_k_k[j],
                                                      od_k_a[j], od_r_k[j].reshape(-1), seg, bsz, seq)
            y = _rwkv_scan(r, lw, k, v, kk, a, bsz, seq)
            x2, hf, top_i, top_w = _epilogue([y], [od_w_o[j].astype(BF16)], x2, gt_m, norm_w[l, 1], sh_f, sc_f,
                                             router_w, router_b, bsz, seq,
                                             rwkv_extra=(g, bonus, od_ln_w[j], od_ln_b[j], seg))
        final = l == depth - 1
        if final:
            nw_next, sh_next, sc_next = final_norm_w, sh_f, sc_f
        else:
            nxt = [mod[l + 1, :, i * d:(i + 1) * d] for i in range(2)]
            nw_next, sh_next, sc_next = norm_w[l + 1, 0], nxt[0], nxt[1]
        res = _moe_block(hf, top_i, top_w, x2, gt_f, moe_w1, moe_w3, moe_w2, l, nw_next, sh_next, sc_next,
                         bsz, seq, final)
        if final:
            out = res
        else:
            x2, h = res
    return out.reshape(bsz, seq, d)
```

```python
import functools

import jax
import jax.numpy as jnp
from jax import lax
from jax.experimental import pallas as pl
from jax.experimental.pallas import tpu as pltpu

F32 = jnp.float32
BF16 = jnp.bfloat16
I32 = jnp.int32

NORM_EPS = 1e-6
RW_GN_EPS = 64e-5
RW_DECAY_SCALE = 0.6065306597126334
LANES = 128
SUBLANES = 8
VMEM_LIMIT = 56 * 1024 * 1024

HG_DK = 128
SSM_P = 64
SSM_N = 128
SSM_GROUPS = 2
SSM_CONV = 4
RW_N = 64
N_GROUPS_MOE = 4
TOP_K = 2

CHUNK = 128
HG_BATCH_ROWS = 2
SSD_BATCH_ROWS = 4
RW_CHUNK = 64
RW_PACK = 4
RW_BATCH_ROWS = 2
MOE_TM = 512
ROW_TM = 256
NORM_TM = 512
EPILOGUE_SUBTILES = 4
DISPATCH_TM = 512
ZERO_CHUNK = 64
GATHER_UNROLL = 16
ROW_PITCH = 9


def _cparams(sem):
    return pltpu.CompilerParams(dimension_semantics=sem, vmem_limit_bytes=VMEM_LIMIT)


def _dot(a, b):
    return lax.dot_general(a, b, (((1,), (0,)), ((), ())), preferred_element_type=F32)


def _dot_nt(a, b):
    return lax.dot_general(a, b, (((1,), (1,)), ((), ())), preferred_element_type=F32)


def _dot_tn(a, b):
    return lax.dot_general(a, b, (((0,), (0,)), ((), ())), preferred_element_type=F32)


def _split(x):
    hi = x.astype(BF16)
    return hi, (x - hi.astype(F32)).astype(BF16)


def _dot_sel(sel, x):
    hi, lo = _split(x)
    return _dot(sel, hi) + _dot(sel, lo)


def _dot_rsel(x, sel):
    hi, lo = _split(x)
    return _dot(hi, sel) + _dot(lo, sel)


def _sigmoid(x):
    return 1.0 / (1.0 + jnp.exp(-x))


def _silu(x):
    return x * _sigmoid(x)


def _softplus(x):
    return jnp.maximum(x, 0.0) + jnp.log(1.0 + jnp.exp(-jnp.abs(x)))


def _iota(shape, dim):
    return lax.broadcasted_iota(I32, shape, dim)


def _store_rows(ref, val):
    tm, width = val.shape
    for j in range(width // LANES):
        ref[pl.ds(j, tm, stride=ROW_PITCH), :] = val[:, j * LANES:(j + 1) * LANES]
    for j in range(width // LANES, ROW_PITCH):
        ref[pl.ds(j, tm, stride=ROW_PITCH), :] = jnp.zeros((tm, LANES), ref.dtype)


def _load_rows(ref, tm, width):
    return jnp.concatenate([ref[pl.ds(j, tm, stride=ROW_PITCH), :] for j in range(width // LANES)], axis=1)


def _mod_kernel(c_ref, w_ref, b_ref, o_ref):
    c = c_ref[...]
    o_ref[0] = _dot(_silu(c).astype(BF16), w_ref[0].astype(BF16)) + b_ref[0]


def _adaln_mod(c, mod_w, mod_b):
    depth, d, width = mod_w.shape
    bsz = c.shape[0]
    c_pad = jnp.zeros((SUBLANES, d), F32).at[:bsz].set(c)
    tn = 1536
    out = pl.pallas_call(
        _mod_kernel,
        grid=(depth, width // tn),
        in_specs=[pl.BlockSpec((SUBLANES, d), lambda l, j: (0, 0)),
                  pl.BlockSpec((1, d, tn), lambda l, j: (l, 0, j)),
                  pl.BlockSpec((1, 1, tn), lambda l, j: (l, 0, j))],
        out_specs=pl.BlockSpec((1, SUBLANES, tn), lambda l, j: (l, 0, j)),
        out_shape=jax.ShapeDtypeStruct((depth, SUBLANES, width), F32),
        compiler_params=_cparams(("arbitrary", "arbitrary")),
        name="adaln_mod",
    )(c_pad, mod_w, mod_b.reshape(depth, 1, width))
    return out[:, :bsz]


def _rms_mod(x, nw, shift, scale):
    y = x * lax.rsqrt(jnp.mean(x * x, axis=-1, keepdims=True) + NORM_EPS) * nw
    return y * (1.0 + scale) + shift


def _normmod_kernel(x_ref, nw_ref, sh_ref, sc_ref, h_ref):
    h_ref[...] = _rms_mod(x_ref[...], nw_ref[...], sh_ref[0], sc_ref[0]).astype(h_ref.dtype)


def _norm_mod(x2, nw, shift, scale, bsz, seq):
    t, d = x2.shape
    tm = NORM_TM
    ns = seq // tm
    return pl.pallas_call(
        _normmod_kernel,
        grid=(bsz, ns),
        in_specs=[pl.BlockSpec((tm, d), lambda b, s: (b * ns + s, 0)),
                  pl.BlockSpec((1, d), lambda b, s: (0, 0)),
                  pl.BlockSpec((1, 1, d), lambda b, s: (b, 0, 0)),
                  pl.BlockSpec((1, 1, d), lambda b, s: (b, 0, 0))],
        out_specs=pl.BlockSpec((tm, d), lambda b, s: (b * ns + s, 0)),
        out_shape=jax.ShapeDtypeStruct((t, d), BF16),
        compiler_params=_cparams(("arbitrary", "arbitrary")),
        name="norm_mod",
    )(x2, nw.reshape(1, d), shift.reshape(bsz, 1, d), scale.reshape(bsz, 1, d))


def _block_mid_ref(b, n2):
    rows, width = b.shape
    n = n2 // 2
    if n2 >= 2 * SUBLANES:
        b3 = b.reshape(rows // n2, n2, width)
        return jnp.broadcast_to(b3[:, n - 1:n, :], b3.shape).reshape(rows, width)
    b3 = b.reshape(rows // SUBLANES, SUBLANES, width)
    sub = _iota(b3.shape, 1)
    r3 = jnp.broadcast_to(b3[:, SUBLANES - n2 + n - 1:SUBLANES - n2 + n, :], b3.shape)
    for g in range(SUBLANES // n2 - 2, -1, -1):
        r3 = jnp.where(sub < (g + 1) * n2, b3[:, g * n2 + n - 1:g * n2 + n, :], r3)
    return r3.reshape(rows, width)


def _hgrn2_kernel(h_ref, w_ref, lb_ref, nw_ref, o_ref, proj_ref, st_ref):
    L = CHUNK
    dk = HG_DK
    n_batch = h_ref.shape[0]
    n_heads = st_ref.shape[0] // n_batch
    width = n_heads * dk

    @pl.when(pl.program_id(1) == 0)
    def _():
        st_ref[...] = jnp.zeros_like(st_ref)

    proj_ref[...] = _dot(h_ref[...].reshape(n_batch * L, h_ref.shape[2]), w_ref[...])
    lb = lb_ref[...]
    f = lb + (1.0 - lb) * _sigmoid(proj_ref[:, width:2 * width])
    logf = jnp.log(f)
    t_i = _iota((L, L), 0)
    s_i = _iota((L, L), 1)
    tril = (t_i >= s_i).astype(BF16)
    b_all = [_dot_sel(tril, logf[bb * L:(bb + 1) * L]) for bb in range(n_batch)]
    eye = t_i == s_i

    levels = []
    n2 = L
    while n2 >= 2:
        n = n2 // 2
        sh = n2.bit_length() - 1
        m = ((t_i >> sh) == (s_i >> sh)) & ((t_i & (n2 - 1)) >= n) & ((s_i & (n2 - 1)) < n)
        levels.append((n2, m))
        n2 = n

    for h, bb in [(h, bb) for h in range(n_heads) for bb in range(n_batch)]:
        sl = slice(h * dk, (h + 1) * dk)
        rows = slice(bb * L, (bb + 1) * L)
        q = proj_ref[rows, sl]
        k = 1.0 - f[rows, sl]
        v = proj_ref[rows, 2 * width + h * dk:2 * width + (h + 1) * dk]
        b = b_all[bb][:, sl]
        st = st_ref[bb * n_heads + h]
        b_last = b[L - 1:L, :]
        o = _dot_nt((q * jnp.exp(b)).astype(BF16), st.astype(BF16))
        a = jnp.where(eye, jnp.sum(q * k, axis=-1, keepdims=True), 0.0)
        qb = q.astype(BF16)
        kb = k.astype(BF16)
        for n2, m in levels:
            e = jnp.exp(-jnp.abs(b - _block_mid_ref(b, n2))).astype(BF16)
            a = jnp.where(m, _dot_nt(qb * e, kb * e), a)
        o = o + _dot(a.astype(BF16), v.astype(BF16))
        ke = k * jnp.exp(b_last - b)
        st_ref[bb * n_heads + h] = st * jnp.exp(b_last) + _dot_tn(v.astype(BF16), ke.astype(BF16))
        g = proj_ref[rows, 3 * width + h * dk:3 * width + (h + 1) * dk]
        ms = jnp.mean(o * o, axis=-1, keepdims=True)
        o_ref[bb, :, sl] = (o * lax.rsqrt(ms + NORM_EPS) * nw_ref[:, sl] * _silu(g)).astype(o_ref.dtype)


def _hgrn2(h, w_hg, lb, hg_norm, bsz, seq):
    t, d = h.shape
    width = lb.shape[0]
    n_heads = width // HG_DK
    ns = seq // CHUNK
    nb = HG_BATCH_ROWS if bsz % HG_BATCH_ROWS == 0 else 1
    out = pl.pallas_call(
        _hgrn2_kernel,
        grid=(bsz // nb, ns),
        in_specs=[pl.BlockSpec((nb, CHUNK, d), lambda b, s: (b, s, 0)),
                  pl.BlockSpec(w_hg.shape, lambda b, s: (0, 0)),
                  pl.BlockSpec((1, width), lambda b, s: (0, 0)),
                  pl.BlockSpec((1, width), lambda b, s: (0, 0))],
        out_specs=pl.BlockSpec((nb, CHUNK, width), lambda b, s: (b, s, 0)),
        out_shape=jax.ShapeDtypeStruct((bsz, seq, width), BF16),
        scratch_shapes=[pltpu.VMEM((nb * CHUNK, w_hg.shape[1]), F32), pltpu.VMEM((nb * n_heads, HG_DK, HG_DK), F32)],
        compiler_params=_cparams(("arbitrary", "arbitrary")),
        name="hgrn2_scan",
    )(h.reshape(bsz, seq, d), w_hg, lb.reshape(1, width), hg_norm.reshape(1, width))
    return out.reshape(t, width)


def _ssd_kernel(h_ref, w_ref, cw_ref, cb_ref, dtb_ref, a_ref, dsk_ref, nw_ref, o_ref,
                proj_ref, carry_ref, xpad_ref, st_ref):
    L = CHUNK
    width = o_ref.shape[2]
    xw = cw_ref.shape[1]
    n_heads = width // SSM_P
    gw = width // SSM_GROUPS
    heads_per_group = n_heads // SSM_GROUPS

    @pl.when(pl.program_id(1) == 0)
    def _():
        carry_ref[...] = jnp.zeros_like(carry_ref)
        st_ref[...] = jnp.zeros_like(st_ref)

    n_batch = h_ref.shape[0]
    proj_ref[...] = _dot(h_ref[...].reshape(n_batch * L, h_ref.shape[2]), w_ref[...])
    t_i = _iota((L, L), 0)
    s_i = _iota((L, L), 1)
    triu = (t_i <= s_i).astype(BF16)
    causal = t_i >= s_i
    diag = t_i == s_i
    lane_lo = _iota((L, 2 * SSM_P), 1) < SSM_P
    bd_mask = (_iota((2 * L, 2 * SSM_P), 0) < L) == (_iota((2 * L, 2 * SSM_P), 1) < SSM_P)

    for bb in range(n_batch):
        rs = slice(bb * L, (bb + 1) * L)
        z_ref = proj_ref.at[rs, 0:width]
        dt_ref = proj_ref.at[rs, width + xw:]
        xraw = proj_ref[rs, width:width + xw]
        xpad_ref[bb, 0:SUBLANES, :] = carry_ref[bb]
        xpad_ref[bb, SUBLANES:SUBLANES + L, :] = xraw
        carry_ref[bb] = xraw[L - SUBLANES:L, :]
        acc = cb_ref[...] + jnp.zeros_like(xraw)
        for j in range(SSM_CONV):
            acc = acc + cw_ref[j:j + 1, :] * xpad_ref[bb, pl.ds(SUBLANES - (SSM_CONV - 1) + j, L), :]
        xc = _silu(acc)
        xs = xc[:, :width]
        bm = xc[:, width:width + SSM_GROUPS * SSM_N]
        cm = xc[:, width + SSM_GROUPS * SSM_N:]

        dt = _softplus(dt_ref[...] + dtb_ref[...])
        da = dt * a_ref[...]
        da_t = da.T
        dt_t = dt.T
        acs_t = _dot_rsel(da_t, triu)

        y_pairs = []
        for g in range(SSM_GROUPS):
            bg = bm[:, g * SSM_N:(g + 1) * SSM_N]
            cg = cm[:, g * SSM_N:(g + 1) * SSM_N]
            cb = _dot_nt(cg.astype(BF16), bg.astype(BF16))
            hg = st_ref[bb, :, g * gw:(g + 1) * gw]
            yoff_g = _dot(cg.astype(BF16), hg.astype(BF16))
            xsc_parts, decay_parts = [], []
            for pr in range(heads_per_group // 2):
                j0 = g * heads_per_group + 2 * pr
                gs, ds, ecol, elast = [], [], [], []
                for j in (j0, j0 + 1):
                    row_b = jnp.broadcast_to(acs_t[j:j + 1, :], (L, L))
                    col_b = row_b.T
                    dt_row = jnp.broadcast_to(dt_t[j:j + 1, :], (L, L))
                    lmat = jnp.exp(jnp.minimum(col_b - row_b, 0.0))
                    gs.append(jnp.where(causal, cb * lmat * dt_row, 0.0))
                    a_last = acs_t[j:j + 1, L - 1:L]
                    ds.append(jnp.where(diag, jnp.exp(a_last - row_b) * dt_row, 0.0))
                    ecol.append(jnp.exp(col_b))
                    elast.append(jnp.exp(a_last))
                lhs = jnp.concatenate([jnp.concatenate(gs, axis=1), jnp.concatenate(ds, axis=1)], axis=0)
                xs_pair = xs[:, j0 * SSM_P:(j0 + 2) * SSM_P]
                bd = jnp.where(bd_mask, jnp.concatenate([xs_pair, xs_pair], axis=0), 0.0)
                res = _dot(lhs.astype(BF16), bd.astype(BF16))
                yoff = yoff_g[:, pr * 2 * SSM_P:(pr + 1) * 2 * SSM_P] * jnp.where(lane_lo, ecol[0], ecol[1])
                y_pairs.append(res[:L] + yoff)
                xsc_parts.append(res[L:])
                decay_parts.append(jnp.where(lane_lo[0:1], elast[0], elast[1]))
            xsc_g = jnp.concatenate(xsc_parts, axis=1)
            decay_g = jnp.concatenate(decay_parts, axis=1)
            st_ref[bb, :, g * gw:(g + 1) * gw] = hg * decay_g + _dot_tn(bg.astype(BF16), xsc_g.astype(BF16))
        y = jnp.concatenate(y_pairs, axis=1) + dsk_ref[...] * xs
        yz = y * _silu(z_ref[...])
        for g in range(SSM_GROUPS):
            seg = yz[:, g * gw:(g + 1) * gw]
            ms = jnp.mean(seg * seg, axis=-1, keepdims=True)
            o_ref[bb, :, g * gw:(g + 1) * gw] = (seg * lax.rsqrt(ms + NORM_EPS)
                                                 * nw_ref[:, g * gw:(g + 1) * gw]).astype(o_ref.dtype)


def _ssd(h, w_ssd, conv_w, conv_b, dt_bias, a_log, d_skip, ssm_norm, bsz, seq):
    t, d = h.shape
    width = ssm_norm.shape[0]
    xw = conv_w.shape[1]
    n_heads = width // SSM_P
    ns = seq // CHUNK
    nb = SSD_BATCH_ROWS if bsz % SSD_BATCH_ROWS == 0 else 1
    pad = lambda v: jnp.zeros((1, LANES), F32).at[0, :n_heads].set(v)
    row = lambda w: pl.BlockSpec((nb, CHUNK, w), lambda b, s: (b, s, 0))
    const = lambda r, w: pl.BlockSpec((r, w), lambda b, s: (0, 0))
    out = pl.pallas_call(
        _ssd_kernel,
        grid=(bsz // nb, ns),
        in_specs=[row(d), const(*w_ssd.shape), const(SSM_CONV, xw), const(1, xw), const(1, LANES),
                  const(1, LANES), const(1, width), const(1, width)],
        out_specs=row(width),
        out_shape=jax.ShapeDtypeStruct((bsz, seq, width), BF16),
        scratch_shapes=[pltpu.VMEM((nb * CHUNK, w_ssd.shape[1]), F32), pltpu.VMEM((nb, SUBLANES, xw), F32),
                        pltpu.VMEM((nb, CHUNK + SUBLANES, xw), F32), pltpu.VMEM((nb, SSM_N, width), F32)],
        compiler_params=_cparams(("arbitrary", "arbitrary")),
        name="ssd_scan",
    )(h.reshape(bsz, seq, d), w_ssd, conv_w, conv_b.reshape(1, xw), pad(dt_bias), pad(-jnp.exp(a_log)),
      jnp.repeat(d_skip, SSM_P).reshape(1, width), ssm_norm.reshape(1, width))
    return out.reshape(t, width)


def _route(probs):
    n_exp = N_GROUPS_MOE * 4
    p = [probs[e:e + 1, :] for e in range(n_exp)]
    gs = []
    for g in range(N_GROUPS_MOE):
        a, b, c, d = p[4 * g:4 * g + 4]
        gs.append(jnp.maximum(jnp.maximum(jnp.maximum(a + b, a + c), jnp.maximum(a + d, b + c)),
                              jnp.maximum(b + d, c + d)))
    best = jnp.zeros_like(gs[0]).astype(I32)
    bs = gs[0]
    for g in range(1, N_GROUPS_MOE):
        upd = gs[g] > bs
        best = jnp.where(upd, g, best)
        bs = jnp.where(upd, gs[g], bs)
    q = [jnp.where(best == 0, p[i], jnp.where(best == 1, p[4 + i], jnp.where(best == 2, p[8 + i], p[12 + i])))
         for i in range(4)]
    i1 = jnp.zeros_like(best)
    v1 = q[0]
    for i in range(1, 4):
        upd = q[i] > v1
        i1 = jnp.where(upd, i, i1)
        v1 = jnp.where(upd, q[i], v1)
    i2 = jnp.zeros_like(best)
    v2 = jnp.full_like(v1, -1.0)
    for i in range(4):
        upd = (i1 != i) & (q[i] > v2)
        i2 = jnp.where(upd, i, i2)
        v2 = jnp.where(upd, q[i], v2)
    den = v1 + v2
    return best * 4 + i1, best * 4 + i2, v1 / den, v2 / den


def _head_stats_expand(stack, seg_ref):
    pw = seg_ref.shape[0]
    seg = seg_ref[...]
    return jnp.concatenate([_dot_rsel(stack[:, p * pw:(p + 1) * pw], seg) for p in range(stack.shape[1] // pw)], axis=1)


def _epilogue_kernel(*refs, n_in, rwkv):
    ins = refs[:n_in]
    pos = n_in
    w_refs = refs[pos:pos + n_in]
    pos += n_in
    if rwkv:
        g_ref, bonus_ref, lnw_ref, lnb_ref, seg_ref = refs[pos:pos + 5]
        pos += 5
    x_ref, gate_ref, nw_ref, sh_ref, sc_ref, rw_ref, rb_ref = refs[pos:pos + 7]
    pos += 7
    xo_ref, h_ref, ti_ref, tw_ref = refs[pos:pos + 4]

    tm = ROW_TM
    rw_hi, rw_lo = _split(rw_ref[...])
    for sub in range(x_ref.shape[0] // tm):
        rs = pl.ds(sub * tm, tm)
        if rwkv:
            o = ins[0][rs, :]
            inv_n = 1.0 / RW_N
            st = _head_stats_expand(jnp.concatenate([o, o * o], axis=0), seg_ref) * inv_n
            mean = st[:tm]
            var = jnp.maximum(st[tm:] - mean * mean, 0.0)
            o = (o - mean) * lax.rsqrt(var + RW_GN_EPS) * lnw_ref[...] + lnb_ref[...]
            o = (o + bonus_ref[rs, :].astype(F32)) * g_ref[rs, :].astype(F32)
            y = _dot(o.astype(BF16), w_refs[0][...])
        else:
            y = _dot(ins[0][rs, :], w_refs[0][...])
            for a_ref, w_ref in zip(ins[1:], w_refs[1:]):
                y = y + _dot(a_ref[rs, :], w_ref[...])
        x_new = x_ref[rs, :] + gate_ref[0] * y
        xo_ref[rs, :] = x_new
        h = _rms_mod(x_new, nw_ref[...], sh_ref[0], sc_ref[0])
        _store_rows(h_ref.at[pl.ds(sub * tm * ROW_PITCH, tm * ROW_PITCH)], h)
        h_hi, h_lo = _split(h)
        logits = _dot_nt(rw_hi, h_hi) + _dot_nt(rw_hi, h_lo) + _dot_nt(rw_lo, h_hi) + rb_ref[...]
        mx = jnp.max(logits, axis=0, keepdims=True)
        ex = jnp.exp(logits - mx)
        probs = ex / jnp.sum(ex, axis=0, keepdims=True)
        e1, e2, w1, w2 = _route(probs)
        zi = jnp.zeros((SUBLANES - TOP_K, tm), I32)
        ti_ref[:, sub * tm:(sub + 1) * tm] = jnp.concatenate([e1, e2, zi], axis=0)
        tw_ref[:, sub * tm:(sub + 1) * tm] = jnp.concatenate([w1, w2, zi.astype(F32)], axis=0)


def _epilogue(ins, ws, x2, gate, nw, shift, scale, router_w, router_b, bsz, seq, rwkv_extra=None):
    t, d = x2.shape
    tm = EPILOGUE_SUBTILES * ROW_TM
    ns = seq // tm
    n_exp = router_w.shape[1]
    rw_t = jnp.zeros((LANES, d), F32).at[:n_exp].set(router_w.T)
    rb = jnp.full((LANES, 1), -1e30, F32).at[:n_exp, 0].set(router_b)
    row = lambda w: pl.BlockSpec((tm, w), lambda b, s: (b * ns + s, 0))
    const = lambda a: pl.BlockSpec(a.shape, lambda b, s: (0,) * a.ndim)
    per_b = pl.BlockSpec((1, 1, d), lambda b, s: (b, 0, 0))
    args = list(ins) + list(ws)
    specs = [row(a.shape[1]) for a in ins] + [const(w) for w in ws]
    if rwkv_extra is not None:
        g, bonus, lnw, lnb, seg = rwkv_extra
        args += [g, bonus, lnw.reshape(1, d), lnb.reshape(1, d), seg]
        specs += [row(d), row(d), pl.BlockSpec((1, d), lambda b, s: (0, 0)), pl.BlockSpec((1, d), lambda b, s: (0, 0)),
                  const(seg)]
    args += [x2, gate.reshape(bsz, 1, d), nw.reshape(1, d), shift.reshape(bsz, 1, d), scale.reshape(bsz, 1, d), rw_t, rb]
    specs += [row(d), per_b, pl.BlockSpec((1, d), lambda b, s: (0, 0)), per_b, per_b, const(rw_t), const(rb)]
    lane_row = pl.BlockSpec((SUBLANES, tm), lambda b, s: (0, b * ns + s))
    return pl.pallas_call(
        functools.partial(_epilogue_kernel, n_in=len(ins), rwkv=rwkv_extra is not None),
        grid=(bsz, ns),
        in_specs=specs,
        out_specs=[row(d), pl.BlockSpec((tm * ROW_PITCH, LANES), lambda b, s: (b * ns + s, 0)), lane_row, lane_row],
        out_shape=[jax.ShapeDtypeStruct((t, d), F32), jax.ShapeDtypeStruct((t * ROW_PITCH, LANES), F32),
                   jax.ShapeDtypeStruct((SUBLANES, t), I32), jax.ShapeDtypeStruct((SUBLANES, t), F32)],
        compiler_params=_cparams(("arbitrary", "arbitrary")),
        name="mixer_epilogue",
    )(*args)


def _route_plan(top_i, n_exp, tm):
    t = top_i.shape[1]
    n_pairs = TOP_K * t
    n_tiles = n_pairs // tm + n_exp
    e_flat = top_i[:TOP_K].reshape(-1)
    onehot = (e_flat[:, None] == jnp.arange(n_exp, dtype=I32)[None, :]).astype(I32)
    csum = jnp.cumsum(onehot, axis=0)
    counts = csum[-1]
    rank = jnp.sum((csum - 1) * onehot, axis=1)
    padded = ((counts + tm - 1) // tm) * tm
    ends = jnp.cumsum(padded)
    pos = (ends - padded)[e_flat] + rank
    tile_start = jnp.arange(n_tiles, dtype=I32) * tm
    tile_exp = jnp.minimum(jnp.sum((tile_start[:, None] >= ends[None, :]).astype(I32), axis=1), n_exp - 1)
    n_used = (ends[-1] // tm).reshape(1).astype(I32)
    gap_lo = jnp.concatenate([ends - padded + counts, ends[-1:]]).astype(I32)
    gap_hi = jnp.concatenate([ends, jnp.full((1,), n_tiles * tm, I32)]).astype(I32)
    return pos.astype(I32), tile_exp.astype(I32), n_used, gap_lo, gap_hi


def _dispatch_kernel(pos_ref, glo_ref, ghi_ref, h_ref, xs_hbm, zbuf, sem, zsem):
    tm = DISPATCH_TM
    n_tok = pos_ref.shape[0] // TOP_K
    i = pl.program_id(0)

    @pl.when(i == 0)
    def _():
        zbuf[...] = jnp.zeros_like(zbuf)

        def chunk_copy(row):
            return pltpu.make_async_copy(zbuf, xs_hbm.at[pl.ds(row * ROW_PITCH, ZERO_CHUNK * ROW_PITCH)], zsem)

        def row_copy(row):
            return pltpu.make_async_copy(zbuf.at[pl.ds(0, ROW_PITCH)], xs_hbm.at[pl.ds(row * ROW_PITCH, ROW_PITCH)], zsem)

        for e in range(glo_ref.shape[0]):
            lo = glo_ref[e]
            hi = ghi_ref[e]
            n_chunks = lax.div(hi - lo, ZERO_CHUNK)
            mid = lo + n_chunks * ZERO_CHUNK

            def start_chunk(c, carry):
                chunk_copy(lo + c * ZERO_CHUNK).start()
                return carry

            def start_row(r, carry):
                row_copy(r).start()
                return carry

            def wait_chunk(c, carry):
                chunk_copy(lo).wait()
                return carry

            def wait_row(r, carry):
                row_copy(lo).wait()
                return carry

            lax.fori_loop(0, n_chunks, start_chunk, 0)
            lax.fori_loop(mid, hi, start_row, 0)
            lax.fori_loop(0, n_chunks, wait_chunk, 0)
            lax.fori_loop(mid, hi, wait_row, 0)

    def body(r8, c):
        for u in range(GATHER_UNROLL):
            r = r8 * GATHER_UNROLL + u
            for k in range(TOP_K):
                p = pos_ref[k * n_tok + i * tm + r]
                pltpu.make_async_copy(h_ref.at[pl.ds(r * ROW_PITCH, ROW_PITCH)],
                                      xs_hbm.at[pl.ds(p, ROW_PITCH)], sem).start(priority=k)
        return c
    lax.fori_loop(0, tm // GATHER_UNROLL, body, 0)
    for k in range(TOP_K):
        pltpu.make_async_copy(h_ref, xs_hbm.at[pl.ds(0, tm * ROW_PITCH)], sem).wait()


def _dispatch(h, pos, gap_lo, gap_hi, n_rows):
    tm = DISPATCH_TM
    t = h.shape[0] // ROW_PITCH
    grid_spec = pltpu.PrefetchScalarGridSpec(
        num_scalar_prefetch=3,
        grid=(t // tm,),
        in_specs=[pl.BlockSpec((tm * ROW_PITCH, LANES), lambda i, p, lo, hi: (i, 0))],
        out_specs=pl.BlockSpec(memory_space=pl.ANY),
        scratch_shapes=[pltpu.VMEM((ZERO_CHUNK * ROW_PITCH, LANES), F32), pltpu.SemaphoreType.DMA(()),
                        pltpu.SemaphoreType.DMA(())],
    )
    return pl.pallas_call(
        _dispatch_kernel,
        grid_spec=grid_spec,
        out_shape=jax.ShapeDtypeStruct((n_rows * ROW_PITCH, LANES), F32),
        compiler_params=_cparams(("arbitrary",)),
        name="moe_dispatch",
    )(pos, gap_lo, gap_hi, h)


def _moe_kernel(texp_ref, nused_ref, x_ref, w1_ref, w3_ref, w2_ref, y_ref, wb1, wb3, wb2):
    tm = MOE_TM
    d = wb1.shape[0]
    i = pl.program_id(0)
    n_used = nused_ref[0]

    @pl.when((i == 0) | (texp_ref[i] != texp_ref[jnp.maximum(i - 1, 0)]))
    def _():
        wb1[...] = w1_ref[0, 0].astype(BF16)
        wb3[...] = w3_ref[0, 0].astype(BF16)
        wb2[...] = w2_ref[0, 0].astype(BF16)

    @pl.when(i < n_used)
    def _():
        x = _load_rows(x_ref, tm, d).astype(BF16)
        a = _dot(x, wb1[...])
        b = _dot(x, wb3[...])
        hid = (_silu(a) * b).astype(BF16)
        _store_rows(y_ref, _dot(hid, wb2[...]))

    @pl.when(i >= n_used)
    def _():
        y_ref[...] = jnp.zeros_like(y_ref)


def _moe(xs, w1, w3, w2, layer, tile_exp, n_used):
    _, n_exp, d, dff = w1.shape
    tm = MOE_TM
    n_tiles = tile_exp.shape[0]
    rows = pl.BlockSpec((tm * ROW_PITCH, LANES), lambda i, te, nu: (i, 0))
    used_rows = pl.BlockSpec((tm * ROW_PITCH, LANES), lambda i, te, nu: (jnp.minimum(i, nu[0] - 1), 0))
    grid_spec = pltpu.PrefetchScalarGridSpec(
        num_scalar_prefetch=2,
        grid=(n_tiles,),
        in_specs=[used_rows,
                  pl.BlockSpec((1, 1, d, dff), lambda i, te, nu: (layer, te[i], 0, 0)),
                  pl.BlockSpec((1, 1, d, dff), lambda i, te, nu: (layer, te[i], 0, 0)),
                  pl.BlockSpec((1, 1, dff, d), lambda i, te, nu: (layer, te[i], 0, 0))],
        out_specs=rows,
        scratch_shapes=[pltpu.VMEM((d, dff), BF16), pltpu.VMEM((d, dff), BF16), pltpu.VMEM((dff, d), BF16)],
    )
    return pl.pallas_call(
        _moe_kernel,
        grid_spec=grid_spec,
        out_shape=jax.ShapeDtypeStruct((n_tiles * tm * ROW_PITCH, LANES), F32),
        compiler_params=_cparams(("arbitrary",)),
        name="moe_experts",
    )(tile_exp, n_used, xs, w1, w3, w2)


def _combine_kernel(pos_ref, y_hbm, x_ref, tw_ref, gate_ref, nw_ref, sh_ref, sc_ref, *out_and_scratch, final, n_tok):
    if final:
        o_ref, ybuf, sem = out_and_scratch
    else:
        xo_ref, h_ref, ybuf, sem = out_and_scratch
    tm, d = x_ref.shape
    n_sub = d // LANES
    i = pl.program_id(0)
    n_steps = pl.num_programs(0)
    slot = i % 2

    def start_gather(tile, sl):
        def body(r8, c):
            for u in range(GATHER_UNROLL):
                r = r8 * GATHER_UNROLL + u
                for k in range(TOP_K):
                    src = pos_ref[k * n_tok + tile * tm + r]
                    pltpu.make_async_copy(y_hbm.at[pl.ds(src, n_sub)], ybuf.at[sl, k, pl.ds(r * ROW_PITCH, n_sub)],
                                          sem.at[sl]).start(priority=k)
            return c
        lax.fori_loop(0, tm // GATHER_UNROLL, body, 0)

    def wait_gather(sl):
        for k in range(TOP_K):
            pltpu.make_async_copy(y_hbm.at[pl.ds(0, tm * n_sub)], ybuf.at[sl, k, pl.ds(0, tm * n_sub)], sem.at[sl]).wait()

    @pl.when(i == 0)
    def _():
        start_gather(0, 0)

    @pl.when(i + 1 < n_steps)
    def _():
        start_gather(i + 1, 1 - slot)

    wait_gather(slot)
    tw = tw_ref[...]
    moe = tw[:, 0:1] * _load_rows(ybuf.at[slot, 0], tm, d) + tw[:, 1:2] * _load_rows(ybuf.at[slot, 1], tm, d)
    x_new = x_ref[...] + gate_ref[0] * moe
    if final:
        o_ref[...] = x_new * lax.rsqrt(jnp.mean(x_new * x_new, axis=-1, keepdims=True) + NORM_EPS) * nw_ref[...]
    else:
        xo_ref[...] = x_new
        h_ref[...] = _rms_mod(x_new, nw_ref[...], sh_ref[0], sc_ref[0]).astype(h_ref.dtype)


def _combine(pos, y_sorted, x2, top_w, gate, nw, shift, scale, bsz, seq, final):
    t, d = x2.shape
    tm = ROW_TM
    ns = seq // tm
    tw = top_w.T
    row = pl.BlockSpec((tm, d), lambda i, p: (i, 0))
    per_b = pl.BlockSpec((1, 1, d), lambda i, p: (i // ns, 0, 0))
    grid_spec = pltpu.PrefetchScalarGridSpec(
        num_scalar_prefetch=1,
        grid=(t // tm,),
        in_specs=[pl.BlockSpec(memory_space=pl.ANY), row,
                  pl.BlockSpec((tm, SUBLANES), lambda i, p: (i, 0)),
                  per_b, pl.BlockSpec((1, d), lambda i, p: (0, 0)), per_b, per_b],
        out_specs=row if final else [row, row],
        scratch_shapes=[pltpu.VMEM((2, TOP_K, tm * ROW_PITCH, LANES), F32), pltpu.SemaphoreType.DMA((2,))],
    )
    out_shape = (jax.ShapeDtypeStruct((t, d), F32) if final else
                 [jax.ShapeDtypeStruct((t, d), F32), jax.ShapeDtypeStruct((t, d), BF16)])
    return pl.pallas_call(
        functools.partial(_combine_kernel, final=final, n_tok=t),
        grid_spec=grid_spec,
        out_shape=out_shape,
        compiler_params=_cparams(("arbitrary",)),
        name="moe_combine",
    )(pos, y_sorted, x2, tw, gate.reshape(bsz, 1, d), nw.reshape(1, d), shift.reshape(bsz, 1, d), scale.reshape(bsz, 1, d))


def _rwkv_prep_kernel(h_ref, mu_ref, wr_ref, wk_ref, wv_ref, dec0_ref, dec1_ref, dec2_ref, a0_ref, a1_ref, a2_ref,
                      g1_ref, g2_ref, kk_ref, ka_ref, rk_ref, seg_ref,
                      r_out, lw_out, k_out, v_out, kkn_out, a_out, g_out, bonus_out, carry_ref, hp_ref):
    tm = h_ref.shape[0]

    @pl.when(pl.program_id(1) == 0)
    def _():
        carry_ref[...] = jnp.zeros_like(carry_ref)

    hb = h_ref[...]
    h = hb.astype(F32)
    hp_ref[0:SUBLANES, :] = carry_ref[...]
    hp_ref[SUBLANES:SUBLANES + tm, :] = h
    carry_ref[...] = h[tm - SUBLANES:tm, :]
    xxb = (hp_ref[pl.ds(SUBLANES - 1, tm), :] - h).astype(BF16)
    mix = lambda i: hb + xxb * mu_ref[i:i + 1, :].astype(BF16)
    r = _dot(mix(0), wr_ref[...])
    k = _dot(mix(1), wk_ref[...])
    v = _dot(mix(2), wv_ref[...])
    wl = dec0_ref[...] + _dot(jnp.tanh(_dot(mix(3), dec1_ref[...])).astype(BF16), dec2_ref[...])
    lw = -RW_DECAY_SCALE * _sigmoid(wl)
    a = _sigmoid(a0_ref[...] + _dot(_dot(mix(4), a1_ref[...]).astype(BF16), a2_ref[...]))
    g = _dot(_sigmoid(_dot(mix(5), g1_ref[...])).astype(BF16), g2_ref[...])
    kk = k * kk_ref[...]
    k_h = k * (1.0 + (a - 1.0) * ka_ref[...])
    st = _head_stats_expand(jnp.concatenate([kk * kk, r * k_h * rk_ref[...]], axis=0), seg_ref)
    kkn = kk * lax.rsqrt(jnp.maximum(st[:tm], 1e-24))
    r_out[...] = r.astype(r_out.dtype)
    lw_out[...] = lw
    k_out[...] = k_h.astype(k_out.dtype)
    v_out[...] = v.astype(v_out.dtype)
    kkn_out[...] = kkn.astype(kkn_out.dtype)
    a_out[...] = a.astype(a_out.dtype)
    g_out[...] = g.astype(g_out.dtype)
    bonus_out[...] = (st[tm:] * v).astype(bonus_out.dtype)


def _rwkv_prep(h, mu, w_rkv, dec0, dec1, dec2, a0, a1, a2, g1, g2, k_k, k_a, r_k, seg, bsz, seq):
    t, d = h.shape
    tm = ROW_TM
    ns = seq // tm
    padc = lambda w: jnp.zeros((d, LANES), F32).at[:, :w.shape[1]].set(w).astype(BF16)
    padr = lambda w: jnp.zeros((LANES, d), F32).at[:w.shape[0]].set(w).astype(BF16)
    vec = lambda v: v.reshape(1, d)
    args = [h, mu, w_rkv[0].astype(BF16), w_rkv[1].astype(BF16), w_rkv[2].astype(BF16), vec(dec0), padc(dec1), padr(dec2),
            vec(a0), padc(a1), padr(a2), padc(g1), padr(g2), vec(k_k), vec(k_a), vec(r_k), seg]
    row = pl.BlockSpec((tm, d), lambda b, s: (b * ns + s, 0))
    const = lambda a: pl.BlockSpec(a.shape, lambda b, s: (0,) * a.ndim)
    outs = [BF16, F32, BF16, BF16, BF16, BF16, BF16, BF16]
    return pl.pallas_call(
        _rwkv_prep_kernel,
        grid=(bsz, ns),
        in_specs=[row] + [const(a) for a in args[1:]],
        out_specs=[row] * len(outs),
        out_shape=[jax.ShapeDtypeStruct((t, d), dt) for dt in outs],
        scratch_shapes=[pltpu.VMEM((SUBLANES, d), F32), pltpu.VMEM((tm + SUBLANES, d), F32)],
        compiler_params=_cparams(("arbitrary", "arbitrary")),
        name="rwkv_prep",
    )(*args)


def _rwkv_scan_kernel(r_ref, lw_ref, k_ref, v_ref, kk_ref, a_ref, y_ref, st_ref):
    L = RW_CHUNK
    pw = RW_PACK * RW_N
    n_packs = st_ref.shape[0]
    sh = RW_N.bit_length() - 1

    @pl.when(pl.program_id(1) == 0)
    def _():
        st_ref[...] = jnp.zeros_like(st_ref)

    t_i = _iota((L, L), 0)
    s_i = _iota((L, L), 1)
    tril = (t_i >= s_i).astype(BF16)
    n_batch = lw_ref.shape[0]
    per_batch = n_packs // n_batch
    wc_all = [_dot_sel(tril, lw_ref[bb]) for bb in range(n_batch)]
    lane_head = _iota((L, pw), 1) >> sh
    s_loc = _iota((L, pw), 1) & (RW_N - 1)
    t_loc = _iota((L, pw), 0)
    strict = s_loc < t_loc
    incl = s_loc <= t_loc
    bd_mask = (_iota((pw, pw), 0) >> sh) == (_iota((pw, pw), 1) >> sh)

    def bdiag(x):
        return jnp.where(bd_mask, jnp.concatenate([x] * RW_PACK, axis=0), 0.0).astype(BF16)

    packs = range(n_packs)
    sls = [(p // per_batch, slice((p % per_batch) * pw, (p % per_batch + 1) * pw)) for p in packs]
    pr, vs, sts, kkas, ks, wcs = [], [], [], [], [], []
    for p in packs:
        bb, sl = sls[p]
        r = r_ref[bb, :, sl].astype(F32)
        lw = lw_ref[bb, :, sl]
        k = k_ref[bb, :, sl].astype(F32)
        kk = kk_ref[bb, :, sl].astype(F32)
        a = a_ref[bb, :, sl].astype(F32)
        wc = wc_all[bb][:, sl]
        e_inv = jnp.exp(-wc)
        kka = kk * a
        al = -kk * jnp.exp(wc - lw)
        rb = r * jnp.exp(wc)
        bt = kka * e_inv
        kt = k * e_inv
        lhs = jnp.concatenate([al, rb], axis=0).astype(BF16)
        rows = [jnp.where(lane_head == hh, x, 0.0) for x in (bt, kt) for hh in range(RW_PACK)]
        st = st_ref[p]
        m = jnp.concatenate(rows + [st], axis=0).astype(BF16)
        pr.append(_dot_nt(lhs, m))
        vs.append(v_ref[bb, :, sl].astype(F32))
        sts.append(st)
        kkas.append(kka)
        ks.append(k)
        wcs.append(wc)
    bd_vs = [bdiag(vs[p]) for p in packs]
    us = [pr[p][:L, 2 * pw:] + _dot(jnp.where(strict, pr[p][:L, pw:2 * pw], 0.0).astype(BF16), bd_vs[p]) for p in packs]
    nmats = [jnp.where(strict, pr[p][:L, 0:pw], 0.0) for p in packs]
    n_steps = L.bit_length() - 1
    for it in range(n_steps):
        us = [us[p] + _dot(nmats[p].astype(BF16), bdiag(us[p])) for p in packs]
        if it + 1 < n_steps:
            nmats = [_dot(nmats[p].astype(BF16), bdiag(nmats[p])) for p in packs]
    for p in packs:
        a_rb = jnp.where(incl, pr[p][L:, 0:pw], 0.0)
        a_rk = jnp.where(incl, pr[p][L:, pw:2 * pw], 0.0)
        bb, sl = sls[p]
        y_ref[bb, :, sl] = pr[p][L:, 2 * pw:] + _dot(jnp.concatenate([a_rb, a_rk], axis=1).astype(BF16),
                                                     jnp.concatenate([bdiag(us[p]), bd_vs[p]], axis=0))
    for p in packs:
        w_last = wcs[p][L - 1:L, :]
        e_last = jnp.exp(w_last - wcs[p])
        upd = _dot_tn(jnp.concatenate([us[p], vs[p]], axis=0).astype(BF16),
                      jnp.concatenate([kkas[p] * e_last, ks[p] * e_last], axis=0).astype(BF16))
        st_ref[p] = jnp.where(bd_mask, sts[p] * jnp.exp(w_last) + upd, 0.0)


def _rwkv_scan(r, lw, k, v, kk, a, bsz, seq):
    t, d = r.shape
    ns = seq // RW_CHUNK
    pw = RW_PACK * RW_N
    nb = RW_BATCH_ROWS if bsz % RW_BATCH_ROWS == 0 else 1
    row = pl.BlockSpec((nb, RW_CHUNK, d), lambda b, s: (b, s, 0))
    y = pl.pallas_call(
        _rwkv_scan_kernel,
        grid=(bsz // nb, ns),
        in_specs=[row] * 6,
        out_specs=row,
        out_shape=jax.ShapeDtypeStruct((bsz, seq, d), F32),
        scratch_shapes=[pltpu.VMEM((nb * (d // pw), pw, pw), F32)],
        compiler_params=_cparams(("arbitrary", "arbitrary")),
        name="rwkv_scan",
    )(*[z.reshape(bsz, seq, d) for z in (r, lw, k, v, kk, a)])
    return y.reshape(t, d)


def _moe_block(h, top_i, top_w, x2, gate, w1, w3, w2, layer, nw, shift, scale, bsz, seq, final):
    n_exp = w1.shape[1]
    pos, tile_exp, n_used, gap_lo, gap_hi = _route_plan(top_i, n_exp, MOE_TM)
    pos = pos * ROW_PITCH
    x_sorted = _dispatch(h, pos, gap_lo, gap_hi, tile_exp.shape[0] * MOE_TM)
    y_sorted = _moe(x_sorted, w1, w3, w2, layer, tile_exp, n_used)
    return _combine(pos, y_sorted, x2, top_w, gate, nw, shift, scale, bsz, seq, final)


def kernel(x, c, mod_w, mod_b, norm_w, hg_lb_logits, ev_w_in, ev_hg_norm, ev_conv_w, ev_conv_b, ev_dt_bias, ev_a_log, ev_d_skip, ev_ssm_norm, ev_w_out, od_mu, od_w_rkv, od_w_dec0, od_w_dec1, od_w_dec2, od_a0, od_a1, od_a2, od_g1, od_g2, od_k_k, od_k_a, od_r_k, od_ln_w, od_ln_b, od_w_o, router_w, router_b, moe_w1, moe_w3, moe_w2, final_norm_w):
    bsz, seq, d = x.shape
    depth = mod_w.shape[0]
    t = bsz * seq
    x2 = x.reshape(t, d)
    mod = _adaln_mod(c, mod_w, mod_b)
    gamma = jax.nn.softmax(hg_lb_logits.astype(F32), axis=0)
    lower_bounds = jnp.cumsum(gamma, axis=0) - gamma[0]
    head_of_lane = jnp.arange(RW_PACK * RW_N, dtype=I32) // RW_N
    seg = (head_of_lane[:, None] == head_of_lane[None, :]).astype(BF16)

    h = None
    out = None
    for l in range(depth):
        sh_m, sc_m, gt_m, sh_f, sc_f, gt_f = [mod[l, :, i * d:(i + 1) * d] for i in range(6)]
        j = l // 2
        if h is None:
            h = _norm_mod(x2, norm_w[l, 0], sh_m, sc_m, bsz, seq)
        if l % 2 == 0:
            w_in = ev_w_in[j]
            hgw = ev_hg_norm.shape[1]
            sw = ev_ssm_norm.shape[1]
            xbw = ev_conv_w.shape[2]
            nh = ev_dt_bias.shape[1]
            c0 = 4 * hgw
            w_hg = w_in[:, :c0].astype(BF16)
            w_ssd = jnp.zeros((d, sw + xbw + LANES), F32).at[:, :sw + xbw + nh].set(w_in[:, c0:]).astype(BF16)
            o_a = _hgrn2(h, w_hg, lower_bounds[l + 1], ev_hg_norm[j], bsz, seq)
            o_b = _ssd(h, w_ssd, ev_conv_w[j], ev_conv_b[j], ev_dt_bias[j], ev_a_log[j], ev_d_skip[j],
                       ev_ssm_norm[j], bsz, seq)
            w_out = ev_w_out[j].astype(BF16)
            x2, hf, top_i, top_w = _epilogue([o_a, o_b], [w_out[:hgw], w_out[hgw:]], x2, gt_m, norm_w[l, 1], sh_f, sc_f,
                                             router_w, router_b, bsz, seq)
        else:
            r, lw, k, v, kk, a, g, bonus = _rwkv_prep(h, od_mu[j], od_w_rkv[j], od_w_dec0[j], od_w_dec1[j], od_w_dec2[j],
                                                      od_a0[j], od_a1[j], od_a2[j], od_g1[j], od_g2[j], od_k_k[j],
                                                      od_k_a[j], od_r_k[j].reshape(-1), seg, bsz, seq)
            y = _rwkv_scan(r, lw, k, v, kk, a, bsz, seq)
            x2, hf, top_i, top_w = _epilogue([y], [od_w_o[j].astype(BF16)], x2, gt_m, norm_w[l, 1], sh_f, sc_f,
                                             router_w, router_b, bsz, seq,
                                             rwkv_extra=(g, bonus, od_ln_w[j], od_ln_b[j], seg))
        final = l == depth - 1
        if final:
            nw_next, sh_next, sc_next = final_norm_w, sh_f, sc_f
        else:
            nxt = [mod[l + 1, :, i * d:(i + 1) * d] for i in range(2)]
            nw_next, sh_next, sc_next = norm_w[l + 1, 0], nxt[0], nxt[1]
        res = _moe_block(hf, top_i, top_w, x2, gt_f, moe_w1, moe_w3, moe_w2, l, nw_next, sh_next, sc_next,
                         bsz, seq, final)
        if final:
            out = res
        else:
            x2, h = res
    return out.reshape(bsz, seq, d)
```

```python
import functools

import jax
import jax.numpy as jnp
from jax import lax
from jax.experimental import pallas as pl
from jax.experimental.pallas import tpu as pltpu

F32 = jnp.float32
BF16 = jnp.bfloat16
I32 = jnp.int32

NORM_EPS = 1e-6
RW_GN_EPS = 64e-5
RW_DECAY_SCALE = 0.6065306597126334
LANES = 128
SUBLANES = 8
VMEM_LIMIT = 56 * 1024 * 1024

HG_DK = 128
SSM_P = 64
SSM_N = 128
SSM_GROUPS = 2
SSM_CONV = 4
RW_N = 64
N_GROUPS_MOE = 4
TOP_K = 2

CHUNK = 128
HG_BATCH_ROWS = 4
SSD_BATCH_ROWS = 4
RW_CHUNK = 64
RW_PACK = 4
RW_BATCH_ROWS = 2
MOE_TM = 512
ROW_TM = 256
NORM_TM = 512
EPILOGUE_SUBTILES = 4
DISPATCH_TM = 512
ZERO_CHUNK = 64
GATHER_UNROLL = 16
ROW_PITCH = 9


def _cparams(sem):
    return pltpu.CompilerParams(dimension_semantics=sem, vmem_limit_bytes=VMEM_LIMIT)


def _dot(a, b):
    return lax.dot_general(a, b, (((1,), (0,)), ((), ())), preferred_element_type=F32)


def _dot_nt(a, b):
    return lax.dot_general(a, b, (((1,), (1,)), ((), ())), preferred_element_type=F32)


def _dot_tn(a, b):
    return lax.dot_general(a, b, (((0,), (0,)), ((), ())), preferred_element_type=F32)


def _split(x):
    hi = x.astype(BF16)
    return hi, (x - hi.astype(F32)).astype(BF16)


def _dot_sel(sel, x):
    hi, lo = _split(x)
    return _dot(sel, hi) + _dot(sel, lo)


def _dot_rsel(x, sel):
    hi, lo = _split(x)
    return _dot(hi, sel) + _dot(lo, sel)


def _sigmoid(x):
    return 1.0 / (1.0 + jnp.exp(-x))


def _silu(x):
    return x * _sigmoid(x)


def _softplus(x):
    return jnp.maximum(x, 0.0) + jnp.log(1.0 + jnp.exp(-jnp.abs(x)))


def _iota(shape, dim):
    return lax.broadcasted_iota(I32, shape, dim)


def _store_rows(ref, val):
    tm, width = val.shape
    for j in range(width // LANES):
        ref[pl.ds(j, tm, stride=ROW_PITCH), :] = val[:, j * LANES:(j + 1) * LANES]
    for j in range(width // LANES, ROW_PITCH):
        ref[pl.ds(j, tm, stride=ROW_PITCH), :] = jnp.zeros((tm, LANES), ref.dtype)


def _load_rows(ref, tm, width):
    return jnp.concatenate([ref[pl.ds(j, tm, stride=ROW_PITCH), :] for j in range(width // LANES)], axis=1)


def _mod_kernel(c_ref, w_ref, b_ref, o_ref):
    c = c_ref[...]
    o_ref[0] = _dot(_silu(c).astype(BF16), w_ref[0].astype(BF16)) + b_ref[0]


def _adaln_mod(c, mod_w, mod_b):
    depth, d, width = mod_w.shape
    bsz = c.shape[0]
    c_pad = jnp.zeros((SUBLANES, d), F32).at[:bsz].set(c)
    tn = 1536
    out = pl.pallas_call(
        _mod_kernel,
        grid=(depth, width // tn),
        in_specs=[pl.BlockSpec((SUBLANES, d), lambda l, j: (0, 0)),
                  pl.BlockSpec((1, d, tn), lambda l, j: (l, 0, j)),
                  pl.BlockSpec((1, 1, tn), lambda l, j: (l, 0, j))],
        out_specs=pl.BlockSpec((1, SUBLANES, tn), lambda l, j: (l, 0, j)),
        out_shape=jax.ShapeDtypeStruct((depth, SUBLANES, width), F32),
        compiler_params=_cparams(("arbitrary", "arbitrary")),
        name="adaln_mod",
    )(c_pad, mod_w, mod_b.reshape(depth, 1, width))
    return out[:, :bsz]


def _rms_mod(x, nw, shift, scale):
    y = x * lax.rsqrt(jnp.mean(x * x, axis=-1, keepdims=True) + NORM_EPS) * nw
    return y * (1.0 + scale) + shift


def _normmod_kernel(x_ref, nw_ref, sh_ref, sc_ref, h_ref):
    h_ref[...] = _rms_mod(x_ref[...], nw_ref[...], sh_ref[0], sc_ref[0]).astype(h_ref.dtype)


def _norm_mod(x2, nw, shift, scale, bsz, seq):
    t, d = x2.shape
    tm = NORM_TM
    ns = seq // tm
    return pl.pallas_call(
        _normmod_kernel,
        grid=(bsz, ns),
        in_specs=[pl.BlockSpec((tm, d), lambda b, s: (b * ns + s, 0)),
                  pl.BlockSpec((1, d), lambda b, s: (0, 0)),
                  pl.BlockSpec((1, 1, d), lambda b, s: (b, 0, 0)),
                  pl.BlockSpec((1, 1, d), lambda b, s: (b, 0, 0))],
        out_specs=pl.BlockSpec((tm, d), lambda b, s: (b * ns + s, 0)),
        out_shape=jax.ShapeDtypeStruct((t, d), BF16),
        compiler_params=_cparams(("arbitrary", "arbitrary")),
        name="norm_mod",
    )(x2, nw.reshape(1, d), shift.reshape(bsz, 1, d), scale.reshape(bsz, 1, d))


def _block_mid_ref(b, n2):
    rows, width = b.shape
    n = n2 // 2
    if n2 >= 2 * SUBLANES:
        b3 = b.reshape(rows // n2, n2, width)
        return jnp.broadcast_to(b3[:, n - 1:n, :], b3.shape).reshape(rows, width)
    b3 = b.reshape(rows // SUBLANES, SUBLANES, width)
    sub = _iota(b3.shape, 1)
    r3 = jnp.broadcast_to(b3[:, SUBLANES - n2 + n - 1:SUBLANES - n2 + n, :], b3.shape)
    for g in range(SUBLANES // n2 - 2, -1, -1):
        r3 = jnp.where(sub < (g + 1) * n2, b3[:, g * n2 + n - 1:g * n2 + n, :], r3)
    return r3.reshape(rows, width)


def _hgrn2_kernel(h_ref, w_ref, lb_ref, nw_ref, o_ref, proj_ref, st_ref):
    L = CHUNK
    dk = HG_DK
    n_batch = h_ref.shape[0]
    n_heads = st_ref.shape[0] // n_batch
    width = n_heads * dk

    @pl.when(pl.program_id(1) == 0)
    def _():
        st_ref[...] = jnp.zeros_like(st_ref)

    proj_ref[...] = _dot(h_ref[...].reshape(n_batch * L, h_ref.shape[2]), w_ref[...])
    lb = lb_ref[...]
    f = lb + (1.0 - lb) * _sigmoid(proj_ref[:, width:2 * width])
    logf = jnp.log(f)
    t_i = _iota((L, L), 0)
    s_i = _iota((L, L), 1)
    tril = (t_i >= s_i).astype(BF16)
    b_all = [_dot_sel(tril, logf[bb * L:(bb + 1) * L]) for bb in range(n_batch)]
    eye = t_i == s_i

    levels = []
    n2 = L
    while n2 >= 2:
        n = n2 // 2
        sh = n2.bit_length() - 1
        m = ((t_i >> sh) == (s_i >> sh)) & ((t_i & (n2 - 1)) >= n) & ((s_i & (n2 - 1)) < n)
        levels.append((n2, m))
        n2 = n

    for h, bb in [(h, bb) for h in range(n_heads) for bb in range(n_batch)]:
        sl = slice(h * dk, (h + 1) * dk)
        rows = slice(bb * L, (bb + 1) * L)
        q = proj_ref[rows, sl]
        k = 1.0 - f[rows, sl]
        v = proj_ref[rows, 2 * width + h * dk:2 * width + (h + 1) * dk]
        b = b_all[bb][:, sl]
        st = st_ref[bb * n_heads + h]
        b_last = b[L - 1:L, :]
        o = _dot_nt((q * jnp.exp(b)).astype(BF16), st.astype(BF16))
        a = jnp.where(eye, jnp.sum(q * k, axis=-1, keepdims=True), 0.0)
        qb = q.astype(BF16)
        kb = k.astype(BF16)
        for n2, m in levels:
            e = jnp.exp(-jnp.abs(b - _block_mid_ref(b, n2))).astype(BF16)
            a = jnp.where(m, _dot_nt(qb * e, kb * e), a)
        o = o + _dot(a.astype(BF16), v.astype(BF16))
        ke = k * jnp.exp(b_last - b)
        st_ref[bb * n_heads + h] = st * jnp.exp(b_last) + _dot_tn(v.astype(BF16), ke.astype(BF16))
        g = proj_ref[rows, 3 * width + h * dk:3 * width + (h + 1) * dk]
        ms = jnp.mean(o * o, axis=-1, keepdims=True)
        o_ref[bb, :, sl] = (o * lax.rsqrt(ms + NORM_EPS) * nw_ref[:, sl] * _silu(g)).astype(o_ref.dtype)


def _hgrn2(h, w_hg, lb, hg_norm, bsz, seq):
    t, d = h.shape
    width = lb.shape[0]
    n_heads = width // HG_DK
    ns = seq // CHUNK
    nb = HG_BATCH_ROWS if bsz % HG_BATCH_ROWS == 0 else 1
    out = pl.pallas_call(
        _hgrn2_kernel,
        grid=(bsz // nb, ns),
        in_specs=[pl.BlockSpec((nb, CHUNK, d), lambda b, s: (b, s, 0)),
                  pl.BlockSpec(w_hg.shape, lambda b, s: (0, 0)),
                  pl.BlockSpec((1, width), lambda b, s: (0, 0)),
                  pl.BlockSpec((1, width), lambda b, s: (0, 0))],
        out_specs=pl.BlockSpec((nb, CHUNK, width), lambda b, s: (b, s, 0)),
        out_shape=jax.ShapeDtypeStruct((bsz, seq, width), BF16),
        scratch_shapes=[pltpu.VMEM((nb * CHUNK, w_hg.shape[1]), F32), pltpu.VMEM((nb * n_heads, HG_DK, HG_DK), F32)],
        compiler_params=_cparams(("arbitrary", "arbitrary")),
        name="hgrn2_scan",
    )(h.reshape(bsz, seq, d), w_hg, lb.reshape(1, width), hg_norm.reshape(1, width))
    return out.reshape(t, width)


def _ssd_kernel(h_ref, w_ref, cw_ref, cb_ref, dtb_ref, a_ref, dsk_ref, nw_ref, o_ref,
                proj_ref, carry_ref, xpad_ref, st_ref):
    L = CHUNK
    width = o_ref.shape[2]
    xw = cw_ref.shape[1]
    n_heads = width // SSM_P
    gw = width // SSM_GROUPS
    heads_per_group = n_heads // SSM_GROUPS

    @pl.when(pl.program_id(1) == 0)
    def _():
        carry_ref[...] = jnp.zeros_like(carry_ref)
        st_ref[...] = jnp.zeros_like(st_ref)

    n_batch = h_ref.shape[0]
    proj_ref[...] = _dot(h_ref[...].reshape(n_batch * L, h_ref.shape[2]), w_ref[...])
    t_i = _iota((L, L), 0)
    s_i = _iota((L, L), 1)
    triu = (t_i <= s_i).astype(BF16)
    causal = t_i >= s_i
    diag = t_i == s_i
    lane_lo = _iota((L, 2 * SSM_P), 1) < SSM_P
    bd_mask = (_iota((2 * L, 2 * SSM_P), 0) < L) == (_iota((2 * L, 2 * SSM_P), 1) < SSM_P)

    for bb in range(n_batch):
        rs = slice(bb * L, (bb + 1) * L)
        z_ref = proj_ref.at[rs, 0:width]
        dt_ref = proj_ref.at[rs, width + xw:]
        xraw = proj_ref[rs, width:width + xw]
        xpad_ref[bb, 0:SUBLANES, :] = carry_ref[bb]
        xpad_ref[bb, SUBLANES:SUBLANES + L, :] = xraw
        carry_ref[bb] = xraw[L - SUBLANES:L, :]
        acc = cb_ref[...] + jnp.zeros_like(xraw)
        for j in range(SSM_CONV):
            acc = acc + cw_ref[j:j + 1, :] * xpad_ref[bb, pl.ds(SUBLANES - (SSM_CONV - 1) + j, L), :]
        xc = _silu(acc)
        xs = xc[:, :width]
        bm = xc[:, width:width + SSM_GROUPS * SSM_N]
        cm = xc[:, width + SSM_GROUPS * SSM_N:]

        dt = _softplus(dt_ref[...] + dtb_ref[...])
        da = dt * a_ref[...]
        da_t = da.T
        dt_t = dt.T
        acs_t = _dot_rsel(da_t, triu)

        y_pairs = []
        for g in range(SSM_GROUPS):
            bg = bm[:, g * SSM_N:(g + 1) * SSM_N]
            cg = cm[:, g * SSM_N:(g + 1) * SSM_N]
            cb = _dot_nt(cg.astype(BF16), bg.astype(BF16))
            hg = st_ref[bb, :, g * gw:(g + 1) * gw]
            yoff_g = _dot(cg.astype(BF16), hg.astype(BF16))
            xsc_parts, decay_parts = [], []
            for pr in range(heads_per_group // 2):
                j0 = g * heads_per_group + 2 * pr
                gs, ds, ecol, elast = [], [], [], []
                for j in (j0, j0 + 1):
                    row_b = jnp.broadcast_to(acs_t[j:j + 1, :], (L, L))
                    col_b = row_b.T
                    dt_row = jnp.broadcast_to(dt_t[j:j + 1, :], (L, L))
                    lmat = jnp.exp(jnp.minimum(col_b - row_b, 0.0))
                    gs.append(jnp.where(causal, cb * lmat * dt_row, 0.0))
                    a_last = acs_t[j:j + 1, L - 1:L]
                    ds.append(jnp.where(diag, jnp.exp(a_last - row_b) * dt_row, 0.0))
                    ecol.append(jnp.exp(col_b))
                    elast.append(jnp.exp(a_last))
                lhs = jnp.concatenate([jnp.concatenate(gs, axis=1), jnp.concatenate(ds, axis=1)], axis=0)
                xs_pair = xs[:, j0 * SSM_P:(j0 + 2) * SSM_P]
                bd = jnp.where(bd_mask, jnp.concatenate([xs_pair, xs_pair], axis=0), 0.0)
                res = _dot(lhs.astype(BF16), bd.astype(BF16))
                yoff = yoff_g[:, pr * 2 * SSM_P:(pr + 1) * 2 * SSM_P] * jnp.where(lane_lo, ecol[0], ecol[1])
                y_pairs.append(res[:L] + yoff)
                xsc_parts.append(res[L:])
                decay_parts.append(jnp.where(lane_lo[0:1], elast[0], elast[1]))
            xsc_g = jnp.concatenate(xsc_parts, axis=1)
            decay_g = jnp.concatenate(decay_parts, axis=1)
            st_ref[bb, :, g * gw:(g + 1) * gw] = hg * decay_g + _dot_tn(bg.astype(BF16), xsc_g.astype(BF16))
        y = jnp.concatenate(y_pairs, axis=1) + dsk_ref[...] * xs
        yz = y * _silu(z_ref[...])
        for g in range(SSM_GROUPS):
            seg = yz[:, g * gw:(g + 1) * gw]
            ms = jnp.mean(seg * seg, axis=-1, keepdims=True)
            o_ref[bb, :, g * gw:(g + 1) * gw] = (seg * lax.rsqrt(ms + NORM_EPS)
                                                 * nw_ref[:, g * gw:(g + 1) * gw]).astype(o_ref.dtype)


def _ssd(h, w_ssd, conv_w, conv_b, dt_bias, a_log, d_skip, ssm_norm, bsz, seq):
    t, d = h.shape
    width = ssm_norm.shape[0]
    xw = conv_w.shape[1]
    n_heads = width // SSM_P
    ns = seq // CHUNK
    nb = SSD_BATCH_ROWS if bsz % SSD_BATCH_ROWS == 0 else 1
    pad = lambda v: jnp.zeros((1, LANES), F32).at[0, :n_heads].set(v)
    row = lambda w: pl.BlockSpec((nb, CHUNK, w), lambda b, s: (b, s, 0))
    const = lambda r, w: pl.BlockSpec((r, w), lambda b, s: (0, 0))
    out = pl.pallas_call(
        _ssd_kernel,
        grid=(bsz // nb, ns),
        in_specs=[row(d), const(*w_ssd.shape), const(SSM_CONV, xw), const(1, xw), const(1, LANES),
                  const(1, LANES), const(1, width), const(1, width)],
        out_specs=row(width),
        out_shape=jax.ShapeDtypeStruct((bsz, seq, width), BF16),
        scratch_shapes=[pltpu.VMEM((nb * CHUNK, w_ssd.shape[1]), F32), pltpu.VMEM((nb, SUBLANES, xw), F32),
                        pltpu.VMEM((nb, CHUNK + SUBLANES, xw), F32), pltpu.VMEM((nb, SSM_N, width), F32)],
        compiler_params=_cparams(("arbitrary", "arbitrary")),
        name="ssd_scan",
    )(h.reshape(bsz, seq, d), w_ssd, conv_w, conv_b.reshape(1, xw), pad(dt_bias), pad(-jnp.exp(a_log)),
      jnp.repeat(d_skip, SSM_P).reshape(1, width), ssm_norm.reshape(1, width))
    return out.reshape(t, width)


def _route(probs):
    n_exp = N_GROUPS_MOE * 4
    p = [probs[e:e + 1, :] for e in range(n_exp)]
    gs = []
    for g in range(N_GROUPS_MOE):
        a, b, c, d = p[4 * g:4 * g + 4]
        gs.append(jnp.maximum(jnp.maximum(jnp.maximum(a + b, a + c), jnp.maximum(a + d, b + c)),
                              jnp.maximum(b + d, c + d)))
    best = jnp.zeros_like(gs[0]).astype(I32)
    bs = gs[0]
    for g in range(1, N_GROUPS_MOE):
        upd = gs[g] > bs
        best = jnp.where(upd, g, best)
        bs = jnp.where(upd, gs[g], bs)
    q = [jnp.where(best == 0, p[i], jnp.where(best == 1, p[4 + i], jnp.where(best == 2, p[8 + i], p[12 + i])))
         for i in range(4)]
    i1 = jnp.zeros_like(best)
    v1 = q[0]
    for i in range(1, 4):
        upd = q[i] > v1
        i1 = jnp.where(upd, i, i1)
        v1 = jnp.where(upd, q[i], v1)
    i2 = jnp.zeros_like(best)
    v2 = jnp.full_like(v1, -1.0)
    for i in range(4):
        upd = (i1 != i) & (q[i] > v2)
        i2 = jnp.where(upd, i, i2)
        v2 = jnp.where(upd, q[i], v2)
    den = v1 + v2
    return best * 4 + i1, best * 4 + i2, v1 / den, v2 / den


def _head_stats_expand(stack, seg_ref):
    pw = seg_ref.shape[0]
    seg = seg_ref[...]
    return jnp.concatenate([_dot_rsel(stack[:, p * pw:(p + 1) * pw], seg) for p in range(stack.shape[1] // pw)], axis=1)


def _epilogue_kernel(*refs, n_in, rwkv):
    ins = refs[:n_in]
    pos = n_in
    w_refs = refs[pos:pos + n_in]
    pos += n_in
    if rwkv:
        g_ref, bonus_ref, lnw_ref, lnb_ref, seg_ref = refs[pos:pos + 5]
        pos += 5
    x_ref, gate_ref, nw_ref, sh_ref, sc_ref, rw_ref, rb_ref = refs[pos:pos + 7]
    pos += 7
    xo_ref, h_ref, ti_ref, tw_ref = refs[pos:pos + 4]

    tm = ROW_TM
    rw_hi, rw_lo = _split(rw_ref[...])
    for sub in range(x_ref.shape[0] // tm):
        rs = pl.ds(sub * tm, tm)
        if rwkv:
            o = ins[0][rs, :]
            inv_n = 1.0 / RW_N
            st = _head_stats_expand(jnp.concatenate([o, o * o], axis=0), seg_ref) * inv_n
            mean = st[:tm]
            var = jnp.maximum(st[tm:] - mean * mean, 0.0)
            o = (o - mean) * lax.rsqrt(var + RW_GN_EPS) * lnw_ref[...] + lnb_ref[...]
            o = (o + bonus_ref[rs, :].astype(F32)) * g_ref[rs, :].astype(F32)
            y = _dot(o.astype(BF16), w_refs[0][...])
        else:
            y = _dot(ins[0][rs, :], w_refs[0][...])
            for a_ref, w_ref in zip(ins[1:], w_refs[1:]):
                y = y + _dot(a_ref[rs, :], w_ref[...])
        x_new = x_ref[rs, :] + gate_ref[0] * y
        xo_ref[rs, :] = x_new
        h = _rms_mod(x_new, nw_ref[...], sh_ref[0], sc_ref[0])
        _store_rows(h_ref.at[pl.ds(sub * tm * ROW_PITCH, tm * ROW_PITCH)], h)
        h_hi, h_lo = _split(h)
        logits = _dot_nt(rw_hi, h_hi) + _dot_nt(rw_hi, h_lo) + _dot_nt(rw_lo, h_hi) + rb_ref[...]
        mx = jnp.max(logits, axis=0, keepdims=True)
        ex = jnp.exp(logits - mx)
        probs = ex / jnp.sum(ex, axis=0, keepdims=True)
        e1, e2, w1, w2 = _route(probs)
        zi = jnp.zeros((SUBLANES - TOP_K, tm), I32)
        ti_ref[:, sub * tm:(sub + 1) * tm] = jnp.concatenate([e1, e2, zi], axis=0)
        tw_ref[:, sub * tm:(sub + 1) * tm] = jnp.concatenate([w1, w2, zi.astype(F32)], axis=0)


def _epilogue(ins, ws, x2, gate, nw, shift, scale, router_w, router_b, bsz, seq, rwkv_extra=None):
    t, d = x2.shape
    tm = EPILOGUE_SUBTILES * ROW_TM
    ns = seq // tm
    n_exp = router_w.shape[1]
    rw_t = jnp.zeros((LANES, d), F32).at[:n_exp].set(router_w.T)
    rb = jnp.full((LANES, 1), -1e30, F32).at[:n_exp, 0].set(router_b)
    row = lambda w: pl.BlockSpec((tm, w), lambda b, s: (b * ns + s, 0))
    const = lambda a: pl.BlockSpec(a.shape, lambda b, s: (0,) * a.ndim)
    per_b = pl.BlockSpec((1, 1, d), lambda b, s: (b, 0, 0))
    args = list(ins) + list(ws)
    specs = [row(a.shape[1]) for a in ins] + [const(w) for w in ws]
    if rwkv_extra is not None:
        g, bonus, lnw, lnb, seg = rwkv_extra
        args += [g, bonus, lnw.reshape(1, d), lnb.reshape(1, d), seg]
        specs += [row(d), row(d), pl.BlockSpec((1, d), lambda b, s: (0, 0)), pl.BlockSpec((1, d), lambda b, s: (0, 0)),
                  const(seg)]
    args += [x2, gate.reshape(bsz, 1, d), nw.reshape(1, d), shift.reshape(bsz, 1, d), scale.reshape(bsz, 1, d), rw_t, rb]
    specs += [row(d), per_b, pl.BlockSpec((1, d), lambda b, s: (0, 0)), per_b, per_b, const(rw_t), const(rb)]
    lane_row = pl.BlockSpec((SUBLANES, tm), lambda b, s: (0, b * ns + s))
    return pl.pallas_call(
        functools.partial(_epilogue_kernel, n_in=len(ins), rwkv=rwkv_extra is not None),
        grid=(bsz, ns),
        in_specs=specs,
        out_specs=[row(d), pl.BlockSpec((tm * ROW_PITCH, LANES), lambda b, s: (b * ns + s, 0)), lane_row, lane_row],
        out_shape=[jax.ShapeDtypeStruct((t, d), F32), jax.ShapeDtypeStruct((t * ROW_PITCH, LANES), F32),
                   jax.ShapeDtypeStruct((SUBLANES, t), I32), jax.ShapeDtypeStruct((SUBLANES, t), F32)],
        compiler_params=_cparams(("arbitrary", "arbitrary")),
        name="mixer_epilogue",
    )(*args)


def _route_plan(top_i, n_exp, tm):
    t = top_i.shape[1]
    n_pairs = TOP_K * t
    n_tiles = n_pairs // tm + n_exp
    e_flat = top_i[:TOP_K].reshape(-1)
    onehot = (e_flat[:, None] == jnp.arange(n_exp, dtype=I32)[None, :]).astype(I32)
    csum = jnp.cumsum(onehot, axis=0)
    counts = csum[-1]
    rank = jnp.sum((csum - 1) * onehot, axis=1)
    padded = ((counts + tm - 1) // tm) * tm
    ends = jnp.cumsum(padded)
    pos = (ends - padded)[e_flat] + rank
    tile_start = jnp.arange(n_tiles, dtype=I32) * tm
    tile_exp = jnp.minimum(jnp.sum((tile_start[:, None] >= ends[None, :]).astype(I32), axis=1), n_exp - 1)
    n_used = (ends[-1] // tm).reshape(1).astype(I32)
    gap_lo = jnp.concatenate([ends - padded + counts, ends[-1:]]).astype(I32)
    gap_hi = jnp.concatenate([ends, jnp.full((1,), n_tiles * tm, I32)]).astype(I32)
    return pos.astype(I32), tile_exp.astype(I32), n_used, gap_lo, gap_hi


def _dispatch_kernel(pos_ref, glo_ref, ghi_ref, h_ref, xs_hbm, zbuf, sem, zsem):
    tm = DISPATCH_TM
    n_tok = pos_ref.shape[0] // TOP_K
    i = pl.program_id(0)

    @pl.when(i == 0)
    def _():
        zbuf[...] = jnp.zeros_like(zbuf)

        def chunk_copy(row):
            return pltpu.make_async_copy(zbuf, xs_hbm.at[pl.ds(row * ROW_PITCH, ZERO_CHUNK * ROW_PITCH)], zsem)

        def row_copy(row):
            return pltpu.make_async_copy(zbuf.at[pl.ds(0, ROW_PITCH)], xs_hbm.at[pl.ds(row * ROW_PITCH, ROW_PITCH)], zsem)

        for e in range(glo_ref.shape[0]):
            lo = glo_ref[e]
            hi = ghi_ref[e]
            n_chunks = lax.div(hi - lo, ZERO_CHUNK)
            mid = lo + n_chunks * ZERO_CHUNK

            def start_chunk(c, carry):
                chunk_copy(lo + c * ZERO_CHUNK).start()
                return carry

            def start_row(r, carry):
                row_copy(r).start()
                return carry

            def wait_chunk(c, carry):
                chunk_copy(lo).wait()
                return carry

            def wait_row(r, carry):
                row_copy(lo).wait()
                return carry

            lax.fori_loop(0, n_chunks, start_chunk, 0)
            lax.fori_loop(mid, hi, start_row, 0)
            lax.fori_loop(0, n_chunks, wait_chunk, 0)
            lax.fori_loop(mid, hi, wait_row, 0)

    def body(r8, c):
        for u in range(GATHER_UNROLL):
            r = r8 * GATHER_UNROLL + u
            for k in range(TOP_K):
                p = pos_ref[k * n_tok + i * tm + r]
                pltpu.make_async_copy(h_ref.at[pl.ds(r * ROW_PITCH, ROW_PITCH)],
                                      xs_hbm.at[pl.ds(p, ROW_PITCH)], sem).start(priority=k)
        return c
    lax.fori_loop(0, tm // GATHER_UNROLL, body, 0)
    for k in range(TOP_K):
        pltpu.make_async_copy(h_ref, xs_hbm.at[pl.ds(0, tm * ROW_PITCH)], sem).wait()


def _dispatch(h, pos, gap_lo, gap_hi, n_rows):
    tm = DISPATCH_TM
    t = h.shape[0] // ROW_PITCH
    grid_spec = pltpu.PrefetchScalarGridSpec(
        num_scalar_prefetch=3,
        grid=(t // tm,),
        in_specs=[pl.BlockSpec((tm * ROW_PITCH, LANES), lambda i, p, lo, hi: (i, 0))],
        out_specs=pl.BlockSpec(memory_space=pl.ANY),
        scratch_shapes=[pltpu.VMEM((ZERO_CHUNK * ROW_PITCH, LANES), F32), pltpu.SemaphoreType.DMA(()),
                        pltpu.SemaphoreType.DMA(())],
    )
    return pl.pallas_call(
        _dispatch_kernel,
        grid_spec=grid_spec,
        out_shape=jax.ShapeDtypeStruct((n_rows * ROW_PITCH, LANES), F32),
        compiler_params=_cparams(("arbitrary",)),
        name="moe_dispatch",
    )(pos, gap_lo, gap_hi, h)


def _moe_kernel(texp_ref, nused_ref, x_ref, w1_ref, w3_ref, w2_ref, y_ref, wb1, wb3, wb2):
    tm = MOE_TM
    d = wb1.shape[0]
    i = pl.program_id(0)
    n_used = nused_ref[0]

    @pl.when((i == 0) | (texp_ref[i] != texp_ref[jnp.maximum(i - 1, 0)]))
    def _():
        wb1[...] = w1_ref[0, 0].astype(BF16)
        wb3[...] = w3_ref[0, 0].astype(BF16)
        wb2[...] = w2_ref[0, 0].astype(BF16)

    @pl.when(i < n_used)
    def _():
        x = _load_rows(x_ref, tm, d).astype(BF16)
        a = _dot(x, wb1[...])
        b = _dot(x, wb3[...])
        hid = (_silu(a) * b).astype(BF16)
        _store_rows(y_ref, _dot(hid, wb2[...]))

    @pl.when(i >= n_used)
    def _():
        y_ref[...] = jnp.zeros_like(y_ref)


def _moe(xs, w1, w3, w2, layer, tile_exp, n_used):
    _, n_exp, d, dff = w1.shape
    tm = MOE_TM
    n_tiles = tile_exp.shape[0]
    rows = pl.BlockSpec((tm * ROW_PITCH, LANES), lambda i, te, nu: (i, 0))
    used_rows = pl.BlockSpec((tm * ROW_PITCH, LANES), lambda i, te, nu: (jnp.minimum(i, nu[0] - 1), 0))
    grid_spec = pltpu.PrefetchScalarGridSpec(
        num_scalar_prefetch=2,
        grid=(n_tiles,),
        in_specs=[used_rows,
                  pl.BlockSpec((1, 1, d, dff), lambda i, te, nu: (layer, te[i], 0, 0)),
                  pl.BlockSpec((1, 1, d, dff), lambda i, te, nu: (layer, te[i], 0, 0)),
                  pl.BlockSpec((1, 1, dff, d), lambda i, te, nu: (layer, te[i], 0, 0))],
        out_specs=rows,
        scratch_shapes=[pltpu.VMEM((d, dff), BF16), pltpu.VMEM((d, dff), BF16), pltpu.VMEM((dff, d), BF16)],
    )
    return pl.pallas_call(
        _moe_kernel,
        grid_spec=grid_spec,
        out_shape=jax.ShapeDtypeStruct((n_tiles * tm * ROW_PITCH, LANES), F32),
        compiler_params=_cparams(("arbitrary",)),
        name="moe_experts",
    )(tile_exp, n_used, xs, w1, w3, w2)


def _combine_kernel(pos_ref, y_hbm, x_ref, tw_ref, gate_ref, nw_ref, sh_ref, sc_ref, *out_and_scratch, final, n_tok):
    if final:
        o_ref, ybuf, sem = out_and_scratch
    else:
        xo_ref, h_ref, ybuf, sem = out_and_scratch
    tm, d = x_ref.shape
    n_sub = d // LANES
    i = pl.program_id(0)
    n_steps = pl.num_programs(0)
    slot = i % 2

    def start_gather(tile, sl):
        def body(r8, c):
            for u in range(GATHER_UNROLL):
                r = r8 * GATHER_UNROLL + u
                for k in range(TOP_K):
                    src = pos_ref[k * n_tok + tile * tm + r]
                    pltpu.make_async_copy(y_hbm.at[pl.ds(src, n_sub)], ybuf.at[sl, k, pl.ds(r * ROW_PITCH, n_sub)],
                                          sem.at[sl]).start(priority=k)
            return c
        lax.fori_loop(0, tm // GATHER_UNROLL, body, 0)

    def wait_gather(sl):
        for k in range(TOP_K):
            pltpu.make_async_copy(y_hbm.at[pl.ds(0, tm * n_sub)], ybuf.at[sl, k, pl.ds(0, tm * n_sub)], sem.at[sl]).wait()

    @pl.when(i == 0)
    def _():
        start_gather(0, 0)

    @pl.when(i + 1 < n_steps)
    def _():
        start_gather(i + 1, 1 - slot)

    wait_gather(slot)
    tw = tw_ref[...]
    moe = tw[:, 0:1] * _load_rows(ybuf.at[slot, 0], tm, d) + tw[:, 1:2] * _load_rows(ybuf.at[slot, 1], tm, d)
    x_new = x_ref[...] + gate_ref[0] * moe
    if final:
        o_ref[...] = x_new * lax.rsqrt(jnp.mean(x_new * x_new, axis=-1, keepdims=True) + NORM_EPS) * nw_ref[...]
    else:
        xo_ref[...] = x_new
        h_ref[...] = _rms_mod(x_new, nw_ref[...], sh_ref[0], sc_ref[0]).astype(h_ref.dtype)


def _combine(pos, y_sorted, x2, top_w, gate, nw, shift, scale, bsz, seq, final):
    t, d = x2.shape
    tm = ROW_TM
    ns = seq // tm
    tw = top_w.T
    row = pl.BlockSpec((tm, d), lambda i, p: (i, 0))
    per_b = pl.BlockSpec((1, 1, d), lambda i, p: (i // ns, 0, 0))
    grid_spec = pltpu.PrefetchScalarGridSpec(
        num_scalar_prefetch=1,
        grid=(t // tm,),
        in_specs=[pl.BlockSpec(memory_space=pl.ANY), row,
                  pl.BlockSpec((tm, SUBLANES), lambda i, p: (i, 0)),
                  per_b, pl.BlockSpec((1, d), lambda i, p: (0, 0)), per_b, per_b],
        out_specs=row if final else [row, row],
        scratch_shapes=[pltpu.VMEM((2, TOP_K, tm * ROW_PITCH, LANES), F32), pltpu.SemaphoreType.DMA((2,))],
    )
    out_shape = (jax.ShapeDtypeStruct((t, d), F32) if final else
                 [jax.ShapeDtypeStruct((t, d), F32), jax.ShapeDtypeStruct((t, d), BF16)])
    return pl.pallas_call(
        functools.partial(_combine_kernel, final=final, n_tok=t),
        grid_spec=grid_spec,
        out_shape=out_shape,
        compiler_params=_cparams(("arbitrary",)),
        name="moe_combine",
    )(pos, y_sorted, x2, tw, gate.reshape(bsz, 1, d), nw.reshape(1, d), shift.reshape(bsz, 1, d), scale.reshape(bsz, 1, d))


def _rwkv_prep_kernel(h_ref, mu_ref, wr_ref, wk_ref, wv_ref, dec0_ref, dec1_ref, dec2_ref, a0_ref, a1_ref, a2_ref,
                      g1_ref, g2_ref, kk_ref, ka_ref, rk_ref, seg_ref,
                      r_out, lw_out, k_out, v_out, kkn_out, a_out, g_out, bonus_out, carry_ref, hp_ref):
    tm = h_ref.shape[0]

    @pl.when(pl.program_id(1) == 0)
    def _():
        carry_ref[...] = jnp.zeros_like(carry_ref)

    hb = h_ref[...]
    h = hb.astype(F32)
    hp_ref[0:SUBLANES, :] = carry_ref[...]
    hp_ref[SUBLANES:SUBLANES + tm, :] = h
    carry_ref[...] = h[tm - SUBLANES:tm, :]
    xxb = (hp_ref[pl.ds(SUBLANES - 1, tm), :] - h).astype(BF16)
    mix = lambda i: hb + xxb * mu_ref[i:i + 1, :].astype(BF16)
    r = _dot(mix(0), wr_ref[...])
    k = _dot(mix(1), wk_ref[...])
    v = _dot(mix(2), wv_ref[...])
    wl = dec0_ref[...] + _dot(jnp.tanh(_dot(mix(3), dec1_ref[...])).astype(BF16), dec2_ref[...])
    lw = -RW_DECAY_SCALE * _sigmoid(wl)
    a = _sigmoid(a0_ref[...] + _dot(_dot(mix(4), a1_ref[...]).astype(BF16), a2_ref[...]))
    g = _dot(_sigmoid(_dot(mix(5), g1_ref[...])).astype(BF16), g2_ref[...])
    kk = k * kk_ref[...]
    k_h = k * (1.0 + (a - 1.0) * ka_ref[...])
    st = _head_stats_expand(jnp.concatenate([kk * kk, r * k_h * rk_ref[...]], axis=0), seg_ref)
    kkn = kk * lax.rsqrt(jnp.maximum(st[:tm], 1e-24))
    r_out[...] = r.astype(r_out.dtype)
    lw_out[...] = lw
    k_out[...] = k_h.astype(k_out.dtype)
    v_out[...] = v.astype(v_out.dtype)
    kkn_out[...] = kkn.astype(kkn_out.dtype)
    a_out[...] = a.astype(a_out.dtype)
    g_out[...] = g.astype(g_out.dtype)
    bonus_out[...] = (st[tm:] * v).astype(bonus_out.dtype)


def _rwkv_prep(h, mu, w_rkv, dec0, dec1, dec2, a0, a1, a2, g1, g2, k_k, k_a, r_k, seg, bsz, seq):
    t, d = h.shape
    tm = ROW_TM
    ns = seq // tm
    padc = lambda w: jnp.zeros((d, LANES), F32).at[:, :w.shape[1]].set(w).astype(BF16)
    padr = lambda w: jnp.zeros((LANES, d), F32).at[:w.shape[0]].set(w).astype(BF16)
    vec = lambda v: v.reshape(1, d)
    args = [h, mu, w_rkv[0].astype(BF16), w_rkv[1].astype(BF16), w_rkv[2].astype(BF16), vec(dec0), padc(dec1), padr(dec2),
            vec(a0), padc(a1), padr(a2), padc(g1), padr(g2), vec(k_k), vec(k_a), vec(r_k), seg]
    row = pl.BlockSpec((tm, d), lambda b, s: (b * ns + s, 0))
    const = lambda a: pl.BlockSpec(a.shape, lambda b, s: (0,) * a.ndim)
    outs = [BF16, F32, BF16, BF16, BF16, BF16, BF16, BF16]
    return pl.pallas_call(
        _rwkv_prep_kernel,
        grid=(bsz, ns),
        in_specs=[row] + [const(a) for a in args[1:]],
        out_specs=[row] * len(outs),
        out_shape=[jax.ShapeDtypeStruct((t, d), dt) for dt in outs],
        scratch_shapes=[pltpu.VMEM((SUBLANES, d), F32), pltpu.VMEM((tm + SUBLANES, d), F32)],
        compiler_params=_cparams(("arbitrary", "arbitrary")),
        name="rwkv_prep",
    )(*args)


def _rwkv_scan_kernel(r_ref, lw_ref, k_ref, v_ref, kk_ref, a_ref, y_ref, st_ref):
    L = RW_CHUNK
    pw = RW_PACK * RW_N
    n_packs = st_ref.shape[0]
    sh = RW_N.bit_length() - 1

    @pl.when(pl.program_id(1) == 0)
    def _():
        st_ref[...] = jnp.zeros_like(st_ref)

    t_i = _iota((L, L), 0)
    s_i = _iota((L, L), 1)
    tril = (t_i >= s_i).astype(BF16)
    n_batch = lw_ref.shape[0]
    per_batch = n_packs // n_batch
    wc_all = [_dot_sel(tril, lw_ref[bb]) for bb in range(n_batch)]
    lane_head = _iota((L, pw), 1) >> sh
    s_loc = _iota((L, pw), 1) & (RW_N - 1)
    t_loc = _iota((L, pw), 0)
    strict = s_loc < t_loc
    incl = s_loc <= t_loc
    bd_mask = (_iota((pw, pw), 0) >> sh) == (_iota((pw, pw), 1) >> sh)

    def bdiag(x):
        return jnp.where(bd_mask, jnp.concatenate([x] * RW_PACK, axis=0), 0.0).astype(BF16)

    packs = range(n_packs)
    sls = [(p // per_batch, slice((p % per_batch) * pw, (p % per_batch + 1) * pw)) for p in packs]
    pr, vs, sts, kkas, ks, wcs = [], [], [], [], [], []
    for p in packs:
        bb, sl = sls[p]
        r = r_ref[bb, :, sl].astype(F32)
        lw = lw_ref[bb, :, sl]
        k = k_ref[bb, :, sl].astype(F32)
        kk = kk_ref[bb, :, sl].astype(F32)
        a = a_ref[bb, :, sl].astype(F32)
        wc = wc_all[bb][:, sl]
        e_inv = jnp.exp(-wc)
        kka = kk * a
        al = -kk * jnp.exp(wc - lw)
        rb = r * jnp.exp(wc)
        bt = kka * e_inv
        kt = k * e_inv
        lhs = jnp.concatenate([al, rb], axis=0).astype(BF16)
        rows = [jnp.where(lane_head == hh, x, 0.0) for x in (bt, kt) for hh in range(RW_PACK)]
        st = st_ref[p]
        m = jnp.concatenate(rows + [st], axis=0).astype(BF16)
        pr.append(_dot_nt(lhs, m))
        vs.append(v_ref[bb, :, sl].astype(F32))
        sts.append(st)
        kkas.append(kka)
        ks.append(k)
        wcs.append(wc)
    bd_vs = [bdiag(vs[p]) for p in packs]
    us = [pr[p][:L, 2 * pw:] + _dot(jnp.where(strict, pr[p][:L, pw:2 * pw], 0.0).astype(BF16), bd_vs[p]) for p in packs]
    nmats = [jnp.where(strict, pr[p][:L, 0:pw], 0.0) for p in packs]
    n_steps = L.bit_length() - 1
    for it in range(n_steps):
        us = [us[p] + _dot(nmats[p].astype(BF16), bdiag(us[p])) for p in packs]
        if it + 1 < n_steps:
            nmats = [_dot(nmats[p].astype(BF16), bdiag(nmats[p])) for p in packs]
    for p in packs:
        a_rb = jnp.where(incl, pr[p][L:, 0:pw], 0.0)
        a_rk = jnp.where(incl, pr[p][L:, pw:2 * pw], 0.0)
        bb, sl = sls[p]
        y_ref[bb, :, sl] = pr[p][L:, 2 * pw:] + _dot(jnp.concatenate([a_rb, a_rk], axis=1).astype(BF16),
                                                     jnp.concatenate([bdiag(us[p]), bd_vs[p]], axis=0))
    for p in packs:
        w_last = wcs[p][L - 1:L, :]
        e_last = jnp.exp(w_last - wcs[p])
        upd = _dot_tn(jnp.concatenate([us[p], vs[p]], axis=0).astype(BF16),
                      jnp.concatenate([kkas[p] * e_last, ks[p] * e_last], axis=0).astype(BF16))
        st_ref[p] = jnp.where(bd_mask, sts[p] * jnp.exp(w_last) + upd, 0.0)


def _rwkv_scan(r, lw, k, v, kk, a, bsz, seq):
    t, d = r.shape
    ns = seq // RW_CHUNK
    pw = RW_PACK * RW_N
    nb = RW_BATCH_ROWS if bsz % RW_BATCH_ROWS == 0 else 1
    row = pl.BlockSpec((nb, RW_CHUNK, d), lambda b, s: (b, s, 0))
    y = pl.pallas_call(
        _rwkv_scan_kernel,
        grid=(bsz // nb, ns),
        in_specs=[row] * 6,
        out_specs=row,
        out_shape=jax.ShapeDtypeStruct((bsz, seq, d), F32),
        scratch_shapes=[pltpu.VMEM((nb * (d // pw), pw, pw), F32)],
        compiler_params=_cparams(("arbitrary", "arbitrary")),
        name="rwkv_scan",
    )(*[z.reshape(bsz, seq, d) for z in (r, lw, k, v, kk, a)])
    return y.reshape(t, d)


def _moe_block(h, top_i, top_w, x2, gate, w1, w3, w2, layer, nw, shift, scale, bsz, seq, final):
    n_exp = w1.shape[1]
    pos, tile_exp, n_used, gap_lo, gap_hi = _route_plan(top_i, n_exp, MOE_TM)
    pos = pos * ROW_PITCH
    x_sorted = _dispatch(h, pos, gap_lo, gap_hi, tile_exp.shape[0] * MOE_TM)
    y_sorted = _moe(x_sorted, w1, w3, w2, layer, tile_exp, n_used)
    return _combine(pos, y_sorted, x2, top_w, gate, nw, shift, scale, bsz, seq, final)


def kernel(x, c, mod_w, mod_b, norm_w, hg_lb_logits, ev_w_in, ev_hg_norm, ev_conv_w, ev_conv_b, ev_dt_bias, ev_a_log, ev_d_skip, ev_ssm_norm, ev_w_out, od_mu, od_w_rkv, od_w_dec0, od_w_dec1, od_w_dec2, od_a0, od_a1, od_a2, od_g1, od_g2, od_k_k, od_k_a, od_r_k, od_ln_w, od_ln_b, od_w_o, router_w, router_b, moe_w1, moe_w3, moe_w2, final_norm_w):
    bsz, seq, d = x.shape
    depth = mod_w.shape[0]
    t = bsz * seq
    x2 = x.reshape(t, d)
    mod = _adaln_mod(c, mod_w, mod_b)
    gamma = jax.nn.softmax(hg_lb_logits.astype(F32), axis=0)
    lower_bounds = jnp.cumsum(gamma, axis=0) - gamma[0]
    head_of_lane = jnp.arange(RW_PACK * RW_N, dtype=I32) // RW_N
    seg = (head_of_lane[:, None] == head_of_lane[None, :]).astype(BF16)

    h = None
    out = None
    for l in range(depth):
        sh_m, sc_m, gt_m, sh_f, sc_f, gt_f = [mod[l, :, i * d:(i + 1) * d] for i in range(6)]
        j = l // 2
        if h is None:
            h = _norm_mod(x2, norm_w[l, 0], sh_m, sc_m, bsz, seq)
        if l % 2 == 0:
            w_in = ev_w_in[j]
            hgw = ev_hg_norm.shape[1]
            sw = ev_ssm_norm.shape[1]
            xbw = ev_conv_w.shape[2]
            nh = ev_dt_bias.shape[1]
            c0 = 4 * hgw
            w_hg = w_in[:, :c0].astype(BF16)
            w_ssd = jnp.zeros((d, sw + xbw + LANES), F32).at[:, :sw + xbw + nh].set(w_in[:, c0:]).astype(BF16)
            o_a = _hgrn2(h, w_hg, lower_bounds[l + 1], ev_hg_norm[j], bsz, seq)
            o_b = _ssd(h, w_ssd, ev_conv_w[j], ev_conv_b[j], ev_dt_bias[j], ev_a_log[j], ev_d_skip[j],
                       ev_ssm_norm[j], bsz, seq)
            w_out = ev_w_out[j].astype(BF16)
            x2, hf, top_i, top_w = _epilogue([o_a, o_b], [w_out[:hgw], w_out[hgw:]], x2, gt_m, norm_w[l, 1], sh_f, sc_f,
                                             router_w, router_b, bsz, seq)
        else:
            r, lw, k, v, kk, a, g, bonus = _rwkv_prep(h, od_mu[j], od_w_rkv[j], od_w_dec0[j], od_w_dec1[j], od_w_dec2[j],
                                                      od_a0[j], od_a1[j], od_a2[j], od_g1[j], od_g2[j], od_k_k[j],
                                                      od_k_a[j], od_r_k[j].reshape(-1), seg, bsz, seq)
            y = _rwkv_scan(r, lw, k, v, kk, a, bsz, seq)
            x2, hf, top_i, top_w = _epilogue([y], [od_w_o[j].astype(BF16)], x2, gt_m, norm_w[l, 1], sh_f, sc_f,
                                             router_w, router_b, bsz, seq,
                                             rwkv_extra=(g, bonus, od_ln_w[j], od_ln_b[j], seg))
        final = l == depth - 1
        if final:
            nw_next, sh_next, sc_next = final_norm_w, sh_f, sc_f
        else:
            nxt = [mod[l + 1, :, i * d:(i + 1) * d] for i in range(2)]
            nw_next, sh_next, sc_next = norm_w[l + 1, 0], nxt[0], nxt[1]
        res = _moe_block(hf, top_i, top_w, x2, gt_f, moe_w1, moe_w3, moe_w2, l, nw_next, sh_next, sc_next,
                         bsz, seq, final)
        if final:
            out = res
        else:
            x2, h = res
    return out.reshape(bsz, seq, d)
```

```python
import functools

import jax
import jax.numpy as jnp
from jax import lax
from jax.experimental import pallas as pl
from jax.experimental.pallas import tpu as pltpu

F32 = jnp.float32
BF16 = jnp.bfloat16
I32 = jnp.int32

NORM_EPS = 1e-6
RW_GN_EPS = 64e-5
RW_DECAY_SCALE = 0.6065306597126334
LANES = 128
SUBLANES = 8
VMEM_LIMIT = 56 * 1024 * 1024

HG_DK = 128
SSM_P = 64
SSM_N = 128
SSM_GROUPS = 2
SSM_CONV = 4
RW_N = 64
N_GROUPS_MOE = 4
TOP_K = 2

CHUNK = 128
HG_BATCH_ROWS = 4
SSD_BATCH_ROWS = 4
RW_CHUNK = 64
RW_PACK = 4
RW_BATCH_ROWS = 2
MOE_TM = 512
ROW_TM = 256
NORM_TM = 512
EPILOGUE_SUBTILES = 4
DISPATCH_TM = 1024
ZERO_CHUNK = 64
GATHER_UNROLL = 16
ROW_PITCH = 9


def _cparams(sem):
    return pltpu.CompilerParams(dimension_semantics=sem, vmem_limit_bytes=VMEM_LIMIT)


def _dot(a, b):
    return lax.dot_general(a, b, (((1,), (0,)), ((), ())), preferred_element_type=F32)


def _dot_nt(a, b):
    return lax.dot_general(a, b, (((1,), (1,)), ((), ())), preferred_element_type=F32)


def _dot_tn(a, b):
    return lax.dot_general(a, b, (((0,), (0,)), ((), ())), preferred_element_type=F32)


def _split(x):
    hi = x.astype(BF16)
    return hi, (x - hi.astype(F32)).astype(BF16)


def _dot_sel(sel, x):
    hi, lo = _split(x)
    return _dot(sel, hi) + _dot(sel, lo)


def _dot_rsel(x, sel):
    hi, lo = _split(x)
    return _dot(hi, sel) + _dot(lo, sel)


def _sigmoid(x):
    return 1.0 / (1.0 + jnp.exp(-x))


def _silu(x):
    return x * _sigmoid(x)


def _softplus(x):
    return jnp.maximum(x, 0.0) + jnp.log(1.0 + jnp.exp(-jnp.abs(x)))


def _iota(shape, dim):
    return lax.broadcasted_iota(I32, shape, dim)


def _store_rows(ref, val):
    tm, width = val.shape
    for j in range(width // LANES):
        ref[pl.ds(j, tm, stride=ROW_PITCH), :] = val[:, j * LANES:(j + 1) * LANES]
    for j in range(width // LANES, ROW_PITCH):
        ref[pl.ds(j, tm, stride=ROW_PITCH), :] = jnp.zeros((tm, LANES), ref.dtype)


def _load_rows(ref, tm, width):
    return jnp.concatenate([ref[pl.ds(j, tm, stride=ROW_PITCH), :] for j in range(width // LANES)], axis=1)


def _mod_kernel(c_ref, w_ref, b_ref, o_ref):
    c = c_ref[...]
    o_ref[0] = _dot(_silu(c).astype(BF16), w_ref[0].astype(BF16)) + b_ref[0]


def _adaln_mod(c, mod_w, mod_b):
    depth, d, width = mod_w.shape
    bsz = c.shape[0]
    c_pad = jnp.zeros((SUBLANES, d), F32).at[:bsz].set(c)
    tn = 1536
    out = pl.pallas_call(
        _mod_kernel,
        grid=(depth, width // tn),
        in_specs=[pl.BlockSpec((SUBLANES, d), lambda l, j: (0, 0)),
                  pl.BlockSpec((1, d, tn), lambda l, j: (l, 0, j)),
                  pl.BlockSpec((1, 1, tn), lambda l, j: (l, 0, j))],
        out_specs=pl.BlockSpec((1, SUBLANES, tn), lambda l, j: (l, 0, j)),
        out_shape=jax.ShapeDtypeStruct((depth, SUBLANES, width), F32),
        compiler_params=_cparams(("arbitrary", "arbitrary")),
        name="adaln_mod",
    )(c_pad, mod_w, mod_b.reshape(depth, 1, width))
    return out[:, :bsz]


def _rms_mod(x, nw, shift, scale):
    y = x * lax.rsqrt(jnp.mean(x * x, axis=-1, keepdims=True) + NORM_EPS) * nw
    return y * (1.0 + scale) + shift


def _normmod_kernel(x_ref, nw_ref, sh_ref, sc_ref, h_ref):
    h_ref[...] = _rms_mod(x_ref[...], nw_ref[...], sh_ref[0], sc_ref[0]).astype(h_ref.dtype)


def _norm_mod(x2, nw, shift, scale, bsz, seq):
    t, d = x2.shape
    tm = NORM_TM
    ns = seq // tm
    return pl.pallas_call(
        _normmod_kernel,
        grid=(bsz, ns),
        in_specs=[pl.BlockSpec((tm, d), lambda b, s: (b * ns + s, 0)),
                  pl.BlockSpec((1, d), lambda b, s: (0, 0)),
                  pl.BlockSpec((1, 1, d), lambda b, s: (b, 0, 0)),
                  pl.BlockSpec((1, 1, d), lambda b, s: (b, 0, 0))],
        out_specs=pl.BlockSpec((tm, d), lambda b, s: (b * ns + s, 0)),
        out_shape=jax.ShapeDtypeStruct((t, d), BF16),
        compiler_params=_cparams(("arbitrary", "arbitrary")),
        name="norm_mod",
    )(x2, nw.reshape(1, d), shift.reshape(bsz, 1, d), scale.reshape(bsz, 1, d))


def _block_mid_ref(b, n2):
    rows, width = b.shape
    n = n2 // 2
    if n2 >= 2 * SUBLANES:
        b3 = b.reshape(rows // n2, n2, width)
        return jnp.broadcast_to(b3[:, n - 1:n, :], b3.shape).reshape(rows, width)
    b3 = b.reshape(rows // SUBLANES, SUBLANES, width)
    sub = _iota(b3.shape, 1)
    r3 = jnp.broadcast_to(b3[:, SUBLANES - n2 + n - 1:SUBLANES - n2 + n, :], b3.shape)
    for g in range(SUBLANES // n2 - 2, -1, -1):
        r3 = jnp.where(sub < (g + 1) * n2, b3[:, g * n2 + n - 1:g * n2 + n, :], r3)
    return r3.reshape(rows, width)


def _hgrn2_kernel(h_ref, w_ref, lb_ref, nw_ref, o_ref, proj_ref, st_ref):
    L = CHUNK
    dk = HG_DK
    n_batch = h_ref.shape[0]
    n_heads = st_ref.shape[0] // n_batch
    width = n_heads * dk

    @pl.when(pl.program_id(1) == 0)
    def _():
        st_ref[...] = jnp.zeros_like(st_ref)

    proj_ref[...] = _dot(h_ref[...].reshape(n_batch * L, h_ref.shape[2]), w_ref[...])
    lb = lb_ref[...]
    f = lb + (1.0 - lb) * _sigmoid(proj_ref[:, width:2 * width])
    logf = jnp.log(f)
    t_i = _iota((L, L), 0)
    s_i = _iota((L, L), 1)
    tril = (t_i >= s_i).astype(BF16)
    b_all = [_dot_sel(tril, logf[bb * L:(bb + 1) * L]) for bb in range(n_batch)]
    eye = t_i == s_i

    levels = []
    n2 = L
    while n2 >= 2:
        n = n2 // 2
        sh = n2.bit_length() - 1
        m = ((t_i >> sh) == (s_i >> sh)) & ((t_i & (n2 - 1)) >= n) & ((s_i & (n2 - 1)) < n)
        levels.append((n2, m))
        n2 = n

    for h, bb in [(h, bb) for h in range(n_heads) for bb in range(n_batch)]:
        sl = slice(h * dk, (h + 1) * dk)
        rows = slice(bb * L, (bb + 1) * L)
        q = proj_ref[rows, sl]
        k = 1.0 - f[rows, sl]
        v = proj_ref[rows, 2 * width + h * dk:2 * width + (h + 1) * dk]
        b = b_all[bb][:, sl]
        st = st_ref[bb * n_heads + h]
        b_last = b[L - 1:L, :]
        o = _dot_nt((q * jnp.exp(b)).astype(BF16), st.astype(BF16))
        a = jnp.where(eye, jnp.sum(q * k, axis=-1, keepdims=True), 0.0)
        qb = q.astype(BF16)
        kb = k.astype(BF16)
        for n2, m in levels:
            e = jnp.exp(-jnp.abs(b - _block_mid_ref(b, n2))).astype(BF16)
            a = jnp.where(m, _dot_nt(qb * e, kb * e), a)
        o = o + _dot(a.astype(BF16), v.astype(BF16))
        ke = k * jnp.exp(b_last - b)
        st_ref[bb * n_heads + h] = st * jnp.exp(b_last) + _dot_tn(v.astype(BF16), ke.astype(BF16))
        g = proj_ref[rows, 3 * width + h * dk:3 * width + (h + 1) * dk]
        ms = jnp.mean(o * o, axis=-1, keepdims=True)
        o_ref[bb, :, sl] = (o * lax.rsqrt(ms + NORM_EPS) * nw_ref[:, sl] * _silu(g)).astype(o_ref.dtype)


def _hgrn2(h, w_hg, lb, hg_norm, bsz, seq):
    t, d = h.shape
    width = lb.shape[0]
    n_heads = width // HG_DK
    ns = seq // CHUNK
    nb = HG_BATCH_ROWS if bsz % HG_BATCH_ROWS == 0 else 1
    out = pl.pallas_call(
        _hgrn2_kernel,
        grid=(bsz // nb, ns),
        in_specs=[pl.BlockSpec((nb, CHUNK, d), lambda b, s: (b, s, 0)),
                  pl.BlockSpec(w_hg.shape, lambda b, s: (0, 0)),
                  pl.BlockSpec((1, width), lambda b, s: (0, 0)),
                  pl.BlockSpec((1, width), lambda b, s: (0, 0))],
        out_specs=pl.BlockSpec((nb, CHUNK, width), lambda b, s: (b, s, 0)),
        out_shape=jax.ShapeDtypeStruct((bsz, seq, width), BF16),
        scratch_shapes=[pltpu.VMEM((nb * CHUNK, w_hg.shape[1]), F32), pltpu.VMEM((nb * n_heads, HG_DK, HG_DK), F32)],
        compiler_params=_cparams(("arbitrary", "arbitrary")),
        name="hgrn2_scan",
    )(h.reshape(bsz, seq, d), w_hg, lb.reshape(1, width), hg_norm.reshape(1, width))
    return out.reshape(t, width)


def _ssd_kernel(h_ref, w_ref, cw_ref, cb_ref, dtb_ref, a_ref, dsk_ref, nw_ref, o_ref,
                proj_ref, carry_ref, xpad_ref, st_ref):
    L = CHUNK
    width = o_ref.shape[2]
    xw = cw_ref.shape[1]
    n_heads = width // SSM_P
    gw = width // SSM_GROUPS
    heads_per_group = n_heads // SSM_GROUPS

    @pl.when(pl.program_id(1) == 0)
    def _():
        carry_ref[...] = jnp.zeros_like(carry_ref)
        st_ref[...] = jnp.zeros_like(st_ref)

    n_batch = h_ref.shape[0]
    proj_ref[...] = _dot(h_ref[...].reshape(n_batch * L, h_ref.shape[2]), w_ref[...])
    t_i = _iota((L, L), 0)
    s_i = _iota((L, L), 1)
    triu = (t_i <= s_i).astype(BF16)
    causal = t_i >= s_i
    diag = t_i == s_i
    lane_lo = _iota((L, 2 * SSM_P), 1) < SSM_P
    bd_mask = (_iota((2 * L, 2 * SSM_P), 0) < L) == (_iota((2 * L, 2 * SSM_P), 1) < SSM_P)

    for bb in range(n_batch):
        rs = slice(bb * L, (bb + 1) * L)
        z_ref = proj_ref.at[rs, 0:width]
        dt_ref = proj_ref.at[rs, width + xw:]
        xraw = proj_ref[rs, width:width + xw]
        xpad_ref[bb, 0:SUBLANES, :] = carry_ref[bb]
        xpad_ref[bb, SUBLANES:SUBLANES + L, :] = xraw
        carry_ref[bb] = xraw[L - SUBLANES:L, :]
        acc = cb_ref[...] + jnp.zeros_like(xraw)
        for j in range(SSM_CONV):
            acc = acc + cw_ref[j:j + 1, :] * xpad_ref[bb, pl.ds(SUBLANES - (SSM_CONV - 1) + j, L), :]
        xc = _silu(acc)
        xs = xc[:, :width]
        bm = xc[:, width:width + SSM_GROUPS * SSM_N]
        cm = xc[:, width + SSM_GROUPS * SSM_N:]

        dt = _softplus(dt_ref[...] + dtb_ref[...])
        da = dt * a_ref[...]
        da_t = da.T
        dt_t = dt.T
        acs_t = _dot_rsel(da_t, triu)

        y_pairs = []
        for g in range(SSM_GROUPS):
            bg = bm[:, g * SSM_N:(g + 1) * SSM_N]
            cg = cm[:, g * SSM_N:(g + 1) * SSM_N]
            cb = _dot_nt(cg.astype(BF16), bg.astype(BF16))
            hg = st_ref[bb, :, g * gw:(g + 1) * gw]
            yoff_g = _dot(cg.astype(BF16), hg.astype(BF16))
            xsc_parts, decay_parts = [], []
            for pr in range(heads_per_group // 2):
                j0 = g * heads_per_group + 2 * pr
                gs, ds, ecol, elast = [], [], [], []
                for j in (j0, j0 + 1):
                    row_b = jnp.broadcast_to(acs_t[j:j + 1, :], (L, L))
                    col_b = row_b.T
                    dt_row = jnp.broadcast_to(dt_t[j:j + 1, :], (L, L))
                    lmat = jnp.exp(jnp.minimum(col_b - row_b, 0.0))
                    gs.append(jnp.where(causal, cb * lmat * dt_row, 0.0))
                    a_last = acs_t[j:j + 1, L - 1:L]
                    ds.append(jnp.where(diag, jnp.exp(a_last - row_b) * dt_row, 0.0))
                    ecol.append(jnp.exp(col_b))
                    elast.append(jnp.exp(a_last))
                lhs = jnp.concatenate([jnp.concatenate(gs, axis=1), jnp.concatenate(ds, axis=1)], axis=0)
                xs_pair = xs[:, j0 * SSM_P:(j0 + 2) * SSM_P]
                bd = jnp.where(bd_mask, jnp.concatenate([xs_pair, xs_pair], axis=0), 0.0)
                res = _dot(lhs.astype(BF16), bd.astype(BF16))
                yoff = yoff_g[:, pr * 2 * SSM_P:(pr + 1) * 2 * SSM_P] * jnp.where(lane_lo, ecol[0], ecol[1])
                y_pairs.append(res[:L] + yoff)
                xsc_parts.append(res[L:])
                decay_parts.append(jnp.where(lane_lo[0:1], elast[0], elast[1]))
            xsc_g = jnp.concatenate(xsc_parts, axis=1)
            decay_g = jnp.concatenate(decay_parts, axis=1)
            st_ref[bb, :, g * gw:(g + 1) * gw] = hg * decay_g + _dot_tn(bg.astype(BF16), xsc_g.astype(BF16))
        y = jnp.concatenate(y_pairs, axis=1) + dsk_ref[...] * xs
        yz = y * _silu(z_ref[...])
        for g in range(SSM_GROUPS):
            seg = yz[:, g * gw:(g + 1) * gw]
            ms = jnp.mean(seg * seg, axis=-1, keepdims=True)
            o_ref[bb, :, g * gw:(g + 1) * gw] = (seg * lax.rsqrt(ms + NORM_EPS)
                                                 * nw_ref[:, g * gw:(g + 1) * gw]).astype(o_ref.dtype)


def _ssd(h, w_ssd, conv_w, conv_b, dt_bias, a_log, d_skip, ssm_norm, bsz, seq):
    t, d = h.shape
    width = ssm_norm.shape[0]
    xw = conv_w.shape[1]
    n_heads = width // SSM_P
    ns = seq // CHUNK
    nb = SSD_BATCH_ROWS if bsz % SSD_BATCH_ROWS == 0 else 1
    pad = lambda v: jnp.zeros((1, LANES), F32).at[0, :n_heads].set(v)
    row = lambda w: pl.BlockSpec((nb, CHUNK, w), lambda b, s: (b, s, 0))
    const = lambda r, w: pl.BlockSpec((r, w), lambda b, s: (0, 0))
    out = pl.pallas_call(
        _ssd_kernel,
        grid=(bsz // nb, ns),
        in_specs=[row(d), const(*w_ssd.shape), const(SSM_CONV, xw), const(1, xw), const(1, LANES),
                  const(1, LANES), const(1, width), const(1, width)],
        out_specs=row(width),
        out_shape=jax.ShapeDtypeStruct((bsz, seq, width), BF16),
        scratch_shapes=[pltpu.VMEM((nb * CHUNK, w_ssd.shape[1]), F32), pltpu.VMEM((nb, SUBLANES, xw), F32),
                        pltpu.VMEM((nb, CHUNK + SUBLANES, xw), F32), pltpu.VMEM((nb, SSM_N, width), F32)],
        compiler_params=_cparams(("arbitrary", "arbitrary")),
        name="ssd_scan",
    )(h.reshape(bsz, seq, d), w_ssd, conv_w, conv_b.reshape(1, xw), pad(dt_bias), pad(-jnp.exp(a_log)),
      jnp.repeat(d_skip, SSM_P).reshape(1, width), ssm_norm.reshape(1, width))
    return out.reshape(t, width)


def _route(probs):
    n_exp = N_GROUPS_MOE * 4
    p = [probs[e:e + 1, :] for e in range(n_exp)]
    gs = []
    for g in range(N_GROUPS_MOE):
        a, b, c, d = p[4 * g:4 * g + 4]
        gs.append(jnp.maximum(jnp.maximum(jnp.maximum(a + b, a + c), jnp.maximum(a + d, b + c)),
                              jnp.maximum(b + d, c + d)))
    best = jnp.zeros_like(gs[0]).astype(I32)
    bs = gs[0]
    for g in range(1, N_GROUPS_MOE):
        upd = gs[g] > bs
        best = jnp.where(upd, g, best)
        bs = jnp.where(upd, gs[g], bs)
    q = [jnp.where(best == 0, p[i], jnp.where(best == 1, p[4 + i], jnp.where(best == 2, p[8 + i], p[12 + i])))
         for i in range(4)]
    i1 = jnp.zeros_like(best)
    v1 = q[0]
    for i in range(1, 4):
        upd = q[i] > v1
        i1 = jnp.where(upd, i, i1)
        v1 = jnp.where(upd, q[i], v1)
    i2 = jnp.zeros_like(best)
    v2 = jnp.full_like(v1, -1.0)
    for i in range(4):
        upd = (i1 != i) & (q[i] > v2)
        i2 = jnp.where(upd, i, i2)
        v2 = jnp.where(upd, q[i], v2)
    den = v1 + v2
    return best * 4 + i1, best * 4 + i2, v1 / den, v2 / den


def _head_stats_expand(stack, seg_ref):
    pw = seg_ref.shape[0]
    seg = seg_ref[...]
    return jnp.concatenate([_dot_rsel(stack[:, p * pw:(p + 1) * pw], seg) for p in range(stack.shape[1] // pw)], axis=1)


def _epilogue_kernel(*refs, n_in, rwkv):
    ins = refs[:n_in]
    pos = n_in
    w_refs = refs[pos:pos + n_in]
    pos += n_in
    if rwkv:
        g_ref, bonus_ref, lnw_ref, lnb_ref, seg_ref = refs[pos:pos + 5]
        pos += 5
    x_ref, gate_ref, nw_ref, sh_ref, sc_ref, rw_ref, rb_ref = refs[pos:pos + 7]
    pos += 7
    xo_ref, h_ref, ti_ref, tw_ref = refs[pos:pos + 4]

    tm = ROW_TM
    rw_hi, rw_lo = _split(rw_ref[...])
    for sub in range(x_ref.shape[0] // tm):
        rs = pl.ds(sub * tm, tm)
        if rwkv:
            o = ins[0][rs, :]
            inv_n = 1.0 / RW_N
            st = _head_stats_expand(jnp.concatenate([o, o * o], axis=0), seg_ref) * inv_n
            mean = st[:tm]
            var = jnp.maximum(st[tm:] - mean * mean, 0.0)
            o = (o - mean) * lax.rsqrt(var + RW_GN_EPS) * lnw_ref[...] + lnb_ref[...]
            o = (o + bonus_ref[rs, :].astype(F32)) * g_ref[rs, :].astype(F32)
            y = _dot(o.astype(BF16), w_refs[0][...])
        else:
            y = _dot(ins[0][rs, :], w_refs[0][...])
            for a_ref, w_ref in zip(ins[1:], w_refs[1:]):
                y = y + _dot(a_ref[rs, :], w_ref[...])
        x_new = x_ref[rs, :] + gate_ref[0] * y
        xo_ref[rs, :] = x_new
        h = _rms_mod(x_new, nw_ref[...], sh_ref[0], sc_ref[0])
        _store_rows(h_ref.at[pl.ds(sub * tm * ROW_PITCH, tm * ROW_PITCH)], h)
        h_hi, h_lo = _split(h)
        logits = _dot_nt(rw_hi, h_hi) + _dot_nt(rw_hi, h_lo) + _dot_nt(rw_lo, h_hi) + rb_ref[...]
        mx = jnp.max(logits, axis=0, keepdims=True)
        ex = jnp.exp(logits - mx)
        probs = ex / jnp.sum(ex, axis=0, keepdims=True)
        e1, e2, w1, w2 = _route(probs)
        zi = jnp.zeros((SUBLANES - TOP_K, tm), I32)
        ti_ref[:, sub * tm:(sub + 1) * tm] = jnp.concatenate([e1, e2, zi], axis=0)
        tw_ref[:, sub * tm:(sub + 1) * tm] = jnp.concatenate([w1, w2, zi.astype(F32)], axis=0)


def _epilogue(ins, ws, x2, gate, nw, shift, scale, router_w, router_b, bsz, seq, rwkv_extra=None):
    t, d = x2.shape
    tm = EPILOGUE_SUBTILES * ROW_TM
    ns = seq // tm
    n_exp = router_w.shape[1]
    rw_t = jnp.zeros((LANES, d), F32).at[:n_exp].set(router_w.T)
    rb = jnp.full((LANES, 1), -1e30, F32).at[:n_exp, 0].set(router_b)
    row = lambda w: pl.BlockSpec((tm, w), lambda b, s: (b * ns + s, 0))
    const = lambda a: pl.BlockSpec(a.shape, lambda b, s: (0,) * a.ndim)
    per_b = pl.BlockSpec((1, 1, d), lambda b, s: (b, 0, 0))
    args = list(ins) + list(ws)
    specs = [row(a.shape[1]) for a in ins] + [const(w) for w in ws]
    if rwkv_extra is not None:
        g, bonus, lnw, lnb, seg = rwkv_extra
        args += [g, bonus, lnw.reshape(1, d), lnb.reshape(1, d), seg]
        specs += [row(d), row(d), pl.BlockSpec((1, d), lambda b, s: (0, 0)), pl.BlockSpec((1, d), lambda b, s: (0, 0)),
                  const(seg)]
    args += [x2, gate.reshape(bsz, 1, d), nw.reshape(1, d), shift.reshape(bsz, 1, d), scale.reshape(bsz, 1, d), rw_t, rb]
    specs += [row(d), per_b, pl.BlockSpec((1, d), lambda b, s: (0, 0)), per_b, per_b, const(rw_t), const(rb)]
    lane_row = pl.BlockSpec((SUBLANES, tm), lambda b, s: (0, b * ns + s))
    return pl.pallas_call(
        functools.partial(_epilogue_kernel, n_in=len(ins), rwkv=rwkv_extra is not None),
        grid=(bsz, ns),
        in_specs=specs,
        out_specs=[row(d), pl.BlockSpec((tm * ROW_PITCH, LANES), lambda b, s: (b * ns + s, 0)), lane_row, lane_row],
        out_shape=[jax.ShapeDtypeStruct((t, d), F32), jax.ShapeDtypeStruct((t * ROW_PITCH, LANES), F32),
                   jax.ShapeDtypeStruct((SUBLANES, t), I32), jax.ShapeDtypeStruct((SUBLANES, t), F32)],
        compiler_params=_cparams(("arbitrary", "arbitrary")),
        name="mixer_epilogue",
    )(*args)


def _route_plan(top_i, n_exp, tm):
    t = top_i.shape[1]
    n_pairs = TOP_K * t
    n_tiles = n_pairs // tm + n_exp
    e_flat = top_i[:TOP_K].reshape(-1)
    onehot = (e_flat[:, None] == jnp.arange(n_exp, dtype=I32)[None, :]).astype(I32)
    csum = jnp.cumsum(onehot, axis=0)
    counts = csum[-1]
    rank = jnp.sum((csum - 1) * onehot, axis=1)
    padded = ((counts + tm - 1) // tm) * tm
    ends = jnp.cumsum(padded)
    pos = (ends - padded)[e_flat] + rank
    tile_start = jnp.arange(n_tiles, dtype=I32) * tm
    tile_exp = jnp.minimum(jnp.sum((tile_start[:, None] >= ends[None, :]).astype(I32), axis=1), n_exp - 1)
    n_used = (ends[-1] // tm).reshape(1).astype(I32)
    gap_lo = jnp.concatenate([ends - padded + counts, ends[-1:]]).astype(I32)
    gap_hi = jnp.concatenate([ends, jnp.full((1,), n_tiles * tm, I32)]).astype(I32)
    return pos.astype(I32), tile_exp.astype(I32), n_used, gap_lo, gap_hi


def _dispatch_kernel(pos_ref, glo_ref, ghi_ref, h_ref, xs_hbm, zbuf, sem, zsem):
    tm = DISPATCH_TM
    n_tok = pos_ref.shape[0] // TOP_K
    i = pl.program_id(0)

    @pl.when(i == 0)
    def _():
        zbuf[...] = jnp.zeros_like(zbuf)

        def chunk_copy(row):
            return pltpu.make_async_copy(zbuf, xs_hbm.at[pl.ds(row * ROW_PITCH, ZERO_CHUNK * ROW_PITCH)], zsem)

        def row_copy(row):
            return pltpu.make_async_copy(zbuf.at[pl.ds(0, ROW_PITCH)], xs_hbm.at[pl.ds(row * ROW_PITCH, ROW_PITCH)], zsem)

        for e in range(glo_ref.shape[0]):
            lo = glo_ref[e]
            hi = ghi_ref[e]
            n_chunks = lax.div(hi - lo, ZERO_CHUNK)
            mid = lo + n_chunks * ZERO_CHUNK

            def start_chunk(c, carry):
                chunk_copy(lo + c * ZERO_CHUNK).start()
                return carry

            def start_row(r, carry):
                row_copy(r).start()
                return carry

            def wait_chunk(c, carry):
                chunk_copy(lo).wait()
                return carry

            def wait_row(r, carry):
                row_copy(lo).wait()
                return carry

            lax.fori_loop(0, n_chunks, start_chunk, 0)
            lax.fori_loop(mid, hi, start_row, 0)
            lax.fori_loop(0, n_chunks, wait_chunk, 0)
            lax.fori_loop(mid, hi, wait_row, 0)

    def body(r8, c):
        for u in range(GATHER_UNROLL):
            r = r8 * GATHER_UNROLL + u
            for k in range(TOP_K):
                p = pos_ref[k * n_tok + i * tm + r]
                pltpu.make_async_copy(h_ref.at[pl.ds(r * ROW_PITCH, ROW_PITCH)],
                                      xs_hbm.at[pl.ds(p, ROW_PITCH)], sem).start(priority=k)
        return c
    lax.fori_loop(0, tm // GATHER_UNROLL, body, 0)
    for k in range(TOP_K):
        pltpu.make_async_copy(h_ref, xs_hbm.at[pl.ds(0, tm * ROW_PITCH)], sem).wait()


def _dispatch(h, pos, gap_lo, gap_hi, n_rows):
    tm = DISPATCH_TM
    t = h.shape[0] // ROW_PITCH
    grid_spec = pltpu.PrefetchScalarGridSpec(
        num_scalar_prefetch=3,
        grid=(t // tm,),
        in_specs=[pl.BlockSpec((tm * ROW_PITCH, LANES), lambda i, p, lo, hi: (i, 0))],
        out_specs=pl.BlockSpec(memory_space=pl.ANY),
        scratch_shapes=[pltpu.VMEM((ZERO_CHUNK * ROW_PITCH, LANES), F32), pltpu.SemaphoreType.DMA(()),
                        pltpu.SemaphoreType.DMA(())],
    )
    return pl.pallas_call(
        _dispatch_kernel,
        grid_spec=grid_spec,
        out_shape=jax.ShapeDtypeStruct((n_rows * ROW_PITCH, LANES), F32),
        compiler_params=_cparams(("arbitrary",)),
        name="moe_dispatch",
    )(pos, gap_lo, gap_hi, h)


def _moe_kernel(texp_ref, nused_ref, x_ref, w1_ref, w3_ref, w2_ref, y_ref, wb1, wb3, wb2):
    tm = MOE_TM
    d = wb1.shape[0]
    i = pl.program_id(0)
    n_used = nused_ref[0]

    @pl.when((i == 0) | (texp_ref[i] != texp_ref[jnp.maximum(i - 1, 0)]))
    def _():
        wb1[...] = w1_ref[0, 0].astype(BF16)
        wb3[...] = w3_ref[0, 0].astype(BF16)
        wb2[...] = w2_ref[0, 0].astype(BF16)

    @pl.when(i < n_used)
    def _():
        x = _load_rows(x_ref, tm, d).astype(BF16)
        a = _dot(x, wb1[...])
        b = _dot(x, wb3[...])
        hid = (_silu(a) * b).astype(BF16)
        _store_rows(y_ref, _dot(hid, wb2[...]))

    @pl.when(i >= n_used)
    def _():
        y_ref[...] = jnp.zeros_like(y_ref)


def _moe(xs, w1, w3, w2, layer, tile_exp, n_used):
    _, n_exp, d, dff = w1.shape
    tm = MOE_TM
    n_tiles = tile_exp.shape[0]
    rows = pl.BlockSpec((tm * ROW_PITCH, LANES), lambda i, te, nu: (i, 0))
    used_rows = pl.BlockSpec((tm * ROW_PITCH, LANES), lambda i, te, nu: (jnp.minimum(i, nu[0] - 1), 0))
    grid_spec = pltpu.PrefetchScalarGridSpec(
        num_scalar_prefetch=2,
        grid=(n_tiles,),
        in_specs=[used_rows,
                  pl.BlockSpec((1, 1, d, dff), lambda i, te, nu: (layer, te[i], 0, 0)),
                  pl.BlockSpec((1, 1, d, dff), lambda i, te, nu: (layer, te[i], 0, 0)),
                  pl.BlockSpec((1, 1, dff, d), lambda i, te, nu: (layer, te[i], 0, 0))],
        out_specs=rows,
        scratch_shapes=[pltpu.VMEM((d, dff), BF16), pltpu.VMEM((d, dff), BF16), pltpu.VMEM((dff, d), BF16)],
    )
    return pl.pallas_call(
        _moe_kernel,
        grid_spec=grid_spec,
        out_shape=jax.ShapeDtypeStruct((n_tiles * tm * ROW_PITCH, LANES), F32),
        compiler_params=_cparams(("arbitrary",)),
        name="moe_experts",
    )(tile_exp, n_used, xs, w1, w3, w2)


def _combine_kernel(pos_ref, y_hbm, x_ref, tw_ref, gate_ref, nw_ref, sh_ref, sc_ref, *out_and_scratch, final, n_tok):
    if final:
        o_ref, ybuf, sem = out_and_scratch
    else:
        xo_ref, h_ref, ybuf, sem = out_and_scratch
    tm, d = x_ref.shape
    n_sub = d // LANES
    i = pl.program_id(0)
    n_steps = pl.num_programs(0)
    slot = i % 2

    def start_gather(tile, sl):
        def body(r8, c):
            for u in range(GATHER_UNROLL):
                r = r8 * GATHER_UNROLL + u
                for k in range(TOP_K):
                    src = pos_ref[k * n_tok + tile * tm + r]
                    pltpu.make_async_copy(y_hbm.at[pl.ds(src, n_sub)], ybuf.at[sl, k, pl.ds(r * ROW_PITCH, n_sub)],
                                          sem.at[sl]).start(priority=k)
            return c
        lax.fori_loop(0, tm // GATHER_UNROLL, body, 0)

    def wait_gather(sl):
        for k in range(TOP_K):
            pltpu.make_async_copy(y_hbm.at[pl.ds(0, tm * n_sub)], ybuf.at[sl, k, pl.ds(0, tm * n_sub)], sem.at[sl]).wait()

    @pl.when(i == 0)
    def _():
        start_gather(0, 0)

    @pl.when(i + 1 < n_steps)
    def _():
        start_gather(i + 1, 1 - slot)

    wait_gather(slot)
    tw = tw_ref[...]
    moe = tw[:, 0:1] * _load_rows(ybuf.at[slot, 0], tm, d) + tw[:, 1:2] * _load_rows(ybuf.at[slot, 1], tm, d)
    x_new = x_ref[...] + gate_ref[0] * moe
    if final:
        o_ref[...] = x_new * lax.rsqrt(jnp.mean(x_new * x_new, axis=-1, keepdims=True) + NORM_EPS) * nw_ref[...]
    else:
        xo_ref[...] = x_new
        h_ref[...] = _rms_mod(x_new, nw_ref[...], sh_ref[0], sc_ref[0]).astype(h_ref.dtype)


def _combine(pos, y_sorted, x2, top_w, gate, nw, shift, scale, bsz, seq, final):
    t, d = x2.shape
    tm = ROW_TM
    ns = seq // tm
    tw = top_w.T
    row = pl.BlockSpec((tm, d), lambda i, p: (i, 0))
    per_b = pl.BlockSpec((1, 1, d), lambda i, p: (i // ns, 0, 0))
    grid_spec = pltpu.PrefetchScalarGridSpec(
        num_scalar_prefetch=1,
        grid=(t // tm,),
        in_specs=[pl.BlockSpec(memory_space=pl.ANY), row,
                  pl.BlockSpec((tm, SUBLANES), lambda i, p: (i, 0)),
                  per_b, pl.BlockSpec((1, d), lambda i, p: (0, 0)), per_b, per_b],
        out_specs=row if final else [row, row],
        scratch_shapes=[pltpu.VMEM((2, TOP_K, tm * ROW_PITCH, LANES), F32), pltpu.SemaphoreType.DMA((2,))],
    )
    out_shape = (jax.ShapeDtypeStruct((t, d), F32) if final else
                 [jax.ShapeDtypeStruct((t, d), F32), jax.ShapeDtypeStruct((t, d), BF16)])
    return pl.pallas_call(
        functools.partial(_combine_kernel, final=final, n_tok=t),
        grid_spec=grid_spec,
        out_shape=out_shape,
        compiler_params=_cparams(("arbitrary",)),
        name="moe_combine",
    )(pos, y_sorted, x2, tw, gate.reshape(bsz, 1, d), nw.reshape(1, d), shift.reshape(bsz, 1, d), scale.reshape(bsz, 1, d))


def _rwkv_prep_kernel(h_ref, mu_ref, wr_ref, wk_ref, wv_ref, dec0_ref, dec1_ref, dec2_ref, a0_ref, a1_ref, a2_ref,
                      g1_ref, g2_ref, kk_ref, ka_ref, rk_ref, seg_ref,
                      r_out, lw_out, k_out, v_out, kkn_out, a_out, g_out, bonus_out, carry_ref, hp_ref):
    tm = h_ref.shape[0]

    @pl.when(pl.program_id(1) == 0)
    def _():
        carry_ref[...] = jnp.zeros_like(carry_ref)

    hb = h_ref[...]
    h = hb.astype(F32)
    hp_ref[0:SUBLANES, :] = carry_ref[...]
    hp_ref[SUBLANES:SUBLANES + tm, :] = h
    carry_ref[...] = h[tm - SUBLANES:tm, :]
    xxb = (hp_ref[pl.ds(SUBLANES - 1, tm), :] - h).astype(BF16)
    mix = lambda i: hb + xxb * mu_ref[i:i + 1, :].astype(BF16)
    r = _dot(mix(0), wr_ref[...])
    k = _dot(mix(1), wk_ref[...])
    v = _dot(mix(2), wv_ref[...])
    wl = dec0_ref[...] + _dot(jnp.tanh(_dot(mix(3), dec1_ref[...])).astype(BF16), dec2_ref[...])
    lw = -RW_DECAY_SCALE * _sigmoid(wl)
    a = _sigmoid(a0_ref[...] + _dot(_dot(mix(4), a1_ref[...]).astype(BF16), a2_ref[...]))
    g = _dot(_sigmoid(_dot(mix(5), g1_ref[...])).astype(BF16), g2_ref[...])
    kk = k * kk_ref[...]
    k_h = k * (1.0 + (a - 1.0) * ka_ref[...])
    st = _head_stats_expand(jnp.concatenate([kk * kk, r * k_h * rk_ref[...]], axis=0), seg_ref)
    kkn = kk * lax.rsqrt(jnp.maximum(st[:tm], 1e-24))
    r_out[...] = r.astype(r_out.dtype)
    lw_out[...] = lw
    k_out[...] = k_h.astype(k_out.dtype)
    v_out[...] = v.astype(v_out.dtype)
    kkn_out[...] = kkn.astype(kkn_out.dtype)
    a_out[...] = a.astype(a_out.dtype)
    g_out[...] = g.astype(g_out.dtype)
    bonus_out[...] = (st[tm:] * v).astype(bonus_out.dtype)


def _rwkv_prep(h, mu, w_rkv, dec0, dec1, dec2, a0, a1, a2, g1, g2, k_k, k_a, r_k, seg, bsz, seq):
    t, d = h.shape
    tm = ROW_TM
    ns = seq // tm
    padc = lambda w: jnp.zeros((d, LANES), F32).at[:, :w.shape[1]].set(w).astype(BF16)
    padr = lambda w: jnp.zeros((LANES, d), F32).at[:w.shape[0]].set(w).astype(BF16)
    vec = lambda v: v.reshape(1, d)
    args = [h, mu, w_rkv[0].astype(BF16), w_rkv[1].astype(BF16), w_rkv[2].astype(BF16), vec(dec0), padc(dec1), padr(dec2),
            vec(a0), padc(a1), padr(a2), padc(g1), padr(g2), vec(k_k), vec(k_a), vec(r_k), seg]
    row = pl.BlockSpec((tm, d), lambda b, s: (b * ns + s, 0))
    const = lambda a: pl.BlockSpec(a.shape, lambda b, s: (0,) * a.ndim)
    outs = [BF16, F32, BF16, BF16, BF16, BF16, BF16, BF16]
    return pl.pallas_call(
        _rwkv_prep_kernel,
        grid=(bsz, ns),
        in_specs=[row] + [const(a) for a in args[1:]],
        out_specs=[row] * len(outs),
        out_shape=[jax.ShapeDtypeStruct((t, d), dt) for dt in outs],
        scratch_shapes=[pltpu.VMEM((SUBLANES, d), F32), pltpu.VMEM((tm + SUBLANES, d), F32)],
        compiler_params=_cparams(("arbitrary", "arbitrary")),
        name="rwkv_prep",
    )(*args)


def _rwkv_scan_kernel(r_ref, lw_ref, k_ref, v_ref, kk_ref, a_ref, y_ref, st_ref):
    L = RW_CHUNK
    pw = RW_PACK * RW_N
    n_packs = st_ref.shape[0]
    sh = RW_N.bit_length() - 1

    @pl.when(pl.program_id(1) == 0)
    def _():
        st_ref[...] = jnp.zeros_like(st_ref)

    t_i = _iota((L, L), 0)
    s_i = _iota((L, L), 1)
    tril = (t_i >= s_i).astype(BF16)
    n_batch = lw_ref.shape[0]
    per_batch = n_packs // n_batch
    wc_all = [_dot_sel(tril, lw_ref[bb]) for bb in range(n_batch)]
    lane_head = _iota((L, pw), 1) >> sh
    s_loc = _iota((L, pw), 1) & (RW_N - 1)
    t_loc = _iota((L, pw), 0)
    strict = s_loc < t_loc
    incl = s_loc <= t_loc
    bd_mask = (_iota((pw, pw), 0) >> sh) == (_iota((pw, pw), 1) >> sh)

    def bdiag(x):
        return jnp.where(bd_mask, jnp.concatenate([x] * RW_PACK, axis=0), 0.0).astype(BF16)

    packs = range(n_packs)
    sls = [(p // per_batch, slice((p % per_batch) * pw, (p % per_batch + 1) * pw)) for p in packs]
    pr, vs, sts, kkas, ks, wcs = [], [], [], [], [], []
    for p in packs:
        bb, sl = sls[p]
        r = r_ref[bb, :, sl].astype(F32)
        lw = lw_ref[bb, :, sl]
        k = k_ref[bb, :, sl].astype(F32)
        kk = kk_ref[bb, :, sl].astype(F32)
        a = a_ref[bb, :, sl].astype(F32)
        wc = wc_all[bb][:, sl]
        e_inv = jnp.exp(-wc)
        kka = kk * a
        al = -kk * jnp.exp(wc - lw)
        rb = r * jnp.exp(wc)
        bt = kka * e_inv
        kt = k * e_inv
        lhs = jnp.concatenate([al, rb], axis=0).astype(BF16)
        rows = [jnp.where(lane_head == hh, x, 0.0) for x in (bt, kt) for hh in range(RW_PACK)]
        st = st_ref[p]
        m = jnp.concatenate(rows + [st], axis=0).astype(BF16)
        pr.append(_dot_nt(lhs, m))
        vs.append(v_ref[bb, :, sl].astype(F32))
        sts.append(st)
        kkas.append(kka)
        ks.append(k)
        wcs.append(wc)
    bd_vs = [bdiag(vs[p]) for p in packs]
    us = [pr[p][:L, 2 * pw:] + _dot(jnp.where(strict, pr[p][:L, pw:2 * pw], 0.0).astype(BF16), bd_vs[p]) for p in packs]
    nmats = [jnp.where(strict, pr[p][:L, 0:pw], 0.0) for p in packs]
    n_steps = L.bit_length() - 1
    for it in range(n_steps):
        us = [us[p] + _dot(nmats[p].astype(BF16), bdiag(us[p])) for p in packs]
        if it + 1 < n_steps:
            nmats = [_dot(nmats[p].astype(BF16), bdiag(nmats[p])) for p in packs]
    for p in packs:
        a_rb = jnp.where(incl, pr[p][L:, 0:pw], 0.0)
        a_rk = jnp.where(incl, pr[p][L:, pw:2 * pw], 0.0)
        bb, sl = sls[p]
        y_ref[bb, :, sl] = pr[p][L:, 2 * pw:] + _dot(jnp.concatenate([a_rb, a_rk], axis=1).astype(BF16),
                                                     jnp.concatenate([bdiag(us[p]), bd_vs[p]], axis=0))
    for p in packs:
        w_last = wcs[p][L - 1:L, :]
        e_last = jnp.exp(w_last - wcs[p])
        upd = _dot_tn(jnp.concatenate([us[p], vs[p]], axis=0).astype(BF16),
                      jnp.concatenate([kkas[p] * e_last, ks[p] * e_last], axis=0).astype(BF16))
        st_ref[p] = jnp.where(bd_mask, sts[p] * jnp.exp(w_last) + upd, 0.0)


def _rwkv_scan(r, lw, k, v, kk, a, bsz, seq):
    t, d = r.shape
    ns = seq // RW_CHUNK
    pw = RW_PACK * RW_N
    nb = RW_BATCH_ROWS if bsz % RW_BATCH_ROWS == 0 else 1
    row = pl.BlockSpec((nb, RW_CHUNK, d), lambda b, s: (b, s, 0))
    y = pl.pallas_call(
        _rwkv_scan_kernel,
        grid=(bsz // nb, ns),
        in_specs=[row] * 6,
        out_specs=row,
        out_shape=jax.ShapeDtypeStruct((bsz, seq, d), F32),
        scratch_shapes=[pltpu.VMEM((nb * (d // pw), pw, pw), F32)],
        compiler_params=_cparams(("arbitrary", "arbitrary")),
        name="rwkv_scan",
    )(*[z.reshape(bsz, seq, d) for z in (r, lw, k, v, kk, a)])
    return y.reshape(t, d)


def _moe_block(h, top_i, top_w, x2, gate, w1, w3, w2, layer, nw, shift, scale, bsz, seq, final):
    n_exp = w1.shape[1]
    pos, tile_exp, n_used, gap_lo, gap_hi = _route_plan(top_i, n_exp, MOE_TM)
    pos = pos * ROW_PITCH
    x_sorted = _dispatch(h, pos, gap_lo, gap_hi, tile_exp.shape[0] * MOE_TM)
    y_sorted = _moe(x_sorted, w1, w3, w2, layer, tile_exp, n_used)
    return _combine(pos, y_sorted, x2, top_w, gate, nw, shift, scale, bsz, seq, final)


def kernel(x, c, mod_w, mod_b, norm_w, hg_lb_logits, ev_w_in, ev_hg_norm, ev_conv_w, ev_conv_b, ev_dt_bias, ev_a_log, ev_d_skip, ev_ssm_norm, ev_w_out, od_mu, od_w_rkv, od_w_dec0, od_w_dec1, od_w_dec2, od_a0, od_a1, od_a2, od_g1, od_g2, od_k_k, od_k_a, od_r_k, od_ln_w, od_ln_b, od_w_o, router_w, router_b, moe_w1, moe_w3, moe_w2, final_norm_w):
    bsz, seq, d = x.shape
    depth = mod_w.shape[0]
    t = bsz * seq
    x2 = x.reshape(t, d)
    mod = _adaln_mod(c, mod_w, mod_b)
    gamma = jax.nn.softmax(hg_lb_logits.astype(F32), axis=0)
    lower_bounds = jnp.cumsum(gamma, axis=0) - gamma[0]
    head_of_lane = jnp.arange(RW_PACK * RW_N, dtype=I32) // RW_N
    seg = (head_of_lane[:, None] == head_of_lane[None, :]).astype(BF16)

    h = None
    out = None
    for l in range(depth):
        sh_m, sc_m, gt_m, sh_f, sc_f, gt_f = [mod[l, :, i * d:(i + 1) * d] for i in range(6)]
        j = l // 2
        if h is None:
            h = _norm_mod(x2, norm_w[l, 0], sh_m, sc_m, bsz, seq)
        if l % 2 == 0:
            w_in = ev_w_in[j]
            hgw = ev_hg_norm.shape[1]
            sw = ev_ssm_norm.shape[1]
            xbw = ev_conv_w.shape[2]
            nh = ev_dt_bias.shape[1]
            c0 = 4 * hgw
            w_hg = w_in[:, :c0].astype(BF16)
            w_ssd = jnp.zeros((d, sw + xbw + LANES), F32).at[:, :sw + xbw + nh].set(w_in[:, c0:]).astype(BF16)
            o_a = _hgrn2(h, w_hg, lower_bounds[l + 1], ev_hg_norm[j], bsz, seq)
            o_b = _ssd(h, w_ssd, ev_conv_w[j], ev_conv_b[j], ev_dt_bias[j], ev_a_log[j], ev_d_skip[j],
                       ev_ssm_norm[j], bsz, seq)
            w_out = ev_w_out[j].astype(BF16)
            x2, hf, top_i, top_w = _epilogue([o_a, o_b], [w_out[:hgw], w_out[hgw:]], x2, gt_m, norm_w[l, 1], sh_f, sc_f,
                                             router_w, router_b, bsz, seq)
        else:
            r, lw, k, v, kk, a, g, bonus = _rwkv_prep(h, od_mu[j], od_w_rkv[j], od_w_dec0[j], od_w_dec1[j], od_w_dec2[j],
                                                      od_a0[j], od_a1[j], od_a2[j], od_g1[j], od_g2[j], od_k_k[j],
                                                      od_k_a[j], od_r_k[j].reshape(-1), seg, bsz, seq)
            y = _rwkv_scan(r, lw, k, v, kk, a, bsz, seq)
            x2, hf, top_i, top_w = _epilogue([y], [od_w_o[j].astype(BF16)], x2, gt_m, norm_w[l, 1], sh_f, sc_f,
                                             router_w, router_b, bsz, seq,
                                             rwkv_extra=(g, bonus, od_ln_w[j], od_ln_b[j], seg))
        final = l == depth - 1
        if final:
            nw_next, sh_next, sc_next = final_norm_w, sh_f, sc_f
        else:
            nxt = [mod[l + 1, :, i * d:(i + 1) * d] for i in range(2)]
            nw_next, sh_next, sc_next = norm_w[l + 1, 0], nxt[0], nxt[1]
        res = _moe_block(hf, top_i, top_w, x2, gt_f, moe_w1, moe_w3, moe_w2, l, nw_next, sh_next, sc_next,
                         bsz, seq, final)
        if final:
            out = res
        else:
            x2, h = res
    return out.reshape(bsz, seq, d)
```
